```python
import math
import jax, jax.numpy as jnp
from jax import lax
import numpy as np

D_MODEL = 4096
BATCH = 2
SEQ = 4096
DEPTH = 1

ROPE_THETA = 10000.0
LN_EPS = 1e-5
NEG_INF = -1e30
Q_BLOCK = 128

A_HEADS = 64
A_KV_HEADS = 8
A_HEAD_DIM = D_MODEL // A_HEADS
A_WINDOW = 128

B_HEADS = 32
B_KV_HEADS = 4
B_HEAD_DIM = D_MODEL // B_HEADS
CMP_LEN = 32
CMP_STRIDE = 16
CMP_HIDDEN = 256
SLC_LEN = 64
SLC_TOPK = 16
SLC_Q_BLOCK = 64
WIN_LEN = 512

N_EXPERTS = 64
TOP_K = 8
EXPERT_FF = 512
SHARED_FF = 512
ROUTED_SCALE = 2.5
MOE_BLOCK = 256

DN_ALPHA = (2.0 * DEPTH) ** 0.25
DN_BETA = (8.0 * DEPTH) ** -0.25

kernel_name = "hybrid_swa_sink_nsa_moe_deepnorm"


def _in_layout():
    qa, kva = A_HEADS * A_HEAD_DIM, A_KV_HEADS * A_HEAD_DIM
    qb, kvb = B_HEADS * B_HEAD_DIM, B_KV_HEADS * B_HEAD_DIM
    return [(qa, False), (kva, False), (kva, True),
            (qb, False),
            (kvb, False), (kvb, True),
            (kvb, False), (kvb, True),
            (kvb, False), (kvb, True),
            (3 * B_HEADS, False),
            (D_MODEL, False), (D_MODEL, False)]


def layer_norm(x, g, b):
    xf = x.astype(jnp.float32)
    mu = xf.mean(-1, keepdims=True)
    var = jnp.square(xf - mu).mean(-1, keepdims=True)
    y = (xf - mu) * lax.rsqrt(var + LN_EPS) * g.astype(jnp.float32) + b.astype(jnp.float32)
    return y.astype(x.dtype)


def rope(x, positions):
    dh = x.shape[-1]
    half = dh // 2
    inv = ROPE_THETA ** (-jnp.arange(half, dtype=jnp.float32) * 2.0 / dh)
    ang = positions.astype(jnp.float32)[:, :, None] * inv
    cos, sin = jnp.cos(ang)[:, :, None, :], jnp.sin(ang)[:, :, None, :]
    x1, x2 = x[..., :half].astype(jnp.float32), x[..., half:].astype(jnp.float32)
    return jnp.concatenate([x1 * cos - x2 * sin, x2 * cos + x1 * sin], -1).astype(x.dtype)


def banded_attention(q, k, v, window, sinks):
    B, S, H, dh = q.shape
    G = k.shape[2]
    rep = H // G
    blk = Q_BLOCK
    nb = S // blk
    nprev = -(-(window - 1) // blk)
    pad = ((0, 0), (nprev * blk, 0), (0, 0), (0, 0))
    kb = jnp.pad(k, pad).reshape(B, nb + nprev, blk, G, dh)
    vb = jnp.pad(v, pad).reshape(B, nb + nprev, blk, G, dh)
    kw = jnp.concatenate([kb[:, i:i + nb] for i in range(nprev + 1)], axis=2)
    vw = jnp.concatenate([vb[:, i:i + nb] for i in range(nprev + 1)], axis=2)
    qb = q.reshape(B, nb, blk, G, rep, dh)
    s = jnp.einsum('bnqgrd,bnkgd->bngrqk', qb, kw).astype(jnp.float32) * (dh ** -0.5)
    nidx = jnp.arange(nb)[:, None, None]
    qpos = nidx * blk + jnp.arange(blk)[None, :, None]
    kpos = (nidx - nprev) * blk + jnp.arange((nprev + 1) * blk)[None, None, :]
    delta = qpos - kpos
    mask = (kpos >= 0) & (delta >= 0) & (delta < window)
    s = jnp.where(mask[None, :, None, None], s, NEG_INF)
    if sinks is None:
        p = jax.nn.softmax(s, axis=-1)
    else:
        sk = sinks.astype(jnp.float32).reshape(G, rep)[None, None, :, :, None, None]
        m = jnp.maximum(s.max(-1, keepdims=True), sk)
        e = jnp.exp(s - m)
        p = e / (e.sum(-1, keepdims=True) + jnp.exp(sk - m))
    o = jnp.einsum('bngrqk,bnkgd->bnqgrd', p.astype(v.dtype), vw)
    return o.reshape(B, S, H, dh)


def compress(t, pos_emb, w1, w2):
    B, S, G, dh = t.shape
    nshift = CMP_LEN // CMP_STRIDE
    c = t.reshape(B, S // CMP_STRIDE, CMP_STRIDE, G, dh)
    nc = S // CMP_STRIDE - nshift + 1
    blocks = jnp.concatenate([c[:, i:i + nc] for i in range(nshift)], axis=2)
    blocks = blocks + pos_emb[None, None, :, None, :]
    flat = blocks.transpose(0, 1, 3, 2, 4).reshape(B, nc, G, CMP_LEN * dh)
    return jax.nn.gelu(flat @ w1) @ w2


def nsa_compressed(q, kc, vc, kpe, kw1, kw2, vpe, vw1, vw2):
    B, S, H, dh = q.shape
    G = kc.shape[2]
    rep = H // G
    kcmp = compress(kc, kpe, kw1, kw2)
    vcmp = compress(vc, vpe, vw1, vw2)
    nc = kcmp.shape[1]
    qg = q.reshape(B, S, G, rep, dh)
    s = jnp.einsum('bsgrd,bcgd->bgrsc', qg, kcmp).astype(jnp.float32) * (dh ** -0.5)
    c_end = jnp.arange(nc) * CMP_STRIDE + CMP_LEN - 1
    valid = c_end[None, :] <= jnp.arange(S)[:, None]
    s = jnp.where(valid, s, NEG_INF)
    m = s.max(-1, keepdims=True)
    e = jnp.where(valid, jnp.exp(s - m), 0.0)
    den = e.sum(-1, keepdims=True)
    p = e / jnp.where(den > 0, den, 1.0)
    o = jnp.einsum('bgrsc,bcgd->bsgrd', p.astype(vcmp.dtype), vcmp).reshape(B, S, H, dh)
    return o, p.sum(axis=2)


def nsa_selected(q, ks, vs, p_grp):
    B, S, H, dh = q.shape
    G = ks.shape[2]
    rep = H // G
    ns = S // SLC_LEN
    r1, r2 = SLC_LEN // CMP_STRIDE, CMP_LEN // CMP_STRIDE
    pp = jnp.pad(p_grp, ((0, 0), (0, 0), (0, 0), (r2 - 1, r1)))
    p_slc = jnp.zeros(p_grp.shape[:3] + (ns,), jnp.float32)
    for mm in range(r1):
        for nn in range(r2):
            p_slc = p_slc + pp[..., mm - nn + r2 - 1::r1][..., :ns]
    t = jnp.arange(S)[:, None]
    j = jnp.arange(ns)[None, :]
    cur = t // SLC_LEN
    valid = j * SLC_LEN <= t
    forced = (j == 0) | (j == cur) | (j == cur - 1)
    score = jnp.where(forced, 1e9, jnp.where(valid, p_slc, NEG_INF))
    n_sel = min(SLC_TOPK, ns)
    _, idx = lax.top_k(score, n_sel)
    kblk = ks.reshape(B, ns, SLC_LEN, G, dh).transpose(0, 3, 1, 2, 4)
    vblk = vs.reshape(B, ns, SLC_LEN, G, dh).transpose(0, 3, 1, 2, 4)
    tq = SLC_Q_BLOCK
    nq = S // tq
    q_ch = q.reshape(B, nq, tq, G, rep, dh).transpose(1, 0, 2, 3, 4, 5)
    i_ch = idx.reshape(B, G, nq, tq, n_sel).transpose(2, 0, 1, 3, 4)
    bi = jnp.arange(B)[:, None, None, None]
    gi = jnp.arange(G)[None, :, None, None]
    scale = dh ** -0.5

    def chunk(args):
        n, qc, ic = args
        kg = kblk[bi, gi, ic]
        vg = vblk[bi, gi, ic]
        s = jnp.einsum('bqgrd,bgqnkd->bgrqnk', qc, kg).astype(jnp.float32) * scale
        qpos = n * tq + jnp.arange(tq)
        kpos = ic[..., None] * SLC_LEN + jnp.arange(SLC_LEN)
        mask = kpos <= qpos[None, None, :, None, None]
        s = jnp.where(mask[:, :, None], s, NEG_INF)
        p = jax.nn.softmax(s.reshape(s.shape[:4] + (-1,)), axis=-1).reshape(s.shape)
        return jnp.einsum('bgrqnk,bgqnkd->bqgrd', p.astype(vg.dtype), vg)

    o = lax.map(chunk, (jnp.arange(nq), q_ch, i_ch))
    return o.transpose(1, 0, 2, 3, 4, 5).reshape(B, S, H, dh)


def routed_experts(xf, w_gate, w_up, w_down, idx, gates):
    N, D = xf.shape
    E = w_gate.shape[0]
    A = N * TOP_K
    flat_e = idx.reshape(-1)
    flat_tok = jnp.arange(A, dtype=jnp.int32) // TOP_K
    flat_w = gates.reshape(-1)
    order = jnp.argsort(flat_e)
    e_s, tok_s, w_s = flat_e[order], flat_tok[order], flat_w[order]
    counts = jnp.bincount(flat_e, length=E)
    starts = jnp.cumsum(counts) - counts
    padded = (counts + MOE_BLOCK - 1) // MOE_BLOCK * MOE_BLOCK
    pends = jnp.cumsum(padded)
    pstarts = pends - padded
    dest = pstarts[e_s] + (jnp.arange(A, dtype=jnp.int32) - starts[e_s])
    n_blocks = -(-A // MOE_BLOCK) + E
    P = n_blocks * MOE_BLOCK
    row_tok = jnp.full((P,), N, jnp.int32).at[dest].set(tok_s)
    row_w = jnp.zeros((P,), xf.dtype).at[dest].set(w_s)
    block_e = jnp.minimum(jnp.searchsorted(pends, jnp.arange(n_blocks, dtype=jnp.int32) * MOE_BLOCK,
                                           side='right'), E - 1)
    x_pad = jnp.concatenate([xf, jnp.zeros((1, D), xf.dtype)], axis=0)

    def body(acc, blk):
        rows, wts, e = blk
        xb = x_pad[rows]
        h = jax.nn.silu(xb @ w_gate[e]) * (xb @ w_up[e])
        yb = (h @ w_down[e]) * wts[:, None]
        return acc.at[rows].add(yb), None

    acc, _ = lax.scan(body, jnp.zeros((N + 1, D), xf.dtype),
                      (row_tok.reshape(n_blocks, MOE_BLOCK), row_w.reshape(n_blocks, MOE_BLOCK), block_e))
    return acc[:N]


def hybrid_layer(x, positions, w_in, a_sinks, cmp_k_pos, cmp_k_w1, cmp_k_w2, cmp_v_pos, cmp_v_w1,
                 cmp_v_w2, w_o, ln1_g, ln1_b, w_router, router_bias, exp_w_gate, exp_w_up, exp_w_down,
                 sh_w_gate, sh_w_up, sh_w_down, ln2_g, ln2_b):
    B, S, D = x.shape
    h = x @ w_in
    cuts = np.cumsum([w for w, _ in _in_layout()])[:-1].tolist()
    qa, ka, va, qb, kc, vc, ks, vs, kwn, vwn, g_nsa, g_a, g_b = jnp.split(h, cuts, axis=-1)

    qa = rope(qa.reshape(B, S, A_HEADS, A_HEAD_DIM), positions)
    ka = rope(ka.reshape(B, S, A_KV_HEADS, A_HEAD_DIM), positions)
    va = va.reshape(B, S, A_KV_HEADS, A_HEAD_DIM)
    o_a = banded_attention(qa, ka, va, A_WINDOW, a_sinks).reshape(B, S, A_HEADS * A_HEAD_DIM)

    kvshape = (B, S, B_KV_HEADS, B_HEAD_DIM)
    qb = rope(qb.reshape(B, S, B_HEADS, B_HEAD_DIM), positions)
    kc, ks, kwn = (rope(t.reshape(kvshape), positions) for t in (kc, ks, kwn))
    vc, vs, vwn = (t.reshape(kvshape) for t in (vc, vs, vwn))
    o_cmp, p_grp = nsa_compressed(qb, kc, vc, cmp_k_pos, cmp_k_w1, cmp_k_w2, cmp_v_pos, cmp_v_w1, cmp_v_w2)
    o_slc = nsa_selected(qb, ks, vs, p_grp)
    o_win = banded_attention(qb, kwn, vwn, WIN_LEN, None)
    gb = jax.nn.sigmoid(g_nsa.astype(jnp.float32)).astype(x.dtype).reshape(B, S, 3, B_HEADS, 1)
    o_b = (gb[:, :, 0] * o_cmp + gb[:, :, 1] * o_slc + gb[:, :, 2] * o_win).reshape(B, S, B_HEADS * B_HEAD_DIM)

    y = jax.nn.sigmoid(g_a) * o_a + jax.nn.sigmoid(g_b) * o_b
    x1 = layer_norm(DN_ALPHA * x + y @ w_o, ln1_g, ln1_b)

    xf = x1.reshape(B * S, D)
    scores = jax.nn.sigmoid((xf @ w_router).astype(jnp.float32))
    _, idx = lax.top_k(scores + router_bias.astype(jnp.float32), TOP_K)
    sv = jnp.take_along_axis(scores, idx, axis=-1)
    gates = (sv / sv.sum(-1, keepdims=True) * ROUTED_SCALE).astype(x.dtype)
    routed = routed_experts(xf, exp_w_gate, exp_w_up, exp_w_down, idx, gates)
    shared = (jax.nn.silu(xf @ sh_w_gate) * (xf @ sh_w_up)) @ sh_w_down
    return layer_norm(DN_ALPHA * x1 + (routed + shared).reshape(B, S, D), ln2_g, ln2_b)


def setup_inputs(seed: int = 0) -> dict:
    key = jax.random.key(seed)
    ks = jax.random.split(key, 24)

    def nrm(k, shape, scale):
        return jax.random.normal(k, shape, jnp.float32) * scale

    L, D = DEPTH, D_MODEL
    layout = _in_layout()
    col_scale = jnp.concatenate([jnp.full((w,), DN_BETA if isv else 1.0, jnp.float32) for w, isv in layout])
    total = sum(w for w, _ in layout)
    cin = CMP_LEN * B_HEAD_DIM
    return {
        "x": nrm(ks[0], (BATCH, SEQ, D), 1.0),
        "positions": jnp.broadcast_to(jnp.arange(SEQ, dtype=jnp.int32)[None, :], (BATCH, SEQ)),
        "w_in": nrm(ks[1], (L, D, total), D ** -0.5) * col_scale,
        "a_sinks": nrm(ks[2], (L, A_HEADS), 0.5),
        "cmp_k_pos": nrm(ks[3], (L, CMP_LEN, B_HEAD_DIM), 0.1),
        "cmp_k_w1": nrm(ks[4], (L, cin, CMP_HIDDEN), cin ** -0.5),
        "cmp_k_w2": nrm(ks[5], (L, CMP_HIDDEN, B_HEAD_DIM), (CMP_HIDDEN / 2.0) ** -0.5),
        "cmp_v_pos": nrm(ks[6], (L, CMP_LEN, B_HEAD_DIM), 0.1),
        "cmp_v_w1": nrm(ks[7], (L, cin, CMP_HIDDEN), cin ** -0.5),
        "cmp_v_w2": nrm(ks[8], (L, CMP_HIDDEN, B_HEAD_DIM), (CMP_HIDDEN / 2.0) ** -0.5 * DN_BETA),
        "w_o": nrm(ks[9], (L, D, D), D ** -0.5 * DN_BETA),
        "ln1_g": 1.0 + nrm(ks[10], (L, D), 0.02),
        "ln1_b": nrm(ks[11], (L, D), 0.02),
        "w_router": nrm(ks[12], (L, D, N_EXPERTS), D ** -0.5),
        "router_bias": nrm(ks[13], (L, N_EXPERTS), 0.01),
        "exp_w_gate": nrm(ks[14], (L, N_EXPERTS, D, EXPERT_FF), D ** -0.5),
        "exp_w_up": nrm(ks[15], (L, N_EXPERTS, D, EXPERT_FF), D ** -0.5),
        "exp_w_down": nrm(ks[16], (L, N_EXPERTS, EXPERT_FF, D), EXPERT_FF ** -0.5 * DN_BETA),
        "sh_w_gate": nrm(ks[17], (L, D, SHARED_FF), D ** -0.5),
        "sh_w_up": nrm(ks[18], (L, D, SHARED_FF), D ** -0.5),
        "sh_w_down": nrm(ks[19], (L, SHARED_FF, D), SHARED_FF ** -0.5 * DN_BETA),
        "ln2_g": 1.0 + nrm(ks[20], (L, D), 0.02),
        "ln2_b": nrm(ks[21], (L, D), 0.02),
    }


def reference(x, positions, w_in, a_sinks, cmp_k_pos, cmp_k_w1, cmp_k_w2, cmp_v_pos, cmp_v_w1, cmp_v_w2,
              w_o, ln1_g, ln1_b, w_router, router_bias, exp_w_gate, exp_w_up, exp_w_down,
              sh_w_gate, sh_w_up, sh_w_down, ln2_g, ln2_b):
    for l in range(DEPTH):
        x = hybrid_layer(x, positions, w_in[l], a_sinks[l], cmp_k_pos[l], cmp_k_w1[l], cmp_k_w2[l],
                         cmp_v_pos[l], cmp_v_w1[l], cmp_v_w2[l], w_o[l], ln1_g[l], ln1_b[l],
                         w_router[l], router_bias[l], exp_w_gate[l], exp_w_up[l], exp_w_down[l],
                         sh_w_gate[l], sh_w_up[l], sh_w_down[l], ln2_g[l], ln2_b[l])
    return x
```

```python
import functools

import jax
import jax.numpy as jnp
import numpy as np
from jax import lax
from jax.experimental import pallas as pl
from jax.experimental.pallas import tpu as pltpu

F32 = jnp.float32
BF16 = jnp.bfloat16

ROPE_THETA = 10000.0
LN_EPS = 1e-5
NEG_INF = -1e30

A_HEADS, A_KV_HEADS, A_HEAD_DIM, A_WINDOW = 64, 8, 64, 128
B_HEADS, B_KV_HEADS, B_HEAD_DIM = 32, 4, 128
CMP_LEN, CMP_STRIDE, CMP_HIDDEN = 32, 16, 256
SLC_LEN, SLC_TOPK, WIN_LEN = 64, 16, 512
N_EXPERTS, TOP_K, EXPERT_FF, SHARED_FF = 64, 8, 512, 512
ROUTED_SCALE = 2.5
MOE_BLOCK = 256

LANES = 128
TQ = 128
VMEM_LIMIT = 56 * 1024 * 1024


def _cparams(sem):
    return pltpu.CompilerParams(dimension_semantics=sem, vmem_limit_bytes=VMEM_LIMIT)


def _dot(a, b):
    return jnp.dot(a, b, preferred_element_type=F32)


def _dot_nt(a, b):
    return lax.dot_general(a, b, (((1,), (1,)), ((), ())), preferred_element_type=F32)


def _sigmoid(v):
    return 1.0 / (1.0 + jnp.exp(-v))


def _layer_norm(z, g, b):
    mu = jnp.mean(z, axis=-1, keepdims=True)
    zc = z - mu
    var = jnp.mean(zc * zc, axis=-1, keepdims=True)
    return zc * lax.rsqrt(var + LN_EPS) * g + b


def _proj_kernel(x_ref, w_ref, cos_ref, sin_ref, o_ref, *, epi):
    acc = _dot(x_ref[...], w_ref[...])
    if epi == "none":
        o_ref[...] = acc.astype(o_ref.dtype)
    elif epi == "sigmoid":
        o_ref[...] = _sigmoid(acc).astype(o_ref.dtype)
    else:
        cos = cos_ref[...]
        sin = sin_ref[...]
        lane = lax.broadcasted_iota(jnp.int32, (1, LANES), 1)
        for c in range(acc.shape[1] // LANES):
            y = acc[:, c * LANES:(c + 1) * LANES]
            if epi == "rope128":
                rot = pltpu.roll(y, 64, 1)
            else:
                rot = jnp.where((lane % 64) < 32, pltpu.roll(y, 96, 1), pltpu.roll(y, 32, 1))
            o_ref[:, c * LANES:(c + 1) * LANES] = (y * cos + rot * sin).astype(o_ref.dtype)


def _project(xb, w, cos, sin, epi, out_dtype, tm=512, tn=512):
    n, d = xb.shape
    ncol = w.shape[1]
    tm = min(tm, n)
    assert n % tm == 0 and ncol % tn == 0
    return pl.pallas_call(
        functools.partial(_proj_kernel, epi=epi),
        grid=(ncol // tn, n // tm),
        in_specs=[pl.BlockSpec((tm, d), lambda j, i: (i, 0)),
                  pl.BlockSpec((d, tn), lambda j, i: (0, j)),
                  pl.BlockSpec((tm, LANES), lambda j, i: (i, 0)),
                  pl.BlockSpec((tm, LANES), lambda j, i: (i, 0))],
        out_specs=pl.BlockSpec((tm, tn), lambda j, i: (i, j)),
        out_shape=jax.ShapeDtypeStruct((n, ncol), out_dtype),
        compiler_params=_cparams(("parallel", "parallel")),
        name="proj_" + epi,
    )(xb, w, cos, sin)


def _swa_kernel(sink_ref, q_ref, kp_ref, kc_ref, vp_ref, vc_ref, o_ref):
    n = pl.program_id(1)
    g = pl.program_id(2)
    kk = jnp.concatenate([kp_ref[...], kc_ref[...]], axis=0)
    vv = jnp.concatenate([vp_ref[...], vc_ref[...]], axis=0)
    lane = lax.broadcasted_iota(jnp.int32, (1, LANES), 1)
    lo = lane < A_HEAD_DIM
    zv = jnp.zeros_like(vv)
    v_halves = (jnp.where(lo, vv, zv), jnp.where(lo, zv, vv))
    nk = 2 * TQ
    qpos = n * TQ + lax.broadcasted_iota(jnp.int32, (TQ, nk), 0)
    kpos = (n - 1) * TQ + lax.broadcasted_iota(jnp.int32, (TQ, nk), 1)
    delta = qpos - kpos
    mask = (kpos >= 0) & (delta >= 0) & (delta < A_WINDOW)
    scale = A_HEAD_DIM ** -0.5
    for pair in range(4):
        qp = q_ref[:, pair * LANES:(pair + 1) * LANES]
        zq = jnp.zeros_like(qp)
        out = jnp.zeros((TQ, LANES), F32)
        for half in range(2):
            qh = jnp.where(lo, qp, zq) if half == 0 else jnp.where(lo, zq, qp)
            s = _dot_nt(qh, kk) * scale
            s = jnp.where(mask, s, NEG_INF)
            sk = sink_ref[g * 8 + pair * 2 + half]
            m = jnp.maximum(jnp.max(s, axis=-1, keepdims=True), sk)
            e = jnp.exp(s - m)
            p = e / (jnp.sum(e, axis=-1, keepdims=True) + jnp.exp(sk - m))
            out = out + _dot(p.astype(BF16), v_halves[half])
        o_ref[:, pair * LANES:(pair + 1) * LANES] = out


def _swa_attention(qk, v, sinks, bsz, seq, k_col0, v_col0):
    n = bsz * seq
    nb = seq // TQ
    kb0 = k_col0 // LANES
    vb0 = v_col0 // LANES
    return pl.pallas_call(
        _swa_kernel,
        grid=(bsz, nb, A_KV_HEADS),
        in_specs=[pl.BlockSpec(memory_space=pltpu.SMEM),
                  pl.BlockSpec((TQ, 4 * LANES), lambda b, i, g: (b * nb + i, g)),
                  pl.BlockSpec((TQ, LANES), lambda b, i, g: (b * nb + jnp.maximum(i - 1, 0), kb0 + g)),
                  pl.BlockSpec((TQ, LANES), lambda b, i, g: (b * nb + i, kb0 + g)),
                  pl.BlockSpec((TQ, LANES), lambda b, i, g: (b * nb + jnp.maximum(i - 1, 0), vb0 + g)),
                  pl.BlockSpec((TQ, LANES), lambda b, i, g: (b * nb + i, vb0 + g))],
        out_specs=pl.BlockSpec((TQ, 4 * LANES), lambda b, i, g: (b * nb + i, g)),
        out_shape=jax.ShapeDtypeStruct((n, A_HEADS * A_HEAD_DIM), F32),
        compiler_params=_cparams(("parallel", "parallel", "parallel")),
        name="swa",
    )(sinks, qk, qk, qk, v, v)


def _compress_kernel(t_ref, pe_ref, w1_ref, w2_ref, o_ref):
    nch = o_ref.shape[0]
    u = jnp.zeros((nch, CMP_HIDDEN), F32)
    v = jnp.zeros((nch, CMP_HIDDEN), F32)
    for l in range(CMP_STRIDE):
        t_l = t_ref[pl.ds(l, nch, stride=CMP_STRIDE), :]
        a = (t_l + pe_ref[l:l + 1, :]).astype(BF16)
        b = (t_l + pe_ref[CMP_STRIDE + l:CMP_STRIDE + l + 1, :]).astype(BF16)
        u = u + _dot(a, w1_ref[l * LANES:(l + 1) * LANES, :])
        v = v + _dot(b, w1_ref[(CMP_STRIDE + l) * LANES:(CMP_STRIDE + l + 1) * LANES, :])
    pre = u + pltpu.roll(v, nch - 1, 0)
    h = jax.nn.gelu(pre)
    o_ref[...] = _dot(h.astype(BF16), w2_ref[...]).astype(o_ref.dtype)


def _compress(t, col0, pe, w1, w2, bsz, seq):
    nch = seq // CMP_STRIDE
    cb0 = col0 // LANES
    return pl.pallas_call(
        _compress_kernel,
        grid=(bsz, B_KV_HEADS),
        in_specs=[pl.BlockSpec((seq, LANES), lambda b, g: (b, cb0 + g)),
                  pl.BlockSpec((CMP_LEN, LANES), lambda b, g: (0, 0)),
                  pl.BlockSpec((CMP_LEN * LANES, CMP_HIDDEN), lambda b, g: (0, 0)),
                  pl.BlockSpec((CMP_HIDDEN, LANES), lambda b, g: (0, 0))],
        out_specs=pl.BlockSpec((None, None, nch, LANES), lambda b, g: (b, g, 0, 0)),
        out_shape=jax.ShapeDtypeStruct((bsz, B_KV_HEADS, nch, LANES), BF16),
        compiler_params=_cparams(("parallel", "parallel")),
        name="nsa_compress",
    )(t, pe, w1, w2)


def _nsa_kernel(q_ref, gate_ref, kcmp_ref, vcmp_ref, ks_ref, vs_ref, kw_ref, vw_ref, o_ref, *, seq):
    qi = pl.program_id(2)
    rep = B_HEADS // B_KV_HEADS
    rows = rep * TQ
    scale = B_HEAD_DIM ** -0.5
    nch = seq // CMP_STRIDE
    ns = seq // SLC_LEN
    n_sel = min(SLC_TOPK, ns)
    q_all = jnp.concatenate([q_ref[:, r * LANES:(r + 1) * LANES] for r in range(rep)], axis=0)
    tpos = qi * TQ + lax.broadcasted_iota(jnp.int32, (TQ, 1), 0)

    s = (_dot_nt(q_all, kcmp_ref[...]) * scale).reshape(rep, TQ, nch)
    c_end = lax.broadcasted_iota(jnp.int32, (1, nch), 1) * CMP_STRIDE + (CMP_LEN - 1)
    valid = (c_end <= tpos)[None]
    s = jnp.where(valid, s, NEG_INF)
    m = jnp.max(s, axis=-1, keepdims=True)
    e = jnp.where(valid, jnp.exp(s - m), 0.0)
    den = jnp.sum(e, axis=-1, keepdims=True)
    p = e / jnp.where(den > 0, den, 1.0)
    o_cmp = _dot(p.reshape(rows, nch).astype(BF16), vcmp_ref[...])
    p_grp = jnp.sum(p, axis=0)

    ci = lax.broadcasted_iota(jnp.int32, (nch, ns), 0)
    ji = lax.broadcasted_iota(jnp.int32, (nch, ns), 1)
    d = ci - (SLC_LEN // CMP_STRIDE) * ji
    msel = jnp.where((d == -1) | (d == 3), 1.0, jnp.where((d >= 0) & (d <= 2), 2.0, 0.0)).astype(BF16)
    p_hi = p_grp.astype(BF16)
    r1 = p_grp - p_hi.astype(F32)
    p_mid = r1.astype(BF16)
    p_lo = (r1 - p_mid.astype(F32)).astype(BF16)
    p_slc = _dot(p_hi, msel) + _dot(p_mid, msel) + _dot(p_lo, msel)

    jb = lax.broadcasted_iota(jnp.int32, (1, ns), 1)
    cur = tpos // SLC_LEN
    forced = (jb == 0) | (jb == cur) | (jb == cur - 1)
    score = jnp.where(forced, 1e9, jnp.where(jb * SLC_LEN <= tpos, p_slc, NEG_INF))
    rank = jnp.zeros((TQ, ns), jnp.int32)
    for i in range(ns):
        si = score[:, i:i + 1]
        rank = rank + ((si > score) | ((si == score) & (i < jb))).astype(jnp.int32)
    sel = (rank < n_sel).astype(BF16)

    tk = 4 * SLC_LEN
    jcol = lax.broadcasted_iota(jnp.int32, (ns, 1), 0)

    def slc_body(kt, carry):
        m_i, l_i, acc = carry
        row0 = pl.multiple_of(kt * tk, tk)
        k_t = ks_ref[pl.ds(row0, tk), :]
        v_t = vs_ref[pl.ds(row0, tk), :]
        sc = (_dot_nt(q_all, k_t) * scale).reshape(rep, TQ, tk)
        kpos = kt * tk + lax.broadcasted_iota(jnp.int32, (1, tk), 1)
        expand = (jcol == kpos // SLC_LEN).astype(BF16)
        keep = ((_dot(sel, expand) > 0.5) & (kpos <= tpos))[None]
        sc = jnp.where(keep, sc, NEG_INF)
        m_new = jnp.maximum(m_i, jnp.max(sc, axis=-1, keepdims=True))
        alpha = jnp.exp(m_i - m_new)
        ee = jnp.where(keep, jnp.exp(sc - m_new), 0.0)
        l_new = alpha * l_i + jnp.sum(ee, axis=-1, keepdims=True)
        acc = alpha.reshape(rows, 1) * acc + _dot(ee.reshape(rows, tk).astype(BF16), v_t)
        return m_new, l_new, acc

    n_kt = (qi * TQ + TQ + tk - 1) // tk
    m0 = jnp.full((rep, TQ, 1), NEG_INF, F32)
    l0 = jnp.zeros((rep, TQ, 1), F32)
    a0 = jnp.zeros((rows, LANES), F32)
    _, l_f, acc_f = lax.fori_loop(0, n_kt, slc_body, (m0, l0, a0))
    o_slc = acc_f / l_f.reshape(rows, 1)

    nprev = -(-(WIN_LEN - 1) // TQ)
    k_parts, v_parts = [], []
    for i in range(nprev + 1):
        row0 = pl.multiple_of(jnp.maximum(qi - nprev + i, 0) * TQ, TQ)
        k_parts.append(kw_ref[pl.ds(row0, TQ), :])
        v_parts.append(vw_ref[pl.ds(row0, TQ), :])
    kwin = jnp.concatenate(k_parts, axis=0)
    vwin = jnp.concatenate(v_parts, axis=0)
    nkw = (nprev + 1) * TQ
    sw = (_dot_nt(q_all, kwin) * scale).reshape(rep, TQ, nkw)
    kposw = (qi - nprev) * TQ + lax.broadcasted_iota(jnp.int32, (1, nkw), 1)
    dlt = tpos - kposw
    keepw = ((kposw >= 0) & (dlt >= 0) & (dlt < WIN_LEN))[None]
    sw = jnp.where(keepw, sw, NEG_INF)
    mw = jnp.max(sw, axis=-1, keepdims=True)
    ew = jnp.exp(sw - mw)
    pw = ew / jnp.sum(ew, axis=-1, keepdims=True)
    o_win = _dot(pw.reshape(rows, nkw).astype(BF16), vwin)

    gate = gate_ref[...]
    for r in range(rep):
        sl = slice(r * TQ, (r + 1) * TQ)
        o_ref[:, r * LANES:(r + 1) * LANES] = (gate[:, r:r + 1] * o_cmp[sl]
                                               + gate[:, rep + r:rep + r + 1] * o_slc[sl]
                                               + gate[:, 2 * rep + r:2 * rep + r + 1] * o_win[sl])


def _nsa_attention(qk, v, gates, kcmp, vcmp, bsz, seq, cols):
    n = bsz * seq
    nq = seq // TQ
    nch = seq // CMP_STRIDE
    rep = B_HEADS // B_KV_HEADS
    qb0 = cols["q"] // (rep * LANES)
    ksb, kwb, vsb, vwb, gb0 = (cols[k] // LANES for k in ("ks", "kw", "vs", "vw", "gate"))
    return pl.pallas_call(
        functools.partial(_nsa_kernel, seq=seq),
        grid=(bsz, B_KV_HEADS, nq),
        in_specs=[pl.BlockSpec((TQ, rep * LANES), lambda b, g, i: (b * nq + i, qb0 + g)),
                  pl.BlockSpec((TQ, LANES), lambda b, g, i: (b * nq + i, gb0 + g)),
                  pl.BlockSpec((None, None, nch, LANES), lambda b, g, i: (b, g, 0, 0)),
                  pl.BlockSpec((None, None, nch, LANES), lambda b, g, i: (b, g, 0, 0)),
                  pl.BlockSpec((seq, LANES), lambda b, g, i: (b, ksb + g)),
                  pl.BlockSpec((seq, LANES), lambda b, g, i: (b, vsb + g)),
                  pl.BlockSpec((seq, LANES), lambda b, g, i: (b, kwb + g)),
                  pl.BlockSpec((seq, LANES), lambda b, g, i: (b, vwb + g))],
        out_specs=pl.BlockSpec((TQ, rep * LANES), lambda b, g, i: (b * nq + i, g)),
        out_shape=jax.ShapeDtypeStruct((n, B_HEADS * B_HEAD_DIM), F32),
        compiler_params=_cparams(("parallel", "parallel", "arbitrary")),
        name="nsa",
    )(qk, gates, kcmp, vcmp, qk, v, qk, v)


def _oproj_kernel(oa_ref, ob_ref, ga_ref, gb_ref, wo_ref, x_ref, g_ref, b_ref, wr_ref,
                  x1_ref, sc_ref, acc_ref, *, alpha):
    k = pl.program_id(1)

    @pl.when(k == 0)
    def _():
        acc_ref[...] = jnp.zeros_like(acc_ref)

    y = (ga_ref[...] * oa_ref[...] + gb_ref[...] * ob_ref[...]).astype(BF16)
    acc_ref[...] += _dot(y, wo_ref[...])

    @pl.when(k == pl.num_programs(1) - 1)
    def _():
        x1 = _layer_norm(alpha * x_ref[...] + acc_ref[...], g_ref[...], b_ref[...])
        x1_ref[...] = x1
        sc_ref[...] = _sigmoid(_dot(x1.astype(BF16), wr_ref[...]))


def _out_proj(o_a, o_b, gates, ga_col0, gb_col0, w_o, x, ln_g, ln_b, w_r, alpha, tm=256, tk=512):
    n, d = x.shape
    tm = min(tm, n)
    ga0, gb0 = ga_col0 // tk, gb_col0 // tk
    return pl.pallas_call(
        functools.partial(_oproj_kernel, alpha=alpha),
        grid=(n // tm, d // tk),
        in_specs=[pl.BlockSpec((tm, tk), lambda i, k: (i, k)),
                  pl.BlockSpec((tm, tk), lambda i, k: (i, k)),
                  pl.BlockSpec((tm, tk), lambda i, k: (i, ga0 + k)),
                  pl.BlockSpec((tm, tk), lambda i, k: (i, gb0 + k)),
                  pl.BlockSpec((tk, d), lambda i, k: (k, 0)),
                  pl.BlockSpec((tm, d), lambda i, k: (i, 0)),
                  pl.BlockSpec((1, d), lambda i, k: (0, 0)),
                  pl.BlockSpec((1, d), lambda i, k: (0, 0)),
                  pl.BlockSpec((d, LANES), lambda i, k: (0, 0))],
        out_specs=[pl.BlockSpec((tm, d), lambda i, k: (i, 0)),
                   pl.BlockSpec((tm, LANES), lambda i, k: (i, 0))],
        out_shape=[jax.ShapeDtypeStruct((n, d), F32), jax.ShapeDtypeStruct((n, LANES), F32)],
        scratch_shapes=[pltpu.VMEM((tm, d), F32)],
        compiler_params=_cparams(("parallel", "arbitrary")),
        name="oproj_ln1_router",
    )(o_a, o_b, gates, gates, w_o, x, ln_g, ln_b, w_r)


def _route_kernel(sc_ref, bias_ref, ek_ref, rk_ref, gk_ref, cnt_ref, run_ref):
    i = pl.program_id(0)
    tm = sc_ref.shape[0]

    @pl.when(i == 0)
    def _():
        run_ref[...] = jnp.zeros_like(run_ref)

    sc = sc_ref[...]
    lane = lax.broadcasted_iota(jnp.int32, (1, LANES), 1)
    is_e = lane < N_EXPERTS
    biased = jnp.where(is_e, sc + bias_ref[...], -jnp.inf)
    rank = jnp.zeros((tm, LANES), jnp.int32)
    for e in range(N_EXPERTS):
        be = biased[:, e:e + 1]
        rank = rank + ((be > biased) | ((be == biased) & (e < lane))).astype(jnp.int32)
    sel = (rank < TOP_K) & is_e
    sv = jnp.where(sel, sc, 0.0)
    gates = sv / jnp.sum(sv, axis=-1, keepdims=True) * ROUTED_SCALE
    selb = sel.astype(BF16)
    ri = lax.broadcasted_iota(jnp.int32, (tm, tm), 0)
    ci = lax.broadcasted_iota(jnp.int32, (tm, tm), 1)
    cum = _dot((ci < ri).astype(BF16), selb) + run_ref[...]
    run_ref[...] += jnp.sum(sel.astype(F32), axis=0, keepdims=True)
    cnt_ref[...] = run_ref[...]
    ai = lax.broadcasted_iota(jnp.int32, (LANES, LANES), 0)
    bi = lax.broadcasted_iota(jnp.int32, (LANES, LANES), 1)
    before = _dot(selb, (ai < bi).astype(BF16))
    lane_f = lane.astype(F32)
    ek = jnp.zeros((tm, LANES), F32)
    rk = jnp.zeros((tm, LANES), F32)
    gk = jnp.zeros((tm, LANES), F32)
    for k in range(TOP_K):
        oh = sel & (before == k)
        ek = jnp.where(lane == k, jnp.sum(jnp.where(oh, lane_f, 0.0), axis=-1, keepdims=True), ek)
        rk = jnp.where(lane == k, jnp.sum(jnp.where(oh, cum, 0.0), axis=-1, keepdims=True), rk)
        gk = jnp.where(lane == k, jnp.sum(jnp.where(oh, gates, 0.0), axis=-1, keepdims=True), gk)
    ek_ref[...] = ek.astype(jnp.int32)
    rk_ref[...] = rk.astype(jnp.int32)
    gk_ref[...] = gk


def _route(scores, bias, tm=256):
    n = scores.shape[0]
    tm = min(tm, n)
    blk = pl.BlockSpec((tm, LANES), lambda i: (i, 0))
    one = pl.BlockSpec((1, LANES), lambda i: (0, 0))
    return pl.pallas_call(
        _route_kernel,
        grid=(n // tm,),
        in_specs=[blk, one],
        out_specs=[blk, blk, blk, one],
        out_shape=[jax.ShapeDtypeStruct((n, LANES), jnp.int32), jax.ShapeDtypeStruct((n, LANES), jnp.int32),
                   jax.ShapeDtypeStruct((n, LANES), F32), jax.ShapeDtypeStruct((1, LANES), F32)],
        scratch_shapes=[pltpu.VMEM((1, LANES), F32)],
        compiler_params=_cparams(("arbitrary",)),
        name="route",
    )(scores, bias)


def _row_copy(src, s_row, dst, d_row, sem):
    return pltpu.make_async_copy(src.at[pl.ds(s_row, 1)], dst.at[pl.ds(d_row, 1)], sem)


def _scatter_kernel(cnt_ref, pad_ref, pst_ref, nu_ref, dest_ref, x_ref, xs_ref, zrow_ref, zblk_ref, sem,
                    *, tm, n_tok_steps, n_blocks):
    i = pl.program_id(0)

    @pl.when(i < n_tok_steps)
    def _():
        def issue(t, c):
            for k in range(TOP_K):
                _row_copy(x_ref, i * tm + t, xs_ref, dest_ref[t * TOP_K + k], sem).start()
            return c

        lax.fori_loop(0, tm, issue, 0)

        def drain(t, c):
            for k in range(TOP_K):
                _row_copy(x_ref, 0, xs_ref, 0, sem).wait()
            return c

        lax.fori_loop(0, tm, drain, 0)

    @pl.when((i >= n_tok_steps) & (i < n_tok_steps + N_EXPERTS))
    def _():
        e = i - n_tok_steps
        zrow_ref[...] = jnp.zeros_like(zrow_ref)
        lo, hi, base = cnt_ref[e], pad_ref[e], pst_ref[e]

        def issue(r, c):
            pltpu.make_async_copy(zrow_ref, xs_ref.at[pl.ds(base + r, 1)], sem).start()
            return c

        lax.fori_loop(lo, hi, issue, 0)

        def drain(r, c):
            pltpu.make_async_copy(zrow_ref, xs_ref.at[pl.ds(0, 1)], sem).wait()
            return c

        lax.fori_loop(lo, hi, drain, 0)

    @pl.when(i == n_tok_steps + N_EXPERTS)
    def _():
        zblk_ref[...] = jnp.zeros_like(zblk_ref)

        def issue(p, c):
            pltpu.make_async_copy(zblk_ref, xs_ref.at[pl.ds(p * MOE_BLOCK, MOE_BLOCK)], sem).start()
            return c

        lax.fori_loop(nu_ref[0], n_blocks, issue, 0)

        def drain(p, c):
            pltpu.make_async_copy(zblk_ref, xs_ref.at[pl.ds(0, MOE_BLOCK)], sem).wait()
            return c

        lax.fori_loop(nu_ref[0], n_blocks, drain, 0)


def _scatter_rows(x1, dest_flat, counts, padded, pstarts, n_used, n_blocks, tm=128):
    n, d = x1.shape
    tm = min(tm, n)
    n_tok_steps = n // tm
    return pl.pallas_call(
        functools.partial(_scatter_kernel, tm=tm, n_tok_steps=n_tok_steps, n_blocks=n_blocks),
        grid_spec=pltpu.PrefetchScalarGridSpec(
            num_scalar_prefetch=4,
            grid=(n_tok_steps + N_EXPERTS + 1,),
            in_specs=[pl.BlockSpec((tm * TOP_K,), lambda i, *_: (jnp.minimum(i, n_tok_steps - 1),),
                                   memory_space=pltpu.SMEM),
                      pl.BlockSpec(memory_space=pl.ANY)],
            out_specs=pl.BlockSpec(memory_space=pl.ANY),
            scratch_shapes=[pltpu.VMEM((1, d), x1.dtype), pltpu.VMEM((MOE_BLOCK, d), x1.dtype),
                            pltpu.SemaphoreType.DMA(())]),
        out_shape=jax.ShapeDtypeStruct((n_blocks * MOE_BLOCK, d), x1.dtype),
        compiler_params=_cparams(("arbitrary",)),
        name="moe_scatter",
    )(counts, padded, pstarts, n_used, dest_flat, x1)


def _expert_kernel(be_ref, nu_ref, xs_ref, wg_ref, wu_ref, wd_ref, y_ref):
    del be_ref
    used = pl.program_id(0) < nu_ref[0]

    @pl.when(used)
    def _():
        xb = xs_ref[...].astype(BF16)
        hg = _dot(xb, wg_ref[...])
        hu = _dot(xb, wu_ref[...])
        h = (hg * _sigmoid(hg)) * hu
        y_ref[...] = _dot(h.astype(BF16), wd_ref[...])

    @pl.when(jnp.logical_not(used))
    def _():
        y_ref[...] = jnp.zeros_like(y_ref)


def _expert_mlp(xs, block_e, n_used, wg, wu, wd):
    n_rows, d = xs.shape
    nblk = n_rows // MOE_BLOCK
    ff = wg.shape[-1]

    def row_map(p, be, nu):
        return (jnp.minimum(p, nu[0] - 1), 0)

    return pl.pallas_call(
        _expert_kernel,
        grid_spec=pltpu.PrefetchScalarGridSpec(
            num_scalar_prefetch=2,
            grid=(nblk,),
            in_specs=[pl.BlockSpec((MOE_BLOCK, d), row_map),
                      pl.BlockSpec((None, d, ff), lambda p, be, nu: (be[p], 0, 0)),
                      pl.BlockSpec((None, d, ff), lambda p, be, nu: (be[p], 0, 0)),
                      pl.BlockSpec((None, ff, d), lambda p, be, nu: (be[p], 0, 0))],
            out_specs=pl.BlockSpec((MOE_BLOCK, d), lambda p, be, nu: (p, 0))),
        out_shape=jax.ShapeDtypeStruct((n_rows, d), F32),
        compiler_params=_cparams(("arbitrary",)),
        name="moe_experts",
    )(block_e, n_used, xs, wg, wu, wd)


def _combine_kernel(dest_ref, x1_ref, gk_ref, sg_ref, su_ref, sd_ref, g_ref, b_ref, ys_ref,
                    o_ref, ybuf_ref, sem, *, tm, alpha):
    def issue(t, c):
        for k in range(TOP_K):
            _row_copy(ys_ref, dest_ref[t * TOP_K + k], ybuf_ref, k * tm + t, sem).start()
        return c

    lax.fori_loop(0, tm, issue, 0)

    x1 = x1_ref[...]
    xb = x1.astype(BF16)
    hg = _dot(xb, sg_ref[...])
    hu = _dot(xb, su_ref[...])
    shared = _dot(((hg * _sigmoid(hg)) * hu).astype(BF16), sd_ref[...])

    def drain(t, c):
        for k in range(TOP_K):
            _row_copy(ys_ref, 0, ybuf_ref, 0, sem).wait()
        return c

    lax.fori_loop(0, tm, drain, 0)

    gk = gk_ref[...]
    routed = jnp.zeros_like(x1)
    for k in range(TOP_K):
        routed = routed + gk[:, k:k + 1] * ybuf_ref[k * tm:(k + 1) * tm, :]
    o_ref[...] = _layer_norm(alpha * x1 + (routed + shared), g_ref[...], b_ref[...])


def _combine(x1, dest_flat, gk, ys, sg, su, sd, ln_g, ln_b, alpha, tm=128):
    n, d = x1.shape
    tm = min(tm, n)
    ff = sg.shape[-1]
    return pl.pallas_call(
        functools.partial(_combine_kernel, tm=tm, alpha=alpha),
        grid=(n // tm,),
        in_specs=[pl.BlockSpec((tm * TOP_K,), lambda i: (i,), memory_space=pltpu.SMEM),
                  pl.BlockSpec((tm, d), lambda i: (i, 0)),
                  pl.BlockSpec((tm, LANES), lambda i: (i, 0)),
                  pl.BlockSpec((d, ff), lambda i: (0, 0)),
                  pl.BlockSpec((d, ff), lambda i: (0, 0)),
                  pl.BlockSpec((ff, d), lambda i: (0, 0)),
                  pl.BlockSpec((1, d), lambda i: (0, 0)),
                  pl.BlockSpec((1, d), lambda i: (0, 0)),
                  pl.BlockSpec(memory_space=pl.ANY)],
        out_specs=pl.BlockSpec((tm, d), lambda i: (i, 0)),
        out_shape=jax.ShapeDtypeStruct((n, d), F32),
        scratch_shapes=[pltpu.VMEM((TOP_K * tm, d), F32), pltpu.SemaphoreType.DMA(())],
        compiler_params=_cparams(("arbitrary",)),
        name="moe_combine_ln2",
    )(dest_flat, x1, gk, sg, su, sd, ln_g, ln_b, ys)


def _rope_tables(positions, dh):
    half = dh // 2
    inv = ROPE_THETA ** (-jnp.arange(half, dtype=F32) * 2.0 / dh)
    ang = positions.reshape(-1).astype(F32)[:, None] * inv
    cos, sin = jnp.cos(ang), jnp.sin(ang)
    reps = LANES // dh
    return (jnp.tile(jnp.concatenate([cos, cos], -1), (1, reps)),
            jnp.tile(jnp.concatenate([-sin, sin], -1), (1, reps)))


def _dup_heads(w, heads, dh):
    d = w.shape[0]
    w3 = w.reshape(d, heads, dh)
    return jnp.concatenate([w3, w3], axis=-1).reshape(d, heads * 2 * dh)


def _hybrid_layer(x, positions, w_in, a_sinks, cmp_k_pos, cmp_k_w1, cmp_k_w2, cmp_v_pos, cmp_v_w1,
                  cmp_v_w2, w_o, ln1_g, ln1_b, w_router, router_bias, exp_w_gate, exp_w_up, exp_w_down,
                  sh_w_gate, sh_w_up, sh_w_down, ln2_g, ln2_b, alpha):
    bsz, seq, d = x.shape
    n = bsz * seq
    assert seq % (4 * SLC_LEN) == 0 and A_WINDOW <= TQ
    xf = x.reshape(n, d)
    xb = xf.astype(BF16)

    qa_w, kva = A_HEADS * A_HEAD_DIM, A_KV_HEADS * A_HEAD_DIM
    qb_w, kvb = B_HEADS * B_HEAD_DIM, B_KV_HEADS * B_HEAD_DIM
    widths = [qa_w, kva, kva, qb_w, kvb, kvb, kvb, kvb, kvb, kvb, 3 * B_HEADS, d, d]
    offs = np.concatenate([[0], np.cumsum(widths)]).tolist()
    col = lambda i: w_in[:, offs[i]:offs[i + 1]]
    w_qa, w_ka, w_va, w_qb, w_kc, w_vc, w_ks, w_vs, w_kw, w_vw, w_gn, w_ga, w_gb = (col(i) for i in range(13))
    rep = B_HEADS // B_KV_HEADS
    gidx = np.zeros((B_KV_HEADS, LANES), np.int32)
    gmask = np.zeros((B_KV_HEADS, LANES), np.float32)
    for g in range(B_KV_HEADS):
        for br in range(3):
            for r in range(rep):
                gidx[g, br * rep + r] = br * B_HEADS + g * rep + r
                gmask[g, br * rep + r] = 1.0
    w_gn_x = w_gn[:, gidx.reshape(-1)] * jnp.asarray(gmask.reshape(-1))

    w_r64 = jnp.concatenate([w_qa, _dup_heads(w_ka, A_KV_HEADS, A_HEAD_DIM)], axis=1).astype(BF16)
    w_vals = jnp.concatenate([_dup_heads(w_va, A_KV_HEADS, A_HEAD_DIM), w_vs, w_vw], axis=1).astype(BF16)
    w_r128 = jnp.concatenate([w_qb, w_ks, w_kw], axis=1).astype(BF16)
    w_sig = jnp.concatenate([w_gn_x, w_ga, w_gb], axis=1).astype(BF16)

    cos64, sin64 = _rope_tables(positions, A_HEAD_DIM)
    cos128, sin128 = _rope_tables(positions, B_HEAD_DIM)

    qk_a = _project(xb, w_r64, cos64, sin64, "rope64", BF16)
    vals = _project(xb, w_vals, cos64, sin64, "none", BF16)
    qk_b = _project(xb, w_r128, cos128, sin128, "rope128", BF16)
    kc = _project(xb, w_kc.astype(BF16), cos128, sin128, "rope128", F32)
    vc = _project(xb, w_vc.astype(BF16), cos128, sin128, "none", F32)
    gates = _project(xb, w_sig, cos128, sin128, "sigmoid", F32)

    o_a = _swa_attention(qk_a, vals, a_sinks, bsz, seq, k_col0=qa_w, v_col0=0)

    kcmp = _compress(kc, 0, cmp_k_pos, cmp_k_w1.astype(BF16), cmp_k_w2.astype(BF16), bsz, seq)
    vcmp = _compress(vc, 0, cmp_v_pos, cmp_v_w1.astype(BF16), cmp_v_w2.astype(BF16), bsz, seq)
    nsa_cols = dict(q=0, ks=qb_w, kw=qb_w + kvb, vs=2 * kva, vw=2 * kva + kvb, gate=0)
    o_b = _nsa_attention(qk_b, vals, gates, kcmp, vcmp, bsz, seq, nsa_cols)

    w_r = jnp.pad(w_router, ((0, 0), (0, LANES - N_EXPERTS))).astype(BF16)
    x1, scores = _out_proj(o_a, o_b, gates, B_KV_HEADS * LANES, B_KV_HEADS * LANES + d, w_o.astype(BF16), xf,
                           ln1_g.reshape(1, d), ln1_b.reshape(1, d), w_r, alpha)

    bias = jnp.pad(router_bias.astype(F32), (0, LANES - N_EXPERTS)).reshape(1, LANES)
    ek, rk, gk, cnt = _route(scores, bias)

    counts = cnt[0, :N_EXPERTS].astype(jnp.int32)
    padded = (counts + MOE_BLOCK - 1) // MOE_BLOCK * MOE_BLOCK
    pends = jnp.cumsum(padded)
    pstarts = pends - padded
    nblk = -(-(n * TOP_K) // MOE_BLOCK) + N_EXPERTS
    block_e = jnp.minimum(jnp.searchsorted(pends, jnp.arange(nblk, dtype=jnp.int32) * MOE_BLOCK, side="right"),
                          N_EXPERTS - 1).astype(jnp.int32)
    n_used = (pends[-1:] // MOE_BLOCK).astype(jnp.int32)
    dest = (pstarts[ek[:, :TOP_K]] + rk[:, :TOP_K]).astype(jnp.int32).reshape(-1)

    xs = _scatter_rows(x1, dest, counts, padded, pstarts.astype(jnp.int32), n_used, nblk)
    ys = _expert_mlp(xs, block_e, n_used, exp_w_gate.astype(BF16), exp_w_up.astype(BF16),
                     exp_w_down.astype(BF16))
    out = _combine(x1, dest, gk, ys, sh_w_gate.astype(BF16), sh_w_up.astype(BF16), sh_w_down.astype(BF16),
                   ln2_g.reshape(1, d), ln2_b.reshape(1, d), alpha)
    return out.reshape(bsz, seq, d)


def kernel(x, positions, w_in, a_sinks, cmp_k_pos, cmp_k_w1, cmp_k_w2, cmp_v_pos, cmp_v_w1, cmp_v_w2,
           w_o, ln1_g, ln1_b, w_router, router_bias, exp_w_gate, exp_w_up, exp_w_down,
           sh_w_gate, sh_w_up, sh_w_down, ln2_g, ln2_b):
    depth = w_in.shape[0]
    alpha = (2.0 * depth) ** 0.25
    for l in range(depth):
        x = _hybrid_layer(x, positions, w_in[l], a_sinks[l], cmp_k_pos[l], cmp_k_w1[l], cmp_k_w2[l],
                          cmp_v_pos[l], cmp_v_w1[l], cmp_v_w2[l], w_o[l], ln1_g[l], ln1_b[l],
                          w_router[l], router_bias[l], exp_w_gate[l], exp_w_up[l], exp_w_down[l],
                          sh_w_gate[l], sh_w_up[l], sh_w_down[l], ln2_g[l], ln2_b[l], alpha)
    return x
```

```python
import functools

import jax
import jax.numpy as jnp
import numpy as np
from jax import lax
from jax.experimental import pallas as pl
from jax.experimental.pallas import tpu as pltpu

F32 = jnp.float32
BF16 = jnp.bfloat16

ROPE_THETA = 10000.0
LN_EPS = 1e-5
NEG_INF = -1e30

A_HEADS, A_KV_HEADS, A_HEAD_DIM, A_WINDOW = 64, 8, 64, 128
B_HEADS, B_KV_HEADS, B_HEAD_DIM = 32, 4, 128
CMP_LEN, CMP_STRIDE, CMP_HIDDEN = 32, 16, 256
SLC_LEN, SLC_TOPK, WIN_LEN = 64, 16, 512
N_EXPERTS, TOP_K, EXPERT_FF, SHARED_FF = 64, 8, 512, 512
ROUTED_SCALE = 2.5
MOE_BLOCK = 256

LOG2E = 1.4426950408889634
LANES = 128
TQ = 128
VMEM_LIMIT = 56 * 1024 * 1024


def _cparams(sem):
    return pltpu.CompilerParams(dimension_semantics=sem, vmem_limit_bytes=VMEM_LIMIT)


def _dot(a, b):
    return jnp.dot(a, b, preferred_element_type=F32)


def _dot_nt(a, b):
    return lax.dot_general(a, b, (((1,), (1,)), ((), ())), preferred_element_type=F32)


def _sigmoid(v):
    return 1.0 / (1.0 + jnp.exp(-v))


def _layer_norm(z, g, b):
    mu = jnp.mean(z, axis=-1, keepdims=True)
    zc = z - mu
    var = jnp.mean(zc * zc, axis=-1, keepdims=True)
    return zc * lax.rsqrt(var + LN_EPS) * g + b


def _proj_kernel(x_ref, w_ref, cos_ref, sin_ref, o_ref, *, epi, q_blocks, q_scale):
    acc = _dot(x_ref[...], w_ref[...])
    if epi == "none":
        o_ref[...] = acc.astype(o_ref.dtype)
    elif epi == "sigmoid":
        o_ref[...] = _sigmoid(acc).astype(o_ref.dtype)
    else:
        cs = jnp.where(pl.program_id(0) < q_blocks, q_scale, 1.0)
        cos = cos_ref[...] * cs
        sin = sin_ref[...] * cs
        lane = lax.broadcasted_iota(jnp.int32, (1, LANES), 1)
        for c in range(acc.shape[1] // LANES):
            y = acc[:, c * LANES:(c + 1) * LANES]
            if epi == "rope128":
                rot = pltpu.roll(y, 64, 1)
            else:
                rot = jnp.where((lane % 64) < 32, pltpu.roll(y, 96, 1), pltpu.roll(y, 32, 1))
            o_ref[:, c * LANES:(c + 1) * LANES] = (y * cos + rot * sin).astype(o_ref.dtype)


def _project(xb, w, cos, sin, epi, out_dtype, q_cols=0, q_scale=1.0, tm=512, tn=512):
    n, d = xb.shape
    ncol = w.shape[1]
    tm = min(tm, n)
    assert n % tm == 0 and ncol % tn == 0 and q_cols % tn == 0
    return pl.pallas_call(
        functools.partial(_proj_kernel, epi=epi, q_blocks=q_cols // tn, q_scale=q_scale),
        grid=(ncol // tn, n // tm),
        in_specs=[pl.BlockSpec((tm, d), lambda j, i: (i, 0)),
                  pl.BlockSpec((d, tn), lambda j, i: (0, j)),
                  pl.BlockSpec((tm, LANES), lambda j, i: (i, 0)),
                  pl.BlockSpec((tm, LANES), lambda j, i: (i, 0))],
        out_specs=pl.BlockSpec((tm, tn), lambda j, i: (i, j)),
        out_shape=jax.ShapeDtypeStruct((n, ncol), out_dtype),
        compiler_params=_cparams(("parallel", "parallel")),
        name="proj_" + epi,
    )(xb, w, cos, sin)


def _swa_kernel(sink_ref, q_ref, kp_ref, kc_ref, vp_ref, vc_ref, o_ref, s_ref, p_ref):
    n = pl.program_id(1)
    g = pl.program_id(2)
    kk = jnp.concatenate([kp_ref[...], kc_ref[...]], axis=0)
    vv = jnp.concatenate([vp_ref[...], vc_ref[...]], axis=0)
    lane = lax.broadcasted_iota(jnp.int32, (1, LANES), 1)
    lo = lane < A_HEAD_DIM
    zv = jnp.zeros_like(vv)
    v_halves = (jnp.where(lo, vv, zv), jnp.where(lo, zv, vv))
    nk = 2 * TQ
    qpos = n * TQ + lax.broadcasted_iota(jnp.int32, (TQ, nk), 0)
    kpos = (n - 1) * TQ + lax.broadcasted_iota(jnp.int32, (TQ, nk), 1)
    delta = qpos - kpos
    mask = (kpos >= 0) & (delta >= 0) & (delta < A_WINDOW)
    npair = q_ref.shape[1] // LANES
    zq = jnp.zeros((TQ, LANES), q_ref.dtype)
    q_lo = [jnp.where(lo, q_ref[:, c * LANES:(c + 1) * LANES], zq) for c in range(npair)]
    q_hi = [jnp.where(lo, zq, q_ref[:, c * LANES:(c + 1) * LANES]) for c in range(npair)]
    s_ref[...] = _dot_nt(jnp.concatenate(q_lo + q_hi, axis=0), kk)
    for half in range(2):
        for pair in range(npair):
            sl = slice((half * npair + pair) * TQ, (half * npair + pair + 1) * TQ)
            s = jnp.where(mask, s_ref[sl, :], NEG_INF)
            sk = sink_ref[g * (2 * npair) + pair * 2 + half] * LOG2E
            m = jnp.maximum(jnp.max(s, axis=-1, keepdims=True), sk)
            e = jnp.exp2(s - m)
            p = e / (jnp.sum(e, axis=-1, keepdims=True) + jnp.exp2(sk - m))
            p_ref[sl, :] = p.astype(BF16)
    half_rows = npair * TQ
    out = _dot(p_ref[:half_rows, :], v_halves[0]) + _dot(p_ref[half_rows:, :], v_halves[1])
    for pair in range(npair):
        o_ref[:, pair * LANES:(pair + 1) * LANES] = out[pair * TQ:(pair + 1) * TQ, :]


def _swa_attention(qk, v, sinks, bsz, seq, k_col0, v_col0):
    n = bsz * seq
    nb = seq // TQ
    kb0 = k_col0 // LANES
    vb0 = v_col0 // LANES
    return pl.pallas_call(
        _swa_kernel,
        grid=(bsz, nb, A_KV_HEADS),
        in_specs=[pl.BlockSpec(memory_space=pltpu.SMEM),
                  pl.BlockSpec((TQ, 4 * LANES), lambda b, i, g: (b * nb + i, g)),
                  pl.BlockSpec((TQ, LANES), lambda b, i, g: (b * nb + jnp.maximum(i - 1, 0), kb0 + g)),
                  pl.BlockSpec((TQ, LANES), lambda b, i, g: (b * nb + i, kb0 + g)),
                  pl.BlockSpec((TQ, LANES), lambda b, i, g: (b * nb + jnp.maximum(i - 1, 0), vb0 + g)),
                  pl.BlockSpec((TQ, LANES), lambda b, i, g: (b * nb + i, vb0 + g))],
        out_specs=pl.BlockSpec((TQ, 4 * LANES), lambda b, i, g: (b * nb + i, g)),
        out_shape=jax.ShapeDtypeStruct((n, A_HEADS * A_HEAD_DIM), F32),
        scratch_shapes=[pltpu.VMEM((8 * TQ, 2 * TQ), F32), pltpu.VMEM((8 * TQ, 2 * TQ), BF16)],
        compiler_params=_cparams(("parallel", "parallel", "parallel")),
        name="swa",
    )(sinks, qk, qk, qk, v, v)


def _compress_kernel(t_ref, pe_ref, w1_ref, w2_ref, o_ref):
    nch = o_ref.shape[0]
    u = jnp.zeros((nch, CMP_HIDDEN), F32)
    v = jnp.zeros((nch, CMP_HIDDEN), F32)
    for l in range(CMP_STRIDE):
        t_l = t_ref[pl.ds(l, nch, stride=CMP_STRIDE), :]
        a = (t_l + pe_ref[l:l + 1, :]).astype(BF16)
        b = (t_l + pe_ref[CMP_STRIDE + l:CMP_STRIDE + l + 1, :]).astype(BF16)
        u = u + _dot(a, w1_ref[l * LANES:(l + 1) * LANES, :])
        v = v + _dot(b, w1_ref[(CMP_STRIDE + l) * LANES:(CMP_STRIDE + l + 1) * LANES, :])
    pre = u + pltpu.roll(v, nch - 1, 0)
    h = jax.nn.gelu(pre)
    o_ref[...] = _dot(h.astype(BF16), w2_ref[...]).astype(o_ref.dtype)


def _compress(t, col0, pe, w1, w2, bsz, seq):
    nch = seq // CMP_STRIDE
    cb0 = col0 // LANES
    return pl.pallas_call(
        _compress_kernel,
        grid=(bsz, B_KV_HEADS),
        in_specs=[pl.BlockSpec((seq, LANES), lambda b, g: (b, cb0 + g)),
                  pl.BlockSpec((CMP_LEN, LANES), lambda b, g: (0, 0)),
                  pl.BlockSpec((CMP_LEN * LANES, CMP_HIDDEN), lambda b, g: (0, 0)),
                  pl.BlockSpec((CMP_HIDDEN, LANES), lambda b, g: (0, 0))],
        out_specs=pl.BlockSpec((None, None, nch, LANES), lambda b, g: (b, g, 0, 0)),
        out_shape=jax.ShapeDtypeStruct((bsz, B_KV_HEADS, nch, LANES), BF16),
        compiler_params=_cparams(("parallel", "parallel")),
        name="nsa_compress",
    )(t, pe, w1, w2)


NSA_TK = 512
NSA_NKW = (-(-(WIN_LEN - 1) // TQ) + 1) * TQ


def _nsa_kernel(q_ref, gate_ref, kcmp_ref, vcmp_ref, ks_ref, vs_ref, kw_ref, vw_ref, o_ref,
                qs_ref, s_ref, p_ref, m_ref, l_ref, acc_ref, ocmp_ref, *, seq):
    qi = pl.program_id(2)
    rep = B_HEADS // B_KV_HEADS
    nch = seq // CMP_STRIDE
    ns = seq // SLC_LEN
    n_sel = min(SLC_TOPK, ns)
    heads = [slice(r * TQ, (r + 1) * TQ) for r in range(rep)]
    for r in range(rep):
        qs_ref[heads[r], :] = q_ref[:, r * LANES:(r + 1) * LANES]
    tpos = qi * TQ + lax.broadcasted_iota(jnp.int32, (TQ, 1), 0)

    s_ref[:, :nch] = _dot_nt(qs_ref[...], kcmp_ref[...])
    c_end = lax.broadcasted_iota(jnp.int32, (1, nch), 1) * CMP_STRIDE + (CMP_LEN - 1)
    valid = c_end <= tpos
    p_grp = jnp.zeros((TQ, nch), F32)
    for sl in heads:
        s = jnp.where(valid, s_ref[sl, :nch], NEG_INF)
        m = jnp.max(s, axis=-1, keepdims=True)
        e = jnp.where(valid, jnp.exp2(s - m), 0.0)
        den = jnp.sum(e, axis=-1, keepdims=True)
        p = e / jnp.where(den > 0, den, 1.0)
        p_grp = p_grp + p
        p_ref[sl, :nch] = p.astype(BF16)
    ocmp_ref[...] = _dot(p_ref[:, :nch], vcmp_ref[...])

    jrow = lax.broadcasted_iota(jnp.int32, (ns, 1), 0)
    d = lax.broadcasted_iota(jnp.int32, (ns, nch), 1) - (SLC_LEN // CMP_STRIDE) * jrow
    msel = jnp.where((d == -1) | (d == 3), 1.0, jnp.where((d >= 0) & (d <= 2), 2.0, 0.0)).astype(BF16)
    p_hi = p_grp.astype(BF16)
    r1 = p_grp - p_hi.astype(F32)
    p_mid = r1.astype(BF16)
    p_lo = (r1 - p_mid.astype(F32)).astype(BF16)
    p_slc = _dot_nt(msel, p_hi) + _dot_nt(msel, p_mid) + _dot_nt(msel, p_lo)

    tpos_l = qi * TQ + lax.broadcasted_iota(jnp.int32, (1, TQ), 1)
    cur = tpos_l // SLC_LEN
    forced = (jrow == 0) | (jrow == cur) | (jrow == cur - 1)
    score = jnp.where(forced, 1e9, jnp.where(jrow * SLC_LEN <= tpos_l, p_slc, NEG_INF))
    rank = jnp.zeros((ns, TQ), jnp.int32)
    for i in range(ns):
        si = score[i:i + 1, :]
        rank = rank + ((si > score) | ((si == score) & (jrow > i))).astype(jnp.int32)
    sel_t = (rank < n_sel).astype(F32)
    if ns < LANES:
        sel_t = jnp.concatenate([sel_t, jnp.zeros((LANES - ns, TQ), F32)], axis=0)
    sel = sel_t.T.astype(BF16)

    m_ref[...] = jnp.full_like(m_ref, NEG_INF)
    l_ref[...] = jnp.zeros_like(l_ref)
    acc_ref[...] = jnp.zeros_like(acc_ref)
    jexp = lax.broadcasted_iota(jnp.int32, (LANES, 1), 0)

    def slc_body(kt, c):
        row0 = pl.multiple_of(kt * NSA_TK, NSA_TK)
        s_ref[:, :NSA_TK] = _dot_nt(qs_ref[...], ks_ref[pl.ds(row0, NSA_TK), :])
        kpos = kt * NSA_TK + lax.broadcasted_iota(jnp.int32, (1, NSA_TK), 1)
        expand = (jexp == kpos // SLC_LEN).astype(BF16)
        keep = (_dot(sel, expand) > 0.5) & (kpos <= tpos)
        for sl in heads:
            s = jnp.where(keep, s_ref[sl, :NSA_TK], NEG_INF)
            m_old = m_ref[sl, :]
            m_new = jnp.maximum(m_old, jnp.max(s, axis=-1, keepdims=True))
            alpha = jnp.exp2(m_old - m_new)
            e = jnp.exp2(s - m_new)
            l_ref[sl, :] = alpha * l_ref[sl, :] + jnp.sum(e, axis=-1, keepdims=True)
            m_ref[sl, :] = m_new
            acc_ref[sl, :] = alpha * acc_ref[sl, :]
            p_ref[sl, :NSA_TK] = e.astype(BF16)
        acc_ref[...] += _dot(p_ref[:, :NSA_TK], vs_ref[pl.ds(row0, NSA_TK), :])
        return c

    lax.fori_loop(0, (qi * TQ + TQ + NSA_TK - 1) // NSA_TK, slc_body, 0)

    nprev = NSA_NKW // TQ - 1
    k_parts, v_parts = [], []
    for i in range(nprev + 1):
        row0 = pl.multiple_of(jnp.maximum(qi - nprev + i, 0) * TQ, TQ)
        k_parts.append(kw_ref[pl.ds(row0, TQ), :])
        v_parts.append(vw_ref[pl.ds(row0, TQ), :])
    s_ref[:, :NSA_NKW] = _dot_nt(qs_ref[...], jnp.concatenate(k_parts, axis=0))
    kposw = (qi - nprev) * TQ + lax.broadcasted_iota(jnp.int32, (1, NSA_NKW), 1)
    dlt = tpos - kposw
    keepw = (kposw >= 0) & (dlt >= 0) & (dlt < WIN_LEN)
    for sl in heads:
        s = jnp.where(keepw, s_ref[sl, :NSA_NKW], NEG_INF)
        e = jnp.exp2(s - jnp.max(s, axis=-1, keepdims=True))
        p_ref[sl, :NSA_NKW] = (e / jnp.sum(e, axis=-1, keepdims=True)).astype(BF16)
    o_win = _dot(p_ref[:, :NSA_NKW], jnp.concatenate(v_parts, axis=0))

    gate = gate_ref[...]
    for r, sl in enumerate(heads):
        o_ref[:, r * LANES:(r + 1) * LANES] = (gate[:, r:r + 1] * ocmp_ref[sl, :]
                                               + gate[:, rep + r:rep + r + 1] * (acc_ref[sl, :] / l_ref[sl, :])
                                               + gate[:, 2 * rep + r:2 * rep + r + 1] * o_win[sl])


def _nsa_attention(qk, v, gates, kcmp, vcmp, bsz, seq, cols):
    n = bsz * seq
    nq = seq // TQ
    nch = seq // CMP_STRIDE
    rep = B_HEADS // B_KV_HEADS
    qb0 = cols["q"] // (rep * LANES)
    ksb, kwb, vsb, vwb, gb0 = (cols[k] // LANES for k in ("ks", "kw", "vs", "vw", "gate"))
    return pl.pallas_call(
        functools.partial(_nsa_kernel, seq=seq),
        grid=(bsz, B_KV_HEADS, nq),
        in_specs=[pl.BlockSpec((TQ, rep * LANES), lambda b, g, i: (b * nq + i, qb0 + g)),
                  pl.BlockSpec((TQ, LANES), lambda b, g, i: (b * nq + i, gb0 + g)),
                  pl.BlockSpec((None, None, nch, LANES), lambda b, g, i: (b, g, 0, 0)),
                  pl.BlockSpec((None, None, nch, LANES), lambda b, g, i: (b, g, 0, 0)),
                  pl.BlockSpec((seq, LANES), lambda b, g, i: (b, ksb + g)),
                  pl.BlockSpec((seq, LANES), lambda b, g, i: (b, vsb + g)),
                  pl.BlockSpec((seq, LANES), lambda b, g, i: (b, kwb + g)),
                  pl.BlockSpec((seq, LANES), lambda b, g, i: (b, vwb + g))],
        out_specs=pl.BlockSpec((TQ, rep * LANES), lambda b, g, i: (b * nq + i, g)),
        out_shape=jax.ShapeDtypeStruct((n, B_HEADS * B_HEAD_DIM), F32),
        scratch_shapes=[pltpu.VMEM((rep * TQ, LANES), BF16),
                        pltpu.VMEM((rep * TQ, max(NSA_NKW, NSA_TK, nch)), F32),
                        pltpu.VMEM((rep * TQ, max(NSA_NKW, NSA_TK, nch)), BF16),
                        pltpu.VMEM((rep * TQ, 1), F32), pltpu.VMEM((rep * TQ, 1), F32),
                        pltpu.VMEM((rep * TQ, LANES), F32), pltpu.VMEM((rep * TQ, LANES), F32)],
        compiler_params=_cparams(("parallel", "parallel", "arbitrary")),
        name="nsa",
    )(qk, gates, kcmp, vcmp, qk, v, qk, v)


def _oproj_kernel(oa_ref, ob_ref, ga_ref, gb_ref, wo_ref, x_ref, g_ref, b_ref, wr_ref,
                  x1_ref, sc_ref, acc_ref, *, alpha):
    k = pl.program_id(1)

    @pl.when(k == 0)
    def _():
        acc_ref[...] = jnp.zeros_like(acc_ref)

    y = (ga_ref[...] * oa_ref[...] + gb_ref[...] * ob_ref[...]).astype(BF16)
    acc_ref[...] += _dot(y, wo_ref[...])

    @pl.when(k == pl.num_programs(1) - 1)
    def _():
        x1 = _layer_norm(alpha * x_ref[...] + acc_ref[...], g_ref[...], b_ref[...])
        x1_ref[...] = x1
        sc_ref[...] = _sigmoid(_dot(x1.astype(BF16), wr_ref[...]))


def _out_proj(o_a, o_b, gates, ga_col0, gb_col0, w_o, x, ln_g, ln_b, w_r, alpha, tm=256, tk=512):
    n, d = x.shape
    tm = min(tm, n)
    ga0, gb0 = ga_col0 // tk, gb_col0 // tk
    return pl.pallas_call(
        functools.partial(_oproj_kernel, alpha=alpha),
        grid=(n // tm, d // tk),
        in_specs=[pl.BlockSpec((tm, tk), lambda i, k: (i, k)),
                  pl.BlockSpec((tm, tk), lambda i, k: (i, k)),
                  pl.BlockSpec((tm, tk), lambda i, k: (i, ga0 + k)),
                  pl.BlockSpec((tm, tk), lambda i, k: (i, gb0 + k)),
                  pl.BlockSpec((tk, d), lambda i, k: (k, 0)),
                  pl.BlockSpec((tm, d), lambda i, k: (i, 0)),
                  pl.BlockSpec((1, d), lambda i, k: (0, 0)),
                  pl.BlockSpec((1, d), lambda i, k: (0, 0)),
                  pl.BlockSpec((d, LANES), lambda i, k: (0, 0))],
        out_specs=[pl.BlockSpec((tm, d), lambda i, k: (i, 0)),
                   pl.BlockSpec((tm, LANES), lambda i, k: (i, 0))],
        out_shape=[jax.ShapeDtypeStruct((n, d), F32), jax.ShapeDtypeStruct((n, LANES), F32)],
        scratch_shapes=[pltpu.VMEM((tm, d), F32)],
        compiler_params=_cparams(("parallel", "arbitrary")),
        name="oproj_ln1_router",
    )(o_a, o_b, gates, gates, w_o, x, ln_g, ln_b, w_r)


def _route_kernel(sc_ref, bias_ref, ek_ref, rk_ref, gk_ref, cnt_ref, run_ref):
    i = pl.program_id(0)
    tm = sc_ref.shape[0]

    @pl.when(i == 0)
    def _():
        run_ref[...] = jnp.zeros_like(run_ref)

    sc = sc_ref[...]
    lane = lax.broadcasted_iota(jnp.int32, (1, LANES), 1)
    is_e = lane < N_EXPERTS
    biased = jnp.where(is_e, sc + bias_ref[...], -jnp.inf)
    rank = jnp.zeros((tm, LANES), jnp.int32)
    for e in range(N_EXPERTS):
        be = biased[:, e:e + 1]
        rank = rank + ((be > biased) | ((be == biased) & (e < lane))).astype(jnp.int32)
    sel = (rank < TOP_K) & is_e
    sv = jnp.where(sel, sc, 0.0)
    gates = sv / jnp.sum(sv, axis=-1, keepdims=True) * ROUTED_SCALE
    selb = sel.astype(BF16)
    ri = lax.broadcasted_iota(jnp.int32, (tm, tm), 0)
    ci = lax.broadcasted_iota(jnp.int32, (tm, tm), 1)
    cum = _dot((ci < ri).astype(BF16), selb) + run_ref[...]
    run_ref[...] += jnp.sum(sel.astype(F32), axis=0, keepdims=True)
    cnt_ref[...] = run_ref[...]
    ai = lax.broadcasted_iota(jnp.int32, (LANES, LANES), 0)
    bi = lax.broadcasted_iota(jnp.int32, (LANES, LANES), 1)
    before = _dot(selb, (ai < bi).astype(BF16))
    lane_f = lane.astype(F32)
    ek = jnp.zeros((tm, LANES), F32)
    rk = jnp.zeros((tm, LANES), F32)
    gk = jnp.zeros((tm, LANES), F32)
    for k in range(TOP_K):
        oh = sel & (before == k)
        ek = jnp.where(lane == k, jnp.sum(jnp.where(oh, lane_f, 0.0), axis=-1, keepdims=True), ek)
        rk = jnp.where(lane == k, jnp.sum(jnp.where(oh, cum, 0.0), axis=-1, keepdims=True), rk)
        gk = jnp.where(lane == k, jnp.sum(jnp.where(oh, gates, 0.0), axis=-1, keepdims=True), gk)
    ek_ref[...] = ek.astype(jnp.int32)
    rk_ref[...] = rk.astype(jnp.int32)
    gk_ref[...] = gk


def _route(scores, bias, tm=256):
    n = scores.shape[0]
    tm = min(tm, n)
    blk = pl.BlockSpec((tm, LANES), lambda i: (i, 0))
    one = pl.BlockSpec((1, LANES), lambda i: (0, 0))
    return pl.pallas_call(
        _route_kernel,
        grid=(n // tm,),
        in_specs=[blk, one],
        out_specs=[blk, blk, blk, one],
        out_shape=[jax.ShapeDtypeStruct((n, LANES), jnp.int32), jax.ShapeDtypeStruct((n, LANES), jnp.int32),
                   jax.ShapeDtypeStruct((n, LANES), F32), jax.ShapeDtypeStruct((1, LANES), F32)],
        scratch_shapes=[pltpu.VMEM((1, LANES), F32)],
        compiler_params=_cparams(("arbitrary",)),
        name="route",
    )(scores, bias)


def _row_copy(src, s_row, dst, d_row, sem):
    return pltpu.make_async_copy(src.at[pl.ds(s_row, 1)], dst.at[pl.ds(d_row, 1)], sem)


def _scatter_kernel(cnt_ref, pad_ref, pst_ref, nu_ref, dest_ref, x_ref, xs_ref, zrow_ref, zblk_ref, sem,
                    *, tm, n_tok_steps, n_blocks):
    i = pl.program_id(0)

    @pl.when(i < n_tok_steps)
    def _():
        def issue(t, c):
            for k in range(TOP_K):
                _row_copy(x_ref, t, xs_ref, dest_ref[t * TOP_K + k], sem).start()
            return c

        lax.fori_loop(0, tm, issue, 0)

        def drain(t, c):
            for k in range(TOP_K):
                _row_copy(x_ref, 0, xs_ref, 0, sem).wait()
            return c

        lax.fori_loop(0, tm, drain, 0)

    @pl.when((i >= n_tok_steps) & (i < n_tok_steps + N_EXPERTS))
    def _():
        e = i - n_tok_steps
        zrow_ref[...] = jnp.zeros_like(zrow_ref)
        lo, hi, base = cnt_ref[e], pad_ref[e], pst_ref[e]

        def issue(r, c):
            pltpu.make_async_copy(zrow_ref, xs_ref.at[pl.ds(base + r, 1)], sem).start()
            return c

        lax.fori_loop(lo, hi, issue, 0)

        def drain(r, c):
            pltpu.make_async_copy(zrow_ref, xs_ref.at[pl.ds(0, 1)], sem).wait()
            return c

        lax.fori_loop(lo, hi, drain, 0)

    @pl.when(i == n_tok_steps + N_EXPERTS)
    def _():
        zblk_ref[...] = jnp.zeros_like(zblk_ref)

        def issue(p, c):
            pltpu.make_async_copy(zblk_ref, xs_ref.at[pl.ds(p * MOE_BLOCK, MOE_BLOCK)], sem).start()
            return c

        lax.fori_loop(nu_ref[0], n_blocks, issue, 0)

        def drain(p, c):
            pltpu.make_async_copy(zblk_ref, xs_ref.at[pl.ds(0, MOE_BLOCK)], sem).wait()
            return c

        lax.fori_loop(nu_ref[0], n_blocks, drain, 0)


def _scatter_rows(x1, dest_flat, counts, padded, pstarts, n_used, n_blocks, tm=128):
    n, d = x1.shape
    tm = min(tm, n)
    n_tok_steps = n // tm
    return pl.pallas_call(
        functools.partial(_scatter_kernel, tm=tm, n_tok_steps=n_tok_steps, n_blocks=n_blocks),
        grid_spec=pltpu.PrefetchScalarGridSpec(
            num_scalar_prefetch=4,
            grid=(n_tok_steps + N_EXPERTS + 1,),
            in_specs=[pl.BlockSpec((tm * TOP_K,), lambda i, *_: (jnp.minimum(i, n_tok_steps - 1),),
                                   memory_space=pltpu.SMEM),
                      pl.BlockSpec((tm, d), lambda i, *_: (jnp.minimum(i, n_tok_steps - 1), 0))],
            out_specs=pl.BlockSpec(memory_space=pl.ANY),
            scratch_shapes=[pltpu.VMEM((1, d), x1.dtype), pltpu.VMEM((MOE_BLOCK, d), x1.dtype),
                            pltpu.SemaphoreType.DMA(())]),
        out_shape=jax.ShapeDtypeStruct((n_blocks * MOE_BLOCK, d), x1.dtype),
        compiler_params=_cparams(("arbitrary",)),
        name="moe_scatter",
    )(counts, padded, pstarts, n_used, dest_flat, x1)


def _expert_kernel(be_ref, nu_ref, xs_ref, wg_ref, wu_ref, wd_ref, y_ref):
    del be_ref
    used = pl.program_id(0) < nu_ref[0]

    @pl.when(used)
    def _():
        xb = xs_ref[...].astype(BF16)
        hg = _dot(xb, wg_ref[...])
        hu = _dot(xb, wu_ref[...])
        h = (hg * _sigmoid(hg)) * hu
        y_ref[...] = _dot(h.astype(BF16), wd_ref[...])

    @pl.when(jnp.logical_not(used))
    def _():
        y_ref[...] = jnp.zeros_like(y_ref)


def _expert_mlp(xs, block_e, n_used, wg, wu, wd):
    n_rows, d = xs.shape
    nblk = n_rows // MOE_BLOCK
    ff = wg.shape[-1]

    def row_map(p, be, nu):
        return (jnp.minimum(p, nu[0] - 1), 0)

    return pl.pallas_call(
        _expert_kernel,
        grid_spec=pltpu.PrefetchScalarGridSpec(
            num_scalar_prefetch=2,
            grid=(nblk,),
            in_specs=[pl.BlockSpec((MOE_BLOCK, d), row_map),
                      pl.BlockSpec((None, d, ff), lambda p, be, nu: (be[p], 0, 0)),
                      pl.BlockSpec((None, d, ff), lambda p, be, nu: (be[p], 0, 0)),
                      pl.BlockSpec((None, ff, d), lambda p, be, nu: (be[p], 0, 0))],
            out_specs=pl.BlockSpec((MOE_BLOCK, d), lambda p, be, nu: (p, 0))),
        out_shape=jax.ShapeDtypeStruct((n_rows, d), F32),
        compiler_params=_cparams(("arbitrary",)),
        name="moe_experts",
    )(block_e, n_used, xs, wg, wu, wd)


def _combine_kernel(dest_ref, x1_ref, gk_ref, sg_ref, su_ref, sd_ref, g_ref, b_ref, ys_ref,
                    o_ref, ybuf_ref, sem, *, tm, alpha):
    def issue(t, c):
        for k in range(TOP_K):
            _row_copy(ys_ref, dest_ref[t * TOP_K + k], ybuf_ref, k * tm + t, sem).start()
        return c

    lax.fori_loop(0, tm, issue, 0)

    x1 = x1_ref[...]
    xb = x1.astype(BF16)
    hg = _dot(xb, sg_ref[...])
    hu = _dot(xb, su_ref[...])
    shared = _dot(((hg * _sigmoid(hg)) * hu).astype(BF16), sd_ref[...])

    def drain(t, c):
        for k in range(TOP_K):
            _row_copy(ys_ref, 0, ybuf_ref, 0, sem).wait()
        return c

    lax.fori_loop(0, tm, drain, 0)

    gk = gk_ref[...]
    routed = jnp.zeros_like(x1)
    for k in range(TOP_K):
        routed = routed + gk[:, k:k + 1] * ybuf_ref[k * tm:(k + 1) * tm, :]
    o_ref[...] = _layer_norm(alpha * x1 + (routed + shared), g_ref[...], b_ref[...])


def _combine(x1, dest_flat, gk, ys, sg, su, sd, ln_g, ln_b, alpha, tm=128):
    n, d = x1.shape
    tm = min(tm, n)
    ff = sg.shape[-1]
    return pl.pallas_call(
        functools.partial(_combine_kernel, tm=tm, alpha=alpha),
        grid=(n // tm,),
        in_specs=[pl.BlockSpec((tm * TOP_K,), lambda i: (i,), memory_space=pltpu.SMEM),
                  pl.BlockSpec((tm, d), lambda i: (i, 0)),
                  pl.BlockSpec((tm, LANES), lambda i: (i, 0)),
                  pl.BlockSpec((d, ff), lambda i: (0, 0)),
                  pl.BlockSpec((d, ff), lambda i: (0, 0)),
                  pl.BlockSpec((ff, d), lambda i: (0, 0)),
                  pl.BlockSpec((1, d), lambda i: (0, 0)),
                  pl.BlockSpec((1, d), lambda i: (0, 0)),
                  pl.BlockSpec(memory_space=pl.ANY)],
        out_specs=pl.BlockSpec((tm, d), lambda i: (i, 0)),
        out_shape=jax.ShapeDtypeStruct((n, d), F32),
        scratch_shapes=[pltpu.VMEM((TOP_K * tm, d), F32), pltpu.SemaphoreType.DMA(())],
        compiler_params=_cparams(("arbitrary",)),
        name="moe_combine_ln2",
    )(dest_flat, x1, gk, sg, su, sd, ln_g, ln_b, ys)


def _rope_kernel(pos_ref, inv_ref, sign_ref, cos_ref, sin_ref):
    ang = pos_ref[...].astype(F32) * inv_ref[...]
    cos_ref[...] = jnp.cos(ang)
    sin_ref[...] = jnp.sin(ang) * sign_ref[...]


def _rope_tables(positions, dh, tm=512):
    half = dh // 2
    inv = ROPE_THETA ** (-jnp.arange(half, dtype=F32) * 2.0 / dh)
    reps = LANES // dh
    inv_l = jnp.tile(jnp.concatenate([inv, inv]), reps).reshape(1, LANES)
    sign_l = jnp.tile(jnp.concatenate([-jnp.ones((half,), F32), jnp.ones((half,), F32)]), reps).reshape(1, LANES)
    pos = positions.reshape(-1, 1)
    n = pos.shape[0]
    tm = min(tm, n)
    one = pl.BlockSpec((1, LANES), lambda i: (0, 0))
    blk = pl.BlockSpec((tm, LANES), lambda i: (i, 0))
    return pl.pallas_call(
        _rope_kernel,
        grid=(n // tm,),
        in_specs=[pl.BlockSpec((tm, 1), lambda i: (i, 0)), one, one],
        out_specs=[blk, blk],
        out_shape=[jax.ShapeDtypeStruct((n, LANES), F32)] * 2,
        compiler_params=_cparams(("parallel",)),
        name="rope_tables",
    )(pos, inv_l, sign_l)


def _dup_heads(w, heads, dh):
    d = w.shape[0]
    w3 = w.reshape(d, heads, dh)
    return jnp.concatenate([w3, w3], axis=-1).reshape(d, heads * 2 * dh)


def _hybrid_layer(x, positions, w_in, a_sinks, cmp_k_pos, cmp_k_w1, cmp_k_w2, cmp_v_pos, cmp_v_w1,
                  cmp_v_w2, w_o, ln1_g, ln1_b, w_router, router_bias, exp_w_gate, exp_w_up, exp_w_down,
                  sh_w_gate, sh_w_up, sh_w_down, ln2_g, ln2_b, alpha):
    bsz, seq, d = x.shape
    n = bsz * seq
    assert seq % NSA_TK == 0 and (seq // CMP_STRIDE) % LANES == 0 and A_WINDOW <= TQ
    assert 3 <= seq // SLC_LEN <= LANES
    xf = x.reshape(n, d)
    xb = xf.astype(BF16)

    qa_w, kva = A_HEADS * A_HEAD_DIM, A_KV_HEADS * A_HEAD_DIM
    qb_w, kvb = B_HEADS * B_HEAD_DIM, B_KV_HEADS * B_HEAD_DIM
    widths = [qa_w, kva, kva, qb_w, kvb, kvb, kvb, kvb, kvb, kvb, 3 * B_HEADS, d, d]
    offs = np.concatenate([[0], np.cumsum(widths)]).tolist()
    col = lambda i: w_in[:, offs[i]:offs[i + 1]]
    w_qa, w_ka, w_va, w_qb, w_kc, w_vc, w_ks, w_vs, w_kw, w_vw, w_gn, w_ga, w_gb = (col(i) for i in range(13))
    rep = B_HEADS // B_KV_HEADS
    gidx = np.zeros((B_KV_HEADS, LANES), np.int32)
    gmask = np.zeros((B_KV_HEADS, LANES), np.float32)
    for g in range(B_KV_HEADS):
        for br in range(3):
            for r in range(rep):
                gidx[g, br * rep + r] = br * B_HEADS + g * rep + r
                gmask[g, br * rep + r] = 1.0
    w_gn_x = w_gn[:, gidx.reshape(-1)] * jnp.asarray(gmask.reshape(-1))

    w_r64 = jnp.concatenate([w_qa, _dup_heads(w_ka, A_KV_HEADS, A_HEAD_DIM)], axis=1).astype(BF16)
    w_vals = jnp.concatenate([_dup_heads(w_va, A_KV_HEADS, A_HEAD_DIM), w_vs, w_vw], axis=1).astype(BF16)
    w_r128 = jnp.concatenate([w_qb, w_ks, w_kw], axis=1).astype(BF16)
    w_sig = jnp.concatenate([w_gn_x, w_ga, w_gb], axis=1).astype(BF16)

    cos64, sin64 = _rope_tables(positions, A_HEAD_DIM)
    cos128, sin128 = _rope_tables(positions, B_HEAD_DIM)

    qk_a = _project(xb, w_r64, cos64, sin64, "rope64", BF16,
                    q_cols=qa_w, q_scale=A_HEAD_DIM ** -0.5 * LOG2E)
    vals = _project(xb, w_vals, cos64, sin64, "none", BF16)
    qk_b = _project(xb, w_r128, cos128, sin128, "rope128", BF16,
                    q_cols=qb_w, q_scale=B_HEAD_DIM ** -0.5 * LOG2E)
    kc = _project(xb, w_kc.astype(BF16), cos128, sin128, "rope128", F32)
    vc = _project(xb, w_vc.astype(BF16), cos128, sin128, "none", F32)
    gates = _project(xb, w_sig, cos128, sin128, "sigmoid", F32)

    o_a = _swa_attention(qk_a, vals, a_sinks, bsz, seq, k_col0=qa_w, v_col0=0)

    kcmp = _compress(kc, 0, cmp_k_pos, cmp_k_w1.astype(BF16), cmp_k_w2.astype(BF16), bsz, seq)
    vcmp = _compress(vc, 0, cmp_v_pos, cmp_v_w1.astype(BF16), cmp_v_w2.astype(BF16), bsz, seq)
    nsa_cols = dict(q=0, ks=qb_w, kw=qb_w + kvb, vs=2 * kva, vw=2 * kva + kvb, gate=0)
    o_b = _nsa_attention(qk_b, vals, gates, kcmp, vcmp, bsz, seq, nsa_cols)

    w_r = jnp.pad(w_router, ((0, 0), (0, LANES - N_EXPERTS))).astype(BF16)
    x1, scores = _out_proj(o_a, o_b, gates, B_KV_HEADS * LANES, B_KV_HEADS * LANES + d, w_o.astype(BF16), xf,
                           ln1_g.reshape(1, d), ln1_b.reshape(1, d), w_r, alpha)

    bias = jnp.pad(router_bias.astype(F32), (0, LANES - N_EXPERTS)).reshape(1, LANES)
    ek, rk, gk, cnt = _route(scores, bias)

    counts = cnt[0, :N_EXPERTS].astype(jnp.int32)
    padded = (counts + MOE_BLOCK - 1) // MOE_BLOCK * MOE_BLOCK
    pends = jnp.cumsum(padded)
    pstarts = pends - padded
    nblk = -(-(n * TOP_K) // MOE_BLOCK) + N_EXPERTS
    blk_start = jnp.arange(nblk, dtype=jnp.int32) * MOE_BLOCK
    block_e = jnp.minimum(jnp.sum((pends[None, :] <= blk_start[:, None]).astype(jnp.int32), axis=1),
                          N_EXPERTS - 1)
    n_used = (pends[-1:] // MOE_BLOCK).astype(jnp.int32)
    dest = (pstarts[ek[:, :TOP_K]] + rk[:, :TOP_K]).astype(jnp.int32).reshape(-1)

    xs = _scatter_rows(x1, dest, counts, padded, pstarts.astype(jnp.int32), n_used, nblk)
    ys = _expert_mlp(xs, block_e, n_used, exp_w_gate.astype(BF16), exp_w_up.astype(BF16),
                     exp_w_down.astype(BF16))
    out = _combine(x1, dest, gk, ys, sh_w_gate.astype(BF16), sh_w_up.astype(BF16), sh_w_down.astype(BF16),
                   ln2_g.reshape(1, d), ln2_b.reshape(1, d), alpha)
    return out.reshape(bsz, seq, d)


def kernel(x, positions, w_in, a_sinks, cmp_k_pos, cmp_k_w1, cmp_k_w2, cmp_v_pos, cmp_v_w1, cmp_v_w2,
           w_o, ln1_g, ln1_b, w_router, router_bias, exp_w_gate, exp_w_up, exp_w_down,
           sh_w_gate, sh_w_up, sh_w_down, ln2_g, ln2_b):
    depth = w_in.shape[0]
    alpha = (2.0 * depth) ** 0.25
    for l in range(depth):
        x = _hybrid_layer(x, positions, w_in[l], a_sinks[l], cmp_k_pos[l], cmp_k_w1[l], cmp_k_w2[l],
                          cmp_v_pos[l], cmp_v_w1[l], cmp_v_w2[l], w_o[l], ln1_g[l], ln1_b[l],
                          w_router[l], router_bias[l], exp_w_gate[l], exp_w_up[l], exp_w_down[l],
                          sh_w_gate[l], sh_w_up[l], sh_w_down[l], ln2_g[l], ln2_b[l], alpha)
    return x
```

```python
import functools

import jax
import jax.numpy as jnp
import numpy as np
from jax import lax
from jax.experimental import pallas as pl
from jax.experimental.pallas import tpu as pltpu

F32 = jnp.float32
BF16 = jnp.bfloat16

ROPE_THETA = 10000.0
LN_EPS = 1e-5
NEG_INF = -1e30

A_HEADS, A_KV_HEADS, A_HEAD_DIM, A_WINDOW = 64, 8, 64, 128
B_HEADS, B_KV_HEADS, B_HEAD_DIM = 32, 4, 128
CMP_LEN, CMP_STRIDE, CMP_HIDDEN = 32, 16, 256
SLC_LEN, SLC_TOPK, WIN_LEN = 64, 16, 512
N_EXPERTS, TOP_K, EXPERT_FF, SHARED_FF = 64, 8, 512, 512
ROUTED_SCALE = 2.5
MOE_BLOCK = 256

LOG2E = 1.4426950408889634
LANES = 128
TQ = 128
VMEM_LIMIT = 56 * 1024 * 1024


def _cparams(sem):
    return pltpu.CompilerParams(dimension_semantics=sem, vmem_limit_bytes=VMEM_LIMIT)


def _dot(a, b):
    return jnp.dot(a, b, preferred_element_type=F32)


def _dot_nt(a, b):
    return lax.dot_general(a, b, (((1,), (1,)), ((), ())), preferred_element_type=F32)


def _sigmoid(v):
    return 1.0 / (1.0 + jnp.exp(-v))


def _layer_norm(z, g, b):
    mu = jnp.mean(z, axis=-1, keepdims=True)
    zc = z - mu
    var = jnp.mean(zc * zc, axis=-1, keepdims=True)
    return zc * lax.rsqrt(var + LN_EPS) * g + b


def _proj_kernel(tbl_ref, x_ref, w_ref, cos_ref, sin_ref, o_ref, *scratch, epi, q_blocks, q_scale):
    del tbl_ref
    if scratch:
        (wb_ref,) = scratch

        @pl.when(pl.program_id(1) == 0)
        def _():
            wb_ref[...] = w_ref[...].astype(BF16)

        w = wb_ref[...]
    else:
        w = w_ref[...]
    acc = _dot_nt(x_ref[...], w)
    if epi == "none":
        o_ref[...] = acc.astype(o_ref.dtype)
    elif epi == "sigmoid":
        o_ref[...] = _sigmoid(acc).astype(o_ref.dtype)
    else:
        cs = jnp.where(pl.program_id(0) < q_blocks, q_scale, 1.0)
        cos = cos_ref[...] * cs
        sin = sin_ref[...] * cs
        lane = lax.broadcasted_iota(jnp.int32, (1, LANES), 1)
        for c in range(acc.shape[1] // LANES):
            y = acc[:, c * LANES:(c + 1) * LANES]
            if epi == "rope128":
                rot = pltpu.roll(y, 64, 1)
            else:
                rot = jnp.where((lane % 64) < 32, pltpu.roll(y, 96, 1), pltpu.roll(y, 32, 1))
            o_ref[:, c * LANES:(c + 1) * LANES] = (y * cos + rot * sin).astype(o_ref.dtype)


PROJ_TN = 512


def _project(xb, w_t, blocks, cos, sin, epi, out_dtype, q_cols=0, q_scale=1.0, tm=1024):
    n, d = xb.shape
    tn = PROJ_TN
    ncol = len(blocks) * tn
    tm = min(tm, n)
    assert n % tm == 0 and q_cols % tn == 0 and (max(blocks) + 1) * tn <= w_t.shape[0]
    cast = w_t.dtype != BF16
    return pl.pallas_call(
        functools.partial(_proj_kernel, epi=epi, q_blocks=q_cols // tn, q_scale=q_scale),
        grid_spec=pltpu.PrefetchScalarGridSpec(
            num_scalar_prefetch=1,
            grid=(ncol // tn, n // tm),
            in_specs=[pl.BlockSpec((tm, d), lambda j, i, tbl: (i, 0)),
                      pl.BlockSpec((tn, d), lambda j, i, tbl: (tbl[j], 0)),
                      pl.BlockSpec((tm, LANES), lambda j, i, tbl: (i, 0)),
                      pl.BlockSpec((tm, LANES), lambda j, i, tbl: (i, 0))],
            out_specs=pl.BlockSpec((tm, tn), lambda j, i, tbl: (i, j)),
            scratch_shapes=[pltpu.VMEM((tn, d), BF16)] if cast else []),
        out_shape=jax.ShapeDtypeStruct((n, ncol), out_dtype),
        compiler_params=_cparams(("arbitrary", "arbitrary")),
        name="proj_" + epi,
    )(jnp.asarray(blocks, jnp.int32), xb, w_t, cos, sin)


def _swa_kernel(sink_ref, q_ref, kp_ref, kc_ref, vp_ref, vc_ref, o_ref, s_ref, p_ref):
    n = pl.program_id(1)
    g = pl.program_id(2)
    kk = jnp.concatenate([kp_ref[...], kc_ref[...]], axis=0)
    vv = jnp.concatenate([vp_ref[...], vc_ref[...]], axis=0)
    lane = lax.broadcasted_iota(jnp.int32, (1, LANES), 1)
    lo = lane < A_HEAD_DIM
    zv = jnp.zeros_like(vv)
    v_halves = (jnp.where(lo, vv, zv), jnp.where(lo, zv, vv))
    nk = 2 * TQ
    qpos = n * TQ + lax.broadcasted_iota(jnp.int32, (TQ, nk), 0)
    kpos = (n - 1) * TQ + lax.broadcasted_iota(jnp.int32, (TQ, nk), 1)
    delta = qpos - kpos
    mask = (kpos >= 0) & (delta >= 0) & (delta < A_WINDOW)
    npair = q_ref.shape[1] // LANES
    zq = jnp.zeros((TQ, LANES), q_ref.dtype)
    q_lo = [jnp.where(lo, q_ref[:, c * LANES:(c + 1) * LANES], zq) for c in range(npair)]
    q_hi = [jnp.where(lo, zq, q_ref[:, c * LANES:(c + 1) * LANES]) for c in range(npair)]
    s_ref[...] = _dot_nt(jnp.concatenate(q_lo + q_hi, axis=0), kk)
    for half in range(2):
        for pair in range(npair):
            sl = slice((half * npair + pair) * TQ, (half * npair + pair + 1) * TQ)
            s = jnp.where(mask, s_ref[sl, :], NEG_INF)
            sk = sink_ref[g * (2 * npair) + pair * 2 + half] * LOG2E
            m = jnp.maximum(jnp.max(s, axis=-1, keepdims=True), sk)
            e = jnp.exp2(s - m)
            p = e / (jnp.sum(e, axis=-1, keepdims=True) + jnp.exp2(sk - m))
            p_ref[sl, :] = p.astype(BF16)
    half_rows = npair * TQ
    out = _dot(p_ref[:half_rows, :], v_halves[0]) + _dot(p_ref[half_rows:, :], v_halves[1])
    for pair in range(npair):
        o_ref[:, pair * LANES:(pair + 1) * LANES] = out[pair * TQ:(pair + 1) * TQ, :]


def _swa_attention(q, k, v, sinks, bsz, seq):
    n = bsz * seq
    nb = seq // TQ
    kb0 = vb0 = 0
    return pl.pallas_call(
        _swa_kernel,
        grid=(bsz, nb, A_KV_HEADS),
        in_specs=[pl.BlockSpec(memory_space=pltpu.SMEM),
                  pl.BlockSpec((TQ, 4 * LANES), lambda b, i, g: (b * nb + i, g)),
                  pl.BlockSpec((TQ, LANES), lambda b, i, g: (b * nb + jnp.maximum(i - 1, 0), kb0 + g)),
                  pl.BlockSpec((TQ, LANES), lambda b, i, g: (b * nb + i, kb0 + g)),
                  pl.BlockSpec((TQ, LANES), lambda b, i, g: (b * nb + jnp.maximum(i - 1, 0), vb0 + g)),
                  pl.BlockSpec((TQ, LANES), lambda b, i, g: (b * nb + i, vb0 + g))],
        out_specs=pl.BlockSpec((TQ, 4 * LANES), lambda b, i, g: (b * nb + i, g)),
        out_shape=jax.ShapeDtypeStruct((n, A_HEADS * A_HEAD_DIM), F32),
        scratch_shapes=[pltpu.VMEM((8 * TQ, 2 * TQ), F32), pltpu.VMEM((8 * TQ, 2 * TQ), BF16)],
        compiler_params=_cparams(("parallel", "parallel", "parallel")),
        name="swa",
    )(sinks, q, k, k, v, v)


def _compress_kernel(t_ref, pe_ref, w1_ref, w2_ref, o_ref):
    nch = o_ref.shape[0]
    u = jnp.zeros((nch, CMP_HIDDEN), F32)
    v = jnp.zeros((nch, CMP_HIDDEN), F32)
    for l in range(CMP_STRIDE):
        t_l = t_ref[pl.ds(l, nch, stride=CMP_STRIDE), :]
        a = (t_l + pe_ref[l:l + 1, :]).astype(BF16)
        b = (t_l + pe_ref[CMP_STRIDE + l:CMP_STRIDE + l + 1, :]).astype(BF16)
        u = u + _dot(a, w1_ref[l * LANES:(l + 1) * LANES, :])
        v = v + _dot(b, w1_ref[(CMP_STRIDE + l) * LANES:(CMP_STRIDE + l + 1) * LANES, :])
    pre = u + pltpu.roll(v, nch - 1, 0)
    h = jax.nn.gelu(pre)
    o_ref[...] = _dot(h.astype(BF16), w2_ref[...]).astype(o_ref.dtype)


def _compress(t, col0, pe, w1, w2, bsz, seq):
    nch = seq // CMP_STRIDE
    cb0 = col0 // LANES
    return pl.pallas_call(
        _compress_kernel,
        grid=(bsz, B_KV_HEADS),
        in_specs=[pl.BlockSpec((seq, LANES), lambda b, g: (b, cb0 + g)),
                  pl.BlockSpec((CMP_LEN, LANES), lambda b, g: (0, 0)),
                  pl.BlockSpec((CMP_LEN * LANES, CMP_HIDDEN), lambda b, g: (0, 0)),
                  pl.BlockSpec((CMP_HIDDEN, LANES), lambda b, g: (0, 0))],
        out_specs=pl.BlockSpec((None, None, nch, LANES), lambda b, g: (b, g, 0, 0)),
        out_shape=jax.ShapeDtypeStruct((bsz, B_KV_HEADS, nch, LANES), BF16),
        compiler_params=_cparams(("parallel", "parallel")),
        name="nsa_compress",
    )(t, pe, w1, w2)


NSA_TK = 512
NSA_NKW = (-(-(WIN_LEN - 1) // TQ) + 1) * TQ


def _nsa_kernel(q_ref, gate_ref, kcmp_ref, vcmp_ref, ks_ref, vs_ref, kw_ref, vw_ref, o_ref,
                qs_ref, s_ref, p_ref, m_ref, l_ref, acc_ref, ocmp_ref, *, seq):
    qi = pl.program_id(2)
    rep = B_HEADS // B_KV_HEADS
    nch = seq // CMP_STRIDE
    ns = seq // SLC_LEN
    n_sel = min(SLC_TOPK, ns)
    heads = [slice(r * TQ, (r + 1) * TQ) for r in range(rep)]
    for r in range(rep):
        qs_ref[heads[r], :] = q_ref[:, r * LANES:(r + 1) * LANES]
    tpos = qi * TQ + lax.broadcasted_iota(jnp.int32, (TQ, 1), 0)

    s_ref[:, :nch] = _dot_nt(qs_ref[...], kcmp_ref[...])
    c_end = lax.broadcasted_iota(jnp.int32, (1, nch), 1) * CMP_STRIDE + (CMP_LEN - 1)
    valid = c_end <= tpos
    p_grp = jnp.zeros((TQ, nch), F32)
    for sl in heads:
        s = jnp.where(valid, s_ref[sl, :nch], NEG_INF)
        m = jnp.max(s, axis=-1, keepdims=True)
        e = jnp.where(valid, jnp.exp2(s - m), 0.0)
        den = jnp.sum(e, axis=-1, keepdims=True)
        p = e / jnp.where(den > 0, den, 1.0)
        p_grp = p_grp + p
        p_ref[sl, :nch] = p.astype(BF16)
    ocmp_ref[...] = _dot(p_ref[:, :nch], vcmp_ref[...])

    jrow = lax.broadcasted_iota(jnp.int32, (ns, 1), 0)
    d = lax.broadcasted_iota(jnp.int32, (ns, nch), 1) - (SLC_LEN // CMP_STRIDE) * jrow
    msel = jnp.where((d == -1) | (d == 3), 1.0, jnp.where((d >= 0) & (d <= 2), 2.0, 0.0)).astype(BF16)
    p_hi = p_grp.astype(BF16)
    r1 = p_grp - p_hi.astype(F32)
    p_mid = r1.astype(BF16)
    p_lo = (r1 - p_mid.astype(F32)).astype(BF16)
    p_slc = _dot_nt(msel, p_hi) + _dot_nt(msel, p_mid) + _dot_nt(msel, p_lo)

    tpos_l = qi * TQ + lax.broadcasted_iota(jnp.int32, (1, TQ), 1)
    cur = tpos_l // SLC_LEN
    forced = (jrow == 0) | (jrow == cur) | (jrow == cur - 1)
    score = jnp.where(forced, 1e9, jnp.where(jrow * SLC_LEN <= tpos_l, p_slc, NEG_INF))
    rank = jnp.zeros((ns, TQ), jnp.int32)
    for i in range(ns):
        si = score[i:i + 1, :]
        rank = rank + ((si > score) | ((si == score) & (jrow > i))).astype(jnp.int32)
    sel_t = (rank < n_sel).astype(F32)
    if ns < LANES:
        sel_t = jnp.concatenate([sel_t, jnp.zeros((LANES - ns, TQ), F32)], axis=0)
    sel = sel_t.T.astype(BF16)

    m_ref[...] = jnp.full_like(m_ref, NEG_INF)
    l_ref[...] = jnp.zeros_like(l_ref)
    acc_ref[...] = jnp.zeros_like(acc_ref)
    jexp = lax.broadcasted_iota(jnp.int32, (LANES, 1), 0)

    def slc_body(kt, c):
        row0 = pl.multiple_of(kt * NSA_TK, NSA_TK)
        s_ref[:, :NSA_TK] = _dot_nt(qs_ref[...], ks_ref[pl.ds(row0, NSA_TK), :])
        kpos = kt * NSA_TK + lax.broadcasted_iota(jnp.int32, (1, NSA_TK), 1)
        expand = (jexp == kpos // SLC_LEN).astype(BF16)
        keep = (_dot(sel, expand) > 0.5) & (kpos <= tpos)
        for sl in heads:
            s = jnp.where(keep, s_ref[sl, :NSA_TK], NEG_INF)
            m_old = m_ref[sl, :]
            m_new = jnp.maximum(m_old, jnp.max(s, axis=-1, keepdims=True))
            alpha = jnp.exp2(m_old - m_new)
            e = jnp.exp2(s - m_new)
            l_ref[sl, :] = alpha * l_ref[sl, :] + jnp.sum(e, axis=-1, keepdims=True)
            m_ref[sl, :] = m_new
            acc_ref[sl, :] = alpha * acc_ref[sl, :]
            p_ref[sl, :NSA_TK] = e.astype(BF16)
        acc_ref[...] += _dot(p_ref[:, :NSA_TK], vs_ref[pl.ds(row0, NSA_TK), :])
        return c

    lax.fori_loop(0, (qi * TQ + TQ + NSA_TK - 1) // NSA_TK, slc_body, 0)

    nprev = NSA_NKW // TQ - 1
    k_parts, v_parts = [], []
    for i in range(nprev + 1):
        row0 = pl.multiple_of(jnp.maximum(qi - nprev + i, 0) * TQ, TQ)
        k_parts.append(kw_ref[pl.ds(row0, TQ), :])
        v_parts.append(vw_ref[pl.ds(row0, TQ), :])
    s_ref[:, :NSA_NKW] = _dot_nt(qs_ref[...], jnp.concatenate(k_parts, axis=0))
    kposw = (qi - nprev) * TQ + lax.broadcasted_iota(jnp.int32, (1, NSA_NKW), 1)
    dlt = tpos - kposw
    keepw = (kposw >= 0) & (dlt >= 0) & (dlt < WIN_LEN)
    for sl in heads:
        s = jnp.where(keepw, s_ref[sl, :NSA_NKW], NEG_INF)
        e = jnp.exp2(s - jnp.max(s, axis=-1, keepdims=True))
        p_ref[sl, :NSA_NKW] = (e / jnp.sum(e, axis=-1, keepdims=True)).astype(BF16)
    o_win = _dot(p_ref[:, :NSA_NKW], jnp.concatenate(v_parts, axis=0))

    gate = gate_ref[...]
    for r, sl in enumerate(heads):
        o_ref[:, r * LANES:(r + 1) * LANES] = (gate[:, r:r + 1] * ocmp_ref[sl, :]
                                               + gate[:, rep + r:rep + r + 1] * (acc_ref[sl, :] / l_ref[sl, :])
                                               + gate[:, 2 * rep + r:2 * rep + r + 1] * o_win[sl])


def _nsa_attention(qk, v, gates, kcmp, vcmp, bsz, seq, cols):
    n = bsz * seq
    nq = seq // TQ
    nch = seq // CMP_STRIDE
    rep = B_HEADS // B_KV_HEADS
    qb0 = cols["q"] // (rep * LANES)
    ksb, kwb, vsb, vwb, gb0 = (cols[k] // LANES for k in ("ks", "kw", "vs", "vw", "gate"))
    return pl.pallas_call(
        functools.partial(_nsa_kernel, seq=seq),
        grid=(bsz, B_KV_HEADS, nq),
        in_specs=[pl.BlockSpec((TQ, rep * LANES), lambda b, g, i: (b * nq + i, qb0 + g)),
                  pl.BlockSpec((TQ, LANES), lambda b, g, i: (b * nq + i, gb0 + g)),
                  pl.BlockSpec((None, None, nch, LANES), lambda b, g, i: (b, g, 0, 0)),
                  pl.BlockSpec((None, None, nch, LANES), lambda b, g, i: (b, g, 0, 0)),
                  pl.BlockSpec((seq, LANES), lambda b, g, i: (b, ksb + g)),
                  pl.BlockSpec((seq, LANES), lambda b, g, i: (b, vsb + g)),
                  pl.BlockSpec((seq, LANES), lambda b, g, i: (b, kwb + g)),
                  pl.BlockSpec((seq, LANES), lambda b, g, i: (b, vwb + g))],
        out_specs=pl.BlockSpec((TQ, rep * LANES), lambda b, g, i: (b * nq + i, g)),
        out_shape=jax.ShapeDtypeStruct((n, B_HEADS * B_HEAD_DIM), F32),
        scratch_shapes=[pltpu.VMEM((rep * TQ, LANES), BF16),
                        pltpu.VMEM((rep * TQ, max(NSA_NKW, NSA_TK, nch)), F32),
                        pltpu.VMEM((rep * TQ, max(NSA_NKW, NSA_TK, nch)), BF16),
                        pltpu.VMEM((rep * TQ, 1), F32), pltpu.VMEM((rep * TQ, 1), F32),
                        pltpu.VMEM((rep * TQ, LANES), F32), pltpu.VMEM((rep * TQ, LANES), F32)],
        compiler_params=_cparams(("parallel", "parallel", "arbitrary")),
        name="nsa",
    )(qk, gates, kcmp, vcmp, qk, v, qk, v)


def _oproj_kernel(oa_ref, ob_ref, ga_ref, gb_ref, wo_ref, x_ref, g_ref, b_ref, wr_ref,
                  x1_ref, sc_ref, acc_ref, *, alpha):
    k = pl.program_id(1)

    @pl.when(k == 0)
    def _():
        acc_ref[...] = jnp.zeros_like(acc_ref)

    y = (ga_ref[...] * oa_ref[...] + gb_ref[...] * ob_ref[...]).astype(BF16)
    acc_ref[...] += _dot(y, wo_ref[...])

    @pl.when(k == pl.num_programs(1) - 1)
    def _():
        x1 = _layer_norm(alpha * x_ref[...] + acc_ref[...], g_ref[...], b_ref[...])
        x1_ref[...] = x1
        sc_ref[...] = _sigmoid(_dot(x1.astype(BF16), wr_ref[...]))


def _out_proj(o_a, o_b, gates, ga_col0, gb_col0, w_o, x, ln_g, ln_b, w_r, alpha, tm=256, tk=512):
    n, d = x.shape
    tm = min(tm, n)
    ga0, gb0 = ga_col0 // tk, gb_col0 // tk
    return pl.pallas_call(
        functools.partial(_oproj_kernel, alpha=alpha),
        grid=(n // tm, d // tk),
        in_specs=[pl.BlockSpec((tm, tk), lambda i, k: (i, k)),
                  pl.BlockSpec((tm, tk), lambda i, k: (i, k)),
                  pl.BlockSpec((tm, tk), lambda i, k: (i, ga0 + k)),
                  pl.BlockSpec((tm, tk), lambda i, k: (i, gb0 + k)),
                  pl.BlockSpec((tk, d), lambda i, k: (k, 0)),
                  pl.BlockSpec((tm, d), lambda i, k: (i, 0)),
                  pl.BlockSpec((1, d), lambda i, k: (0, 0)),
                  pl.BlockSpec((1, d), lambda i, k: (0, 0)),
                  pl.BlockSpec((d, LANES), lambda i, k: (0, 0))],
        out_specs=[pl.BlockSpec((tm, d), lambda i, k: (i, 0)),
                   pl.BlockSpec((tm, LANES), lambda i, k: (i, 0))],
        out_shape=[jax.ShapeDtypeStruct((n, d), F32), jax.ShapeDtypeStruct((n, LANES), F32)],
        scratch_shapes=[pltpu.VMEM((tm, d), F32)],
        compiler_params=_cparams(("parallel", "arbitrary")),
        name="oproj_ln1_router",
    )(o_a, o_b, gates, gates, w_o, x, ln_g, ln_b, w_r)


def _route_kernel(sc_ref, bias_ref, ek_ref, rk_ref, gk_ref, cnt_ref, run_ref):
    i = pl.program_id(0)
    tm = sc_ref.shape[0]

    @pl.when(i == 0)
    def _():
        run_ref[...] = jnp.zeros_like(run_ref)

    sc = sc_ref[...]
    lane = lax.broadcasted_iota(jnp.int32, (1, LANES), 1)
    is_e = lane < N_EXPERTS
    biased = jnp.where(is_e, sc + bias_ref[...], -jnp.inf)
    rank = jnp.zeros((tm, LANES), jnp.int32)
    for e in range(N_EXPERTS):
        be = biased[:, e:e + 1]
        rank = rank + ((be > biased) | ((be == biased) & (e < lane))).astype(jnp.int32)
    sel = (rank < TOP_K) & is_e
    sv = jnp.where(sel, sc, 0.0)
    gates = sv / jnp.sum(sv, axis=-1, keepdims=True) * ROUTED_SCALE
    selb = sel.astype(BF16)
    ri = lax.broadcasted_iota(jnp.int32, (tm, tm), 0)
    ci = lax.broadcasted_iota(jnp.int32, (tm, tm), 1)
    cum = _dot((ci < ri).astype(BF16), selb) + run_ref[...]
    run_ref[...] += jnp.sum(sel.astype(F32), axis=0, keepdims=True)
    cnt_ref[...] = run_ref[...]
    ai = lax.broadcasted_iota(jnp.int32, (LANES, LANES), 0)
    bi = lax.broadcasted_iota(jnp.int32, (LANES, LANES), 1)
    before = _dot(selb, (ai < bi).astype(BF16))
    lane_f = lane.astype(F32)
    ek = jnp.zeros((tm, LANES), F32)
    rk = jnp.zeros((tm, LANES), F32)
    gk = jnp.zeros((tm, LANES), F32)
    for k in range(TOP_K):
        oh = sel & (before == k)
        ek = jnp.where(lane == k, jnp.sum(jnp.where(oh, lane_f, 0.0), axis=-1, keepdims=True), ek)
        rk = jnp.where(lane == k, jnp.sum(jnp.where(oh, cum, 0.0), axis=-1, keepdims=True), rk)
        gk = jnp.where(lane == k, jnp.sum(jnp.where(oh, gates, 0.0), axis=-1, keepdims=True), gk)
    ek_ref[...] = ek.astype(jnp.int32)
    rk_ref[...] = rk.astype(jnp.int32)
    gk_ref[...] = gk


def _route(scores, bias, tm=256):
    n = scores.shape[0]
    tm = min(tm, n)
    blk = pl.BlockSpec((tm, LANES), lambda i: (i, 0))
    one = pl.BlockSpec((1, LANES), lambda i: (0, 0))
    return pl.pallas_call(
        _route_kernel,
        grid=(n // tm,),
        in_specs=[blk, one],
        out_specs=[blk, blk, blk, one],
        out_shape=[jax.ShapeDtypeStruct((n, LANES), jnp.int32), jax.ShapeDtypeStruct((n, LANES), jnp.int32),
                   jax.ShapeDtypeStruct((n, LANES), F32), jax.ShapeDtypeStruct((1, LANES), F32)],
        scratch_shapes=[pltpu.VMEM((1, LANES), F32)],
        compiler_params=_cparams(("arbitrary",)),
        name="route",
    )(scores, bias)


def _row_copy(src, s_row, dst, d_row, sem):
    return pltpu.make_async_copy(src.at[pl.ds(s_row, 1)], dst.at[pl.ds(d_row, 1)], sem)


def _scatter_kernel(cnt_ref, pad_ref, pst_ref, nu_ref, dest_ref, x_ref, xs_ref, zrow_ref, zblk_ref, sem,
                    *, tm, n_tok_steps, n_blocks):
    i = pl.program_id(0)

    @pl.when(i < n_tok_steps)
    def _():
        def issue(t, c):
            for k in range(TOP_K):
                _row_copy(x_ref, t, xs_ref, dest_ref[t * TOP_K + k], sem).start()
            return c

        lax.fori_loop(0, tm, issue, 0)

        for k in range(TOP_K):
            pltpu.make_async_copy(x_ref, xs_ref.at[pl.ds(0, tm)], sem).wait()

    @pl.when((i >= n_tok_steps) & (i < n_tok_steps + N_EXPERTS))
    def _():
        e = i - n_tok_steps
        zrow_ref[...] = jnp.zeros_like(zrow_ref)
        lo, hi, base = cnt_ref[e], pad_ref[e], pst_ref[e]

        def issue(r, c):
            pltpu.make_async_copy(zrow_ref, xs_ref.at[pl.ds(base + r, 1)], sem).start()
            return c

        lax.fori_loop(lo, hi, issue, 0)

        def drain(r, c):
            pltpu.make_async_copy(zrow_ref, xs_ref.at[pl.ds(0, 1)], sem).wait()
            return c

        lax.fori_loop(lo, hi, drain, 0)

    @pl.when(i == n_tok_steps + N_EXPERTS)
    def _():
        zblk_ref[...] = jnp.zeros_like(zblk_ref)

        def issue(p, c):
            pltpu.make_async_copy(zblk_ref, xs_ref.at[pl.ds(p * MOE_BLOCK, MOE_BLOCK)], sem).start()
            return c

        lax.fori_loop(nu_ref[0], n_blocks, issue, 0)

        def drain(p, c):
            pltpu.make_async_copy(zblk_ref, xs_ref.at[pl.ds(0, MOE_BLOCK)], sem).wait()
            return c

        lax.fori_loop(nu_ref[0], n_blocks, drain, 0)


def _scatter_rows(x1, dest_flat, counts, padded, pstarts, n_used, n_blocks, tm=128):
    n, d = x1.shape
    tm = min(tm, n)
    n_tok_steps = n // tm
    return pl.pallas_call(
        functools.partial(_scatter_kernel, tm=tm, n_tok_steps=n_tok_steps, n_blocks=n_blocks),
        grid_spec=pltpu.PrefetchScalarGridSpec(
            num_scalar_prefetch=4,
            grid=(n_tok_steps + N_EXPERTS + 1,),
            in_specs=[pl.BlockSpec((tm * TOP_K,), lambda i, *_: (jnp.minimum(i, n_tok_steps - 1),),
                                   memory_space=pltpu.SMEM),
                      pl.BlockSpec((tm, d), lambda i, *_: (jnp.minimum(i, n_tok_steps - 1), 0))],
            out_specs=pl.BlockSpec(memory_space=pl.ANY),
            scratch_shapes=[pltpu.VMEM((1, d), x1.dtype), pltpu.VMEM((MOE_BLOCK, d), x1.dtype),
                            pltpu.SemaphoreType.DMA(())]),
        out_shape=jax.ShapeDtypeStruct((n_blocks * MOE_BLOCK, d), x1.dtype),
        compiler_params=_cparams(("arbitrary",)),
        name="moe_scatter",
    )(counts, padded, pstarts, n_used, dest_flat, x1)


def _expert_kernel(be_ref, nu_ref, xs_ref, wg_ref, wu_ref, wd_ref, y_ref):
    del be_ref
    used = pl.program_id(0) < nu_ref[0]

    @pl.when(used)
    def _():
        xb = xs_ref[...].astype(BF16)
        hg = _dot(xb, wg_ref[...])
        hu = _dot(xb, wu_ref[...])
        h = (hg * _sigmoid(hg)) * hu
        y_ref[...] = _dot(h.astype(BF16), wd_ref[...])

    @pl.when(jnp.logical_not(used))
    def _():
        y_ref[...] = jnp.zeros_like(y_ref)


def _expert_mlp(xs, block_e, n_used, wg, wu, wd):
    n_rows, d = xs.shape
    nblk = n_rows // MOE_BLOCK
    ff = wg.shape[-1]

    def row_map(p, be, nu):
        return (jnp.minimum(p, nu[0] - 1), 0)

    return pl.pallas_call(
        _expert_kernel,
        grid_spec=pltpu.PrefetchScalarGridSpec(
            num_scalar_prefetch=2,
            grid=(nblk,),
            in_specs=[pl.BlockSpec((MOE_BLOCK, d), row_map),
                      pl.BlockSpec((None, d, ff), lambda p, be, nu: (be[p], 0, 0)),
                      pl.BlockSpec((None, d, ff), lambda p, be, nu: (be[p], 0, 0)),
                      pl.BlockSpec((None, ff, d), lambda p, be, nu: (be[p], 0, 0))],
            out_specs=pl.BlockSpec((MOE_BLOCK, d), lambda p, be, nu: (p, 0))),
        out_shape=jax.ShapeDtypeStruct((n_rows, d), F32),
        compiler_params=_cparams(("arbitrary",)),
        name="moe_experts",
    )(block_e, n_used, xs, wg, wu, wd)


COMBINE_PARTS = 4


def _combine_kernel(dest_ref, x1_ref, gk_ref, sg_ref, su_ref, sd_ref, g_ref, b_ref, ys_ref,
                    o_ref, ybuf_ref, sh_ref, sem, *, tm, alpha):
    tp = tm // COMBINE_PARTS

    def issue(part):
        def body(t, c):
            for k in range(TOP_K):
                _row_copy(ys_ref, dest_ref[t * TOP_K + k], ybuf_ref, k * tm + t, sem.at[part]).start()
            return c

        lax.fori_loop(part * tp, (part + 1) * tp, body, 0)

    issue(0)
    issue(1)
    xb = x1_ref[...].astype(BF16)
    hg = _dot(xb, sg_ref[...])
    hu = _dot(xb, su_ref[...])
    sh_ref[...] = _dot(((hg * _sigmoid(hg)) * hu).astype(BF16), sd_ref[...])

    for part in range(COMBINE_PARTS):
        rows = slice(part * tp, (part + 1) * tp)
        for k in range(TOP_K):
            pltpu.make_async_copy(ys_ref.at[pl.ds(0, tp)], ybuf_ref.at[pl.ds(0, tp)], sem.at[part]).wait()
        if part + 2 < COMBINE_PARTS:
            issue(part + 2)
        gk = gk_ref[rows, :]
        routed = jnp.zeros((tp, x1_ref.shape[1]), F32)
        for k in range(TOP_K):
            routed = routed + gk[:, k:k + 1] * ybuf_ref[k * tm + part * tp:k * tm + (part + 1) * tp, :]
        o_ref[rows, :] = _layer_norm(alpha * x1_ref[rows, :] + (routed + sh_ref[rows, :]), g_ref[...], b_ref[...])


def _combine(x1, dest_flat, gk, ys, sg, su, sd, ln_g, ln_b, alpha, tm=128):
    n, d = x1.shape
    tm = min(tm, n)
    ff = sg.shape[-1]
    return pl.pallas_call(
        functools.partial(_combine_kernel, tm=tm, alpha=alpha),
        grid=(n // tm,),
        in_specs=[pl.BlockSpec((tm * TOP_K,), lambda i: (i,), memory_space=pltpu.SMEM),
                  pl.BlockSpec((tm, d), lambda i: (i, 0)),
                  pl.BlockSpec((tm, LANES), lambda i: (i, 0)),
                  pl.BlockSpec((d, ff), lambda i: (0, 0), pipeline_mode=pl.Buffered(1)),
                  pl.BlockSpec((d, ff), lambda i: (0, 0), pipeline_mode=pl.Buffered(1)),
                  pl.BlockSpec((ff, d), lambda i: (0, 0), pipeline_mode=pl.Buffered(1)),
                  pl.BlockSpec((1, d), lambda i: (0, 0)),
                  pl.BlockSpec((1, d), lambda i: (0, 0)),
                  pl.BlockSpec(memory_space=pl.ANY)],
        out_specs=pl.BlockSpec((tm, d), lambda i: (i, 0)),
        out_shape=jax.ShapeDtypeStruct((n, d), F32),
        scratch_shapes=[pltpu.VMEM((TOP_K * tm, d), F32), pltpu.VMEM((tm, d), F32),
                        pltpu.SemaphoreType.DMA((COMBINE_PARTS,))],
        compiler_params=_cparams(("arbitrary",)),
        name="moe_combine_ln2",
    )(dest_flat, x1, gk, sg, su, sd, ln_g, ln_b, ys)


def _rope_kernel(pos_ref, inv_ref, sign_ref, cos_ref, sin_ref):
    ang = pos_ref[...].astype(F32) * inv_ref[...]
    cos_ref[...] = jnp.cos(ang)
    sin_ref[...] = jnp.sin(ang) * sign_ref[...]


def _rope_tables(positions, dh, tm=512):
    half = dh // 2
    inv = ROPE_THETA ** (-jnp.arange(half, dtype=F32) * 2.0 / dh)
    reps = LANES // dh
    inv_l = jnp.tile(jnp.concatenate([inv, inv]), reps).reshape(1, LANES)
    sign_l = jnp.tile(jnp.concatenate([-jnp.ones((half,), F32), jnp.ones((half,), F32)]), reps).reshape(1, LANES)
    pos = positions.reshape(-1, 1)
    n = pos.shape[0]
    tm = min(tm, n)
    one = pl.BlockSpec((1, LANES), lambda i: (0, 0))
    blk = pl.BlockSpec((tm, LANES), lambda i: (i, 0))
    return pl.pallas_call(
        _rope_kernel,
        grid=(n // tm,),
        in_specs=[pl.BlockSpec((tm, 1), lambda i: (i, 0)), one, one],
        out_specs=[blk, blk],
        out_shape=[jax.ShapeDtypeStruct((n, LANES), F32)] * 2,
        compiler_params=_cparams(("parallel",)),
        name="rope_tables",
    )(pos, inv_l, sign_l)


def _dup_heads(w_rows, heads, dh):
    d = w_rows.shape[1]
    w3 = w_rows.reshape(heads, dh, d)
    return jnp.concatenate([w3, w3], axis=1).reshape(heads * 2 * dh, d)


def _hybrid_layer(x, positions, w_in, a_sinks, cmp_k_pos, cmp_k_w1, cmp_k_w2, cmp_v_pos, cmp_v_w1,
                  cmp_v_w2, w_o, ln1_g, ln1_b, w_router, router_bias, exp_w_gate, exp_w_up, exp_w_down,
                  sh_w_gate, sh_w_up, sh_w_down, ln2_g, ln2_b, alpha):
    bsz, seq, d = x.shape
    n = bsz * seq
    assert seq % NSA_TK == 0 and (seq // CMP_STRIDE) % LANES == 0 and A_WINDOW <= TQ
    assert 3 <= seq // SLC_LEN <= LANES
    xf = x.reshape(n, d)
    xb = xf.astype(BF16)

    qa_w, kva = A_HEADS * A_HEAD_DIM, A_KV_HEADS * A_HEAD_DIM
    qb_w, kvb = B_HEADS * B_HEAD_DIM, B_KV_HEADS * B_HEAD_DIM
    widths = [qa_w, kva, kva, qb_w, kvb, kvb, kvb, kvb, kvb, kvb, 3 * B_HEADS, d, d]
    offs = np.concatenate([[0], np.cumsum(widths)]).tolist()
    (o_qa, o_ka, o_va, o_qb, o_kc, o_vc, o_ks, o_vs, o_kw, o_vw, o_gn, o_ga, o_end) = (
        offs[i] for i in (0, 1, 2, 3, 4, 5, 6, 7, 8, 9, 10, 11, 13))
    w_t = w_in.T
    blk = lambda start, width: list(range(start // PROJ_TN, (start + width) // PROJ_TN))
    assert all(o % PROJ_TN == 0 for o in (o_qa, o_qb, o_kc, o_vc, o_ks, o_vs, o_kw, o_vw))
    rep = B_HEADS // B_KV_HEADS
    gidx = np.zeros((B_KV_HEADS, LANES), np.int32)
    gmask = np.zeros((B_KV_HEADS, LANES), np.float32)
    for g in range(B_KV_HEADS):
        for br in range(3):
            for r in range(rep):
                gidx[g, br * rep + r] = o_gn + br * B_HEADS + g * rep + r
                gmask[g, br * rep + r] = 1.0
    w_gn_x = w_t[gidx.reshape(-1)] * jnp.asarray(gmask.reshape(-1, 1))
    w_sig = jnp.concatenate([w_gn_x, w_t[o_ga:o_end]], axis=0).astype(BF16)
    w_ka2 = _dup_heads(w_t[o_ka:o_va], A_KV_HEADS, A_HEAD_DIM).astype(BF16)
    w_va2 = _dup_heads(w_t[o_va:o_qb], A_KV_HEADS, A_HEAD_DIM).astype(BF16)
    all_blocks = lambda w: list(range(w.shape[0] // PROJ_TN))

    cos64, sin64 = _rope_tables(positions, A_HEAD_DIM)
    cos128, sin128 = _rope_tables(positions, B_HEAD_DIM)

    q_a = _project(xb, w_t, blk(o_qa, qa_w), cos64, sin64, "rope64", BF16,
                   q_cols=qa_w, q_scale=A_HEAD_DIM ** -0.5 * LOG2E)
    k_a = _project(xb, w_ka2, all_blocks(w_ka2), cos64, sin64, "rope64", BF16)
    v_a = _project(xb, w_va2, all_blocks(w_va2), cos64, sin64, "none", BF16)
    qk_b = _project(xb, w_t, blk(o_qb, qb_w) + blk(o_ks, kvb) + blk(o_kw, kvb), cos128, sin128, "rope128", BF16,
                    q_cols=qb_w, q_scale=B_HEAD_DIM ** -0.5 * LOG2E)
    v_b = _project(xb, w_t, blk(o_vs, kvb) + blk(o_vw, kvb), cos128, sin128, "none", BF16)
    kc = _project(xb, w_t, blk(o_kc, kvb), cos128, sin128, "rope128", F32)
    vc = _project(xb, w_t, blk(o_vc, kvb), cos128, sin128, "none", F32)
    gates = _project(xb, w_sig, all_blocks(w_sig), cos128, sin128, "sigmoid", F32)

    o_a = _swa_attention(q_a, k_a, v_a, a_sinks, bsz, seq)

    kcmp = _compress(kc, 0, cmp_k_pos, cmp_k_w1.astype(BF16), cmp_k_w2.astype(BF16), bsz, seq)
    vcmp = _compress(vc, 0, cmp_v_pos, cmp_v_w1.astype(BF16), cmp_v_w2.astype(BF16), bsz, seq)
    nsa_cols = dict(q=0, ks=qb_w, kw=qb_w + kvb, vs=0, vw=kvb, gate=0)
    o_b = _nsa_attention(qk_b, v_b, gates, kcmp, vcmp, bsz, seq, nsa_cols)

    w_r = jnp.pad(w_router, ((0, 0), (0, LANES - N_EXPERTS))).astype(BF16)
    x1, scores = _out_proj(o_a, o_b, gates, B_KV_HEADS * LANES, B_KV_HEADS * LANES + d, w_o.astype(BF16), xf,
                           ln1_g.reshape(1, d), ln1_b.reshape(1, d), w_r, alpha)

    bias = jnp.pad(router_bias.astype(F32), (0, LANES - N_EXPERTS)).reshape(1, LANES)
    ek, rk, gk, cnt = _route(scores, bias)

    counts = cnt[0, :N_EXPERTS].astype(jnp.int32)
    padded = (counts + MOE_BLOCK - 1) // MOE_BLOCK * MOE_BLOCK
    pends = jnp.cumsum(padded)
    pstarts = pends - padded
    nblk = -(-(n * TOP_K) // MOE_BLOCK) + N_EXPERTS
    blk_start = jnp.arange(nblk, dtype=jnp.int32) * MOE_BLOCK
    block_e = jnp.minimum(jnp.sum((pends[None, :] <= blk_start[:, None]).astype(jnp.int32), axis=1),
                          N_EXPERTS - 1)
    n_used = (pends[-1:] // MOE_BLOCK).astype(jnp.int32)
    dest = (pstarts[ek[:, :TOP_K]] + rk[:, :TOP_K]).astype(jnp.int32).reshape(-1)

    xs = _scatter_rows(x1, dest, counts, padded, pstarts.astype(jnp.int32), n_used, nblk)
    ys = _expert_mlp(xs, block_e, n_used, exp_w_gate.astype(BF16), exp_w_up.astype(BF16),
                     exp_w_down.astype(BF16))
    out = _combine(x1, dest, gk, ys, sh_w_gate.astype(BF16), sh_w_up.astype(BF16), sh_w_down.astype(BF16),
                   ln2_g.reshape(1, d), ln2_b.reshape(1, d), alpha)
    return out.reshape(bsz, seq, d)


def kernel(x, positions, w_in, a_sinks, cmp_k_pos, cmp_k_w1, cmp_k_w2, cmp_v_pos, cmp_v_w1, cmp_v_w2,
           w_o, ln1_g, ln1_b, w_router, router_bias, exp_w_gate, exp_w_up, exp_w_down,
           sh_w_gate, sh_w_up, sh_w_down, ln2_g, ln2_b):
    depth = w_in.shape[0]
    alpha = (2.0 * depth) ** 0.25
    for l in range(depth):
        x = _hybrid_layer(x, positions, w_in[l], a_sinks[l], cmp_k_pos[l], cmp_k_w1[l], cmp_k_w2[l],
                          cmp_v_pos[l], cmp_v_w1[l], cmp_v_w2[l], w_o[l], ln1_g[l], ln1_b[l],
                          w_router[l], router_bias[l], exp_w_gate[l], exp_w_up[l], exp_w_down[l],
                          sh_w_gate[l], sh_w_up[l], sh_w_down[l], ln2_g[l], ln2_b[l], alpha)
    return x
```

```python
import functools

import jax
import jax.numpy as jnp
import numpy as np
from jax import lax
from jax.experimental import pallas as pl
from jax.experimental.pallas import tpu as pltpu

F32 = jnp.float32
BF16 = jnp.bfloat16

ROPE_THETA = 10000.0
LN_EPS = 1e-5
NEG_INF = -1e30

A_HEADS, A_KV_HEADS, A_HEAD_DIM, A_WINDOW = 64, 8, 64, 128
B_HEADS, B_KV_HEADS, B_HEAD_DIM = 32, 4, 128
CMP_LEN, CMP_STRIDE, CMP_HIDDEN = 32, 16, 256
SLC_LEN, SLC_TOPK, WIN_LEN = 64, 16, 512
N_EXPERTS, TOP_K, EXPERT_FF, SHARED_FF = 64, 8, 512, 512
ROUTED_SCALE = 2.5
MOE_BLOCK = 256

LOG2E = 1.4426950408889634
LANES = 128
TQ = 128
VMEM_LIMIT = 56 * 1024 * 1024


def _cparams(sem):
    return pltpu.CompilerParams(dimension_semantics=sem, vmem_limit_bytes=VMEM_LIMIT)


def _dot(a, b):
    return jnp.dot(a, b, preferred_element_type=F32)


def _dot_nt(a, b):
    return lax.dot_general(a, b, (((1,), (1,)), ((), ())), preferred_element_type=F32)


def _sigmoid(v):
    return 1.0 / (1.0 + jnp.exp(-v))


def _pack_halves(v):
    h = v.shape[1] // 2
    lo = lax.bitcast_convert_type(v[:, :h].astype(BF16).astype(F32), jnp.uint32)
    hi = lax.bitcast_convert_type(v[:, h:].astype(BF16).astype(F32), jnp.uint32)
    return (hi & jnp.uint32(0xFFFF0000)) | (lo >> 16)


def _unpack_halves(w):
    lo = lax.bitcast_convert_type(w << 16, F32)
    hi = lax.bitcast_convert_type(w & jnp.uint32(0xFFFF0000), F32)
    return lo, hi


def _layer_norm(z, g, b):
    mu = jnp.mean(z, axis=-1, keepdims=True)
    zc = z - mu
    var = jnp.mean(zc * zc, axis=-1, keepdims=True)
    return zc * lax.rsqrt(var + LN_EPS) * g + b


def _proj_kernel(tbl_ref, x_ref, w_ref, cos_ref, sin_ref, o_ref, *scratch, epi, q_blocks, q_scale):
    del tbl_ref
    if scratch:
        (wb_ref,) = scratch

        @pl.when(pl.program_id(1) == 0)
        def _():
            wb_ref[...] = w_ref[...].astype(BF16)

        w = wb_ref[...]
    else:
        w = w_ref[...]
    acc = _dot_nt(x_ref[...], w)
    if epi == "none":
        o_ref[...] = acc.astype(o_ref.dtype)
    elif epi == "sigmoid":
        o_ref[...] = _sigmoid(acc).astype(o_ref.dtype)
    else:
        cs = jnp.where(pl.program_id(0) < q_blocks, q_scale, 1.0)
        cos = cos_ref[...] * cs
        sin = sin_ref[...] * cs
        lane = lax.broadcasted_iota(jnp.int32, (1, LANES), 1)
        for c in range(acc.shape[1] // LANES):
            y = acc[:, c * LANES:(c + 1) * LANES]
            if epi == "rope128":
                rot = pltpu.roll(y, 64, 1)
            else:
                rot = jnp.where((lane % 64) < 32, pltpu.roll(y, 96, 1), pltpu.roll(y, 32, 1))
            o_ref[:, c * LANES:(c + 1) * LANES] = (y * cos + rot * sin).astype(o_ref.dtype)


PROJ_TN = 512


def _project(xb, w_t, blocks, cos, sin, epi, out_dtype, q_cols=0, q_scale=1.0, tm=1024):
    n, d = xb.shape
    tn = PROJ_TN
    ncol = len(blocks) * tn
    tm = min(tm, n)
    assert n % tm == 0 and q_cols % tn == 0 and (max(blocks) + 1) * tn <= w_t.shape[0]
    cast = w_t.dtype != BF16
    return pl.pallas_call(
        functools.partial(_proj_kernel, epi=epi, q_blocks=q_cols // tn, q_scale=q_scale),
        grid_spec=pltpu.PrefetchScalarGridSpec(
            num_scalar_prefetch=1,
            grid=(ncol // tn, n // tm),
            in_specs=[pl.BlockSpec((tm, d), lambda j, i, tbl: (i, 0)),
                      pl.BlockSpec((tn, d), lambda j, i, tbl: (tbl[j], 0)),
                      pl.BlockSpec((tm, LANES), lambda j, i, tbl: (i, 0)),
                      pl.BlockSpec((tm, LANES), lambda j, i, tbl: (i, 0))],
            out_specs=pl.BlockSpec((tm, tn), lambda j, i, tbl: (i, j)),
            scratch_shapes=[pltpu.VMEM((tn, d), BF16)] if cast else []),
        out_shape=jax.ShapeDtypeStruct((n, ncol), out_dtype),
        compiler_params=_cparams(("arbitrary", "arbitrary")),
        name="proj_" + epi,
    )(jnp.asarray(blocks, jnp.int32), xb, w_t, cos, sin)


SWA_GROUPS = 2


def _swa_kernel(sink_ref, q_ref, kp_ref, kc_ref, vp_ref, vc_ref, o_ref, s_ref, p_ref):
    n = pl.program_id(1)
    lane = lax.broadcasted_iota(jnp.int32, (1, LANES), 1)
    lo = lane < A_HEAD_DIM
    nk = 2 * TQ
    qpos = n * TQ + lax.broadcasted_iota(jnp.int32, (TQ, nk), 0)
    kpos = (n - 1) * TQ + lax.broadcasted_iota(jnp.int32, (TQ, nk), 1)
    delta = qpos - kpos
    mask = (kpos >= 0) & (delta >= 0) & (delta < A_WINDOW)
    rep = A_HEADS // A_KV_HEADS
    npair = rep // 2
    zq = jnp.zeros((TQ, LANES), q_ref.dtype)
    for gg in range(SWA_GROUPS):
        g = pl.program_id(2) * SWA_GROUPS + gg
        gl = slice(gg * LANES, (gg + 1) * LANES)
        kk = jnp.concatenate([kp_ref[:, gl], kc_ref[:, gl]], axis=0)
        vv = jnp.concatenate([vp_ref[:, gl], vc_ref[:, gl]], axis=0)
        zv = jnp.zeros_like(vv)
        v_halves = (jnp.where(lo, vv, zv), jnp.where(lo, zv, vv))
        qcols = [slice((gg * npair + c) * LANES, (gg * npair + c + 1) * LANES) for c in range(npair)]
        q_lo = [jnp.where(lo, q_ref[:, c], zq) for c in qcols]
        q_hi = [jnp.where(lo, zq, q_ref[:, c]) for c in qcols]
        r0 = gg * rep * TQ
        s_ref[r0:r0 + rep * TQ, :] = _dot_nt(jnp.concatenate(q_lo + q_hi, axis=0), kk)
        for half in range(2):
            for pair in range(npair):
                sl = slice(r0 + (half * npair + pair) * TQ, r0 + (half * npair + pair + 1) * TQ)
                s = jnp.where(mask, s_ref[sl, :], NEG_INF)
                sk = sink_ref[g * rep + pair * 2 + half] * LOG2E
                m = jnp.maximum(jnp.max(s, axis=-1, keepdims=True), sk)
                e = jnp.exp2(s - m)
                p = e / (jnp.sum(e, axis=-1, keepdims=True) + jnp.exp2(sk - m))
                p_ref[sl, :] = p.astype(BF16)
        half_rows = npair * TQ
        out = (_dot(p_ref[r0:r0 + half_rows, :], v_halves[0])
               + _dot(p_ref[r0 + half_rows:r0 + 2 * half_rows, :], v_halves[1]))
        for pair, c in enumerate(qcols):
            o_ref[:, c] = out[pair * TQ:(pair + 1) * TQ, :]


def _swa_attention(q, k, v, sinks, bsz, seq):
    n = bsz * seq
    nb = seq // TQ
    rows = SWA_GROUPS * (A_HEADS // A_KV_HEADS) * TQ
    qw = SWA_GROUPS * (A_HEADS // A_KV_HEADS) * A_HEAD_DIM
    kw = SWA_GROUPS * LANES
    return pl.pallas_call(
        _swa_kernel,
        grid=(bsz, nb, A_KV_HEADS // SWA_GROUPS),
        in_specs=[pl.BlockSpec(memory_space=pltpu.SMEM),
                  pl.BlockSpec((TQ, qw), lambda b, i, g: (b * nb + i, g)),
                  pl.BlockSpec((TQ, kw), lambda b, i, g: (b * nb + jnp.maximum(i - 1, 0), g)),
                  pl.BlockSpec((TQ, kw), lambda b, i, g: (b * nb + i, g)),
                  pl.BlockSpec((TQ, kw), lambda b, i, g: (b * nb + jnp.maximum(i - 1, 0), g)),
                  pl.BlockSpec((TQ, kw), lambda b, i, g: (b * nb + i, g))],
        out_specs=pl.BlockSpec((TQ, qw), lambda b, i, g: (b * nb + i, g)),
        out_shape=jax.ShapeDtypeStruct((n, A_HEADS * A_HEAD_DIM), F32),
        scratch_shapes=[pltpu.VMEM((rows, 2 * TQ), F32), pltpu.VMEM((rows, 2 * TQ), BF16)],
        compiler_params=_cparams(("parallel", "parallel", "parallel")),
        name="swa",
    )(sinks, q, k, k, v, v)


def _compress_kernel(t_ref, pe_ref, w1_ref, w2_ref, o_ref):
    nch = o_ref.shape[0]
    u = jnp.zeros((nch, CMP_HIDDEN), F32)
    v = jnp.zeros((nch, CMP_HIDDEN), F32)
    for l in range(CMP_STRIDE):
        t_l = t_ref[pl.ds(l, nch, stride=CMP_STRIDE), :]
        a = (t_l + pe_ref[l:l + 1, :]).astype(BF16)
        b = (t_l + pe_ref[CMP_STRIDE + l:CMP_STRIDE + l + 1, :]).astype(BF16)
        u = u + _dot(a, w1_ref[l * LANES:(l + 1) * LANES, :])
        v = v + _dot(b, w1_ref[(CMP_STRIDE + l) * LANES:(CMP_STRIDE + l + 1) * LANES, :])
    pre = u + pltpu.roll(v, nch - 1, 0)
    h = jax.nn.gelu(pre)
    o_ref[...] = _dot(h.astype(BF16), w2_ref[...]).astype(o_ref.dtype)


def _compress(t, col0, pe, w1, w2, bsz, seq):
    nch = seq // CMP_STRIDE
    cb0 = col0 // LANES
    return pl.pallas_call(
        _compress_kernel,
        grid=(bsz, B_KV_HEADS),
        in_specs=[pl.BlockSpec((seq, LANES), lambda b, g: (b, cb0 + g)),
                  pl.BlockSpec((CMP_LEN, LANES), lambda b, g: (0, 0)),
                  pl.BlockSpec((CMP_LEN * LANES, CMP_HIDDEN), lambda b, g: (0, 0)),
                  pl.BlockSpec((CMP_HIDDEN, LANES), lambda b, g: (0, 0))],
        out_specs=pl.BlockSpec((None, None, nch, LANES), lambda b, g: (b, g, 0, 0)),
        out_shape=jax.ShapeDtypeStruct((bsz, B_KV_HEADS, nch, LANES), BF16),
        compiler_params=_cparams(("parallel", "parallel")),
        name="nsa_compress",
    )(t, pe, w1, w2)


NSA_TK = 512
NSA_NKW = (-(-(WIN_LEN - 1) // TQ) + 1) * TQ


def _nsa_kernel(q_ref, gate_ref, kcmp_ref, vcmp_ref, ks_ref, vs_ref, kw_ref, vw_ref, o_ref,
                qs_ref, s_ref, p_ref, m_ref, l_ref, acc_ref, ocmp_ref, *, seq):
    qi = pl.program_id(2)
    rep = B_HEADS // B_KV_HEADS
    nch = seq // CMP_STRIDE
    ns = seq // SLC_LEN
    n_sel = min(SLC_TOPK, ns)
    heads = [slice(r * TQ, (r + 1) * TQ) for r in range(rep)]
    for r in range(rep):
        qs_ref[heads[r], :] = q_ref[:, r * LANES:(r + 1) * LANES]
    tpos = qi * TQ + lax.broadcasted_iota(jnp.int32, (TQ, 1), 0)

    s_ref[:, :nch] = _dot_nt(qs_ref[...], kcmp_ref[...])
    c_end = lax.broadcasted_iota(jnp.int32, (1, nch), 1) * CMP_STRIDE + (CMP_LEN - 1)
    valid = c_end <= tpos
    p_grp = jnp.zeros((TQ, nch), F32)
    for sl in heads:
        s = jnp.where(valid, s_ref[sl, :nch], NEG_INF)
        m = jnp.max(s, axis=-1, keepdims=True)
        e = jnp.where(valid, jnp.exp2(s - m), 0.0)
        den = jnp.sum(e, axis=-1, keepdims=True)
        p = e / jnp.where(den > 0, den, 1.0)
        p_grp = p_grp + p
        p_ref[sl, :nch] = p.astype(BF16)
    ocmp_ref[...] = _dot(p_ref[:, :nch], vcmp_ref[...])

    jrow = lax.broadcasted_iota(jnp.int32, (ns, 1), 0)
    d = lax.broadcasted_iota(jnp.int32, (ns, nch), 1) - (SLC_LEN // CMP_STRIDE) * jrow
    msel = jnp.where((d == -1) | (d == 3), 1.0, jnp.where((d >= 0) & (d <= 2), 2.0, 0.0)).astype(BF16)
    p_hi = p_grp.astype(BF16)
    r1 = p_grp - p_hi.astype(F32)
    p_mid = r1.astype(BF16)
    p_lo = (r1 - p_mid.astype(F32)).astype(BF16)
    p_slc = _dot_nt(msel, p_hi) + _dot_nt(msel, p_mid) + _dot_nt(msel, p_lo)

    tpos_l = qi * TQ + lax.broadcasted_iota(jnp.int32, (1, TQ), 1)
    cur = tpos_l // SLC_LEN
    forced = (jrow == 0) | (jrow == cur) | (jrow == cur - 1)
    score = jnp.where(forced, 1e9, jnp.where(jrow * SLC_LEN <= tpos_l, p_slc, NEG_INF))
    rank = jnp.zeros((ns, TQ), jnp.int32)
    for i in range(ns):
        si = score[i:i + 1, :]
        rank = rank + ((si > score) | ((si == score) & (jrow > i))).astype(jnp.int32)
    sel_t = (rank < n_sel).astype(F32)
    if ns < LANES:
        sel_t = jnp.concatenate([sel_t, jnp.zeros((LANES - ns, TQ), F32)], axis=0)
    sel = sel_t.T.astype(BF16)

    m_ref[...] = jnp.full_like(m_ref, NEG_INF)
    l_ref[...] = jnp.zeros_like(l_ref)
    acc_ref[...] = jnp.zeros_like(acc_ref)
    jexp = lax.broadcasted_iota(jnp.int32, (LANES, 1), 0)

    def slc_body(kt, c):
        row0 = pl.multiple_of(kt * NSA_TK, NSA_TK)
        s_ref[:, :NSA_TK] = _dot_nt(qs_ref[...], ks_ref[pl.ds(row0, NSA_TK), :])
        kpos = kt * NSA_TK + lax.broadcasted_iota(jnp.int32, (1, NSA_TK), 1)
        expand = (jexp == kpos // SLC_LEN).astype(BF16)
        keep = (_dot(sel, expand) > 0.5) & (kpos <= tpos)
        for sl in heads:
            s = jnp.where(keep, s_ref[sl, :NSA_TK], NEG_INF)
            m_old = m_ref[sl, :]
            m_new = jnp.maximum(m_old, jnp.max(s, axis=-1, keepdims=True))
            alpha = jnp.exp2(m_old - m_new)
            e = jnp.exp2(s - m_new)
            l_ref[sl, :] = alpha * l_ref[sl, :] + jnp.sum(e, axis=-1, keepdims=True)
            m_ref[sl, :] = m_new
            acc_ref[sl, :] = alpha * acc_ref[sl, :]
            p_ref[sl, :NSA_TK] = e.astype(BF16)
        acc_ref[...] += _dot(p_ref[:, :NSA_TK], vs_ref[pl.ds(row0, NSA_TK), :])
        return c

    lax.fori_loop(0, (qi * TQ + TQ + NSA_TK - 1) // NSA_TK, slc_body, 0)

    nprev = NSA_NKW // TQ - 1
    k_parts, v_parts = [], []
    for i in range(nprev + 1):
        row0 = pl.multiple_of(jnp.maximum(qi - nprev + i, 0) * TQ, TQ)
        k_parts.append(kw_ref[pl.ds(row0, TQ), :])
        v_parts.append(vw_ref[pl.ds(row0, TQ), :])
    s_ref[:, :NSA_NKW] = _dot_nt(qs_ref[...], jnp.concatenate(k_parts, axis=0))
    kposw = (qi - nprev) * TQ + lax.broadcasted_iota(jnp.int32, (1, NSA_NKW), 1)
    dlt = tpos - kposw
    keepw = (kposw >= 0) & (dlt >= 0) & (dlt < WIN_LEN)
    for sl in heads:
        s = jnp.where(keepw, s_ref[sl, :NSA_NKW], NEG_INF)
        e = jnp.exp2(s - jnp.max(s, axis=-1, keepdims=True))
        p_ref[sl, :NSA_NKW] = (e / jnp.sum(e, axis=-1, keepdims=True)).astype(BF16)
    o_win = _dot(p_ref[:, :NSA_NKW], jnp.concatenate(v_parts, axis=0))

    gate = gate_ref[...]
    for r, sl in enumerate(heads):
        o_ref[:, r * LANES:(r + 1) * LANES] = (gate[:, r:r + 1] * ocmp_ref[sl, :]
                                               + gate[:, rep + r:rep + r + 1] * (acc_ref[sl, :] / l_ref[sl, :])
                                               + gate[:, 2 * rep + r:2 * rep + r + 1] * o_win[sl])


def _nsa_attention(qk, v, gates, kcmp, vcmp, bsz, seq, cols):
    n = bsz * seq
    nq = seq // TQ
    nch = seq // CMP_STRIDE
    rep = B_HEADS // B_KV_HEADS
    qb0 = cols["q"] // (rep * LANES)
    ksb, kwb, vsb, vwb, gb0 = (cols[k] // LANES for k in ("ks", "kw", "vs", "vw", "gate"))
    return pl.pallas_call(
        functools.partial(_nsa_kernel, seq=seq),
        grid=(bsz, B_KV_HEADS, nq),
        in_specs=[pl.BlockSpec((TQ, rep * LANES), lambda b, g, i: (b * nq + i, qb0 + g)),
                  pl.BlockSpec((TQ, LANES), lambda b, g, i: (b * nq + i, gb0 + g)),
                  pl.BlockSpec((None, None, nch, LANES), lambda b, g, i: (b, g, 0, 0)),
                  pl.BlockSpec((None, None, nch, LANES), lambda b, g, i: (b, g, 0, 0)),
                  pl.BlockSpec((seq, LANES), lambda b, g, i: (b, ksb + g)),
                  pl.BlockSpec((seq, LANES), lambda b, g, i: (b, vsb + g)),
                  pl.BlockSpec((seq, LANES), lambda b, g, i: (b, kwb + g)),
                  pl.BlockSpec((seq, LANES), lambda b, g, i: (b, vwb + g))],
        out_specs=pl.BlockSpec((TQ, rep * LANES), lambda b, g, i: (b * nq + i, g)),
        out_shape=jax.ShapeDtypeStruct((n, B_HEADS * B_HEAD_DIM), F32),
        scratch_shapes=[pltpu.VMEM((rep * TQ, LANES), BF16),
                        pltpu.VMEM((rep * TQ, max(NSA_NKW, NSA_TK, nch)), F32),
                        pltpu.VMEM((rep * TQ, max(NSA_NKW, NSA_TK, nch)), BF16),
                        pltpu.VMEM((rep * TQ, 1), F32), pltpu.VMEM((rep * TQ, 1), F32),
                        pltpu.VMEM((rep * TQ, LANES), F32), pltpu.VMEM((rep * TQ, LANES), F32)],
        compiler_params=_cparams(("parallel", "parallel", "arbitrary")),
        name="nsa",
    )(qk, gates, kcmp, vcmp, qk, v, qk, v)


def _oproj_kernel(oa_ref, ob_ref, ga_ref, gb_ref, wo_ref, x_ref, g_ref, b_ref, wr_ref,
                  x1_ref, x1p_ref, sc_ref, acc_ref, *, alpha):
    k = pl.program_id(1)

    @pl.when(k == 0)
    def _():
        acc_ref[...] = jnp.zeros_like(acc_ref)

    y = (ga_ref[...] * oa_ref[...] + gb_ref[...] * ob_ref[...]).astype(BF16)
    acc_ref[...] += _dot(y, wo_ref[...])

    @pl.when(k == pl.num_programs(1) - 1)
    def _():
        x1 = _layer_norm(alpha * x_ref[...] + acc_ref[...], g_ref[...], b_ref[...])
        x1_ref[...] = x1
        x1p_ref[...] = _pack_halves(x1)
        sc_ref[...] = _sigmoid(_dot(x1.astype(BF16), wr_ref[...]))


def _out_proj(o_a, o_b, gates, ga_col0, gb_col0, w_o, x, ln_g, ln_b, w_r, alpha, tm=256, tk=512):
    n, d = x.shape
    tm = min(tm, n)
    ga0, gb0 = ga_col0 // tk, gb_col0 // tk
    return pl.pallas_call(
        functools.partial(_oproj_kernel, alpha=alpha),
        grid=(n // tm, d // tk),
        in_specs=[pl.BlockSpec((tm, tk), lambda i, k: (i, k)),
                  pl.BlockSpec((tm, tk), lambda i, k: (i, k)),
                  pl.BlockSpec((tm, tk), lambda i, k: (i, ga0 + k)),
                  pl.BlockSpec((tm, tk), lambda i, k: (i, gb0 + k)),
                  pl.BlockSpec((tk, d), lambda i, k: (k, 0)),
                  pl.BlockSpec((tm, d), lambda i, k: (i, 0)),
                  pl.BlockSpec((1, d), lambda i, k: (0, 0)),
                  pl.BlockSpec((1, d), lambda i, k: (0, 0)),
                  pl.BlockSpec((d, LANES), lambda i, k: (0, 0))],
        out_specs=[pl.BlockSpec((tm, d), lambda i, k: (i, 0)),
                   pl.BlockSpec((tm, d // 2), lambda i, k: (i, 0)),
                   pl.BlockSpec((tm, LANES), lambda i, k: (i, 0))],
        out_shape=[jax.ShapeDtypeStruct((n, d), F32), jax.ShapeDtypeStruct((n, d // 2), jnp.uint32),
                   jax.ShapeDtypeStruct((n, LANES), F32)],
        scratch_shapes=[pltpu.VMEM((tm, d), F32)],
        compiler_params=_cparams(("parallel", "arbitrary")),
        name="oproj_ln1_router",
    )(o_a, o_b, gates, gates, w_o, x, ln_g, ln_b, w_r)


def _route_kernel(sc_ref, bias_ref, ek_ref, rk_ref, gk_ref, cnt_ref, run_ref):
    i = pl.program_id(0)
    tm = sc_ref.shape[0]

    @pl.when(i == 0)
    def _():
        run_ref[...] = jnp.zeros_like(run_ref)

    sc = sc_ref[...]
    lane = lax.broadcasted_iota(jnp.int32, (1, LANES), 1)
    is_e = lane < N_EXPERTS
    biased = jnp.where(is_e, sc + bias_ref[...], -jnp.inf)
    rank = jnp.zeros((tm, LANES), jnp.int32)
    for e in range(N_EXPERTS):
        be = biased[:, e:e + 1]
        rank = rank + ((be > biased) | ((be == biased) & (e < lane))).astype(jnp.int32)
    sel = (rank < TOP_K) & is_e
    sv = jnp.where(sel, sc, 0.0)
    gates = sv / jnp.sum(sv, axis=-1, keepdims=True) * ROUTED_SCALE
    selb = sel.astype(BF16)
    ri = lax.broadcasted_iota(jnp.int32, (tm, tm), 0)
    ci = lax.broadcasted_iota(jnp.int32, (tm, tm), 1)
    cum = _dot((ci < ri).astype(BF16), selb) + run_ref[...]
    run_ref[...] += jnp.sum(sel.astype(F32), axis=0, keepdims=True)
    cnt_ref[...] = run_ref[...]
    ai = lax.broadcasted_iota(jnp.int32, (LANES, LANES), 0)
    bi = lax.broadcasted_iota(jnp.int32, (LANES, LANES), 1)
    before = _dot(selb, (ai < bi).astype(BF16))
    lane_f = lane.astype(F32)
    ek = jnp.zeros((tm, LANES), F32)
    rk = jnp.zeros((tm, LANES), F32)
    gk = jnp.zeros((tm, LANES), F32)
    for k in range(TOP_K):
        oh = sel & (before == k)
        ek = jnp.where(lane == k, jnp.sum(jnp.where(oh, lane_f, 0.0), axis=-1, keepdims=True), ek)
        rk = jnp.where(lane == k, jnp.sum(jnp.where(oh, cum, 0.0), axis=-1, keepdims=True), rk)
        gk = jnp.where(lane == k, jnp.sum(jnp.where(oh, gates, 0.0), axis=-1, keepdims=True), gk)
    ek_ref[...] = ek.astype(jnp.int32)
    rk_ref[...] = rk.astype(jnp.int32)
    gk_ref[...] = gk


def _route(scores, bias, tm=256):
    n = scores.shape[0]
    tm = min(tm, n)
    blk = pl.BlockSpec((tm, LANES), lambda i: (i, 0))
    one = pl.BlockSpec((1, LANES), lambda i: (0, 0))
    return pl.pallas_call(
        _route_kernel,
        grid=(n // tm,),
        in_specs=[blk, one],
        out_specs=[blk, blk, blk, one],
        out_shape=[jax.ShapeDtypeStruct((n, LANES), jnp.int32), jax.ShapeDtypeStruct((n, LANES), jnp.int32),
                   jax.ShapeDtypeStruct((n, LANES), F32), jax.ShapeDtypeStruct((1, LANES), F32)],
        scratch_shapes=[pltpu.VMEM((1, LANES), F32)],
        compiler_params=_cparams(("arbitrary",)),
        name="route",
    )(scores, bias)


def _row_copy(src, s_row, dst, d_row, sem):
    return pltpu.make_async_copy(src.at[pl.ds(s_row, 1)], dst.at[pl.ds(d_row, 1)], sem)


def _scatter_kernel(cnt_ref, pad_ref, pst_ref, nu_ref, dest_ref, x_ref, xs_ref, zrow_ref, zblk_ref, sem,
                    *, tm, n_tok_steps, n_blocks):
    i = pl.program_id(0)

    @pl.when(i < n_tok_steps)
    def _():
        def issue(t, c):
            for k in range(TOP_K):
                _row_copy(x_ref, t, xs_ref, dest_ref[t * TOP_K + k], sem).start()
            return c

        lax.fori_loop(0, tm, issue, 0)

        for k in range(TOP_K):
            pltpu.make_async_copy(x_ref, xs_ref.at[pl.ds(0, tm)], sem).wait()

    @pl.when((i >= n_tok_steps) & (i < n_tok_steps + N_EXPERTS))
    def _():
        e = i - n_tok_steps
        zrow_ref[...] = jnp.zeros_like(zrow_ref)
        lo, hi, base = cnt_ref[e], pad_ref[e], pst_ref[e]

        def issue(r, c):
            pltpu.make_async_copy(zrow_ref, xs_ref.at[pl.ds(base + r, 1)], sem).start()
            return c

        lax.fori_loop(lo, hi, issue, 0)

        def drain(r, c):
            pltpu.make_async_copy(zrow_ref, xs_ref.at[pl.ds(0, 1)], sem).wait()
            return c

        lax.fori_loop(lo, hi, drain, 0)

    @pl.when(i == n_tok_steps + N_EXPERTS)
    def _():
        zblk_ref[...] = jnp.zeros_like(zblk_ref)

        def issue(p, c):
            pltpu.make_async_copy(zblk_ref, xs_ref.at[pl.ds(p * MOE_BLOCK, MOE_BLOCK)], sem).start()
            return c

        lax.fori_loop(nu_ref[0], n_blocks, issue, 0)

        def drain(p, c):
            pltpu.make_async_copy(zblk_ref, xs_ref.at[pl.ds(0, MOE_BLOCK)], sem).wait()
            return c

        lax.fori_loop(nu_ref[0], n_blocks, drain, 0)


def _scatter_rows(x1, dest_flat, counts, padded, pstarts, n_used, n_blocks, tm=128):
    n, d = x1.shape
    tm = min(tm, n)
    n_tok_steps = n // tm
    return pl.pallas_call(
        functools.partial(_scatter_kernel, tm=tm, n_tok_steps=n_tok_steps, n_blocks=n_blocks),
        grid_spec=pltpu.PrefetchScalarGridSpec(
            num_scalar_prefetch=4,
            grid=(n_tok_steps + N_EXPERTS + 1,),
            in_specs=[pl.BlockSpec((tm * TOP_K,), lambda i, *_: (jnp.minimum(i, n_tok_steps - 1),),
                                   memory_space=pltpu.SMEM),
                      pl.BlockSpec((tm, d), lambda i, *_: (jnp.minimum(i, n_tok_steps - 1), 0))],
            out_specs=pl.BlockSpec(memory_space=pl.ANY),
            scratch_shapes=[pltpu.VMEM((1, d), x1.dtype), pltpu.VMEM((MOE_BLOCK, d), x1.dtype),
                            pltpu.SemaphoreType.DMA(())]),
        out_shape=jax.ShapeDtypeStruct((n_blocks * MOE_BLOCK, d), x1.dtype),
        compiler_params=_cparams(("arbitrary",)),
        name="moe_scatter",
    )(counts, padded, pstarts, n_used, dest_flat, x1)


def _expert_kernel(be_ref, nu_ref, xs_ref, wg_ref, wu_ref, wd_ref, y_ref):
    del be_ref
    used = pl.program_id(0) < nu_ref[0]

    @pl.when(used)
    def _():
        lo, hi = _unpack_halves(xs_ref[...])
        lo, hi = lo.astype(BF16), hi.astype(BF16)
        dh = lo.shape[1]
        hg = _dot(lo, wg_ref[:dh, :]) + _dot(hi, wg_ref[dh:, :])
        hu = _dot(lo, wu_ref[:dh, :]) + _dot(hi, wu_ref[dh:, :])
        h = (hg * _sigmoid(hg)) * hu
        y_ref[...] = _pack_halves(_dot(h.astype(BF16), wd_ref[...]))

    @pl.when(jnp.logical_not(used))
    def _():
        y_ref[...] = jnp.zeros_like(y_ref)


def _expert_mlp(xs, block_e, n_used, wg, wu, wd):
    n_rows, dp = xs.shape
    nblk = n_rows // MOE_BLOCK
    d, ff = wg.shape[-2:]
    assert d == 2 * dp

    def row_map(p, be, nu):
        return (jnp.minimum(p, nu[0] - 1), 0)

    return pl.pallas_call(
        _expert_kernel,
        grid_spec=pltpu.PrefetchScalarGridSpec(
            num_scalar_prefetch=2,
            grid=(nblk,),
            in_specs=[pl.BlockSpec((MOE_BLOCK, dp), row_map),
                      pl.BlockSpec((None, d, ff), lambda p, be, nu: (be[p], 0, 0)),
                      pl.BlockSpec((None, d, ff), lambda p, be, nu: (be[p], 0, 0)),
                      pl.BlockSpec((None, ff, d), lambda p, be, nu: (be[p], 0, 0))],
            out_specs=pl.BlockSpec((MOE_BLOCK, dp), lambda p, be, nu: (p, 0))),
        out_shape=jax.ShapeDtypeStruct((n_rows, dp), jnp.uint32),
        compiler_params=_cparams(("arbitrary",)),
        name="moe_experts",
    )(block_e, n_used, xs, wg, wu, wd)


COMBINE_PARTS = 4


def _combine_kernel(dest_ref, x1_ref, gk_ref, sg_ref, su_ref, sd_ref, g_ref, b_ref, ys_ref,
                    o_ref, ybuf_ref, sh_ref, sem, *, tm, alpha):
    tp = tm // COMBINE_PARTS

    def issue(part):
        def body(t, c):
            for k in range(TOP_K):
                _row_copy(ys_ref, dest_ref[t * TOP_K + k], ybuf_ref, k * tm + t, sem.at[part]).start()
            return c

        lax.fori_loop(part * tp, (part + 1) * tp, body, 0)

    issue(0)
    issue(1)
    xb = x1_ref[...].astype(BF16)
    hg = _dot(xb, sg_ref[...])
    hu = _dot(xb, su_ref[...])
    sh_ref[...] = _dot(((hg * _sigmoid(hg)) * hu).astype(BF16), sd_ref[...])

    for part in range(COMBINE_PARTS):
        rows = slice(part * tp, (part + 1) * tp)
        for k in range(TOP_K):
            pltpu.make_async_copy(ys_ref.at[pl.ds(0, tp)], ybuf_ref.at[pl.ds(0, tp)], sem.at[part]).wait()
        if part + 2 < COMBINE_PARTS:
            issue(part + 2)
        gk = gk_ref[rows, :]
        r_lo = jnp.zeros((tp, ybuf_ref.shape[1]), F32)
        r_hi = jnp.zeros((tp, ybuf_ref.shape[1]), F32)
        for k in range(TOP_K):
            lo, hi = _unpack_halves(ybuf_ref[k * tm + part * tp:k * tm + (part + 1) * tp, :])
            r_lo = r_lo + gk[:, k:k + 1] * lo
            r_hi = r_hi + gk[:, k:k + 1] * hi
        routed = jnp.concatenate([r_lo, r_hi], axis=1)
        o_ref[rows, :] = _layer_norm(alpha * x1_ref[rows, :] + (routed + sh_ref[rows, :]), g_ref[...], b_ref[...])


def _combine(x1, dest_flat, gk, ys, sg, su, sd, ln_g, ln_b, alpha, tm=128):
    n, d = x1.shape
    tm = min(tm, n)
    ff = sg.shape[-1]
    return pl.pallas_call(
        functools.partial(_combine_kernel, tm=tm, alpha=alpha),
        grid=(n // tm,),
        in_specs=[pl.BlockSpec((tm * TOP_K,), lambda i: (i,), memory_space=pltpu.SMEM),
                  pl.BlockSpec((tm, d), lambda i: (i, 0)),
                  pl.BlockSpec((tm, LANES), lambda i: (i, 0)),
                  pl.BlockSpec((d, ff), lambda i: (0, 0), pipeline_mode=pl.Buffered(1)),
                  pl.BlockSpec((d, ff), lambda i: (0, 0), pipeline_mode=pl.Buffered(1)),
                  pl.BlockSpec((ff, d), lambda i: (0, 0), pipeline_mode=pl.Buffered(1)),
                  pl.BlockSpec((1, d), lambda i: (0, 0)),
                  pl.BlockSpec((1, d), lambda i: (0, 0)),
                  pl.BlockSpec(memory_space=pl.ANY)],
        out_specs=pl.BlockSpec((tm, d), lambda i: (i, 0)),
        out_shape=jax.ShapeDtypeStruct((n, d), F32),
        scratch_shapes=[pltpu.VMEM((TOP_K * tm, d // 2), jnp.uint32), pltpu.VMEM((tm, d), F32),
                        pltpu.SemaphoreType.DMA((COMBINE_PARTS,))],
        compiler_params=_cparams(("arbitrary",)),
        name="moe_combine_ln2",
    )(dest_flat, x1, gk, sg, su, sd, ln_g, ln_b, ys)


def _rope_kernel(pos_ref, inv_ref, sign_ref, cos_ref, sin_ref):
    ang = pos_ref[...].astype(F32) * inv_ref[...]
    cos_ref[...] = jnp.cos(ang)
    sin_ref[...] = jnp.sin(ang) * sign_ref[...]


def _rope_tables(positions, dh, tm=512):
    half = dh // 2
    inv = ROPE_THETA ** (-jnp.arange(half, dtype=F32) * 2.0 / dh)
    reps = LANES // dh
    inv_l = jnp.tile(jnp.concatenate([inv, inv]), reps).reshape(1, LANES)
    sign_l = jnp.tile(jnp.concatenate([-jnp.ones((half,), F32), jnp.ones((half,), F32)]), reps).reshape(1, LANES)
    pos = positions.reshape(-1, 1)
    n = pos.shape[0]
    tm = min(tm, n)
    one = pl.BlockSpec((1, LANES), lambda i: (0, 0))
    blk = pl.BlockSpec((tm, LANES), lambda i: (i, 0))
    return pl.pallas_call(
        _rope_kernel,
        grid=(n // tm,),
        in_specs=[pl.BlockSpec((tm, 1), lambda i: (i, 0)), one, one],
        out_specs=[blk, blk],
        out_shape=[jax.ShapeDtypeStruct((n, LANES), F32)] * 2,
        compiler_params=_cparams(("parallel",)),
        name="rope_tables",
    )(pos, inv_l, sign_l)


def _dup_heads(w_rows, heads, dh):
    d = w_rows.shape[1]
    w3 = w_rows.reshape(heads, dh, d)
    return jnp.concatenate([w3, w3], axis=1).reshape(heads * 2 * dh, d)


def _hybrid_layer(x, positions, w_in, a_sinks, cmp_k_pos, cmp_k_w1, cmp_k_w2, cmp_v_pos, cmp_v_w1,
                  cmp_v_w2, w_o, ln1_g, ln1_b, w_router, router_bias, exp_w_gate, exp_w_up, exp_w_down,
                  sh_w_gate, sh_w_up, sh_w_down, ln2_g, ln2_b, alpha):
    bsz, seq, d = x.shape
    n = bsz * seq
    assert seq % NSA_TK == 0 and (seq // CMP_STRIDE) % LANES == 0 and A_WINDOW <= TQ
    assert 3 <= seq // SLC_LEN <= LANES
    xf = x.reshape(n, d)
    xb = xf.astype(BF16)

    qa_w, kva = A_HEADS * A_HEAD_DIM, A_KV_HEADS * A_HEAD_DIM
    qb_w, kvb = B_HEADS * B_HEAD_DIM, B_KV_HEADS * B_HEAD_DIM
    widths = [qa_w, kva, kva, qb_w, kvb, kvb, kvb, kvb, kvb, kvb, 3 * B_HEADS, d, d]
    offs = np.concatenate([[0], np.cumsum(widths)]).tolist()
    (o_qa, o_ka, o_va, o_qb, o_kc, o_vc, o_ks, o_vs, o_kw, o_vw, o_gn, o_ga, o_end) = (
        offs[i] for i in (0, 1, 2, 3, 4, 5, 6, 7, 8, 9, 10, 11, 13))
    w_t = w_in.T
    blk = lambda start, width: list(range(start // PROJ_TN, (start + width) // PROJ_TN))
    assert all(o % PROJ_TN == 0 for o in (o_qa, o_qb, o_kc, o_vc, o_ks, o_vs, o_kw, o_vw))
    rep = B_HEADS // B_KV_HEADS
    gidx = np.zeros((B_KV_HEADS, LANES), np.int32)
    gmask = np.zeros((B_KV_HEADS, LANES), np.float32)
    for g in range(B_KV_HEADS):
        for br in range(3):
            for r in range(rep):
                gidx[g, br * rep + r] = o_gn + br * B_HEADS + g * rep + r
                gmask[g, br * rep + r] = 1.0
    w_gn_x = w_t[gidx.reshape(-1)] * jnp.asarray(gmask.reshape(-1, 1))
    w_sig = jnp.concatenate([w_gn_x, w_t[o_ga:o_end]], axis=0).astype(BF16)
    w_ka2 = _dup_heads(w_t[o_ka:o_va], A_KV_HEADS, A_HEAD_DIM).astype(BF16)
    w_va2 = _dup_heads(w_t[o_va:o_qb], A_KV_HEADS, A_HEAD_DIM).astype(BF16)
    all_blocks = lambda w: list(range(w.shape[0] // PROJ_TN))

    cos64, sin64 = _rope_tables(positions, A_HEAD_DIM)
    cos128, sin128 = _rope_tables(positions, B_HEAD_DIM)

    q_a = _project(xb, w_t, blk(o_qa, qa_w), cos64, sin64, "rope64", BF16,
                   q_cols=qa_w, q_scale=A_HEAD_DIM ** -0.5 * LOG2E)
    k_a = _project(xb, w_ka2, all_blocks(w_ka2), cos64, sin64, "rope64", BF16)
    v_a = _project(xb, w_va2, all_blocks(w_va2), cos64, sin64, "none", BF16)
    qk_b = _project(xb, w_t, blk(o_qb, qb_w) + blk(o_ks, kvb) + blk(o_kw, kvb), cos128, sin128, "rope128", BF16,
                    q_cols=qb_w, q_scale=B_HEAD_DIM ** -0.5 * LOG2E)
    v_b = _project(xb, w_t, blk(o_vs, kvb) + blk(o_vw, kvb), cos128, sin128, "none", BF16)
    kc = _project(xb, w_t, blk(o_kc, kvb), cos128, sin128, "rope128", F32)
    vc = _project(xb, w_t, blk(o_vc, kvb), cos128, sin128, "none", F32)
    gates = _project(xb, w_sig, all_blocks(w_sig), cos128, sin128, "sigmoid", F32)

    o_a = _swa_attention(q_a, k_a, v_a, a_sinks, bsz, seq)

    kcmp = _compress(kc, 0, cmp_k_pos, cmp_k_w1.astype(BF16), cmp_k_w2.astype(BF16), bsz, seq)
    vcmp = _compress(vc, 0, cmp_v_pos, cmp_v_w1.astype(BF16), cmp_v_w2.astype(BF16), bsz, seq)
    nsa_cols = dict(q=0, ks=qb_w, kw=qb_w + kvb, vs=0, vw=kvb, gate=0)
    o_b = _nsa_attention(qk_b, v_b, gates, kcmp, vcmp, bsz, seq, nsa_cols)

    w_r = jnp.pad(w_router, ((0, 0), (0, LANES - N_EXPERTS))).astype(BF16)
    x1, x1p, scores = _out_proj(o_a, o_b, gates, B_KV_HEADS * LANES, B_KV_HEADS * LANES + d, w_o.astype(BF16), xf,
                           ln1_g.reshape(1, d), ln1_b.reshape(1, d), w_r, alpha)

    bias = jnp.pad(router_bias.astype(F32), (0, LANES - N_EXPERTS)).reshape(1, LANES)
    ek, rk, gk, cnt = _route(scores, bias)

    counts = cnt[0, :N_EXPERTS].astype(jnp.int32)
    padded = (counts + MOE_BLOCK - 1) // MOE_BLOCK * MOE_BLOCK
    pends = jnp.cumsum(padded)
    pstarts = pends - padded
    nblk = -(-(n * TOP_K) // MOE_BLOCK) + N_EXPERTS
    blk_start = jnp.arange(nblk, dtype=jnp.int32) * MOE_BLOCK
    block_e = jnp.minimum(jnp.sum((pends[None, :] <= blk_start[:, None]).astype(jnp.int32), axis=1),
                          N_EXPERTS - 1)
    n_used = (pends[-1:] // MOE_BLOCK).astype(jnp.int32)
    dest = (pstarts[ek[:, :TOP_K]] + rk[:, :TOP_K]).astype(jnp.int32).reshape(-1)

    xs = _scatter_rows(x1p, dest, counts, padded, pstarts.astype(jnp.int32), n_used, nblk)
    ys = _expert_mlp(xs, block_e, n_used, exp_w_gate.astype(BF16), exp_w_up.astype(BF16),
                     exp_w_down.astype(BF16))
    out = _combine(x1, dest, gk, ys, sh_w_gate.astype(BF16), sh_w_up.astype(BF16), sh_w_down.astype(BF16),
                   ln2_g.reshape(1, d), ln2_b.reshape(1, d), alpha)
    return out.reshape(bsz, seq, d)


def kernel(x, positions, w_in, a_sinks, cmp_k_pos, cmp_k_w1, cmp_k_w2, cmp_v_pos, cmp_v_w1, cmp_v_w2,
           w_o, ln1_g, ln1_b, w_router, router_bias, exp_w_gate, exp_w_up, exp_w_down,
           sh_w_gate, sh_w_up, sh_w_down, ln2_g, ln2_b):
    depth = w_in.shape[0]
    alpha = (2.0 * depth) ** 0.25
    for l in range(depth):
        x = _hybrid_layer(x, positions, w_in[l], a_sinks[l], cmp_k_pos[l], cmp_k_w1[l], cmp_k_w2[l],
                          cmp_v_pos[l], cmp_v_w1[l], cmp_v_w2[l], w_o[l], ln1_g[l], ln1_b[l],
                          w_router[l], router_bias[l], exp_w_gate[l], exp_w_up[l], exp_w_down[l],
                          sh_w_gate[l], sh_w_up[l], sh_w_down[l], ln2_g[l], ln2_b[l], alpha)
    return x
```

```python
import functools

import jax
import jax.numpy as jnp
import numpy as np
from jax import lax
from jax.experimental import pallas as pl
from jax.experimental.pallas import tpu as pltpu

F32 = jnp.float32
BF16 = jnp.bfloat16

ROPE_THETA = 10000.0
LN_EPS = 1e-5
NEG_INF = -1e30

A_HEADS, A_KV_HEADS, A_HEAD_DIM, A_WINDOW = 64, 8, 64, 128
B_HEADS, B_KV_HEADS, B_HEAD_DIM = 32, 4, 128
CMP_LEN, CMP_STRIDE, CMP_HIDDEN = 32, 16, 256
SLC_LEN, SLC_TOPK, WIN_LEN = 64, 16, 512
N_EXPERTS, TOP_K, EXPERT_FF, SHARED_FF = 64, 8, 512, 512
ROUTED_SCALE = 2.5
MOE_BLOCK = 256

LOG2E = 1.4426950408889634
LANES = 128
TQ = 128
VMEM_LIMIT = 56 * 1024 * 1024


def _cparams(sem):
    return pltpu.CompilerParams(dimension_semantics=sem, vmem_limit_bytes=VMEM_LIMIT)


def _dot(a, b):
    return jnp.dot(a, b, preferred_element_type=F32)


def _dot_nt(a, b):
    return lax.dot_general(a, b, (((1,), (1,)), ((), ())), preferred_element_type=F32)


def _sigmoid(v):
    return 1.0 / (1.0 + jnp.exp(-v))


def _pack_halves(v):
    h = v.shape[1] // 2
    lo = lax.bitcast_convert_type(v[:, :h].astype(BF16).astype(F32), jnp.uint32)
    hi = lax.bitcast_convert_type(v[:, h:].astype(BF16).astype(F32), jnp.uint32)
    return (hi & jnp.uint32(0xFFFF0000)) | (lo >> 16)


def _unpack_halves(w):
    lo = lax.bitcast_convert_type(w << 16, F32)
    hi = lax.bitcast_convert_type(w & jnp.uint32(0xFFFF0000), F32)
    return lo, hi


def _layer_norm(z, g, b):
    mu = jnp.mean(z, axis=-1, keepdims=True)
    zc = z - mu
    var = jnp.mean(zc * zc, axis=-1, keepdims=True)
    return zc * lax.rsqrt(var + LN_EPS) * g + b


def _proj_kernel(tbl_ref, x_ref, w_ref, cos_ref, sin_ref, o_ref, *scratch, epi, q_blocks, q_scale):
    del tbl_ref
    if scratch:
        (wb_ref,) = scratch

        @pl.when(pl.program_id(1) == 0)
        def _():
            wb_ref[...] = w_ref[...].astype(BF16)

        w = wb_ref[...]
    else:
        w = w_ref[...]
    acc = _dot_nt(x_ref[...], w)
    if epi == "none":
        o_ref[...] = acc.astype(o_ref.dtype)
    elif epi == "sigmoid":
        o_ref[...] = _sigmoid(acc).astype(o_ref.dtype)
    else:
        cs = jnp.where(pl.program_id(0) < q_blocks, q_scale, 1.0)
        cos = cos_ref[...] * cs
        sin = sin_ref[...] * cs
        lane = lax.broadcasted_iota(jnp.int32, (1, LANES), 1)
        for c in range(acc.shape[1] // LANES):
            y = acc[:, c * LANES:(c + 1) * LANES]
            if epi == "rope128":
                rot = pltpu.roll(y, 64, 1)
            else:
                rot = jnp.where((lane % 64) < 32, pltpu.roll(y, 96, 1), pltpu.roll(y, 32, 1))
            o_ref[:, c * LANES:(c + 1) * LANES] = (y * cos + rot * sin).astype(o_ref.dtype)


PROJ_TN = 512


def _project(xb, w_t, blocks, cos, sin, epi, out_dtype, q_cols=0, q_scale=1.0, tm=1024):
    n, d = xb.shape
    tn = PROJ_TN
    ncol = len(blocks) * tn
    tm = min(tm, n)
    assert n % tm == 0 and q_cols % tn == 0 and (max(blocks) + 1) * tn <= w_t.shape[0]
    cast = w_t.dtype != BF16
    return pl.pallas_call(
        functools.partial(_proj_kernel, epi=epi, q_blocks=q_cols // tn, q_scale=q_scale),
        grid_spec=pltpu.PrefetchScalarGridSpec(
            num_scalar_prefetch=1,
            grid=(ncol // tn, n // tm),
            in_specs=[pl.BlockSpec((tm, d), lambda j, i, tbl: (i, 0)),
                      pl.BlockSpec((tn, d), lambda j, i, tbl: (tbl[j], 0)),
                      pl.BlockSpec((tm, LANES), lambda j, i, tbl: (i, 0)),
                      pl.BlockSpec((tm, LANES), lambda j, i, tbl: (i, 0))],
            out_specs=pl.BlockSpec((tm, tn), lambda j, i, tbl: (i, j)),
            scratch_shapes=[pltpu.VMEM((tn, d), BF16)] if cast else []),
        out_shape=jax.ShapeDtypeStruct((n, ncol), out_dtype),
        compiler_params=_cparams(("arbitrary", "arbitrary")),
        name="proj_" + epi,
    )(jnp.asarray(blocks, jnp.int32), xb, w_t, cos, sin)


SWA_GROUPS = 2


def _swa_kernel(sink_ref, q_ref, kp_ref, kc_ref, vp_ref, vc_ref, o_ref, s_ref, p_ref):
    n = pl.program_id(1)
    lane = lax.broadcasted_iota(jnp.int32, (1, LANES), 1)
    lo = lane < A_HEAD_DIM
    nk = 2 * TQ
    qpos = n * TQ + lax.broadcasted_iota(jnp.int32, (TQ, nk), 0)
    kpos = (n - 1) * TQ + lax.broadcasted_iota(jnp.int32, (TQ, nk), 1)
    delta = qpos - kpos
    mask = (kpos >= 0) & (delta >= 0) & (delta < A_WINDOW)
    rep = A_HEADS // A_KV_HEADS
    npair = rep // 2
    zq = jnp.zeros((TQ, LANES), q_ref.dtype)
    for gg in range(SWA_GROUPS):
        g = pl.program_id(2) * SWA_GROUPS + gg
        gl = slice(gg * LANES, (gg + 1) * LANES)
        kk = jnp.concatenate([kp_ref[:, gl], kc_ref[:, gl]], axis=0)
        vv = jnp.concatenate([vp_ref[:, gl], vc_ref[:, gl]], axis=0)
        zv = jnp.zeros_like(vv)
        v_halves = (jnp.where(lo, vv, zv), jnp.where(lo, zv, vv))
        qcols = [slice((gg * npair + c) * LANES, (gg * npair + c + 1) * LANES) for c in range(npair)]
        q_lo = [jnp.where(lo, q_ref[:, c], zq) for c in qcols]
        q_hi = [jnp.where(lo, zq, q_ref[:, c]) for c in qcols]
        r0 = gg * rep * TQ
        s_ref[r0:r0 + rep * TQ, :] = _dot_nt(jnp.concatenate(q_lo + q_hi, axis=0), kk)
        for half in range(2):
            for pair in range(npair):
                sl = slice(r0 + (half * npair + pair) * TQ, r0 + (half * npair + pair + 1) * TQ)
                s = jnp.where(mask, s_ref[sl, :], NEG_INF)
                sk = sink_ref[g * rep + pair * 2 + half] * LOG2E
                m = jnp.maximum(jnp.max(s, axis=-1, keepdims=True), sk)
                e = jnp.exp2(s - m)
                p = e / (jnp.sum(e, axis=-1, keepdims=True) + jnp.exp2(sk - m))
                p_ref[sl, :] = p.astype(BF16)
        half_rows = npair * TQ
        out = (_dot(p_ref[r0:r0 + half_rows, :], v_halves[0])
               + _dot(p_ref[r0 + half_rows:r0 + 2 * half_rows, :], v_halves[1]))
        for pair, c in enumerate(qcols):
            o_ref[:, c] = out[pair * TQ:(pair + 1) * TQ, :]


def _swa_attention(q, k, v, sinks, bsz, seq):
    n = bsz * seq
    nb = seq // TQ
    rows = SWA_GROUPS * (A_HEADS // A_KV_HEADS) * TQ
    qw = SWA_GROUPS * (A_HEADS // A_KV_HEADS) * A_HEAD_DIM
    kw = SWA_GROUPS * LANES
    return pl.pallas_call(
        _swa_kernel,
        grid=(bsz, nb, A_KV_HEADS // SWA_GROUPS),
        in_specs=[pl.BlockSpec(memory_space=pltpu.SMEM),
                  pl.BlockSpec((TQ, qw), lambda b, i, g: (b * nb + i, g)),
                  pl.BlockSpec((TQ, kw), lambda b, i, g: (b * nb + jnp.maximum(i - 1, 0), g)),
                  pl.BlockSpec((TQ, kw), lambda b, i, g: (b * nb + i, g)),
                  pl.BlockSpec((TQ, kw), lambda b, i, g: (b * nb + jnp.maximum(i - 1, 0), g)),
                  pl.BlockSpec((TQ, kw), lambda b, i, g: (b * nb + i, g))],
        out_specs=pl.BlockSpec((TQ, qw), lambda b, i, g: (b * nb + i, g)),
        out_shape=jax.ShapeDtypeStruct((n, A_HEADS * A_HEAD_DIM), F32),
        scratch_shapes=[pltpu.VMEM((rows, 2 * TQ), F32), pltpu.VMEM((rows, 2 * TQ), BF16)],
        compiler_params=_cparams(("parallel", "parallel", "parallel")),
        name="swa",
    )(sinks, q, k, k, v, v)


def _compress_kernel(t_ref, pe_ref, w1_ref, w2_ref, o_ref):
    nch = o_ref.shape[0]
    u = jnp.zeros((nch, CMP_HIDDEN), F32)
    v = jnp.zeros((nch, CMP_HIDDEN), F32)
    for l in range(CMP_STRIDE):
        t_l = t_ref[pl.ds(l, nch, stride=CMP_STRIDE), :]
        a = (t_l + pe_ref[l:l + 1, :]).astype(BF16)
        b = (t_l + pe_ref[CMP_STRIDE + l:CMP_STRIDE + l + 1, :]).astype(BF16)
        u = u + _dot(a, w1_ref[l * LANES:(l + 1) * LANES, :])
        v = v + _dot(b, w1_ref[(CMP_STRIDE + l) * LANES:(CMP_STRIDE + l + 1) * LANES, :])
    pre = u + pltpu.roll(v, nch - 1, 0)
    h = jax.nn.gelu(pre)
    o_ref[...] = _dot(h.astype(BF16), w2_ref[...]).astype(o_ref.dtype)


def _compress(t, col0, pe, w1, w2, bsz, seq):
    nch = seq // CMP_STRIDE
    cb0 = col0 // LANES
    return pl.pallas_call(
        _compress_kernel,
        grid=(bsz, B_KV_HEADS),
        in_specs=[pl.BlockSpec((seq, LANES), lambda b, g: (b, cb0 + g)),
                  pl.BlockSpec((CMP_LEN, LANES), lambda b, g: (0, 0)),
                  pl.BlockSpec((CMP_LEN * LANES, CMP_HIDDEN), lambda b, g: (0, 0)),
                  pl.BlockSpec((CMP_HIDDEN, LANES), lambda b, g: (0, 0))],
        out_specs=pl.BlockSpec((None, None, nch, LANES), lambda b, g: (b, g, 0, 0)),
        out_shape=jax.ShapeDtypeStruct((bsz, B_KV_HEADS, nch, LANES), BF16),
        compiler_params=_cparams(("parallel", "parallel")),
        name="nsa_compress",
    )(t, pe, w1, w2)


NSA_TK = 512
NSA_NKW = (-(-(WIN_LEN - 1) // TQ) + 1) * TQ


def _nsa_kernel(q_ref, gate_ref, kcmp_ref, vcmp_ref, ks_ref, vs_ref, kw_ref, vw_ref, o_ref,
                qs_ref, s_ref, p_ref, m_ref, l_ref, acc_ref, ocmp_ref, *, seq):
    qi = pl.program_id(2)
    rep = B_HEADS // B_KV_HEADS
    nch = seq // CMP_STRIDE
    ns = seq // SLC_LEN
    n_sel = min(SLC_TOPK, ns)
    heads = [slice(r * TQ, (r + 1) * TQ) for r in range(rep)]
    for r in range(rep):
        qs_ref[heads[r], :] = q_ref[:, r * LANES:(r + 1) * LANES]
    tpos = qi * TQ + lax.broadcasted_iota(jnp.int32, (TQ, 1), 0)

    s_ref[:, :nch] = _dot_nt(qs_ref[...], kcmp_ref[...])
    c_end = lax.broadcasted_iota(jnp.int32, (1, nch), 1) * CMP_STRIDE + (CMP_LEN - 1)
    valid = c_end <= tpos
    p_grp = jnp.zeros((TQ, nch), F32)
    for sl in heads:
        s = jnp.where(valid, s_ref[sl, :nch], NEG_INF)
        m = jnp.max(s, axis=-1, keepdims=True)
        e = jnp.where(valid, jnp.exp2(s - m), 0.0)
        den = jnp.sum(e, axis=-1, keepdims=True)
        p = e / jnp.where(den > 0, den, 1.0)
        p_grp = p_grp + p
        p_ref[sl, :nch] = p.astype(BF16)
    ocmp_ref[...] = _dot(p_ref[:, :nch], vcmp_ref[...])

    jrow = lax.broadcasted_iota(jnp.int32, (ns, 1), 0)
    d = lax.broadcasted_iota(jnp.int32, (ns, nch), 1) - (SLC_LEN // CMP_STRIDE) * jrow
    msel = jnp.where((d == -1) | (d == 3), 1.0, jnp.where((d >= 0) & (d <= 2), 2.0, 0.0)).astype(BF16)
    p_hi = p_grp.astype(BF16)
    r1 = p_grp - p_hi.astype(F32)
    p_mid = r1.astype(BF16)
    p_lo = (r1 - p_mid.astype(F32)).astype(BF16)
    p_slc = _dot_nt(msel, p_hi) + _dot_nt(msel, p_mid) + _dot_nt(msel, p_lo)

    tpos_l = qi * TQ + lax.broadcasted_iota(jnp.int32, (1, TQ), 1)
    cur = tpos_l // SLC_LEN
    forced = (jrow == 0) | (jrow == cur) | (jrow == cur - 1)
    score = jnp.where(forced, 1e9, jnp.where(jrow * SLC_LEN <= tpos_l, p_slc, NEG_INF))
    rank = jnp.zeros((ns, TQ), jnp.int32)
    for i in range(ns):
        si = score[i:i + 1, :]
        rank = rank + ((si > score) | ((si == score) & (jrow > i))).astype(jnp.int32)
    sel_t = (rank < n_sel).astype(F32)
    if ns < LANES:
        sel_t = jnp.concatenate([sel_t, jnp.zeros((LANES - ns, TQ), F32)], axis=0)
    sel = sel_t.T.astype(BF16)

    m_ref[...] = jnp.full_like(m_ref, NEG_INF)
    l_ref[...] = jnp.zeros_like(l_ref)
    acc_ref[...] = jnp.zeros_like(acc_ref)
    jexp = lax.broadcasted_iota(jnp.int32, (LANES, 1), 0)

    def slc_body(kt, c):
        row0 = pl.multiple_of(kt * NSA_TK, NSA_TK)
        s_ref[:, :NSA_TK] = _dot_nt(qs_ref[...], ks_ref[pl.ds(row0, NSA_TK), :])
        kpos = kt * NSA_TK + lax.broadcasted_iota(jnp.int32, (1, NSA_TK), 1)
        expand = (jexp == kpos // SLC_LEN).astype(BF16)
        keep = (_dot(sel, expand) > 0.5) & (kpos <= tpos)
        for sl in heads:
            s = jnp.where(keep, s_ref[sl, :NSA_TK], NEG_INF)
            m_old = m_ref[sl, :]
            m_new = jnp.maximum(m_old, jnp.max(s, axis=-1, keepdims=True))
            alpha = jnp.exp2(m_old - m_new)
            e = jnp.exp2(s - m_new)
            l_ref[sl, :] = alpha * l_ref[sl, :] + jnp.sum(e, axis=-1, keepdims=True)
            m_ref[sl, :] = m_new
            acc_ref[sl, :] = alpha * acc_ref[sl, :]
            p_ref[sl, :NSA_TK] = e.astype(BF16)
        acc_ref[...] += _dot(p_ref[:, :NSA_TK], vs_ref[pl.ds(row0, NSA_TK), :])
        return c

    lax.fori_loop(0, (qi * TQ + TQ + NSA_TK - 1) // NSA_TK, slc_body, 0)

    nprev = NSA_NKW // TQ - 1
    k_parts, v_parts = [], []
    for i in range(nprev + 1):
        row0 = pl.multiple_of(jnp.maximum(qi - nprev + i, 0) * TQ, TQ)
        k_parts.append(kw_ref[pl.ds(row0, TQ), :])
        v_parts.append(vw_ref[pl.ds(row0, TQ), :])
    s_ref[:, :NSA_NKW] = _dot_nt(qs_ref[...], jnp.concatenate(k_parts, axis=0))
    kposw = (qi - nprev) * TQ + lax.broadcasted_iota(jnp.int32, (1, NSA_NKW), 1)
    dlt = tpos - kposw
    keepw = (kposw >= 0) & (dlt >= 0) & (dlt < WIN_LEN)
    for sl in heads:
        s = jnp.where(keepw, s_ref[sl, :NSA_NKW], NEG_INF)
        e = jnp.exp2(s - jnp.max(s, axis=-1, keepdims=True))
        p_ref[sl, :NSA_NKW] = (e / jnp.sum(e, axis=-1, keepdims=True)).astype(BF16)
    o_win = _dot(p_ref[:, :NSA_NKW], jnp.concatenate(v_parts, axis=0))

    gate = gate_ref[...]
    for r, sl in enumerate(heads):
        o_ref[:, r * LANES:(r + 1) * LANES] = (gate[:, r:r + 1] * ocmp_ref[sl, :]
                                               + gate[:, rep + r:rep + r + 1] * (acc_ref[sl, :] / l_ref[sl, :])
                                               + gate[:, 2 * rep + r:2 * rep + r + 1] * o_win[sl])


def _nsa_attention(qk, v, gates, kcmp, vcmp, bsz, seq, cols):
    n = bsz * seq
    nq = seq // TQ
    nch = seq // CMP_STRIDE
    rep = B_HEADS // B_KV_HEADS
    qb0 = cols["q"] // (rep * LANES)
    ksb, kwb, vsb, vwb, gb0 = (cols[k] // LANES for k in ("ks", "kw", "vs", "vw", "gate"))
    return pl.pallas_call(
        functools.partial(_nsa_kernel, seq=seq),
        grid=(bsz, B_KV_HEADS, nq),
        in_specs=[pl.BlockSpec((TQ, rep * LANES), lambda b, g, i: (b * nq + i, qb0 + g)),
                  pl.BlockSpec((TQ, LANES), lambda b, g, i: (b * nq + i, gb0 + g)),
                  pl.BlockSpec((None, None, nch, LANES), lambda b, g, i: (b, g, 0, 0)),
                  pl.BlockSpec((None, None, nch, LANES), lambda b, g, i: (b, g, 0, 0)),
                  pl.BlockSpec((seq, LANES), lambda b, g, i: (b, ksb + g)),
                  pl.BlockSpec((seq, LANES), lambda b, g, i: (b, vsb + g)),
                  pl.BlockSpec((seq, LANES), lambda b, g, i: (b, kwb + g)),
                  pl.BlockSpec((seq, LANES), lambda b, g, i: (b, vwb + g))],
        out_specs=pl.BlockSpec((TQ, rep * LANES), lambda b, g, i: (b * nq + i, g)),
        out_shape=jax.ShapeDtypeStruct((n, B_HEADS * B_HEAD_DIM), F32),
        scratch_shapes=[pltpu.VMEM((rep * TQ, LANES), BF16),
                        pltpu.VMEM((rep * TQ, max(NSA_NKW, NSA_TK, nch)), F32),
                        pltpu.VMEM((rep * TQ, max(NSA_NKW, NSA_TK, nch)), BF16),
                        pltpu.VMEM((rep * TQ, 1), F32), pltpu.VMEM((rep * TQ, 1), F32),
                        pltpu.VMEM((rep * TQ, LANES), F32), pltpu.VMEM((rep * TQ, LANES), F32)],
        compiler_params=_cparams(("parallel", "parallel", "arbitrary")),
        name="nsa",
    )(qk, gates, kcmp, vcmp, qk, v, qk, v)


def _oproj_kernel(oa_ref, ob_ref, ga_ref, gb_ref, wo_ref, x_ref, g_ref, b_ref, wr_ref,
                  x1_ref, x1p_ref, sc_ref, acc_ref, *, alpha):
    k = pl.program_id(1)

    @pl.when(k == 0)
    def _():
        acc_ref[...] = jnp.zeros_like(acc_ref)

    y = (ga_ref[...] * oa_ref[...] + gb_ref[...] * ob_ref[...]).astype(BF16)
    acc_ref[...] += _dot(y, wo_ref[...])

    @pl.when(k == pl.num_programs(1) - 1)
    def _():
        piece = min(128, acc_ref.shape[0])
        for r0 in range(0, acc_ref.shape[0], piece):
            rows = slice(r0, r0 + piece)
            x1 = _layer_norm(alpha * x_ref[rows, :] + acc_ref[rows, :], g_ref[...], b_ref[...])
            x1_ref[rows, :] = x1
            x1p_ref[rows, :] = _pack_halves(x1)
            sc_ref[rows, :] = _sigmoid(_dot(x1.astype(BF16), wr_ref[...]))


def _out_proj(o_a, o_b, gates, ga_col0, gb_col0, w_o, x, ln_g, ln_b, w_r, alpha, tm=512, tk=256):
    n, d = x.shape
    tm = min(tm, n)
    ga0, gb0 = ga_col0 // tk, gb_col0 // tk
    return pl.pallas_call(
        functools.partial(_oproj_kernel, alpha=alpha),
        grid=(n // tm, d // tk),
        in_specs=[pl.BlockSpec((tm, tk), lambda i, k: (i, k)),
                  pl.BlockSpec((tm, tk), lambda i, k: (i, k)),
                  pl.BlockSpec((tm, tk), lambda i, k: (i, ga0 + k)),
                  pl.BlockSpec((tm, tk), lambda i, k: (i, gb0 + k)),
                  pl.BlockSpec((tk, d), lambda i, k: (k, 0)),
                  pl.BlockSpec((tm, d), lambda i, k: (i, 0), pipeline_mode=pl.Buffered(1)),
                  pl.BlockSpec((1, d), lambda i, k: (0, 0)),
                  pl.BlockSpec((1, d), lambda i, k: (0, 0)),
                  pl.BlockSpec((d, LANES), lambda i, k: (0, 0))],
        out_specs=[pl.BlockSpec((tm, d), lambda i, k: (i, 0)),
                   pl.BlockSpec((tm, d // 2), lambda i, k: (i, 0)),
                   pl.BlockSpec((tm, LANES), lambda i, k: (i, 0))],
        out_shape=[jax.ShapeDtypeStruct((n, d), F32), jax.ShapeDtypeStruct((n, d // 2), jnp.uint32),
                   jax.ShapeDtypeStruct((n, LANES), F32)],
        scratch_shapes=[pltpu.VMEM((tm, d), F32)],
        compiler_params=_cparams(("parallel", "arbitrary")),
        name="oproj_ln1_router",
    )(o_a, o_b, gates, gates, w_o, x, ln_g, ln_b, w_r)


def _route_kernel(sc_ref, bias_ref, ek_ref, rk_ref, gk_ref, cnt_ref, run_ref):
    i = pl.program_id(0)
    tm = sc_ref.shape[0]

    @pl.when(i == 0)
    def _():
        run_ref[...] = jnp.zeros_like(run_ref)

    sc = sc_ref[...]
    lane = lax.broadcasted_iota(jnp.int32, (1, LANES), 1)
    is_e = lane < N_EXPERTS
    biased = jnp.where(is_e, sc + bias_ref[...], -jnp.inf)
    rank = jnp.zeros((tm, LANES), jnp.int32)
    for e in range(N_EXPERTS):
        be = biased[:, e:e + 1]
        rank = rank + ((be > biased) | ((be == biased) & (e < lane))).astype(jnp.int32)
    sel = (rank < TOP_K) & is_e
    sv = jnp.where(sel, sc, 0.0)
    gates = sv / jnp.sum(sv, axis=-1, keepdims=True) * ROUTED_SCALE
    selb = sel.astype(BF16)
    ri = lax.broadcasted_iota(jnp.int32, (tm, tm), 0)
    ci = lax.broadcasted_iota(jnp.int32, (tm, tm), 1)
    cum = _dot((ci < ri).astype(BF16), selb) + run_ref[...]
    run_ref[...] += jnp.sum(sel.astype(F32), axis=0, keepdims=True)
    cnt_ref[...] = run_ref[...]
    ai = lax.broadcasted_iota(jnp.int32, (LANES, LANES), 0)
    bi = lax.broadcasted_iota(jnp.int32, (LANES, LANES), 1)
    before = _dot(selb, (ai < bi).astype(BF16))
    lane_f = lane.astype(F32)
    ek = jnp.zeros((tm, LANES), F32)
    rk = jnp.zeros((tm, LANES), F32)
    gk = jnp.zeros((tm, LANES), F32)
    for k in range(TOP_K):
        oh = sel & (before == k)
        ek = jnp.where(lane == k, jnp.sum(jnp.where(oh, lane_f, 0.0), axis=-1, keepdims=True), ek)
        rk = jnp.where(lane == k, jnp.sum(jnp.where(oh, cum, 0.0), axis=-1, keepdims=True), rk)
        gk = jnp.where(lane == k, jnp.sum(jnp.where(oh, gates, 0.0), axis=-1, keepdims=True), gk)
    ek_ref[...] = ek.astype(jnp.int32)
    rk_ref[...] = rk.astype(jnp.int32)
    gk_ref[...] = gk


def _route(scores, bias, tm=256):
    n = scores.shape[0]
    tm = min(tm, n)
    blk = pl.BlockSpec((tm, LANES), lambda i: (i, 0))
    one = pl.BlockSpec((1, LANES), lambda i: (0, 0))
    return pl.pallas_call(
        _route_kernel,
        grid=(n // tm,),
        in_specs=[blk, one],
        out_specs=[blk, blk, blk, one],
        out_shape=[jax.ShapeDtypeStruct((n, LANES), jnp.int32), jax.ShapeDtypeStruct((n, LANES), jnp.int32),
                   jax.ShapeDtypeStruct((n, LANES), F32), jax.ShapeDtypeStruct((1, LANES), F32)],
        scratch_shapes=[pltpu.VMEM((1, LANES), F32)],
        compiler_params=_cparams(("arbitrary",)),
        name="route",
    )(scores, bias)


def _row_copy(src, s_row, dst, d_row, sem):
    return pltpu.make_async_copy(src.at[pl.ds(s_row, 1)], dst.at[pl.ds(d_row, 1)], sem)


def _scatter_kernel(cnt_ref, pad_ref, pst_ref, nu_ref, dest_ref, x_ref, xs_ref, zrow_ref, zblk_ref, sem,
                    *, tm, n_tok_steps, n_blocks):
    i = pl.program_id(0)

    @pl.when(i < n_tok_steps)
    def _():
        def issue(t, c):
            for k in range(TOP_K):
                _row_copy(x_ref, t, xs_ref, dest_ref[t * TOP_K + k], sem).start()
            return c

        lax.fori_loop(0, tm, issue, 0)

        for k in range(TOP_K):
            pltpu.make_async_copy(x_ref, xs_ref.at[pl.ds(0, tm)], sem).wait()

    @pl.when((i >= n_tok_steps) & (i < n_tok_steps + N_EXPERTS))
    def _():
        e = i - n_tok_steps
        zrow_ref[...] = jnp.zeros_like(zrow_ref)
        lo, hi, base = cnt_ref[e], pad_ref[e], pst_ref[e]

        def issue(r, c):
            pltpu.make_async_copy(zrow_ref, xs_ref.at[pl.ds(base + r, 1)], sem).start()
            return c

        lax.fori_loop(lo, hi, issue, 0)

        def drain(r, c):
            pltpu.make_async_copy(zrow_ref, xs_ref.at[pl.ds(0, 1)], sem).wait()
            return c

        lax.fori_loop(lo, hi, drain, 0)

    @pl.when(i == n_tok_steps + N_EXPERTS)
    def _():
        zblk_ref[...] = jnp.zeros_like(zblk_ref)

        def issue(p, c):
            pltpu.make_async_copy(zblk_ref, xs_ref.at[pl.ds(p * MOE_BLOCK, MOE_BLOCK)], sem).start()
            return c

        lax.fori_loop(nu_ref[0], n_blocks, issue, 0)

        def drain(p, c):
            pltpu.make_async_copy(zblk_ref, xs_ref.at[pl.ds(0, MOE_BLOCK)], sem).wait()
            return c

        lax.fori_loop(nu_ref[0], n_blocks, drain, 0)


def _scatter_rows(x1, dest_flat, counts, padded, pstarts, n_used, n_blocks, tm=128):
    n, d = x1.shape
    tm = min(tm, n)
    n_tok_steps = n // tm
    return pl.pallas_call(
        functools.partial(_scatter_kernel, tm=tm, n_tok_steps=n_tok_steps, n_blocks=n_blocks),
        grid_spec=pltpu.PrefetchScalarGridSpec(
            num_scalar_prefetch=4,
            grid=(n_tok_steps + N_EXPERTS + 1,),
            in_specs=[pl.BlockSpec((tm * TOP_K,), lambda i, *_: (jnp.minimum(i, n_tok_steps - 1),),
                                   memory_space=pltpu.SMEM),
                      pl.BlockSpec((tm, d), lambda i, *_: (jnp.minimum(i, n_tok_steps - 1), 0))],
            out_specs=pl.BlockSpec(memory_space=pl.ANY),
            scratch_shapes=[pltpu.VMEM((1, d), x1.dtype), pltpu.VMEM((MOE_BLOCK, d), x1.dtype),
                            pltpu.SemaphoreType.DMA(())]),
        out_shape=jax.ShapeDtypeStruct((n_blocks * MOE_BLOCK, d), x1.dtype),
        compiler_params=_cparams(("arbitrary",)),
        name="moe_scatter",
    )(counts, padded, pstarts, n_used, dest_flat, x1)


def _expert_kernel(be_ref, nu_ref, xs_ref, wg_ref, wu_ref, wd_ref, y_ref):
    del be_ref
    used = pl.program_id(0) < nu_ref[0]

    @pl.when(used)
    def _():
        lo, hi = _unpack_halves(xs_ref[...])
        lo, hi = lo.astype(BF16), hi.astype(BF16)
        dh = lo.shape[1]
        hg = _dot(lo, wg_ref[:dh, :]) + _dot(hi, wg_ref[dh:, :])
        hu = _dot(lo, wu_ref[:dh, :]) + _dot(hi, wu_ref[dh:, :])
        h = (hg * _sigmoid(hg)) * hu
        y_ref[...] = _pack_halves(_dot(h.astype(BF16), wd_ref[...]))

    @pl.when(jnp.logical_not(used))
    def _():
        y_ref[...] = jnp.zeros_like(y_ref)


def _expert_mlp(xs, block_e, n_used, wg, wu, wd):
    n_rows, dp = xs.shape
    nblk = n_rows // MOE_BLOCK
    d, ff = wg.shape[-2:]
    assert d == 2 * dp

    def row_map(p, be, nu):
        return (jnp.minimum(p, nu[0] - 1), 0)

    return pl.pallas_call(
        _expert_kernel,
        grid_spec=pltpu.PrefetchScalarGridSpec(
            num_scalar_prefetch=2,
            grid=(nblk,),
            in_specs=[pl.BlockSpec((MOE_BLOCK, dp), row_map),
                      pl.BlockSpec((None, d, ff), lambda p, be, nu: (be[p], 0, 0)),
                      pl.BlockSpec((None, d, ff), lambda p, be, nu: (be[p], 0, 0)),
                      pl.BlockSpec((None, ff, d), lambda p, be, nu: (be[p], 0, 0))],
            out_specs=pl.BlockSpec((MOE_BLOCK, dp), lambda p, be, nu: (p, 0))),
        out_shape=jax.ShapeDtypeStruct((n_rows, dp), jnp.uint32),
        compiler_params=_cparams(("arbitrary",)),
        name="moe_experts",
    )(block_e, n_used, xs, wg, wu, wd)


COMBINE_PARTS = 4


def _combine_kernel(dest_ref, x1_ref, gk_ref, sg_ref, su_ref, sd_ref, g_ref, b_ref, ys_ref,
                    o_ref, ybuf_ref, sh_ref, sem, *, tm, alpha):
    tp = tm // COMBINE_PARTS

    def issue(part):
        def body(t, c):
            for k in range(TOP_K):
                _row_copy(ys_ref, dest_ref[t * TOP_K + k], ybuf_ref, k * tm + t, sem.at[part]).start()
            return c

        lax.fori_loop(part * tp, (part + 1) * tp, body, 0)

    issue(0)
    issue(1)
    xb = x1_ref[...].astype(BF16)
    hg = _dot(xb, sg_ref[...])
    hu = _dot(xb, su_ref[...])
    sh_ref[...] = _dot(((hg * _sigmoid(hg)) * hu).astype(BF16), sd_ref[...])

    for part in range(COMBINE_PARTS):
        rows = slice(part * tp, (part + 1) * tp)
        for k in range(TOP_K):
            pltpu.make_async_copy(ys_ref.at[pl.ds(0, tp)], ybuf_ref.at[pl.ds(0, tp)], sem.at[part]).wait()
        if part + 2 < COMBINE_PARTS:
            issue(part + 2)
        gk = gk_ref[rows, :]
        r_lo = jnp.zeros((tp, ybuf_ref.shape[1]), F32)
        r_hi = jnp.zeros((tp, ybuf_ref.shape[1]), F32)
        for k in range(TOP_K):
            lo, hi = _unpack_halves(ybuf_ref[k * tm + part * tp:k * tm + (part + 1) * tp, :])
            r_lo = r_lo + gk[:, k:k + 1] * lo
            r_hi = r_hi + gk[:, k:k + 1] * hi
        routed = jnp.concatenate([r_lo, r_hi], axis=1)
        o_ref[rows, :] = _layer_norm(alpha * x1_ref[rows, :] + (routed + sh_ref[rows, :]), g_ref[...], b_ref[...])


def _combine(x1, dest_flat, gk, ys, sg, su, sd, ln_g, ln_b, alpha, tm=128):
    n, d = x1.shape
    tm = min(tm, n)
    ff = sg.shape[-1]
    return pl.pallas_call(
        functools.partial(_combine_kernel, tm=tm, alpha=alpha),
        grid=(n // tm,),
        in_specs=[pl.BlockSpec((tm * TOP_K,), lambda i: (i,), memory_space=pltpu.SMEM),
                  pl.BlockSpec((tm, d), lambda i: (i, 0)),
                  pl.BlockSpec((tm, LANES), lambda i: (i, 0)),
                  pl.BlockSpec((d, ff), lambda i: (0, 0), pipeline_mode=pl.Buffered(1)),
                  pl.BlockSpec((d, ff), lambda i: (0, 0), pipeline_mode=pl.Buffered(1)),
                  pl.BlockSpec((ff, d), lambda i: (0, 0), pipeline_mode=pl.Buffered(1)),
                  pl.BlockSpec((1, d), lambda i: (0, 0)),
                  pl.BlockSpec((1, d), lambda i: (0, 0)),
                  pl.BlockSpec(memory_space=pl.ANY)],
        out_specs=pl.BlockSpec((tm, d), lambda i: (i, 0)),
        out_shape=jax.ShapeDtypeStruct((n, d), F32),
        scratch_shapes=[pltpu.VMEM((TOP_K * tm, d // 2), jnp.uint32), pltpu.VMEM((tm, d), F32),
                        pltpu.SemaphoreType.DMA((COMBINE_PARTS,))],
        compiler_params=_cparams(("arbitrary",)),
        name="moe_combine_ln2",
    )(dest_flat, x1, gk, sg, su, sd, ln_g, ln_b, ys)


def _rope_kernel(pos_ref, inv_ref, sign_ref, cos_ref, sin_ref):
    ang = pos_ref[...].astype(F32) * inv_ref[...]
    cos_ref[...] = jnp.cos(ang)
    sin_ref[...] = jnp.sin(ang) * sign_ref[...]


def _rope_tables(positions, dh, tm=512):
    half = dh // 2
    inv = ROPE_THETA ** (-jnp.arange(half, dtype=F32) * 2.0 / dh)
    reps = LANES // dh
    inv_l = jnp.tile(jnp.concatenate([inv, inv]), reps).reshape(1, LANES)
    sign_l = jnp.tile(jnp.concatenate([-jnp.ones((half,), F32), jnp.ones((half,), F32)]), reps).reshape(1, LANES)
    pos = positions.reshape(-1, 1)
    n = pos.shape[0]
    tm = min(tm, n)
    one = pl.BlockSpec((1, LANES), lambda i: (0, 0))
    blk = pl.BlockSpec((tm, LANES), lambda i: (i, 0))
    return pl.pallas_call(
        _rope_kernel,
        grid=(n // tm,),
        in_specs=[pl.BlockSpec((tm, 1), lambda i: (i, 0)), one, one],
        out_specs=[blk, blk],
        out_shape=[jax.ShapeDtypeStruct((n, LANES), F32)] * 2,
        compiler_params=_cparams(("parallel",)),
        name="rope_tables",
    )(pos, inv_l, sign_l)


def _dup_heads(w_rows, heads, dh):
    d = w_rows.shape[1]
    w3 = w_rows.reshape(heads, dh, d)
    return jnp.concatenate([w3, w3], axis=1).reshape(heads * 2 * dh, d)


def _hybrid_layer(x, positions, w_in, a_sinks, cmp_k_pos, cmp_k_w1, cmp_k_w2, cmp_v_pos, cmp_v_w1,
                  cmp_v_w2, w_o, ln1_g, ln1_b, w_router, router_bias, exp_w_gate, exp_w_up, exp_w_down,
                  sh_w_gate, sh_w_up, sh_w_down, ln2_g, ln2_b, alpha):
    bsz, seq, d = x.shape
    n = bsz * seq
    assert seq % NSA_TK == 0 and (seq // CMP_STRIDE) % LANES == 0 and A_WINDOW <= TQ
    assert 3 <= seq // SLC_LEN <= LANES
    xf = x.reshape(n, d)
    xb = xf.astype(BF16)

    qa_w, kva = A_HEADS * A_HEAD_DIM, A_KV_HEADS * A_HEAD_DIM
    qb_w, kvb = B_HEADS * B_HEAD_DIM, B_KV_HEADS * B_HEAD_DIM
    widths = [qa_w, kva, kva, qb_w, kvb, kvb, kvb, kvb, kvb, kvb, 3 * B_HEADS, d, d]
    offs = np.concatenate([[0], np.cumsum(widths)]).tolist()
    (o_qa, o_ka, o_va, o_qb, o_kc, o_vc, o_ks, o_vs, o_kw, o_vw, o_gn, o_ga, o_end) = (
        offs[i] for i in (0, 1, 2, 3, 4, 5, 6, 7, 8, 9, 10, 11, 13))
    w_t = w_in.T
    blk = lambda start, width: list(range(start // PROJ_TN, (start + width) // PROJ_TN))
    assert all(o % PROJ_TN == 0 for o in (o_qa, o_qb, o_kc, o_vc, o_ks, o_vs, o_kw, o_vw))
    rep = B_HEADS // B_KV_HEADS
    gidx = np.zeros((B_KV_HEADS, LANES), np.int32)
    gmask = np.zeros((B_KV_HEADS, LANES), np.float32)
    for g in range(B_KV_HEADS):
        for br in range(3):
            for r in range(rep):
                gidx[g, br * rep + r] = o_gn + br * B_HEADS + g * rep + r
                gmask[g, br * rep + r] = 1.0
    w_gn_x = w_t[gidx.reshape(-1)] * jnp.asarray(gmask.reshape(-1, 1))
    w_sig = jnp.concatenate([w_gn_x, w_t[o_ga:o_end]], axis=0).astype(BF16)
    w_ka2 = _dup_heads(w_t[o_ka:o_va], A_KV_HEADS, A_HEAD_DIM).astype(BF16)
    w_va2 = _dup_heads(w_t[o_va:o_qb], A_KV_HEADS, A_HEAD_DIM).astype(BF16)
    all_blocks = lambda w: list(range(w.shape[0] // PROJ_TN))

    cos64, sin64 = _rope_tables(positions, A_HEAD_DIM)
    cos128, sin128 = _rope_tables(positions, B_HEAD_DIM)

    q_a = _project(xb, w_t, blk(o_qa, qa_w), cos64, sin64, "rope64", BF16,
                   q_cols=qa_w, q_scale=A_HEAD_DIM ** -0.5 * LOG2E)
    k_a = _project(xb, w_ka2, all_blocks(w_ka2), cos64, sin64, "rope64", BF16)
    v_a = _project(xb, w_va2, all_blocks(w_va2), cos64, sin64, "none", BF16)
    qk_b = _project(xb, w_t, blk(o_qb, qb_w) + blk(o_ks, kvb) + blk(o_kw, kvb), cos128, sin128, "rope128", BF16,
                    q_cols=qb_w, q_scale=B_HEAD_DIM ** -0.5 * LOG2E)
    v_b = _project(xb, w_t, blk(o_vs, kvb) + blk(o_vw, kvb), cos128, sin128, "none", BF16)
    kc = _project(xb, w_t, blk(o_kc, kvb), cos128, sin128, "rope128", F32)
    vc = _project(xb, w_t, blk(o_vc, kvb), cos128, sin128, "none", F32)
    gates = _project(xb, w_sig, all_blocks(w_sig), cos128, sin128, "sigmoid", F32)

    o_a = _swa_attention(q_a, k_a, v_a, a_sinks, bsz, seq)

    kcmp = _compress(kc, 0, cmp_k_pos, cmp_k_w1.astype(BF16), cmp_k_w2.astype(BF16), bsz, seq)
    vcmp = _compress(vc, 0, cmp_v_pos, cmp_v_w1.astype(BF16), cmp_v_w2.astype(BF16), bsz, seq)
    nsa_cols = dict(q=0, ks=qb_w, kw=qb_w + kvb, vs=0, vw=kvb, gate=0)
    o_b = _nsa_attention(qk_b, v_b, gates, kcmp, vcmp, bsz, seq, nsa_cols)

    w_r = jnp.pad(w_router, ((0, 0), (0, LANES - N_EXPERTS))).astype(BF16)
    x1, x1p, scores = _out_proj(o_a, o_b, gates, B_KV_HEADS * LANES, B_KV_HEADS * LANES + d, w_o.astype(BF16), xf,
                           ln1_g.reshape(1, d), ln1_b.reshape(1, d), w_r, alpha)

    bias = jnp.pad(router_bias.astype(F32), (0, LANES - N_EXPERTS)).reshape(1, LANES)
    ek, rk, gk, cnt = _route(scores, bias)

    counts = cnt[0, :N_EXPERTS].astype(jnp.int32)
    padded = (counts + MOE_BLOCK - 1) // MOE_BLOCK * MOE_BLOCK
    pends = jnp.cumsum(padded)
    pstarts = pends - padded
    nblk = -(-(n * TOP_K) // MOE_BLOCK) + N_EXPERTS
    blk_start = jnp.arange(nblk, dtype=jnp.int32) * MOE_BLOCK
    block_e = jnp.minimum(jnp.sum((pends[None, :] <= blk_start[:, None]).astype(jnp.int32), axis=1),
                          N_EXPERTS - 1)
    n_used = (pends[-1:] // MOE_BLOCK).astype(jnp.int32)
    dest = (pstarts[ek[:, :TOP_K]] + rk[:, :TOP_K]).astype(jnp.int32).reshape(-1)

    xs = _scatter_rows(x1p, dest, counts, padded, pstarts.astype(jnp.int32), n_used, nblk)
    ys = _expert_mlp(xs, block_e, n_used, exp_w_gate.astype(BF16), exp_w_up.astype(BF16),
                     exp_w_down.astype(BF16))
    out = _combine(x1, dest, gk, ys, sh_w_gate.astype(BF16), sh_w_up.astype(BF16), sh_w_down.astype(BF16),
                   ln2_g.reshape(1, d), ln2_b.reshape(1, d), alpha)
    return out.reshape(bsz, seq, d)


def kernel(x, positions, w_in, a_sinks, cmp_k_pos, cmp_k_w1, cmp_k_w2, cmp_v_pos, cmp_v_w1, cmp_v_w2,
           w_o, ln1_g, ln1_b, w_router, router_bias, exp_w_gate, exp_w_up, exp_w_down,
           sh_w_gate, sh_w_up, sh_w_down, ln2_g, ln2_b):
    depth = w_in.shape[0]
    alpha = (2.0 * depth) ** 0.25
    for l in range(depth):
        x = _hybrid_layer(x, positions, w_in[l], a_sinks[l], cmp_k_pos[l], cmp_k_w1[l], cmp_k_w2[l],
                          cmp_v_pos[l], cmp_v_w1[l], cmp_v_w2[l], w_o[l], ln1_g[l], ln1_b[l],
                          w_router[l], router_bias[l], exp_w_gate[l], exp_w_up[l], exp_w_down[l],
                          sh_w_gate[l], sh_w_up[l], sh_w_down[l], ln2_g[l], ln2_b[l], alpha)
    return x
```

```python
import functools

import jax
import jax.numpy as jnp
import numpy as np
from jax import lax
from jax.experimental import pallas as pl
from jax.experimental.pallas import tpu as pltpu

F32 = jnp.float32
BF16 = jnp.bfloat16

ROPE_THETA = 10000.0
LN_EPS = 1e-5
NEG_INF = -1e30

A_HEADS, A_KV_HEADS, A_HEAD_DIM, A_WINDOW = 64, 8, 64, 128
B_HEADS, B_KV_HEADS, B_HEAD_DIM = 32, 4, 128
CMP_LEN, CMP_STRIDE, CMP_HIDDEN = 32, 16, 256
SLC_LEN, SLC_TOPK, WIN_LEN = 64, 16, 512
N_EXPERTS, TOP_K, EXPERT_FF, SHARED_FF = 64, 8, 512, 512
ROUTED_SCALE = 2.5
MOE_BLOCK = 256

LOG2E = 1.4426950408889634
LANES = 128
TQ = 128
VMEM_LIMIT = 56 * 1024 * 1024


def _cparams(sem):
    return pltpu.CompilerParams(dimension_semantics=sem, vmem_limit_bytes=VMEM_LIMIT)


def _dot(a, b):
    return jnp.dot(a, b, preferred_element_type=F32)


def _dot_nt(a, b):
    return lax.dot_general(a, b, (((1,), (1,)), ((), ())), preferred_element_type=F32)


def _sigmoid(v):
    return 1.0 / (1.0 + jnp.exp(-v))


def _pack_halves(v):
    h = v.shape[1] // 2
    lo = lax.bitcast_convert_type(v[:, :h].astype(BF16).astype(F32), jnp.uint32)
    hi = lax.bitcast_convert_type(v[:, h:].astype(BF16).astype(F32), jnp.uint32)
    return (hi & jnp.uint32(0xFFFF0000)) | (lo >> 16)


def _unpack_halves(w):
    lo = lax.bitcast_convert_type(w << 16, F32)
    hi = lax.bitcast_convert_type(w & jnp.uint32(0xFFFF0000), F32)
    return lo, hi


def _layer_norm(z, g, b):
    mu = jnp.mean(z, axis=-1, keepdims=True)
    zc = z - mu
    var = jnp.mean(zc * zc, axis=-1, keepdims=True)
    return zc * lax.rsqrt(var + LN_EPS) * g + b


def _proj_kernel(tbl_ref, x_ref, w_ref, cos_ref, sin_ref, o_ref, *scratch, epi, q_blocks, q_scale):
    del tbl_ref
    if scratch:
        (wb_ref,) = scratch

        @pl.when(pl.program_id(1) == 0)
        def _():
            wb_ref[...] = w_ref[...].astype(BF16)

        w = wb_ref[...]
    else:
        w = w_ref[...]
    acc = _dot_nt(x_ref[...], w)
    if epi == "none":
        o_ref[...] = acc.astype(o_ref.dtype)
    elif epi == "sigmoid":
        o_ref[...] = _sigmoid(acc).astype(o_ref.dtype)
    else:
        cs = jnp.where(pl.program_id(0) < q_blocks, q_scale, 1.0)
        cos = cos_ref[...] * cs
        sin = sin_ref[...] * cs
        lane = lax.broadcasted_iota(jnp.int32, (1, LANES), 1)
        for c in range(acc.shape[1] // LANES):
            y = acc[:, c * LANES:(c + 1) * LANES]
            if epi == "rope128":
                rot = pltpu.roll(y, 64, 1)
            else:
                rot = jnp.where((lane % 64) < 32, pltpu.roll(y, 96, 1), pltpu.roll(y, 32, 1))
            o_ref[:, c * LANES:(c + 1) * LANES] = (y * cos + rot * sin).astype(o_ref.dtype)


PROJ_TN = 512


def _project(xb, w_t, blocks, cos, sin, epi, out_dtype, q_cols=0, q_scale=1.0, tm=1024):
    n, d = xb.shape
    tn = PROJ_TN
    ncol = len(blocks) * tn
    tm = min(tm, n)
    assert n % tm == 0 and q_cols % tn == 0 and (max(blocks) + 1) * tn <= w_t.shape[0]
    cast = w_t.dtype != BF16
    return pl.pallas_call(
        functools.partial(_proj_kernel, epi=epi, q_blocks=q_cols // tn, q_scale=q_scale),
        grid_spec=pltpu.PrefetchScalarGridSpec(
            num_scalar_prefetch=1,
            grid=(ncol // tn, n // tm),
            in_specs=[pl.BlockSpec((tm, d), lambda j, i, tbl: (i, 0)),
                      pl.BlockSpec((tn, d), lambda j, i, tbl: (tbl[j], 0)),
                      pl.BlockSpec((tm, LANES), lambda j, i, tbl: (i, 0)),
                      pl.BlockSpec((tm, LANES), lambda j, i, tbl: (i, 0))],
            out_specs=pl.BlockSpec((tm, tn), lambda j, i, tbl: (i, j)),
            scratch_shapes=[pltpu.VMEM((tn, d), BF16)] if cast else []),
        out_shape=jax.ShapeDtypeStruct((n, ncol), out_dtype),
        compiler_params=_cparams(("arbitrary", "arbitrary")),
        name="proj_" + epi,
    )(jnp.asarray(blocks, jnp.int32), xb, w_t, cos, sin)


SWA_GROUPS = 2


def _swa_kernel(sink_ref, q_ref, kp_ref, kc_ref, vp_ref, vc_ref, ga_ref, o_ref, s_ref, p_ref):
    n = pl.program_id(1)
    lane = lax.broadcasted_iota(jnp.int32, (1, LANES), 1)
    lo = lane < A_HEAD_DIM
    nk = 2 * TQ
    qpos = n * TQ + lax.broadcasted_iota(jnp.int32, (TQ, nk), 0)
    kpos = (n - 1) * TQ + lax.broadcasted_iota(jnp.int32, (TQ, nk), 1)
    delta = qpos - kpos
    mask = (kpos >= 0) & (delta >= 0) & (delta < A_WINDOW)
    rep = A_HEADS // A_KV_HEADS
    npair = rep // 2
    zq = jnp.zeros((TQ, LANES), q_ref.dtype)
    for gg in range(SWA_GROUPS):
        g = pl.program_id(2) * SWA_GROUPS + gg
        gl = slice(gg * LANES, (gg + 1) * LANES)
        kk = jnp.concatenate([kp_ref[:, gl], kc_ref[:, gl]], axis=0)
        vv = jnp.concatenate([vp_ref[:, gl], vc_ref[:, gl]], axis=0)
        zv = jnp.zeros_like(vv)
        v_halves = (jnp.where(lo, vv, zv), jnp.where(lo, zv, vv))
        qcols = [slice((gg * npair + c) * LANES, (gg * npair + c + 1) * LANES) for c in range(npair)]
        q_lo = [jnp.where(lo, q_ref[:, c], zq) for c in qcols]
        q_hi = [jnp.where(lo, zq, q_ref[:, c]) for c in qcols]
        r0 = gg * rep * TQ
        s_ref[r0:r0 + rep * TQ, :] = _dot_nt(jnp.concatenate(q_lo + q_hi, axis=0), kk)
        for half in range(2):
            for pair in range(npair):
                sl = slice(r0 + (half * npair + pair) * TQ, r0 + (half * npair + pair + 1) * TQ)
                s = jnp.where(mask, s_ref[sl, :], NEG_INF)
                sk = sink_ref[g * rep + pair * 2 + half] * LOG2E
                m = jnp.maximum(jnp.max(s, axis=-1, keepdims=True), sk)
                e = jnp.exp2(s - m)
                p = e / (jnp.sum(e, axis=-1, keepdims=True) + jnp.exp2(sk - m))
                p_ref[sl, :] = p.astype(BF16)
        half_rows = npair * TQ
        out = (_dot(p_ref[r0:r0 + half_rows, :], v_halves[0])
               + _dot(p_ref[r0 + half_rows:r0 + 2 * half_rows, :], v_halves[1]))
        for pair, c in enumerate(qcols):
            o_ref[:, c] = ga_ref[:, c] * out[pair * TQ:(pair + 1) * TQ, :]


def _swa_attention(q, k, v, sinks, gates, bsz, seq):
    n = bsz * seq
    nb = seq // TQ
    rows = SWA_GROUPS * (A_HEADS // A_KV_HEADS) * TQ
    qw = SWA_GROUPS * (A_HEADS // A_KV_HEADS) * A_HEAD_DIM
    kw = SWA_GROUPS * LANES
    return pl.pallas_call(
        _swa_kernel,
        grid=(bsz, nb, A_KV_HEADS // SWA_GROUPS),
        in_specs=[pl.BlockSpec(memory_space=pltpu.SMEM),
                  pl.BlockSpec((TQ, qw), lambda b, i, g: (b * nb + i, g)),
                  pl.BlockSpec((TQ, kw), lambda b, i, g: (b * nb + jnp.maximum(i - 1, 0), g)),
                  pl.BlockSpec((TQ, kw), lambda b, i, g: (b * nb + i, g)),
                  pl.BlockSpec((TQ, kw), lambda b, i, g: (b * nb + jnp.maximum(i - 1, 0), g)),
                  pl.BlockSpec((TQ, kw), lambda b, i, g: (b * nb + i, g)),
                  pl.BlockSpec((TQ, qw), lambda b, i, g: (b * nb + i, g))],
        out_specs=pl.BlockSpec((TQ, qw), lambda b, i, g: (b * nb + i, g)),
        out_shape=jax.ShapeDtypeStruct((n, A_HEADS * A_HEAD_DIM), F32),
        scratch_shapes=[pltpu.VMEM((rows, 2 * TQ), F32), pltpu.VMEM((rows, 2 * TQ), BF16)],
        compiler_params=_cparams(("parallel", "parallel", "parallel")),
        name="swa",
    )(sinks, q, k, k, v, v, gates)


def _compress_kernel(t_ref, pe_ref, w1_ref, w2_ref, o_ref):
    nch = o_ref.shape[0]
    u = jnp.zeros((nch, CMP_HIDDEN), F32)
    v = jnp.zeros((nch, CMP_HIDDEN), F32)
    for l in range(CMP_STRIDE):
        t_l = t_ref[pl.ds(l, nch, stride=CMP_STRIDE), :]
        a = (t_l + pe_ref[l:l + 1, :]).astype(BF16)
        b = (t_l + pe_ref[CMP_STRIDE + l:CMP_STRIDE + l + 1, :]).astype(BF16)
        u = u + _dot(a, w1_ref[l * LANES:(l + 1) * LANES, :])
        v = v + _dot(b, w1_ref[(CMP_STRIDE + l) * LANES:(CMP_STRIDE + l + 1) * LANES, :])
    pre = u + pltpu.roll(v, nch - 1, 0)
    h = jax.nn.gelu(pre)
    o_ref[...] = _dot(h.astype(BF16), w2_ref[...]).astype(o_ref.dtype)


def _compress(t, col0, pe, w1, w2, bsz, seq):
    nch = seq // CMP_STRIDE
    cb0 = col0 // LANES
    return pl.pallas_call(
        _compress_kernel,
        grid=(bsz, B_KV_HEADS),
        in_specs=[pl.BlockSpec((seq, LANES), lambda b, g: (b, cb0 + g)),
                  pl.BlockSpec((CMP_LEN, LANES), lambda b, g: (0, 0)),
                  pl.BlockSpec((CMP_LEN * LANES, CMP_HIDDEN), lambda b, g: (0, 0)),
                  pl.BlockSpec((CMP_HIDDEN, LANES), lambda b, g: (0, 0))],
        out_specs=pl.BlockSpec((None, None, nch, LANES), lambda b, g: (b, g, 0, 0)),
        out_shape=jax.ShapeDtypeStruct((bsz, B_KV_HEADS, nch, LANES), BF16),
        compiler_params=_cparams(("parallel", "parallel")),
        name="nsa_compress",
    )(t, pe, w1, w2)


NSA_TK = 512
NSA_NKW = (-(-(WIN_LEN - 1) // TQ) + 1) * TQ


def _nsa_kernel(q_ref, gate_ref, kcmp_ref, vcmp_ref, ks_ref, vs_ref, kw_ref, vw_ref, ya_ref, gb_ref, o_ref,
                qs_ref, s_ref, p_ref, m_ref, l_ref, acc_ref, ocmp_ref, *, seq):
    qi = pl.program_id(2)
    rep = B_HEADS // B_KV_HEADS
    nch = seq // CMP_STRIDE
    ns = seq // SLC_LEN
    n_sel = min(SLC_TOPK, ns)
    heads = [slice(r * TQ, (r + 1) * TQ) for r in range(rep)]
    for r in range(rep):
        qs_ref[heads[r], :] = q_ref[:, r * LANES:(r + 1) * LANES]
    tpos = qi * TQ + lax.broadcasted_iota(jnp.int32, (TQ, 1), 0)

    s_ref[:, :nch] = _dot_nt(qs_ref[...], kcmp_ref[...])
    c_end = lax.broadcasted_iota(jnp.int32, (1, nch), 1) * CMP_STRIDE + (CMP_LEN - 1)
    valid = c_end <= tpos
    p_grp = jnp.zeros((TQ, nch), F32)
    for sl in heads:
        s = jnp.where(valid, s_ref[sl, :nch], NEG_INF)
        m = jnp.max(s, axis=-1, keepdims=True)
        e = jnp.where(valid, jnp.exp2(s - m), 0.0)
        den = jnp.sum(e, axis=-1, keepdims=True)
        p = e / jnp.where(den > 0, den, 1.0)
        p_grp = p_grp + p
        p_ref[sl, :nch] = p.astype(BF16)
    ocmp_ref[...] = _dot(p_ref[:, :nch], vcmp_ref[...])

    jrow = lax.broadcasted_iota(jnp.int32, (ns, 1), 0)
    d = lax.broadcasted_iota(jnp.int32, (ns, nch), 1) - (SLC_LEN // CMP_STRIDE) * jrow
    msel = jnp.where((d == -1) | (d == 3), 1.0, jnp.where((d >= 0) & (d <= 2), 2.0, 0.0)).astype(BF16)
    p_hi = p_grp.astype(BF16)
    r1 = p_grp - p_hi.astype(F32)
    p_mid = r1.astype(BF16)
    p_lo = (r1 - p_mid.astype(F32)).astype(BF16)
    p_slc = _dot_nt(msel, p_hi) + _dot_nt(msel, p_mid) + _dot_nt(msel, p_lo)

    tpos_l = qi * TQ + lax.broadcasted_iota(jnp.int32, (1, TQ), 1)
    cur = tpos_l // SLC_LEN
    forced = (jrow == 0) | (jrow == cur) | (jrow == cur - 1)
    score = jnp.where(forced, 1e9, jnp.where(jrow * SLC_LEN <= tpos_l, p_slc, NEG_INF))
    rank = jnp.zeros((ns, TQ), jnp.int32)
    for i in range(ns):
        si = score[i:i + 1, :]
        rank = rank + ((si > score) | ((si == score) & (jrow > i))).astype(jnp.int32)
    sel_t = (rank < n_sel).astype(F32)
    if ns < LANES:
        sel_t = jnp.concatenate([sel_t, jnp.zeros((LANES - ns, TQ), F32)], axis=0)
    sel = sel_t.T.astype(BF16)

    m_ref[...] = jnp.full_like(m_ref, NEG_INF)
    l_ref[...] = jnp.zeros_like(l_ref)
    acc_ref[...] = jnp.zeros_like(acc_ref)
    jexp = lax.broadcasted_iota(jnp.int32, (LANES, 1), 0)

    def slc_body(kt, c):
        row0 = pl.multiple_of(kt * NSA_TK, NSA_TK)
        s_ref[:, :NSA_TK] = _dot_nt(qs_ref[...], ks_ref[pl.ds(row0, NSA_TK), :])
        kpos = kt * NSA_TK + lax.broadcasted_iota(jnp.int32, (1, NSA_TK), 1)
        expand = (jexp == kpos // SLC_LEN).astype(BF16)
        keep = (_dot(sel, expand) > 0.5) & (kpos <= tpos)
        for sl in heads:
            s = jnp.where(keep, s_ref[sl, :NSA_TK], NEG_INF)
            m_old = m_ref[sl, :]
            m_new = jnp.maximum(m_old, jnp.max(s, axis=-1, keepdims=True))
            alpha = jnp.exp2(m_old - m_new)
            e = jnp.exp2(s - m_new)
            l_ref[sl, :] = alpha * l_ref[sl, :] + jnp.sum(e, axis=-1, keepdims=True)
            m_ref[sl, :] = m_new
            acc_ref[sl, :] = alpha * acc_ref[sl, :]
            p_ref[sl, :NSA_TK] = e.astype(BF16)
        acc_ref[...] += _dot(p_ref[:, :NSA_TK], vs_ref[pl.ds(row0, NSA_TK), :])
        return c

    lax.fori_loop(0, (qi * TQ + TQ + NSA_TK - 1) // NSA_TK, slc_body, 0)

    nprev = NSA_NKW // TQ - 1
    k_parts, v_parts = [], []
    for i in range(nprev + 1):
        row0 = pl.multiple_of(jnp.maximum(qi - nprev + i, 0) * TQ, TQ)
        k_parts.append(kw_ref[pl.ds(row0, TQ), :])
        v_parts.append(vw_ref[pl.ds(row0, TQ), :])
    s_ref[:, :NSA_NKW] = _dot_nt(qs_ref[...], jnp.concatenate(k_parts, axis=0))
    kposw = (qi - nprev) * TQ + lax.broadcasted_iota(jnp.int32, (1, NSA_NKW), 1)
    dlt = tpos - kposw
    keepw = (kposw >= 0) & (dlt >= 0) & (dlt < WIN_LEN)
    for sl in heads:
        s = jnp.where(keepw, s_ref[sl, :NSA_NKW], NEG_INF)
        e = jnp.exp2(s - jnp.max(s, axis=-1, keepdims=True))
        p_ref[sl, :NSA_NKW] = (e / jnp.sum(e, axis=-1, keepdims=True)).astype(BF16)
    o_win = _dot(p_ref[:, :NSA_NKW], jnp.concatenate(v_parts, axis=0))

    gate = gate_ref[...]
    for r, sl in enumerate(heads):
        cols = slice(r * LANES, (r + 1) * LANES)
        o_b = (gate[:, r:r + 1] * ocmp_ref[sl, :]
               + gate[:, rep + r:rep + r + 1] * (acc_ref[sl, :] / l_ref[sl, :])
               + gate[:, 2 * rep + r:2 * rep + r + 1] * o_win[sl])
        o_ref[:, cols] = (ya_ref[:, cols] + gb_ref[:, cols] * o_b).astype(o_ref.dtype)


def _nsa_attention(qk, v, gates, ya, kcmp, vcmp, bsz, seq, cols):
    n = bsz * seq
    nq = seq // TQ
    nch = seq // CMP_STRIDE
    rep = B_HEADS // B_KV_HEADS
    qb0 = cols["q"] // (rep * LANES)
    gbb = cols["g_b"] // (rep * LANES)
    ksb, kwb, vsb, vwb, gb0 = (cols[k] // LANES for k in ("ks", "kw", "vs", "vw", "gate"))
    return pl.pallas_call(
        functools.partial(_nsa_kernel, seq=seq),
        grid=(bsz, B_KV_HEADS, nq),
        in_specs=[pl.BlockSpec((TQ, rep * LANES), lambda b, g, i: (b * nq + i, qb0 + g)),
                  pl.BlockSpec((TQ, LANES), lambda b, g, i: (b * nq + i, gb0 + g)),
                  pl.BlockSpec((None, None, nch, LANES), lambda b, g, i: (b, g, 0, 0)),
                  pl.BlockSpec((None, None, nch, LANES), lambda b, g, i: (b, g, 0, 0)),
                  pl.BlockSpec((seq, LANES), lambda b, g, i: (b, ksb + g)),
                  pl.BlockSpec((seq, LANES), lambda b, g, i: (b, vsb + g)),
                  pl.BlockSpec((seq, LANES), lambda b, g, i: (b, kwb + g)),
                  pl.BlockSpec((seq, LANES), lambda b, g, i: (b, vwb + g)),
                  pl.BlockSpec((TQ, rep * LANES), lambda b, g, i: (b * nq + i, g)),
                  pl.BlockSpec((TQ, rep * LANES), lambda b, g, i: (b * nq + i, gbb + g))],
        out_specs=pl.BlockSpec((TQ, rep * LANES), lambda b, g, i: (b * nq + i, g)),
        out_shape=jax.ShapeDtypeStruct((n, B_HEADS * B_HEAD_DIM), BF16),
        scratch_shapes=[pltpu.VMEM((rep * TQ, LANES), BF16),
                        pltpu.VMEM((rep * TQ, max(NSA_NKW, NSA_TK, nch)), F32),
                        pltpu.VMEM((rep * TQ, max(NSA_NKW, NSA_TK, nch)), BF16),
                        pltpu.VMEM((rep * TQ, 1), F32), pltpu.VMEM((rep * TQ, 1), F32),
                        pltpu.VMEM((rep * TQ, LANES), F32), pltpu.VMEM((rep * TQ, LANES), F32)],
        compiler_params=_cparams(("parallel", "parallel", "arbitrary")),
        name="nsa",
    )(qk, gates, kcmp, vcmp, qk, v, qk, v, ya, gates)


def _oproj_kernel(y_ref, wo_ref, x_ref, g_ref, b_ref, wr_ref, x1_ref, x1p_ref, sc_ref, z_ref, *, alpha):
    j = pl.program_id(1)
    z_ref[j] = _dot(y_ref[...], wo_ref[...])

    @pl.when(j == pl.num_programs(1) - 1)
    def _():
        nj, tm, _ = z_ref.shape
        piece = min(128, tm)
        for r0 in range(0, tm, piece):
            rows = slice(r0, r0 + piece)
            z = jnp.concatenate([z_ref[jj, rows, :] for jj in range(nj)], axis=1)
            x1 = _layer_norm(alpha * x_ref[rows, :] + z, g_ref[...], b_ref[...])
            x1_ref[rows, :] = x1
            x1p_ref[rows, :] = _pack_halves(x1)
            sc_ref[rows, :] = _sigmoid(_dot(x1.astype(BF16), wr_ref[...]))


def _out_proj(y, w_o, x, ln_g, ln_b, w_r, alpha, tm=512, tn=512):
    n, d = x.shape
    tm = min(tm, n)
    once = dict(pipeline_mode=pl.Buffered(1))
    return pl.pallas_call(
        functools.partial(_oproj_kernel, alpha=alpha),
        grid=(n // tm, d // tn),
        in_specs=[pl.BlockSpec((tm, d), lambda i, j: (i, 0)),
                  pl.BlockSpec((d, tn), lambda i, j: (0, j)),
                  pl.BlockSpec((tm, d), lambda i, j: (i, 0), **once),
                  pl.BlockSpec((1, d), lambda i, j: (0, 0)),
                  pl.BlockSpec((1, d), lambda i, j: (0, 0)),
                  pl.BlockSpec((d, LANES), lambda i, j: (0, 0))],
        out_specs=[pl.BlockSpec((tm, d), lambda i, j: (i, 0), **once),
                   pl.BlockSpec((tm, d // 2), lambda i, j: (i, 0), **once),
                   pl.BlockSpec((tm, LANES), lambda i, j: (i, 0))],
        out_shape=[jax.ShapeDtypeStruct((n, d), F32), jax.ShapeDtypeStruct((n, d // 2), jnp.uint32),
                   jax.ShapeDtypeStruct((n, LANES), F32)],
        scratch_shapes=[pltpu.VMEM((d // tn, tm, tn), F32)],
        compiler_params=_cparams(("parallel", "arbitrary")),
        name="oproj_ln1_router",
    )(y, w_o, x, ln_g, ln_b, w_r)


def _route_kernel(sc_ref, bias_ref, ek_ref, rk_ref, gk_ref, cnt_ref, run_ref):
    i = pl.program_id(0)
    tm = sc_ref.shape[0]

    @pl.when(i == 0)
    def _():
        run_ref[...] = jnp.zeros_like(run_ref)

    sc = sc_ref[...]
    lane = lax.broadcasted_iota(jnp.int32, (1, LANES), 1)
    is_e = lane < N_EXPERTS
    biased = jnp.where(is_e, sc + bias_ref[...], -jnp.inf)
    rank = jnp.zeros((tm, LANES), jnp.int32)
    for e in range(N_EXPERTS):
        be = biased[:, e:e + 1]
        rank = rank + ((be > biased) | ((be == biased) & (e < lane))).astype(jnp.int32)
    sel = (rank < TOP_K) & is_e
    sv = jnp.where(sel, sc, 0.0)
    gates = sv / jnp.sum(sv, axis=-1, keepdims=True) * ROUTED_SCALE
    selb = sel.astype(BF16)
    ri = lax.broadcasted_iota(jnp.int32, (tm, tm), 0)
    ci = lax.broadcasted_iota(jnp.int32, (tm, tm), 1)
    cum = _dot((ci < ri).astype(BF16), selb) + run_ref[...]
    run_ref[...] += jnp.sum(sel.astype(F32), axis=0, keepdims=True)
    cnt_ref[...] = run_ref[...]
    ai = lax.broadcasted_iota(jnp.int32, (LANES, LANES), 0)
    bi = lax.broadcasted_iota(jnp.int32, (LANES, LANES), 1)
    before = _dot(selb, (ai < bi).astype(BF16))
    lane_f = lane.astype(F32)
    ek = jnp.zeros((tm, LANES), F32)
    rk = jnp.zeros((tm, LANES), F32)
    gk = jnp.zeros((tm, LANES), F32)
    for k in range(TOP_K):
        oh = sel & (before == k)
        ek = jnp.where(lane == k, jnp.sum(jnp.where(oh, lane_f, 0.0), axis=-1, keepdims=True), ek)
        rk = jnp.where(lane == k, jnp.sum(jnp.where(oh, cum, 0.0), axis=-1, keepdims=True), rk)
        gk = jnp.where(lane == k, jnp.sum(jnp.where(oh, gates, 0.0), axis=-1, keepdims=True), gk)
    ek_ref[...] = ek.astype(jnp.int32)
    rk_ref[...] = rk.astype(jnp.int32)
    gk_ref[...] = gk


def _route(scores, bias, tm=256):
    n = scores.shape[0]
    tm = min(tm, n)
    blk = pl.BlockSpec((tm, LANES), lambda i: (i, 0))
    one = pl.BlockSpec((1, LANES), lambda i: (0, 0))
    return pl.pallas_call(
        _route_kernel,
        grid=(n // tm,),
        in_specs=[blk, one],
        out_specs=[blk, blk, blk, one],
        out_shape=[jax.ShapeDtypeStruct((n, LANES), jnp.int32), jax.ShapeDtypeStruct((n, LANES), jnp.int32),
                   jax.ShapeDtypeStruct((n, LANES), F32), jax.ShapeDtypeStruct((1, LANES), F32)],
        scratch_shapes=[pltpu.VMEM((1, LANES), F32)],
        compiler_params=_cparams(("arbitrary",)),
        name="route",
    )(scores, bias)


def _row_copy(src, s_row, dst, d_row, sem):
    return pltpu.make_async_copy(src.at[pl.ds(s_row, 1)], dst.at[pl.ds(d_row, 1)], sem)


def _scatter_kernel(cnt_ref, pad_ref, pst_ref, nu_ref, dest_ref, x_ref, xs_ref, zrow_ref, zblk_ref, sem,
                    *, tm, n_tok_steps, n_blocks):
    i = pl.program_id(0)

    @pl.when(i < n_tok_steps)
    def _():
        def issue(t, c):
            for k in range(TOP_K):
                _row_copy(x_ref, t, xs_ref, dest_ref[t * TOP_K + k], sem).start()
            return c

        lax.fori_loop(0, tm, issue, 0)

        for k in range(TOP_K):
            pltpu.make_async_copy(x_ref, xs_ref.at[pl.ds(0, tm)], sem).wait()

    @pl.when((i >= n_tok_steps) & (i < n_tok_steps + N_EXPERTS))
    def _():
        e = i - n_tok_steps
        zrow_ref[...] = jnp.zeros_like(zrow_ref)
        lo, hi, base = cnt_ref[e], pad_ref[e], pst_ref[e]

        def issue(r, c):
            pltpu.make_async_copy(zrow_ref, xs_ref.at[pl.ds(base + r, 1)], sem).start()
            return c

        lax.fori_loop(lo, hi, issue, 0)

        def drain(r, c):
            pltpu.make_async_copy(zrow_ref, xs_ref.at[pl.ds(0, 1)], sem).wait()
            return c

        lax.fori_loop(lo, hi, drain, 0)

    @pl.when(i == n_tok_steps + N_EXPERTS)
    def _():
        zblk_ref[...] = jnp.zeros_like(zblk_ref)

        def issue(p, c):
            pltpu.make_async_copy(zblk_ref, xs_ref.at[pl.ds(p * MOE_BLOCK, MOE_BLOCK)], sem).start()
            return c

        lax.fori_loop(nu_ref[0], n_blocks, issue, 0)

        def drain(p, c):
            pltpu.make_async_copy(zblk_ref, xs_ref.at[pl.ds(0, MOE_BLOCK)], sem).wait()
            return c

        lax.fori_loop(nu_ref[0], n_blocks, drain, 0)


def _scatter_rows(x1, dest_flat, counts, padded, pstarts, n_used, n_blocks, tm=128):
    n, d = x1.shape
    tm = min(tm, n)
    n_tok_steps = n // tm
    return pl.pallas_call(
        functools.partial(_scatter_kernel, tm=tm, n_tok_steps=n_tok_steps, n_blocks=n_blocks),
        grid_spec=pltpu.PrefetchScalarGridSpec(
            num_scalar_prefetch=4,
            grid=(n_tok_steps + N_EXPERTS + 1,),
            in_specs=[pl.BlockSpec((tm * TOP_K,), lambda i, *_: (jnp.minimum(i, n_tok_steps - 1),),
                                   memory_space=pltpu.SMEM),
                      pl.BlockSpec((tm, d), lambda i, *_: (jnp.minimum(i, n_tok_steps - 1), 0))],
            out_specs=pl.BlockSpec(memory_space=pl.ANY),
            scratch_shapes=[pltpu.VMEM((1, d), x1.dtype), pltpu.VMEM((MOE_BLOCK, d), x1.dtype),
                            pltpu.SemaphoreType.DMA(())]),
        out_shape=jax.ShapeDtypeStruct((n_blocks * MOE_BLOCK, d), x1.dtype),
        compiler_params=_cparams(("arbitrary",)),
        name="moe_scatter",
    )(counts, padded, pstarts, n_used, dest_flat, x1)


def _expert_kernel(be_ref, nu_ref, xs_ref, wg_ref, wu_ref, wd_ref, y_ref):
    del be_ref
    used = pl.program_id(0) < nu_ref[0]

    @pl.when(used)
    def _():
        lo, hi = _unpack_halves(xs_ref[...])
        lo, hi = lo.astype(BF16), hi.astype(BF16)
        dh = lo.shape[1]
        hg = _dot(lo, wg_ref[:dh, :]) + _dot(hi, wg_ref[dh:, :])
        hu = _dot(lo, wu_ref[:dh, :]) + _dot(hi, wu_ref[dh:, :])
        h = (hg * _sigmoid(hg)) * hu
        y_ref[...] = _pack_halves(_dot(h.astype(BF16), wd_ref[...]))

    @pl.when(jnp.logical_not(used))
    def _():
        y_ref[...] = jnp.zeros_like(y_ref)


def _expert_mlp(xs, block_e, n_used, wg, wu, wd):
    n_rows, dp = xs.shape
    nblk = n_rows // MOE_BLOCK
    d, ff = wg.shape[-2:]
    assert d == 2 * dp

    def row_map(p, be, nu):
        return (jnp.minimum(p, nu[0] - 1), 0)

    return pl.pallas_call(
        _expert_kernel,
        grid_spec=pltpu.PrefetchScalarGridSpec(
            num_scalar_prefetch=2,
            grid=(nblk,),
            in_specs=[pl.BlockSpec((MOE_BLOCK, dp), row_map),
                      pl.BlockSpec((None, d, ff), lambda p, be, nu: (be[p], 0, 0)),
                      pl.BlockSpec((None, d, ff), lambda p, be, nu: (be[p], 0, 0)),
                      pl.BlockSpec((None, ff, d), lambda p, be, nu: (be[p], 0, 0))],
            out_specs=pl.BlockSpec((MOE_BLOCK, dp), lambda p, be, nu: (p, 0))),
        out_shape=jax.ShapeDtypeStruct((n_rows, dp), jnp.uint32),
        compiler_params=_cparams(("arbitrary",)),
        name="moe_experts",
    )(block_e, n_used, xs, wg, wu, wd)


COMBINE_PARTS = 4


def _combine_kernel(dest_ref, x1_ref, gk_ref, sg_ref, su_ref, sd_ref, g_ref, b_ref, ys_ref,
                    o_ref, ybuf_ref, sh_ref, sem, *, tm, alpha):
    tp = tm // COMBINE_PARTS

    def issue(part):
        def body(t, c):
            for k in range(TOP_K):
                _row_copy(ys_ref, dest_ref[t * TOP_K + k], ybuf_ref, k * tm + t, sem.at[part]).start()
            return c

        lax.fori_loop(part * tp, (part + 1) * tp, body, 0)

    issue(0)
    issue(1)
    xb = x1_ref[...].astype(BF16)
    hg = _dot(xb, sg_ref[...])
    hu = _dot(xb, su_ref[...])
    sh_ref[...] = _dot(((hg * _sigmoid(hg)) * hu).astype(BF16), sd_ref[...])

    for part in range(COMBINE_PARTS):
        rows = slice(part * tp, (part + 1) * tp)
        for k in range(TOP_K):
            pltpu.make_async_copy(ys_ref.at[pl.ds(0, tp)], ybuf_ref.at[pl.ds(0, tp)], sem.at[part]).wait()
        if part + 2 < COMBINE_PARTS:
            issue(part + 2)
        gk = gk_ref[rows, :]
        r_lo = jnp.zeros((tp, ybuf_ref.shape[1]), F32)
        r_hi = jnp.zeros((tp, ybuf_ref.shape[1]), F32)
        for k in range(TOP_K):
            lo, hi = _unpack_halves(ybuf_ref[k * tm + part * tp:k * tm + (part + 1) * tp, :])
            r_lo = r_lo + gk[:, k:k + 1] * lo
            r_hi = r_hi + gk[:, k:k + 1] * hi
        routed = jnp.concatenate([r_lo, r_hi], axis=1)
        o_ref[rows, :] = _layer_norm(alpha * x1_ref[rows, :] + (routed + sh_ref[rows, :]), g_ref[...], b_ref[...])


def _combine(x1, dest_flat, gk, ys, sg, su, sd, ln_g, ln_b, alpha, tm=128):
    n, d = x1.shape
    tm = min(tm, n)
    ff = sg.shape[-1]
    return pl.pallas_call(
        functools.partial(_combine_kernel, tm=tm, alpha=alpha),
        grid=(n // tm,),
        in_specs=[pl.BlockSpec((tm * TOP_K,), lambda i: (i,), memory_space=pltpu.SMEM),
                  pl.BlockSpec((tm, d), lambda i: (i, 0)),
                  pl.BlockSpec((tm, LANES), lambda i: (i, 0)),
                  pl.BlockSpec((d, ff), lambda i: (0, 0), pipeline_mode=pl.Buffered(1)),
                  pl.BlockSpec((d, ff), lambda i: (0, 0), pipeline_mode=pl.Buffered(1)),
                  pl.BlockSpec((ff, d), lambda i: (0, 0), pipeline_mode=pl.Buffered(1)),
                  pl.BlockSpec((1, d), lambda i: (0, 0)),
                  pl.BlockSpec((1, d), lambda i: (0, 0)),
                  pl.BlockSpec(memory_space=pl.ANY)],
        out_specs=pl.BlockSpec((tm, d), lambda i: (i, 0)),
        out_shape=jax.ShapeDtypeStruct((n, d), F32),
        scratch_shapes=[pltpu.VMEM((TOP_K * tm, d // 2), jnp.uint32), pltpu.VMEM((tm, d), F32),
                        pltpu.SemaphoreType.DMA((COMBINE_PARTS,))],
        compiler_params=_cparams(("arbitrary",)),
        name="moe_combine_ln2",
    )(dest_flat, x1, gk, sg, su, sd, ln_g, ln_b, ys)


def _rope_kernel(pos_ref, inv_ref, sign_ref, cos_ref, sin_ref):
    ang = pos_ref[...].astype(F32) * inv_ref[...]
    cos_ref[...] = jnp.cos(ang)
    sin_ref[...] = jnp.sin(ang) * sign_ref[...]


def _rope_tables(positions, dh, tm=512):
    half = dh // 2
    inv = ROPE_THETA ** (-jnp.arange(half, dtype=F32) * 2.0 / dh)
    reps = LANES // dh
    inv_l = jnp.tile(jnp.concatenate([inv, inv]), reps).reshape(1, LANES)
    sign_l = jnp.tile(jnp.concatenate([-jnp.ones((half,), F32), jnp.ones((half,), F32)]), reps).reshape(1, LANES)
    pos = positions.reshape(-1, 1)
    n = pos.shape[0]
    tm = min(tm, n)
    one = pl.BlockSpec((1, LANES), lambda i: (0, 0))
    blk = pl.BlockSpec((tm, LANES), lambda i: (i, 0))
    return pl.pallas_call(
        _rope_kernel,
        grid=(n // tm,),
        in_specs=[pl.BlockSpec((tm, 1), lambda i: (i, 0)), one, one],
        out_specs=[blk, blk],
        out_shape=[jax.ShapeDtypeStruct((n, LANES), F32)] * 2,
        compiler_params=_cparams(("parallel",)),
        name="rope_tables",
    )(pos, inv_l, sign_l)


def _dup_heads(w_rows, heads, dh):
    d = w_rows.shape[1]
    w3 = w_rows.reshape(heads, dh, d)
    return jnp.concatenate([w3, w3], axis=1).reshape(heads * 2 * dh, d)


def _hybrid_layer(x, positions, w_in, a_sinks, cmp_k_pos, cmp_k_w1, cmp_k_w2, cmp_v_pos, cmp_v_w1,
                  cmp_v_w2, w_o, ln1_g, ln1_b, w_router, router_bias, exp_w_gate, exp_w_up, exp_w_down,
                  sh_w_gate, sh_w_up, sh_w_down, ln2_g, ln2_b, alpha):
    bsz, seq, d = x.shape
    n = bsz * seq
    assert seq % NSA_TK == 0 and (seq // CMP_STRIDE) % LANES == 0 and A_WINDOW <= TQ
    assert 3 <= seq // SLC_LEN <= LANES
    xf = x.reshape(n, d)
    xb = xf.astype(BF16)

    qa_w, kva = A_HEADS * A_HEAD_DIM, A_KV_HEADS * A_HEAD_DIM
    qb_w, kvb = B_HEADS * B_HEAD_DIM, B_KV_HEADS * B_HEAD_DIM
    widths = [qa_w, kva, kva, qb_w, kvb, kvb, kvb, kvb, kvb, kvb, 3 * B_HEADS, d, d]
    offs = np.concatenate([[0], np.cumsum(widths)]).tolist()
    (o_qa, o_ka, o_va, o_qb, o_kc, o_vc, o_ks, o_vs, o_kw, o_vw, o_gn, o_ga, o_end) = (
        offs[i] for i in (0, 1, 2, 3, 4, 5, 6, 7, 8, 9, 10, 11, 13))
    w_t = w_in.T
    blk = lambda start, width: list(range(start // PROJ_TN, (start + width) // PROJ_TN))
    assert all(o % PROJ_TN == 0 for o in (o_qa, o_qb, o_kc, o_vc, o_ks, o_vs, o_kw, o_vw))
    rep = B_HEADS // B_KV_HEADS
    w_gn = w_t[o_gn:o_ga].reshape(3, B_KV_HEADS, rep, d).transpose(1, 0, 2, 3).reshape(B_KV_HEADS, 3 * rep, d)
    w_gn_x = jnp.pad(w_gn, ((0, 0), (0, LANES - 3 * rep), (0, 0))).reshape(B_KV_HEADS * LANES, d)
    w_sig = jnp.concatenate([w_t[o_ga:o_end], w_gn_x], axis=0).astype(BF16)
    w_ka2 = _dup_heads(w_t[o_ka:o_va], A_KV_HEADS, A_HEAD_DIM).astype(BF16)
    w_va2 = _dup_heads(w_t[o_va:o_qb], A_KV_HEADS, A_HEAD_DIM).astype(BF16)
    all_blocks = lambda w: list(range(w.shape[0] // PROJ_TN))

    cos64, sin64 = _rope_tables(positions, A_HEAD_DIM)
    cos128, sin128 = _rope_tables(positions, B_HEAD_DIM)

    q_a = _project(xb, w_t, blk(o_qa, qa_w), cos64, sin64, "rope64", BF16,
                   q_cols=qa_w, q_scale=A_HEAD_DIM ** -0.5 * LOG2E)
    k_a = _project(xb, w_ka2, all_blocks(w_ka2), cos64, sin64, "rope64", BF16)
    v_a = _project(xb, w_va2, all_blocks(w_va2), cos64, sin64, "none", BF16)
    qk_b = _project(xb, w_t, blk(o_qb, qb_w) + blk(o_ks, kvb) + blk(o_kw, kvb), cos128, sin128, "rope128", BF16,
                    q_cols=qb_w, q_scale=B_HEAD_DIM ** -0.5 * LOG2E)
    v_b = _project(xb, w_t, blk(o_vs, kvb) + blk(o_vw, kvb), cos128, sin128, "none", BF16)
    kc = _project(xb, w_t, blk(o_kc, kvb), cos128, sin128, "rope128", F32)
    vc = _project(xb, w_t, blk(o_vc, kvb), cos128, sin128, "none", F32)
    gates = _project(xb, w_sig, all_blocks(w_sig), cos128, sin128, "sigmoid", F32)

    ya = _swa_attention(q_a, k_a, v_a, a_sinks, gates, bsz, seq)

    kcmp = _compress(kc, 0, cmp_k_pos, cmp_k_w1.astype(BF16), cmp_k_w2.astype(BF16), bsz, seq)
    vcmp = _compress(vc, 0, cmp_v_pos, cmp_v_w1.astype(BF16), cmp_v_w2.astype(BF16), bsz, seq)
    nsa_cols = dict(q=0, ks=qb_w, kw=qb_w + kvb, vs=0, vw=kvb, g_b=d, gate=2 * d)
    y = _nsa_attention(qk_b, v_b, gates, ya, kcmp, vcmp, bsz, seq, nsa_cols)

    w_r = jnp.pad(w_router, ((0, 0), (0, LANES - N_EXPERTS))).astype(BF16)
    x1, x1p, scores = _out_proj(y, w_o.astype(BF16), xf, ln1_g.reshape(1, d), ln1_b.reshape(1, d), w_r, alpha)

    bias = jnp.pad(router_bias.astype(F32), (0, LANES - N_EXPERTS)).reshape(1, LANES)
    ek, rk, gk, cnt = _route(scores, bias)

    counts = cnt[0, :N_EXPERTS].astype(jnp.int32)
    padded = (counts + MOE_BLOCK - 1) // MOE_BLOCK * MOE_BLOCK
    pends = jnp.cumsum(padded)
    pstarts = pends - padded
    nblk = -(-(n * TOP_K) // MOE_BLOCK) + N_EXPERTS
    blk_start = jnp.arange(nblk, dtype=jnp.int32) * MOE_BLOCK
    block_e = jnp.minimum(jnp.sum((pends[None, :] <= blk_start[:, None]).astype(jnp.int32), axis=1),
                          N_EXPERTS - 1)
    n_used = (pends[-1:] // MOE_BLOCK).astype(jnp.int32)
    hit = ek[:, :TOP_K, None] == jnp.arange(N_EXPERTS, dtype=jnp.int32)
    dest = (jnp.sum(jnp.where(hit, pstarts.astype(jnp.int32), 0), axis=-1) + rk[:, :TOP_K]).reshape(-1)

    xs = _scatter_rows(x1p, dest, counts, padded, pstarts.astype(jnp.int32), n_used, nblk)
    ys = _expert_mlp(xs, block_e, n_used, exp_w_gate.astype(BF16), exp_w_up.astype(BF16),
                     exp_w_down.astype(BF16))
    out = _combine(x1, dest, gk, ys, sh_w_gate.astype(BF16), sh_w_up.astype(BF16), sh_w_down.astype(BF16),
                   ln2_g.reshape(1, d), ln2_b.reshape(1, d), alpha)
    return out.reshape(bsz, seq, d)


def kernel(x, positions, w_in, a_sinks, cmp_k_pos, cmp_k_w1, cmp_k_w2, cmp_v_pos, cmp_v_w1, cmp_v_w2,
           w_o, ln1_g, ln1_b, w_router, router_bias, exp_w_gate, exp_w_up, exp_w_down,
           sh_w_gate, sh_w_up, sh_w_down, ln2_g, ln2_b):
    depth = w_in.shape[0]
    alpha = (2.0 * depth) ** 0.25
    for l in range(depth):
        x = _hybrid_layer(x, positions, w_in[l], a_sinks[l], cmp_k_pos[l], cmp_k_w1[l], cmp_k_w2[l],
                          cmp_v_pos[l], cmp_v_w1[l], cmp_v_w2[l], w_o[l], ln1_g[l], ln1_b[l],
                          w_router[l], router_bias[l], exp_w_gate[l], exp_w_up[l], exp_w_down[l],
                          sh_w_gate[l], sh_w_up[l], sh_w_down[l], ln2_g[l], ln2_b[l], alpha)
    return x
```

```python
import functools

import jax
import jax.numpy as jnp
import numpy as np
from jax import lax
from jax.experimental import pallas as pl
from jax.experimental.pallas import tpu as pltpu

F32 = jnp.float32
BF16 = jnp.bfloat16

ROPE_THETA = 10000.0
LN_EPS = 1e-5
NEG_INF = -1e30

A_HEADS, A_KV_HEADS, A_HEAD_DIM, A_WINDOW = 64, 8, 64, 128
B_HEADS, B_KV_HEADS, B_HEAD_DIM = 32, 4, 128
CMP_LEN, CMP_STRIDE, CMP_HIDDEN = 32, 16, 256
SLC_LEN, SLC_TOPK, WIN_LEN = 64, 16, 512
N_EXPERTS, TOP_K, EXPERT_FF, SHARED_FF = 64, 8, 512, 512
ROUTED_SCALE = 2.5
MOE_BLOCK = 256

LOG2E = 1.4426950408889634
LANES = 128
TQ = 128
VMEM_LIMIT = 56 * 1024 * 1024


def _cparams(sem):
    return pltpu.CompilerParams(dimension_semantics=sem, vmem_limit_bytes=VMEM_LIMIT)


def _dot(a, b):
    return jnp.dot(a, b, preferred_element_type=F32)


def _dot_nt(a, b):
    return lax.dot_general(a, b, (((1,), (1,)), ((), ())), preferred_element_type=F32)


def _sigmoid(v):
    return 1.0 / (1.0 + jnp.exp(-v))


def _pack_halves(v):
    h = v.shape[1] // 2
    lo = lax.bitcast_convert_type(v[:, :h].astype(BF16).astype(F32), jnp.uint32)
    hi = lax.bitcast_convert_type(v[:, h:].astype(BF16).astype(F32), jnp.uint32)
    return (hi & jnp.uint32(0xFFFF0000)) | (lo >> 16)


def _unpack_halves(w):
    lo = lax.bitcast_convert_type(w << 16, F32)
    hi = lax.bitcast_convert_type(w & jnp.uint32(0xFFFF0000), F32)
    return lo, hi


def _layer_norm(z, g, b):
    mu = jnp.mean(z, axis=-1, keepdims=True)
    zc = z - mu
    var = jnp.mean(zc * zc, axis=-1, keepdims=True)
    return zc * lax.rsqrt(var + LN_EPS) * g + b


def _proj_kernel(tbl_ref, x_ref, *rest, epi, q_blocks, q_scale, shift):
    del tbl_ref
    if shift:
        w_ref, w2_ref, cos_ref, sin_ref, o_ref, *scratch = rest
    else:
        w_ref, cos_ref, sin_ref, o_ref, *scratch = rest
    if scratch:
        (wb_ref,) = scratch

        @pl.when(pl.program_id(1) == 0)
        def _():
            if shift:
                keep = wb_ref.shape[0] - shift
                wb_ref[:keep, :] = w_ref[shift:, :].astype(BF16)
                wb_ref[keep:, :] = w2_ref[:shift, :].astype(BF16)
            else:
                wb_ref[...] = w_ref[...].astype(BF16)

        w = wb_ref[...]
    else:
        w = w_ref[...]
    acc = _dot_nt(x_ref[...], w)
    if epi == "none":
        o_ref[...] = acc.astype(o_ref.dtype)
    elif epi == "sigmoid":
        o_ref[...] = _sigmoid(acc).astype(o_ref.dtype)
    else:
        cs = jnp.where(pl.program_id(0) < q_blocks, q_scale, 1.0)
        cos = cos_ref[...] * cs
        sin = sin_ref[...] * cs
        lane = lax.broadcasted_iota(jnp.int32, (1, LANES), 1)
        for c in range(acc.shape[1] // LANES):
            y = acc[:, c * LANES:(c + 1) * LANES]
            if epi == "rope128":
                rot = pltpu.roll(y, 64, 1)
            else:
                rot = jnp.where((lane % 64) < 32, pltpu.roll(y, 96, 1), pltpu.roll(y, 32, 1))
            o_ref[:, c * LANES:(c + 1) * LANES] = (y * cos + rot * sin).astype(o_ref.dtype)


PROJ_TN = 512


def _project(xb, w_t, blocks, cos, sin, epi, out_dtype, q_cols=0, q_scale=1.0, tm=1024, shift=0):
    n, d = xb.shape
    tn = PROJ_TN
    ncol = len(blocks) * tn
    tm = min(tm, n)
    cast = w_t.dtype != BF16
    assert n % tm == 0 and q_cols % tn == 0 and (max(blocks) + 1) * tn + shift <= w_t.shape[0]
    assert shift % 16 == 0 and (cast or not shift)
    w_specs = [pl.BlockSpec((tn, d), lambda j, i, tbl: (tbl[j], 0))]
    if shift:
        w_specs.append(pl.BlockSpec((tn, d), lambda j, i, tbl: (tbl[j] + 1, 0)))
    return pl.pallas_call(
        functools.partial(_proj_kernel, epi=epi, q_blocks=q_cols // tn, q_scale=q_scale, shift=shift),
        grid_spec=pltpu.PrefetchScalarGridSpec(
            num_scalar_prefetch=1,
            grid=(ncol // tn, n // tm),
            in_specs=[pl.BlockSpec((tm, d), lambda j, i, tbl: (i, 0)), *w_specs,
                      pl.BlockSpec((tm, LANES), lambda j, i, tbl: (i, 0)),
                      pl.BlockSpec((tm, LANES), lambda j, i, tbl: (i, 0))],
            out_specs=pl.BlockSpec((tm, tn), lambda j, i, tbl: (i, j)),
            scratch_shapes=[pltpu.VMEM((tn, d), BF16)] if cast else []),
        out_shape=jax.ShapeDtypeStruct((n, ncol), out_dtype),
        compiler_params=_cparams(("arbitrary", "arbitrary")),
        name="proj_" + epi,
    )(jnp.asarray(blocks, jnp.int32), xb, *([w_t] * len(w_specs)), cos, sin)


SWA_GROUPS = 2


def _swa_kernel(sink_ref, q_ref, kp_ref, kc_ref, vp_ref, vc_ref, ga_ref, o_ref, s_ref, p_ref):
    n = pl.program_id(1)
    lane = lax.broadcasted_iota(jnp.int32, (1, LANES), 1)
    lo = lane < A_HEAD_DIM
    nk = 2 * TQ
    qpos = n * TQ + lax.broadcasted_iota(jnp.int32, (TQ, nk), 0)
    kpos = (n - 1) * TQ + lax.broadcasted_iota(jnp.int32, (TQ, nk), 1)
    delta = qpos - kpos
    mask = (kpos >= 0) & (delta >= 0) & (delta < A_WINDOW)
    rep = A_HEADS // A_KV_HEADS
    npair = rep // 2
    zq = jnp.zeros((TQ, LANES), q_ref.dtype)
    for gg in range(SWA_GROUPS):
        g = pl.program_id(2) * SWA_GROUPS + gg
        gl = slice(gg * LANES, (gg + 1) * LANES)
        kk = jnp.concatenate([kp_ref[:, gl], kc_ref[:, gl]], axis=0)
        vv = jnp.concatenate([vp_ref[:, gl], vc_ref[:, gl]], axis=0)
        zv = jnp.zeros_like(vv)
        v_halves = (jnp.where(lo, vv, zv), jnp.where(lo, zv, vv))
        qcols = [slice((gg * npair + c) * LANES, (gg * npair + c + 1) * LANES) for c in range(npair)]
        q_lo = [jnp.where(lo, q_ref[:, c], zq) for c in qcols]
        q_hi = [jnp.where(lo, zq, q_ref[:, c]) for c in qcols]
        r0 = gg * rep * TQ
        s_ref[r0:r0 + rep * TQ, :] = _dot_nt(jnp.concatenate(q_lo + q_hi, axis=0), kk)
        for half in range(2):
            for pair in range(npair):
                sl = slice(r0 + (half * npair + pair) * TQ, r0 + (half * npair + pair + 1) * TQ)
                s = jnp.where(mask, s_ref[sl, :], NEG_INF)
                sk = sink_ref[g * rep + pair * 2 + half] * LOG2E
                m = jnp.maximum(jnp.max(s, axis=-1, keepdims=True), sk)
                e = jnp.exp2(s - m)
                p = e / (jnp.sum(e, axis=-1, keepdims=True) + jnp.exp2(sk - m))
                p_ref[sl, :] = p.astype(BF16)
        half_rows = npair * TQ
        out = (_dot(p_ref[r0:r0 + half_rows, :], v_halves[0])
               + _dot(p_ref[r0 + half_rows:r0 + 2 * half_rows, :], v_halves[1]))
        for pair, c in enumerate(qcols):
            o_ref[:, c] = ga_ref[:, c] * out[pair * TQ:(pair + 1) * TQ, :]


def _swa_attention(q, k, v, sinks, gates, bsz, seq):
    n = bsz * seq
    nb = seq // TQ
    rows = SWA_GROUPS * (A_HEADS // A_KV_HEADS) * TQ
    qw = SWA_GROUPS * (A_HEADS // A_KV_HEADS) * A_HEAD_DIM
    kw = SWA_GROUPS * LANES
    return pl.pallas_call(
        _swa_kernel,
        grid=(bsz, nb, A_KV_HEADS // SWA_GROUPS),
        in_specs=[pl.BlockSpec(memory_space=pltpu.SMEM),
                  pl.BlockSpec((TQ, qw), lambda b, i, g: (b * nb + i, g)),
                  pl.BlockSpec((TQ, kw), lambda b, i, g: (b * nb + jnp.maximum(i - 1, 0), g)),
                  pl.BlockSpec((TQ, kw), lambda b, i, g: (b * nb + i, g)),
                  pl.BlockSpec((TQ, kw), lambda b, i, g: (b * nb + jnp.maximum(i - 1, 0), g)),
                  pl.BlockSpec((TQ, kw), lambda b, i, g: (b * nb + i, g)),
                  pl.BlockSpec((TQ, qw), lambda b, i, g: (b * nb + i, g))],
        out_specs=pl.BlockSpec((TQ, qw), lambda b, i, g: (b * nb + i, g)),
        out_shape=jax.ShapeDtypeStruct((n, A_HEADS * A_HEAD_DIM), F32),
        scratch_shapes=[pltpu.VMEM((rows, 2 * TQ), F32), pltpu.VMEM((rows, 2 * TQ), BF16)],
        compiler_params=_cparams(("parallel", "parallel", "parallel")),
        name="swa",
    )(sinks, q, k, k, v, v, gates)


def _compress_kernel(t_ref, pe_ref, w1_ref, w2_ref, o_ref):
    nch = o_ref.shape[0]
    u = jnp.zeros((nch, CMP_HIDDEN), F32)
    v = jnp.zeros((nch, CMP_HIDDEN), F32)
    for l in range(CMP_STRIDE):
        t_l = t_ref[pl.ds(l, nch, stride=CMP_STRIDE), :]
        a = (t_l + pe_ref[l:l + 1, :]).astype(BF16)
        b = (t_l + pe_ref[CMP_STRIDE + l:CMP_STRIDE + l + 1, :]).astype(BF16)
        u = u + _dot(a, w1_ref[l * LANES:(l + 1) * LANES, :])
        v = v + _dot(b, w1_ref[(CMP_STRIDE + l) * LANES:(CMP_STRIDE + l + 1) * LANES, :])
    pre = u + pltpu.roll(v, nch - 1, 0)
    h = jax.nn.gelu(pre)
    o_ref[...] = _dot(h.astype(BF16), w2_ref[...]).astype(o_ref.dtype)


def _compress(t, col0, pe, w1, w2, bsz, seq):
    nch = seq // CMP_STRIDE
    cb0 = col0 // LANES
    return pl.pallas_call(
        _compress_kernel,
        grid=(bsz, B_KV_HEADS),
        in_specs=[pl.BlockSpec((seq, LANES), lambda b, g: (b, cb0 + g)),
                  pl.BlockSpec((CMP_LEN, LANES), lambda b, g: (0, 0)),
                  pl.BlockSpec((CMP_LEN * LANES, CMP_HIDDEN), lambda b, g: (0, 0)),
                  pl.BlockSpec((CMP_HIDDEN, LANES), lambda b, g: (0, 0))],
        out_specs=pl.BlockSpec((None, None, nch, LANES), lambda b, g: (b, g, 0, 0)),
        out_shape=jax.ShapeDtypeStruct((bsz, B_KV_HEADS, nch, LANES), BF16),
        compiler_params=_cparams(("parallel", "parallel")),
        name="nsa_compress",
    )(t, pe, w1, w2)


NSA_TK = 512
NSA_NKW = (-(-(WIN_LEN - 1) // TQ) + 1) * TQ


def _nsa_kernel(q_ref, gate_ref, kcmp_ref, vcmp_ref, ks_ref, vs_ref, kw_ref, vw_ref, ya_ref, gb_ref, o_ref,
                qs_ref, s_ref, p_ref, m_ref, l_ref, acc_ref, ocmp_ref, *, seq):
    qi = pl.program_id(2)
    rep = B_HEADS // B_KV_HEADS
    nch = seq // CMP_STRIDE
    ns = seq // SLC_LEN
    n_sel = min(SLC_TOPK, ns)
    heads = [slice(r * TQ, (r + 1) * TQ) for r in range(rep)]
    for r in range(rep):
        qs_ref[heads[r], :] = q_ref[:, r * LANES:(r + 1) * LANES]
    tpos = qi * TQ + lax.broadcasted_iota(jnp.int32, (TQ, 1), 0)

    s_ref[:, :nch] = _dot_nt(qs_ref[...], kcmp_ref[...])
    c_end = lax.broadcasted_iota(jnp.int32, (1, nch), 1) * CMP_STRIDE + (CMP_LEN - 1)
    valid = c_end <= tpos
    p_grp = jnp.zeros((TQ, nch), F32)
    for sl in heads:
        s = jnp.where(valid, s_ref[sl, :nch], NEG_INF)
        m = jnp.max(s, axis=-1, keepdims=True)
        e = jnp.where(valid, jnp.exp2(s - m), 0.0)
        den = jnp.sum(e, axis=-1, keepdims=True)
        p = e / jnp.where(den > 0, den, 1.0)
        p_grp = p_grp + p
        p_ref[sl, :nch] = p.astype(BF16)
    ocmp_ref[...] = _dot(p_ref[:, :nch], vcmp_ref[...])

    jrow = lax.broadcasted_iota(jnp.int32, (ns, 1), 0)
    d = lax.broadcasted_iota(jnp.int32, (ns, nch), 1) - (SLC_LEN // CMP_STRIDE) * jrow
    msel = jnp.where((d == -1) | (d == 3), 1.0, jnp.where((d >= 0) & (d <= 2), 2.0, 0.0)).astype(BF16)
    p_hi = p_grp.astype(BF16)
    r1 = p_grp - p_hi.astype(F32)
    p_mid = r1.astype(BF16)
    p_lo = (r1 - p_mid.astype(F32)).astype(BF16)
    p_slc = _dot_nt(msel, p_hi) + _dot_nt(msel, p_mid) + _dot_nt(msel, p_lo)

    tpos_l = qi * TQ + lax.broadcasted_iota(jnp.int32, (1, TQ), 1)
    cur = tpos_l // SLC_LEN
    forced = (jrow == 0) | (jrow == cur) | (jrow == cur - 1)
    score = jnp.where(forced, 1e9, jnp.where(jrow * SLC_LEN <= tpos_l, p_slc, NEG_INF))
    rank = jnp.zeros((ns, TQ), jnp.int32)
    for i in range(ns):
        si = score[i:i + 1, :]
        rank = rank + ((si > score) | ((si == score) & (jrow > i))).astype(jnp.int32)
    sel_t = (rank < n_sel).astype(F32)
    if ns < LANES:
        sel_t = jnp.concatenate([sel_t, jnp.zeros((LANES - ns, TQ), F32)], axis=0)
    sel = sel_t.T.astype(BF16)

    m_ref[...] = jnp.full_like(m_ref, NEG_INF)
    l_ref[...] = jnp.zeros_like(l_ref)
    acc_ref[...] = jnp.zeros_like(acc_ref)
    jexp = lax.broadcasted_iota(jnp.int32, (LANES, 1), 0)

    def slc_body(kt, c):
        row0 = pl.multiple_of(kt * NSA_TK, NSA_TK)
        s_ref[:, :NSA_TK] = _dot_nt(qs_ref[...], ks_ref[pl.ds(row0, NSA_TK), :])
        kpos = kt * NSA_TK + lax.broadcasted_iota(jnp.int32, (1, NSA_TK), 1)
        expand = (jexp == kpos // SLC_LEN).astype(BF16)
        keep = (_dot(sel, expand) > 0.5) & (kpos <= tpos)
        for sl in heads:
            s = jnp.where(keep, s_ref[sl, :NSA_TK], NEG_INF)
            m_old = m_ref[sl, :]
            m_new = jnp.maximum(m_old, jnp.max(s, axis=-1, keepdims=True))
            alpha = jnp.exp2(m_old - m_new)
            e = jnp.exp2(s - m_new)
            l_ref[sl, :] = alpha * l_ref[sl, :] + jnp.sum(e, axis=-1, keepdims=True)
            m_ref[sl, :] = m_new
            acc_ref[sl, :] = alpha * acc_ref[sl, :]
            p_ref[sl, :NSA_TK] = e.astype(BF16)
        acc_ref[...] += _dot(p_ref[:, :NSA_TK], vs_ref[pl.ds(row0, NSA_TK), :])
        return c

    lax.fori_loop(0, (qi * TQ + TQ + NSA_TK - 1) // NSA_TK, slc_body, 0)

    nprev = NSA_NKW // TQ - 1
    k_parts, v_parts = [], []
    for i in range(nprev + 1):
        row0 = pl.multiple_of(jnp.maximum(qi - nprev + i, 0) * TQ, TQ)
        k_parts.append(kw_ref[pl.ds(row0, TQ), :])
        v_parts.append(vw_ref[pl.ds(row0, TQ), :])
    s_ref[:, :NSA_NKW] = _dot_nt(qs_ref[...], jnp.concatenate(k_parts, axis=0))
    kposw = (qi - nprev) * TQ + lax.broadcasted_iota(jnp.int32, (1, NSA_NKW), 1)
    dlt = tpos - kposw
    keepw = (kposw >= 0) & (dlt >= 0) & (dlt < WIN_LEN)
    for sl in heads:
        s = jnp.where(keepw, s_ref[sl, :NSA_NKW], NEG_INF)
        e = jnp.exp2(s - jnp.max(s, axis=-1, keepdims=True))
        p_ref[sl, :NSA_NKW] = (e / jnp.sum(e, axis=-1, keepdims=True)).astype(BF16)
    o_win = _dot(p_ref[:, :NSA_NKW], jnp.concatenate(v_parts, axis=0))

    gate = gate_ref[...]
    for r, sl in enumerate(heads):
        cols = slice(r * LANES, (r + 1) * LANES)
        o_b = (gate[:, r:r + 1] * ocmp_ref[sl, :]
               + gate[:, rep + r:rep + r + 1] * (acc_ref[sl, :] / l_ref[sl, :])
               + gate[:, 2 * rep + r:2 * rep + r + 1] * o_win[sl])
        o_ref[:, cols] = (ya_ref[:, cols] + gb_ref[:, cols] * o_b).astype(o_ref.dtype)


def _nsa_attention(qk, v, gnx, gates, ya, kcmp, vcmp, bsz, seq, cols):
    n = bsz * seq
    nq = seq // TQ
    nch = seq // CMP_STRIDE
    rep = B_HEADS // B_KV_HEADS
    qb0 = cols["q"] // (rep * LANES)
    gbb = cols["g_b"] // (rep * LANES)
    ksb, kwb, vsb, vwb, gb0 = (cols[k] // LANES for k in ("ks", "kw", "vs", "vw", "gate"))
    return pl.pallas_call(
        functools.partial(_nsa_kernel, seq=seq),
        grid=(bsz, B_KV_HEADS, nq),
        in_specs=[pl.BlockSpec((TQ, rep * LANES), lambda b, g, i: (b * nq + i, qb0 + g)),
                  pl.BlockSpec((TQ, LANES), lambda b, g, i: (b * nq + i, gb0 + g)),
                  pl.BlockSpec((None, None, nch, LANES), lambda b, g, i: (b, g, 0, 0)),
                  pl.BlockSpec((None, None, nch, LANES), lambda b, g, i: (b, g, 0, 0)),
                  pl.BlockSpec((seq, LANES), lambda b, g, i: (b, ksb + g)),
                  pl.BlockSpec((seq, LANES), lambda b, g, i: (b, vsb + g)),
                  pl.BlockSpec((seq, LANES), lambda b, g, i: (b, kwb + g)),
                  pl.BlockSpec((seq, LANES), lambda b, g, i: (b, vwb + g)),
                  pl.BlockSpec((TQ, rep * LANES), lambda b, g, i: (b * nq + i, g)),
                  pl.BlockSpec((TQ, rep * LANES), lambda b, g, i: (b * nq + i, gbb + g))],
        out_specs=pl.BlockSpec((TQ, rep * LANES), lambda b, g, i: (b * nq + i, g)),
        out_shape=jax.ShapeDtypeStruct((n, B_HEADS * B_HEAD_DIM), BF16),
        scratch_shapes=[pltpu.VMEM((rep * TQ, LANES), BF16),
                        pltpu.VMEM((rep * TQ, max(NSA_NKW, NSA_TK, nch)), F32),
                        pltpu.VMEM((rep * TQ, max(NSA_NKW, NSA_TK, nch)), BF16),
                        pltpu.VMEM((rep * TQ, 1), F32), pltpu.VMEM((rep * TQ, 1), F32),
                        pltpu.VMEM((rep * TQ, LANES), F32), pltpu.VMEM((rep * TQ, LANES), F32)],
        compiler_params=_cparams(("parallel", "parallel", "arbitrary")),
        name="nsa",
    )(qk, gnx, kcmp, vcmp, qk, v, qk, v, ya, gates)


def _oproj_kernel(y_ref, wo_ref, x_ref, g_ref, b_ref, wr_ref, x1_ref, x1p_ref, sc_ref, z_ref, *, alpha):
    j = pl.program_id(1)
    z_ref[j] = _dot(y_ref[...], wo_ref[...])

    @pl.when(j == pl.num_programs(1) - 1)
    def _():
        nj, tm, _ = z_ref.shape
        piece = min(128, tm)
        for r0 in range(0, tm, piece):
            rows = slice(r0, r0 + piece)
            z = jnp.concatenate([z_ref[jj, rows, :] for jj in range(nj)], axis=1)
            x1 = _layer_norm(alpha * x_ref[rows, :] + z, g_ref[...], b_ref[...])
            x1_ref[rows, :] = x1
            x1p_ref[rows, :] = _pack_halves(x1)
            sc_ref[rows, :] = _sigmoid(_dot(x1.astype(BF16), wr_ref[...]))


def _out_proj(y, w_o, x, ln_g, ln_b, w_r, alpha, tm=512, tn=512):
    n, d = x.shape
    tm = min(tm, n)
    once = dict(pipeline_mode=pl.Buffered(1))
    return pl.pallas_call(
        functools.partial(_oproj_kernel, alpha=alpha),
        grid=(n // tm, d // tn),
        in_specs=[pl.BlockSpec((tm, d), lambda i, j: (i, 0)),
                  pl.BlockSpec((d, tn), lambda i, j: (0, j)),
                  pl.BlockSpec((tm, d), lambda i, j: (i, 0), **once),
                  pl.BlockSpec((1, d), lambda i, j: (0, 0)),
                  pl.BlockSpec((1, d), lambda i, j: (0, 0)),
                  pl.BlockSpec((d, LANES), lambda i, j: (0, 0))],
        out_specs=[pl.BlockSpec((tm, d), lambda i, j: (i, 0), **once),
                   pl.BlockSpec((tm, d // 2), lambda i, j: (i, 0), **once),
                   pl.BlockSpec((tm, LANES), lambda i, j: (i, 0))],
        out_shape=[jax.ShapeDtypeStruct((n, d), F32), jax.ShapeDtypeStruct((n, d // 2), jnp.uint32),
                   jax.ShapeDtypeStruct((n, LANES), F32)],
        scratch_shapes=[pltpu.VMEM((d // tn, tm, tn), F32)],
        compiler_params=_cparams(("parallel", "arbitrary")),
        name="oproj_ln1_router",
    )(y, w_o, x, ln_g, ln_b, w_r)


def _route_kernel(sc_ref, bias_ref, ek_ref, rk_ref, gk_ref, cnt_ref, run_ref):
    i = pl.program_id(0)
    tm = sc_ref.shape[0]

    @pl.when(i == 0)
    def _():
        run_ref[...] = jnp.zeros_like(run_ref)

    sc = sc_ref[...]
    lane = lax.broadcasted_iota(jnp.int32, (1, LANES), 1)
    is_e = lane < N_EXPERTS
    biased = jnp.where(is_e, sc + bias_ref[...], -jnp.inf)
    rank = jnp.zeros((tm, LANES), jnp.int32)
    for e in range(N_EXPERTS):
        be = biased[:, e:e + 1]
        rank = rank + ((be > biased) | ((be == biased) & (e < lane))).astype(jnp.int32)
    sel = (rank < TOP_K) & is_e
    sv = jnp.where(sel, sc, 0.0)
    gates = sv / jnp.sum(sv, axis=-1, keepdims=True) * ROUTED_SCALE
    selb = sel.astype(BF16)
    ri = lax.broadcasted_iota(jnp.int32, (tm, tm), 0)
    ci = lax.broadcasted_iota(jnp.int32, (tm, tm), 1)
    cum = _dot((ci < ri).astype(BF16), selb) + run_ref[...]
    run_ref[...] += jnp.sum(sel.astype(F32), axis=0, keepdims=True)
    cnt_ref[...] = run_ref[...]
    ai = lax.broadcasted_iota(jnp.int32, (LANES, LANES), 0)
    bi = lax.broadcasted_iota(jnp.int32, (LANES, LANES), 1)
    before = _dot(selb, (ai < bi).astype(BF16))
    lane_f = lane.astype(F32)
    ek = jnp.zeros((tm, LANES), F32)
    rk = jnp.zeros((tm, LANES), F32)
    gk = jnp.zeros((tm, LANES), F32)
    for k in range(TOP_K):
        oh = sel & (before == k)
        ek = jnp.where(lane == k, jnp.sum(jnp.where(oh, lane_f, 0.0), axis=-1, keepdims=True), ek)
        rk = jnp.where(lane == k, jnp.sum(jnp.where(oh, cum, 0.0), axis=-1, keepdims=True), rk)
        gk = jnp.where(lane == k, jnp.sum(jnp.where(oh, gates, 0.0), axis=-1, keepdims=True), gk)
    ek_ref[...] = ek.astype(jnp.int32)
    rk_ref[...] = rk.astype(jnp.int32)
    gk_ref[...] = gk


def _route(scores, bias, tm=256):
    n = scores.shape[0]
    tm = min(tm, n)
    blk = pl.BlockSpec((tm, LANES), lambda i: (i, 0))
    one = pl.BlockSpec((1, LANES), lambda i: (0, 0))
    return pl.pallas_call(
        _route_kernel,
        grid=(n // tm,),
        in_specs=[blk, one],
        out_specs=[blk, blk, blk, one],
        out_shape=[jax.ShapeDtypeStruct((n, LANES), jnp.int32), jax.ShapeDtypeStruct((n, LANES), jnp.int32),
                   jax.ShapeDtypeStruct((n, LANES), F32), jax.ShapeDtypeStruct((1, LANES), F32)],
        scratch_shapes=[pltpu.VMEM((1, LANES), F32)],
        compiler_params=_cparams(("arbitrary",)),
        name="route",
    )(scores, bias)


def _row_copy(src, s_row, dst, d_row, sem):
    return pltpu.make_async_copy(src.at[pl.ds(s_row, 1)], dst.at[pl.ds(d_row, 1)], sem)


def _scatter_kernel(cnt_ref, pad_ref, pst_ref, nu_ref, dest_ref, x_ref, xs_ref, zrow_ref, zblk_ref, sem,
                    *, tm, n_tok_steps, n_blocks):
    i = pl.program_id(0)

    @pl.when(i < n_tok_steps)
    def _():
        def issue(t, c):
            for k in range(TOP_K):
                _row_copy(x_ref, t, xs_ref, dest_ref[t * TOP_K + k], sem).start()
            return c

        lax.fori_loop(0, tm, issue, 0)

        for k in range(TOP_K):
            pltpu.make_async_copy(x_ref, xs_ref.at[pl.ds(0, tm)], sem).wait()

    @pl.when((i >= n_tok_steps) & (i < n_tok_steps + N_EXPERTS))
    def _():
        e = i - n_tok_steps
        zrow_ref[...] = jnp.zeros_like(zrow_ref)
        lo, hi, base = cnt_ref[e], pad_ref[e], pst_ref[e]

        def issue(r, c):
            pltpu.make_async_copy(zrow_ref, xs_ref.at[pl.ds(base + r, 1)], sem).start()
            return c

        lax.fori_loop(lo, hi, issue, 0)

        def drain(r, c):
            pltpu.make_async_copy(zrow_ref, xs_ref.at[pl.ds(0, 1)], sem).wait()
            return c

        lax.fori_loop(lo, hi, drain, 0)

    @pl.when(i == n_tok_steps + N_EXPERTS)
    def _():
        zblk_ref[...] = jnp.zeros_like(zblk_ref)

        def issue(p, c):
            pltpu.make_async_copy(zblk_ref, xs_ref.at[pl.ds(p * MOE_BLOCK, MOE_BLOCK)], sem).start()
            return c

        lax.fori_loop(nu_ref[0], n_blocks, issue, 0)

        def drain(p, c):
            pltpu.make_async_copy(zblk_ref, xs_ref.at[pl.ds(0, MOE_BLOCK)], sem).wait()
            return c

        lax.fori_loop(nu_ref[0], n_blocks, drain, 0)


def _scatter_rows(x1, dest_flat, counts, padded, pstarts, n_used, n_blocks, tm=128):
    n, d = x1.shape
    tm = min(tm, n)
    n_tok_steps = n // tm
    return pl.pallas_call(
        functools.partial(_scatter_kernel, tm=tm, n_tok_steps=n_tok_steps, n_blocks=n_blocks),
        grid_spec=pltpu.PrefetchScalarGridSpec(
            num_scalar_prefetch=4,
            grid=(n_tok_steps + N_EXPERTS + 1,),
            in_specs=[pl.BlockSpec((tm * TOP_K,), lambda i, *_: (jnp.minimum(i, n_tok_steps - 1),),
                                   memory_space=pltpu.SMEM),
                      pl.BlockSpec((tm, d), lambda i, *_: (jnp.minimum(i, n_tok_steps - 1), 0))],
            out_specs=pl.BlockSpec(memory_space=pl.ANY),
            scratch_shapes=[pltpu.VMEM((1, d), x1.dtype), pltpu.VMEM((MOE_BLOCK, d), x1.dtype),
                            pltpu.SemaphoreType.DMA(())]),
        out_shape=jax.ShapeDtypeStruct((n_blocks * MOE_BLOCK, d), x1.dtype),
        compiler_params=_cparams(("arbitrary",)),
        name="moe_scatter",
    )(counts, padded, pstarts, n_used, dest_flat, x1)


def _expert_kernel(be_ref, nu_ref, xs_ref, wg_ref, wu_ref, wd_ref, y_ref):
    del be_ref
    used = pl.program_id(0) < nu_ref[0]

    @pl.when(used)
    def _():
        lo, hi = _unpack_halves(xs_ref[...])
        lo, hi = lo.astype(BF16), hi.astype(BF16)
        dh = lo.shape[1]
        hg = _dot(lo, wg_ref[:dh, :]) + _dot(hi, wg_ref[dh:, :])
        hu = _dot(lo, wu_ref[:dh, :]) + _dot(hi, wu_ref[dh:, :])
        h = (hg * _sigmoid(hg)) * hu
        y_ref[...] = _pack_halves(_dot(h.astype(BF16), wd_ref[...]))

    @pl.when(jnp.logical_not(used))
    def _():
        y_ref[...] = jnp.zeros_like(y_ref)


def _expert_mlp(xs, block_e, n_used, wg, wu, wd):
    n_rows, dp = xs.shape
    nblk = n_rows // MOE_BLOCK
    d, ff = wg.shape[-2:]
    assert d == 2 * dp

    def row_map(p, be, nu):
        return (jnp.minimum(p, nu[0] - 1), 0)

    return pl.pallas_call(
        _expert_kernel,
        grid_spec=pltpu.PrefetchScalarGridSpec(
            num_scalar_prefetch=2,
            grid=(nblk,),
            in_specs=[pl.BlockSpec((MOE_BLOCK, dp), row_map),
                      pl.BlockSpec((None, d, ff), lambda p, be, nu: (be[p], 0, 0)),
                      pl.BlockSpec((None, d, ff), lambda p, be, nu: (be[p], 0, 0)),
                      pl.BlockSpec((None, ff, d), lambda p, be, nu: (be[p], 0, 0))],
            out_specs=pl.BlockSpec((MOE_BLOCK, dp), lambda p, be, nu: (p, 0))),
        out_shape=jax.ShapeDtypeStruct((n_rows, dp), jnp.uint32),
        compiler_params=_cparams(("arbitrary",)),
        name="moe_experts",
    )(block_e, n_used, xs, wg, wu, wd)


COMBINE_PARTS = 4


def _combine_kernel(dest_ref, x1_ref, gk_ref, sg_ref, su_ref, sd_ref, g_ref, b_ref, ys_ref,
                    o_ref, ybuf_ref, sh_ref, sem, *, tm, alpha):
    tp = tm // COMBINE_PARTS

    def issue(part):
        def body(t, c):
            for k in range(TOP_K):
                _row_copy(ys_ref, dest_ref[t * TOP_K + k], ybuf_ref, k * tm + t, sem.at[part]).start()
            return c

        lax.fori_loop(part * tp, (part + 1) * tp, body, 0)

    issue(0)
    issue(1)
    xb = x1_ref[...].astype(BF16)
    hg = _dot(xb, sg_ref[...])
    hu = _dot(xb, su_ref[...])
    sh_ref[...] = _dot(((hg * _sigmoid(hg)) * hu).astype(BF16), sd_ref[...])

    for part in range(COMBINE_PARTS):
        rows = slice(part * tp, (part + 1) * tp)
        for k in range(TOP_K):
            pltpu.make_async_copy(ys_ref.at[pl.ds(0, tp)], ybuf_ref.at[pl.ds(0, tp)], sem.at[part]).wait()
        if part + 2 < COMBINE_PARTS:
            issue(part + 2)
        gk = gk_ref[rows, :]
        r_lo = jnp.zeros((tp, ybuf_ref.shape[1]), F32)
        r_hi = jnp.zeros((tp, ybuf_ref.shape[1]), F32)
        for k in range(TOP_K):
            lo, hi = _unpack_halves(ybuf_ref[k * tm + part * tp:k * tm + (part + 1) * tp, :])
            r_lo = r_lo + gk[:, k:k + 1] * lo
            r_hi = r_hi + gk[:, k:k + 1] * hi
        routed = jnp.concatenate([r_lo, r_hi], axis=1)
        o_ref[rows, :] = _layer_norm(alpha * x1_ref[rows, :] + (routed + sh_ref[rows, :]), g_ref[...], b_ref[...])


def _combine(x1, dest_flat, gk, ys, sg, su, sd, ln_g, ln_b, alpha, tm=128):
    n, d = x1.shape
    tm = min(tm, n)
    ff = sg.shape[-1]
    return pl.pallas_call(
        functools.partial(_combine_kernel, tm=tm, alpha=alpha),
        grid=(n // tm,),
        in_specs=[pl.BlockSpec((tm * TOP_K,), lambda i: (i,), memory_space=pltpu.SMEM),
                  pl.BlockSpec((tm, d), lambda i: (i, 0)),
                  pl.BlockSpec((tm, LANES), lambda i: (i, 0)),
                  pl.BlockSpec((d, ff), lambda i: (0, 0), pipeline_mode=pl.Buffered(1)),
                  pl.BlockSpec((d, ff), lambda i: (0, 0), pipeline_mode=pl.Buffered(1)),
                  pl.BlockSpec((ff, d), lambda i: (0, 0), pipeline_mode=pl.Buffered(1)),
                  pl.BlockSpec((1, d), lambda i: (0, 0)),
                  pl.BlockSpec((1, d), lambda i: (0, 0)),
                  pl.BlockSpec(memory_space=pl.ANY)],
        out_specs=pl.BlockSpec((tm, d), lambda i: (i, 0)),
        out_shape=jax.ShapeDtypeStruct((n, d), F32),
        scratch_shapes=[pltpu.VMEM((TOP_K * tm, d // 2), jnp.uint32), pltpu.VMEM((tm, d), F32),
                        pltpu.SemaphoreType.DMA((COMBINE_PARTS,))],
        compiler_params=_cparams(("arbitrary",)),
        name="moe_combine_ln2",
    )(dest_flat, x1, gk, sg, su, sd, ln_g, ln_b, ys)


def _rope_kernel(pos_ref, inv_ref, sign_ref, cos_ref, sin_ref):
    ang = pos_ref[...].astype(F32) * inv_ref[...]
    cos_ref[...] = jnp.cos(ang)
    sin_ref[...] = jnp.sin(ang) * sign_ref[...]


def _rope_tables(positions, dh, tm=512):
    half = dh // 2
    inv = ROPE_THETA ** (-jnp.arange(half, dtype=F32) * 2.0 / dh)
    reps = LANES // dh
    inv_l = jnp.tile(jnp.concatenate([inv, inv]), reps).reshape(1, LANES)
    sign_l = jnp.tile(jnp.concatenate([-jnp.ones((half,), F32), jnp.ones((half,), F32)]), reps).reshape(1, LANES)
    pos = positions.reshape(-1, 1)
    n = pos.shape[0]
    tm = min(tm, n)
    one = pl.BlockSpec((1, LANES), lambda i: (0, 0))
    blk = pl.BlockSpec((tm, LANES), lambda i: (i, 0))
    return pl.pallas_call(
        _rope_kernel,
        grid=(n // tm,),
        in_specs=[pl.BlockSpec((tm, 1), lambda i: (i, 0)), one, one],
        out_specs=[blk, blk],
        out_shape=[jax.ShapeDtypeStruct((n, LANES), F32)] * 2,
        compiler_params=_cparams(("parallel",)),
        name="rope_tables",
    )(pos, inv_l, sign_l)


def _dup_heads(w_rows, heads, dh):
    d = w_rows.shape[1]
    w3 = w_rows.reshape(heads, dh, d)
    return jnp.concatenate([w3, w3], axis=1).reshape(heads * 2 * dh, d)


def _hybrid_layer(x, positions, w_in, a_sinks, cmp_k_pos, cmp_k_w1, cmp_k_w2, cmp_v_pos, cmp_v_w1,
                  cmp_v_w2, w_o, ln1_g, ln1_b, w_router, router_bias, exp_w_gate, exp_w_up, exp_w_down,
                  sh_w_gate, sh_w_up, sh_w_down, ln2_g, ln2_b, alpha):
    bsz, seq, d = x.shape
    n = bsz * seq
    assert seq % NSA_TK == 0 and (seq // CMP_STRIDE) % LANES == 0 and A_WINDOW <= TQ
    assert 3 <= seq // SLC_LEN <= LANES
    xf = x.reshape(n, d)
    xb = xf.astype(BF16)

    qa_w, kva = A_HEADS * A_HEAD_DIM, A_KV_HEADS * A_HEAD_DIM
    qb_w, kvb = B_HEADS * B_HEAD_DIM, B_KV_HEADS * B_HEAD_DIM
    widths = [qa_w, kva, kva, qb_w, kvb, kvb, kvb, kvb, kvb, kvb, 3 * B_HEADS, d, d]
    offs = np.concatenate([[0], np.cumsum(widths)]).tolist()
    (o_qa, o_ka, o_va, o_qb, o_kc, o_vc, o_ks, o_vs, o_kw, o_vw, o_gn, o_ga, o_end) = (
        offs[i] for i in (0, 1, 2, 3, 4, 5, 6, 7, 8, 9, 10, 11, 13))
    w_t = w_in.T
    blk = lambda start, width: list(range(start // PROJ_TN, (start + width) // PROJ_TN))
    assert all(o % PROJ_TN == 0 for o in (o_qa, o_qb, o_kc, o_vc, o_ks, o_vs, o_kw, o_vw))
    rep = B_HEADS // B_KV_HEADS
    w_gn = w_t[o_gn:o_ga].reshape(3, B_KV_HEADS, rep, d).transpose(1, 0, 2, 3).reshape(B_KV_HEADS, 3 * rep, d)
    w_gn_x = jnp.pad(w_gn, ((0, 0), (0, LANES - 3 * rep), (0, 0))).reshape(B_KV_HEADS * LANES, d)
    w_gn_x = w_gn_x.astype(BF16)
    assert o_ga % PROJ_TN == 3 * B_HEADS and o_end == w_t.shape[0]
    w_ka2 = _dup_heads(w_t[o_ka:o_va], A_KV_HEADS, A_HEAD_DIM).astype(BF16)
    w_va2 = _dup_heads(w_t[o_va:o_qb], A_KV_HEADS, A_HEAD_DIM).astype(BF16)
    all_blocks = lambda w: list(range(w.shape[0] // PROJ_TN))

    cos64, sin64 = _rope_tables(positions, A_HEAD_DIM)
    cos128, sin128 = _rope_tables(positions, B_HEAD_DIM)

    q_a = _project(xb, w_t, blk(o_qa, qa_w), cos64, sin64, "rope64", BF16,
                   q_cols=qa_w, q_scale=A_HEAD_DIM ** -0.5 * LOG2E)
    k_a = _project(xb, w_ka2, all_blocks(w_ka2), cos64, sin64, "rope64", BF16)
    v_a = _project(xb, w_va2, all_blocks(w_va2), cos64, sin64, "none", BF16)
    qk_b = _project(xb, w_t, blk(o_qb, qb_w) + blk(o_ks, kvb) + blk(o_kw, kvb), cos128, sin128, "rope128", BF16,
                    q_cols=qb_w, q_scale=B_HEAD_DIM ** -0.5 * LOG2E)
    v_b = _project(xb, w_t, blk(o_vs, kvb) + blk(o_vw, kvb), cos128, sin128, "none", BF16)
    kc = _project(xb, w_t, blk(o_kc, kvb), cos128, sin128, "rope128", F32)
    vc = _project(xb, w_t, blk(o_vc, kvb), cos128, sin128, "none", F32)
    gates = _project(xb, w_t, blk(o_ga - 3 * B_HEADS, 2 * d), cos128, sin128, "sigmoid", F32,
                     tm=512, shift=3 * B_HEADS)
    gnx = _project(xb, w_gn_x, all_blocks(w_gn_x), cos128, sin128, "sigmoid", F32)

    ya = _swa_attention(q_a, k_a, v_a, a_sinks, gates, bsz, seq)

    kcmp = _compress(kc, 0, cmp_k_pos, cmp_k_w1.astype(BF16), cmp_k_w2.astype(BF16), bsz, seq)
    vcmp = _compress(vc, 0, cmp_v_pos, cmp_v_w1.astype(BF16), cmp_v_w2.astype(BF16), bsz, seq)
    nsa_cols = dict(q=0, ks=qb_w, kw=qb_w + kvb, vs=0, vw=kvb, g_b=d, gate=0)
    y = _nsa_attention(qk_b, v_b, gnx, gates, ya, kcmp, vcmp, bsz, seq, nsa_cols)

    w_r = jnp.pad(w_router, ((0, 0), (0, LANES - N_EXPERTS))).astype(BF16)
    x1, x1p, scores = _out_proj(y, w_o.astype(BF16), xf, ln1_g.reshape(1, d), ln1_b.reshape(1, d), w_r, alpha)

    bias = jnp.pad(router_bias.astype(F32), (0, LANES - N_EXPERTS)).reshape(1, LANES)
    ek, rk, gk, cnt = _route(scores, bias)

    counts = cnt[0, :N_EXPERTS].astype(jnp.int32)
    padded = (counts + MOE_BLOCK - 1) // MOE_BLOCK * MOE_BLOCK
    pends = jnp.cumsum(padded)
    pstarts = pends - padded
    nblk = -(-(n * TOP_K) // MOE_BLOCK) + N_EXPERTS
    blk_start = jnp.arange(nblk, dtype=jnp.int32) * MOE_BLOCK
    block_e = jnp.minimum(jnp.sum((pends[None, :] <= blk_start[:, None]).astype(jnp.int32), axis=1),
                          N_EXPERTS - 1)
    n_used = (pends[-1:] // MOE_BLOCK).astype(jnp.int32)
    hit = ek[:, :TOP_K, None] == jnp.arange(N_EXPERTS, dtype=jnp.int32)
    dest = (jnp.sum(jnp.where(hit, pstarts.astype(jnp.int32), 0), axis=-1) + rk[:, :TOP_K]).reshape(-1)

    xs = _scatter_rows(x1p, dest, counts, padded, pstarts.astype(jnp.int32), n_used, nblk)
    ys = _expert_mlp(xs, block_e, n_used, exp_w_gate.astype(BF16), exp_w_up.astype(BF16),
                     exp_w_down.astype(BF16))
    out = _combine(x1, dest, gk, ys, sh_w_gate.astype(BF16), sh_w_up.astype(BF16), sh_w_down.astype(BF16),
                   ln2_g.reshape(1, d), ln2_b.reshape(1, d), alpha)
    return out.reshape(bsz, seq, d)


def kernel(x, positions, w_in, a_sinks, cmp_k_pos, cmp_k_w1, cmp_k_w2, cmp_v_pos, cmp_v_w1, cmp_v_w2,
           w_o, ln1_g, ln1_b, w_router, router_bias, exp_w_gate, exp_w_up, exp_w_down,
           sh_w_gate, sh_w_up, sh_w_down, ln2_g, ln2_b):
    depth = w_in.shape[0]
    alpha = (2.0 * depth) ** 0.25
    for l in range(depth):
        x = _hybrid_layer(x, positions, w_in[l], a_sinks[l], cmp_k_pos[l], cmp_k_w1[l], cmp_k_w2[l],
                          cmp_v_pos[l], cmp_v_w1[l], cmp_v_w2[l], w_o[l], ln1_g[l], ln1_b[l],
                          w_router[l], router_bias[l], exp_w_gate[l], exp_w_up[l], exp_w_down[l],
                          sh_w_gate[l], sh_w_up[l], sh_w_down[l], ln2_g[l], ln2_b[l], alpha)
    return x
```

```python
import functools

import jax
import jax.numpy as jnp
import numpy as np
from jax import lax
from jax.experimental import pallas as pl
from jax.experimental.pallas import tpu as pltpu

F32 = jnp.float32
BF16 = jnp.bfloat16

ROPE_THETA = 10000.0
LN_EPS = 1e-5
NEG_INF = -1e30

A_HEADS, A_KV_HEADS, A_HEAD_DIM, A_WINDOW = 64, 8, 64, 128
B_HEADS, B_KV_HEADS, B_HEAD_DIM = 32, 4, 128
CMP_LEN, CMP_STRIDE, CMP_HIDDEN = 32, 16, 256
SLC_LEN, SLC_TOPK, WIN_LEN = 64, 16, 512
N_EXPERTS, TOP_K, EXPERT_FF, SHARED_FF = 64, 8, 512, 512
ROUTED_SCALE = 2.5
MOE_BLOCK = 256

LOG2E = 1.4426950408889634
LANES = 128
TQ = 128
VMEM_LIMIT = 56 * 1024 * 1024


def _cparams(sem, vmem_limit=VMEM_LIMIT):
    return pltpu.CompilerParams(dimension_semantics=sem, vmem_limit_bytes=vmem_limit)


def _dot(a, b):
    return jnp.dot(a, b, preferred_element_type=F32)


def _dot_nt(a, b):
    return lax.dot_general(a, b, (((1,), (1,)), ((), ())), preferred_element_type=F32)


def _sigmoid(v):
    return 1.0 / (1.0 + jnp.exp(-v))


def _pack_halves(v):
    h = v.shape[1] // 2
    lo = lax.bitcast_convert_type(v[:, :h].astype(BF16).astype(F32), jnp.uint32)
    hi = lax.bitcast_convert_type(v[:, h:].astype(BF16).astype(F32), jnp.uint32)
    return (hi & jnp.uint32(0xFFFF0000)) | (lo >> 16)


def _unpack_halves(w):
    lo = lax.bitcast_convert_type(w << 16, F32)
    hi = lax.bitcast_convert_type(w & jnp.uint32(0xFFFF0000), F32)
    return lo, hi


def _layer_norm(z, g, b):
    mu = jnp.mean(z, axis=-1, keepdims=True)
    zc = z - mu
    var = jnp.mean(zc * zc, axis=-1, keepdims=True)
    return zc * lax.rsqrt(var + LN_EPS) * g + b


def _proj_kernel(tbl_ref, x_ref, *rest, epi, q_blocks, q_scale, shift):
    del tbl_ref
    if shift:
        w_ref, w2_ref, cos_ref, sin_ref, o_ref, *scratch = rest
    else:
        w_ref, cos_ref, sin_ref, o_ref, *scratch = rest
    if scratch:
        (wb_ref,) = scratch

        @pl.when(pl.program_id(1) == 0)
        def _():
            if shift:
                keep = wb_ref.shape[0] - shift
                wb_ref[:keep, :] = w_ref[shift:, :].astype(BF16)
                wb_ref[keep:, :] = w2_ref[:shift, :].astype(BF16)
            else:
                wb_ref[...] = w_ref[...].astype(BF16)

        w = wb_ref[...]
    else:
        w = w_ref[...]
    acc = _dot_nt(x_ref[...], w)
    if epi == "none":
        o_ref[...] = acc.astype(o_ref.dtype)
    elif epi == "sigmoid":
        o_ref[...] = _sigmoid(acc).astype(o_ref.dtype)
    else:
        cs = jnp.where(pl.program_id(0) < q_blocks, q_scale, 1.0)
        cos = cos_ref[...] * cs
        sin = sin_ref[...] * cs
        lane = lax.broadcasted_iota(jnp.int32, (1, LANES), 1)
        for c in range(acc.shape[1] // LANES):
            y = acc[:, c * LANES:(c + 1) * LANES]
            if epi == "rope128":
                rot = pltpu.roll(y, 64, 1)
            else:
                rot = jnp.where((lane % 64) < 32, pltpu.roll(y, 96, 1), pltpu.roll(y, 32, 1))
            o_ref[:, c * LANES:(c + 1) * LANES] = (y * cos + rot * sin).astype(o_ref.dtype)


PROJ_TN = 512


def _project(xb, w_t, blocks, cos, sin, epi, out_dtype, q_cols=0, q_scale=1.0, tm=1024, shift=0):
    n, d = xb.shape
    tn = PROJ_TN
    ncol = len(blocks) * tn
    tm = min(tm, n)
    cast = w_t.dtype != BF16
    assert n % tm == 0 and q_cols % tn == 0 and (max(blocks) + 1) * tn + shift <= w_t.shape[0]
    assert shift % 16 == 0 and (cast or not shift)
    w_specs = [pl.BlockSpec((tn, d), lambda j, i, tbl: (tbl[j], 0))]
    if shift:
        w_specs.append(pl.BlockSpec((tn, d), lambda j, i, tbl: (tbl[j] + 1, 0)))
    return pl.pallas_call(
        functools.partial(_proj_kernel, epi=epi, q_blocks=q_cols // tn, q_scale=q_scale, shift=shift),
        grid_spec=pltpu.PrefetchScalarGridSpec(
            num_scalar_prefetch=1,
            grid=(ncol // tn, n // tm),
            in_specs=[pl.BlockSpec((tm, d), lambda j, i, tbl: (i, 0)), *w_specs,
                      pl.BlockSpec((tm, LANES), lambda j, i, tbl: (i, 0)),
                      pl.BlockSpec((tm, LANES), lambda j, i, tbl: (i, 0))],
            out_specs=pl.BlockSpec((tm, tn), lambda j, i, tbl: (i, j)),
            scratch_shapes=[pltpu.VMEM((tn, d), BF16)] if cast else []),
        out_shape=jax.ShapeDtypeStruct((n, ncol), out_dtype),
        compiler_params=_cparams(("arbitrary", "arbitrary")),
        name="proj_" + epi,
    )(jnp.asarray(blocks, jnp.int32), xb, *([w_t] * len(w_specs)), cos, sin)


SWA_GROUPS = 2


def _swa_kernel(sink_ref, q_ref, kp_ref, kc_ref, vp_ref, vc_ref, ga_ref, o_ref, s_ref, p_ref):
    n = pl.program_id(1)
    lane = lax.broadcasted_iota(jnp.int32, (1, LANES), 1)
    lo = lane < A_HEAD_DIM
    nk = 2 * TQ
    qpos = n * TQ + lax.broadcasted_iota(jnp.int32, (TQ, nk), 0)
    kpos = (n - 1) * TQ + lax.broadcasted_iota(jnp.int32, (TQ, nk), 1)
    delta = qpos - kpos
    mask = (kpos >= 0) & (delta >= 0) & (delta < A_WINDOW)
    rep = A_HEADS // A_KV_HEADS
    npair = rep // 2
    zq = jnp.zeros((TQ, LANES), q_ref.dtype)
    for gg in range(SWA_GROUPS):
        g = pl.program_id(2) * SWA_GROUPS + gg
        gl = slice(gg * LANES, (gg + 1) * LANES)
        kk = jnp.concatenate([kp_ref[:, gl], kc_ref[:, gl]], axis=0)
        vv = jnp.concatenate([vp_ref[:, gl], vc_ref[:, gl]], axis=0)
        zv = jnp.zeros_like(vv)
        v_halves = (jnp.where(lo, vv, zv), jnp.where(lo, zv, vv))
        qcols = [slice((gg * npair + c) * LANES, (gg * npair + c + 1) * LANES) for c in range(npair)]
        q_lo = [jnp.where(lo, q_ref[:, c], zq) for c in qcols]
        q_hi = [jnp.where(lo, zq, q_ref[:, c]) for c in qcols]
        r0 = gg * rep * TQ
        s_ref[r0:r0 + rep * TQ, :] = _dot_nt(jnp.concatenate(q_lo + q_hi, axis=0), kk)
        for half in range(2):
            for pair in range(npair):
                sl = slice(r0 + (half * npair + pair) * TQ, r0 + (half * npair + pair + 1) * TQ)
                s = jnp.where(mask, s_ref[sl, :], NEG_INF)
                sk = sink_ref[g * rep + pair * 2 + half] * LOG2E
                m = jnp.maximum(jnp.max(s, axis=-1, keepdims=True), sk)
                e = jnp.exp2(s - m)
                p = e / (jnp.sum(e, axis=-1, keepdims=True) + jnp.exp2(sk - m))
                p_ref[sl, :] = p.astype(BF16)
        half_rows = npair * TQ
        out = (_dot(p_ref[r0:r0 + half_rows, :], v_halves[0])
               + _dot(p_ref[r0 + half_rows:r0 + 2 * half_rows, :], v_halves[1]))
        for pair, c in enumerate(qcols):
            o_ref[:, c] = ga_ref[:, c] * out[pair * TQ:(pair + 1) * TQ, :]


def _swa_attention(q, k, v, sinks, gates, bsz, seq):
    n = bsz * seq
    nb = seq // TQ
    rows = SWA_GROUPS * (A_HEADS // A_KV_HEADS) * TQ
    qw = SWA_GROUPS * (A_HEADS // A_KV_HEADS) * A_HEAD_DIM
    kw = SWA_GROUPS * LANES
    return pl.pallas_call(
        _swa_kernel,
        grid=(bsz, nb, A_KV_HEADS // SWA_GROUPS),
        in_specs=[pl.BlockSpec(memory_space=pltpu.SMEM),
                  pl.BlockSpec((TQ, qw), lambda b, i, g: (b * nb + i, g)),
                  pl.BlockSpec((TQ, kw), lambda b, i, g: (b * nb + jnp.maximum(i - 1, 0), g)),
                  pl.BlockSpec((TQ, kw), lambda b, i, g: (b * nb + i, g)),
                  pl.BlockSpec((TQ, kw), lambda b, i, g: (b * nb + jnp.maximum(i - 1, 0), g)),
                  pl.BlockSpec((TQ, kw), lambda b, i, g: (b * nb + i, g)),
                  pl.BlockSpec((TQ, qw), lambda b, i, g: (b * nb + i, g))],
        out_specs=pl.BlockSpec((TQ, qw), lambda b, i, g: (b * nb + i, g)),
        out_shape=jax.ShapeDtypeStruct((n, A_HEADS * A_HEAD_DIM), F32),
        scratch_shapes=[pltpu.VMEM((rows, 2 * TQ), F32), pltpu.VMEM((rows, 2 * TQ), BF16)],
        compiler_params=_cparams(("parallel", "parallel", "parallel")),
        name="swa",
    )(sinks, q, k, k, v, v, gates)


def _compress_kernel(t_ref, pe_ref, w1_ref, w2_ref, o_ref):
    nch = o_ref.shape[0]
    u = jnp.zeros((nch, CMP_HIDDEN), F32)
    v = jnp.zeros((nch, CMP_HIDDEN), F32)
    for l in range(CMP_STRIDE):
        t_l = t_ref[pl.ds(l, nch, stride=CMP_STRIDE), :]
        a = (t_l + pe_ref[l:l + 1, :]).astype(BF16)
        b = (t_l + pe_ref[CMP_STRIDE + l:CMP_STRIDE + l + 1, :]).astype(BF16)
        u = u + _dot(a, w1_ref[l * LANES:(l + 1) * LANES, :])
        v = v + _dot(b, w1_ref[(CMP_STRIDE + l) * LANES:(CMP_STRIDE + l + 1) * LANES, :])
    pre = u + pltpu.roll(v, nch - 1, 0)
    h = jax.nn.gelu(pre)
    o_ref[...] = _dot(h.astype(BF16), w2_ref[...]).astype(o_ref.dtype)


def _compress(t, col0, pe, w1, w2, bsz, seq):
    nch = seq // CMP_STRIDE
    cb0 = col0 // LANES
    return pl.pallas_call(
        _compress_kernel,
        grid=(bsz, B_KV_HEADS),
        in_specs=[pl.BlockSpec((seq, LANES), lambda b, g: (b, cb0 + g)),
                  pl.BlockSpec((CMP_LEN, LANES), lambda b, g: (0, 0)),
                  pl.BlockSpec((CMP_LEN * LANES, CMP_HIDDEN), lambda b, g: (0, 0)),
                  pl.BlockSpec((CMP_HIDDEN, LANES), lambda b, g: (0, 0))],
        out_specs=pl.BlockSpec((None, None, nch, LANES), lambda b, g: (b, g, 0, 0)),
        out_shape=jax.ShapeDtypeStruct((bsz, B_KV_HEADS, nch, LANES), BF16),
        compiler_params=_cparams(("parallel", "parallel")),
        name="nsa_compress",
    )(t, pe, w1, w2)


NSA_TK = 512
NSA_NKW = (-(-(WIN_LEN - 1) // TQ) + 1) * TQ


def _nsa_kernel(q_ref, gate_ref, kcmp_ref, vcmp_ref, ks_ref, vs_ref, kw_ref, vw_ref, ya_ref, gb_ref, o_ref,
                qs_ref, s_ref, p_ref, m_ref, l_ref, acc_ref, ocmp_ref, *, seq):
    qi = pl.program_id(2)
    rep = B_HEADS // B_KV_HEADS
    nch = seq // CMP_STRIDE
    ns = seq // SLC_LEN
    n_sel = min(SLC_TOPK, ns)
    heads = [slice(r * TQ, (r + 1) * TQ) for r in range(rep)]
    for r in range(rep):
        qs_ref[heads[r], :] = q_ref[:, r * LANES:(r + 1) * LANES]
    tpos = qi * TQ + lax.broadcasted_iota(jnp.int32, (TQ, 1), 0)

    s_ref[:, :nch] = _dot_nt(qs_ref[...], kcmp_ref[...])
    c_end = lax.broadcasted_iota(jnp.int32, (1, nch), 1) * CMP_STRIDE + (CMP_LEN - 1)
    valid = c_end <= tpos
    p_grp = jnp.zeros((TQ, nch), F32)
    for sl in heads:
        s = jnp.where(valid, s_ref[sl, :nch], NEG_INF)
        m = jnp.max(s, axis=-1, keepdims=True)
        e = jnp.where(valid, jnp.exp2(s - m), 0.0)
        den = jnp.sum(e, axis=-1, keepdims=True)
        p = e / jnp.where(den > 0, den, 1.0)
        p_grp = p_grp + p
        p_ref[sl, :nch] = p.astype(BF16)
    ocmp_ref[...] = _dot(p_ref[:, :nch], vcmp_ref[...])

    jrow = lax.broadcasted_iota(jnp.int32, (ns, 1), 0)
    d = lax.broadcasted_iota(jnp.int32, (ns, nch), 1) - (SLC_LEN // CMP_STRIDE) * jrow
    msel = jnp.where((d == -1) | (d == 3), 1.0, jnp.where((d >= 0) & (d <= 2), 2.0, 0.0)).astype(BF16)
    p_hi = p_grp.astype(BF16)
    r1 = p_grp - p_hi.astype(F32)
    p_mid = r1.astype(BF16)
    p_lo = (r1 - p_mid.astype(F32)).astype(BF16)
    p_slc = _dot_nt(msel, p_hi) + _dot_nt(msel, p_mid) + _dot_nt(msel, p_lo)

    tpos_l = qi * TQ + lax.broadcasted_iota(jnp.int32, (1, TQ), 1)
    cur = tpos_l // SLC_LEN
    forced = (jrow == 0) | (jrow == cur) | (jrow == cur - 1)
    score = jnp.where(forced, 1e9, jnp.where(jrow * SLC_LEN <= tpos_l, p_slc, NEG_INF))
    rank = jnp.zeros((ns, TQ), jnp.int32)
    for i in range(ns):
        si = score[i:i + 1, :]
        rank = rank + ((si > score) | ((si == score) & (jrow > i))).astype(jnp.int32)
    sel_t = (rank < n_sel).astype(F32)
    if ns < LANES:
        sel_t = jnp.concatenate([sel_t, jnp.zeros((LANES - ns, TQ), F32)], axis=0)
    sel = sel_t.T.astype(BF16)

    m_ref[...] = jnp.full_like(m_ref, NEG_INF)
    l_ref[...] = jnp.zeros_like(l_ref)
    acc_ref[...] = jnp.zeros_like(acc_ref)
    jexp = lax.broadcasted_iota(jnp.int32, (LANES, 1), 0)

    def slc_body(kt, c):
        row0 = pl.multiple_of(kt * NSA_TK, NSA_TK)
        s_ref[:, :NSA_TK] = _dot_nt(qs_ref[...], ks_ref[pl.ds(row0, NSA_TK), :])
        kpos = kt * NSA_TK + lax.broadcasted_iota(jnp.int32, (1, NSA_TK), 1)
        expand = (jexp == kpos // SLC_LEN).astype(BF16)
        keep = (_dot(sel, expand) > 0.5) & (kpos <= tpos)
        for sl in heads:
            s = jnp.where(keep, s_ref[sl, :NSA_TK], NEG_INF)
            m_old = m_ref[sl, :]
            m_new = jnp.maximum(m_old, jnp.max(s, axis=-1, keepdims=True))
            alpha = jnp.exp2(m_old - m_new)
            e = jnp.exp2(s - m_new)
            l_ref[sl, :] = alpha * l_ref[sl, :] + jnp.sum(e, axis=-1, keepdims=True)
            m_ref[sl, :] = m_new
            acc_ref[sl, :] = alpha * acc_ref[sl, :]
            p_ref[sl, :NSA_TK] = e.astype(BF16)
        acc_ref[...] += _dot(p_ref[:, :NSA_TK], vs_ref[pl.ds(row0, NSA_TK), :])
        return c

    lax.fori_loop(0, (qi * TQ + TQ + NSA_TK - 1) // NSA_TK, slc_body, 0)

    nprev = NSA_NKW // TQ - 1
    k_parts, v_parts = [], []
    for i in range(nprev + 1):
        row0 = pl.multiple_of(jnp.maximum(qi - nprev + i, 0) * TQ, TQ)
        k_parts.append(kw_ref[pl.ds(row0, TQ), :])
        v_parts.append(vw_ref[pl.ds(row0, TQ), :])
    s_ref[:, :NSA_NKW] = _dot_nt(qs_ref[...], jnp.concatenate(k_parts, axis=0))
    kposw = (qi - nprev) * TQ + lax.broadcasted_iota(jnp.int32, (1, NSA_NKW), 1)
    dlt = tpos - kposw
    keepw = (kposw >= 0) & (dlt >= 0) & (dlt < WIN_LEN)
    for sl in heads:
        s = jnp.where(keepw, s_ref[sl, :NSA_NKW], NEG_INF)
        e = jnp.exp2(s - jnp.max(s, axis=-1, keepdims=True))
        p_ref[sl, :NSA_NKW] = (e / jnp.sum(e, axis=-1, keepdims=True)).astype(BF16)
    o_win = _dot(p_ref[:, :NSA_NKW], jnp.concatenate(v_parts, axis=0))

    gate = gate_ref[...]
    for r, sl in enumerate(heads):
        cols = slice(r * LANES, (r + 1) * LANES)
        o_b = (gate[:, r:r + 1] * ocmp_ref[sl, :]
               + gate[:, rep + r:rep + r + 1] * (acc_ref[sl, :] / l_ref[sl, :])
               + gate[:, 2 * rep + r:2 * rep + r + 1] * o_win[sl])
        o_ref[:, cols] = (ya_ref[:, cols] + gb_ref[:, cols] * o_b).astype(o_ref.dtype)


def _nsa_attention(qk, v, gnx, gates, ya, kcmp, vcmp, bsz, seq, cols):
    n = bsz * seq
    nq = seq // TQ
    nch = seq // CMP_STRIDE
    rep = B_HEADS // B_KV_HEADS
    qb0 = cols["q"] // (rep * LANES)
    gbb = cols["g_b"] // (rep * LANES)
    ksb, kwb, vsb, vwb, gb0 = (cols[k] // LANES for k in ("ks", "kw", "vs", "vw", "gate"))
    return pl.pallas_call(
        functools.partial(_nsa_kernel, seq=seq),
        grid=(bsz, B_KV_HEADS, nq),
        in_specs=[pl.BlockSpec((TQ, rep * LANES), lambda b, g, i: (b * nq + i, qb0 + g)),
                  pl.BlockSpec((TQ, LANES), lambda b, g, i: (b * nq + i, gb0 + g)),
                  pl.BlockSpec((None, None, nch, LANES), lambda b, g, i: (b, g, 0, 0)),
                  pl.BlockSpec((None, None, nch, LANES), lambda b, g, i: (b, g, 0, 0)),
                  pl.BlockSpec((seq, LANES), lambda b, g, i: (b, ksb + g)),
                  pl.BlockSpec((seq, LANES), lambda b, g, i: (b, vsb + g)),
                  pl.BlockSpec((seq, LANES), lambda b, g, i: (b, kwb + g)),
                  pl.BlockSpec((seq, LANES), lambda b, g, i: (b, vwb + g)),
                  pl.BlockSpec((TQ, rep * LANES), lambda b, g, i: (b * nq + i, g)),
                  pl.BlockSpec((TQ, rep * LANES), lambda b, g, i: (b * nq + i, gbb + g))],
        out_specs=pl.BlockSpec((TQ, rep * LANES), lambda b, g, i: (b * nq + i, g)),
        out_shape=jax.ShapeDtypeStruct((n, B_HEADS * B_HEAD_DIM), BF16),
        scratch_shapes=[pltpu.VMEM((rep * TQ, LANES), BF16),
                        pltpu.VMEM((rep * TQ, max(NSA_NKW, NSA_TK, nch)), F32),
                        pltpu.VMEM((rep * TQ, max(NSA_NKW, NSA_TK, nch)), BF16),
                        pltpu.VMEM((rep * TQ, 1), F32), pltpu.VMEM((rep * TQ, 1), F32),
                        pltpu.VMEM((rep * TQ, LANES), F32), pltpu.VMEM((rep * TQ, LANES), F32)],
        compiler_params=_cparams(("parallel", "parallel", "arbitrary")),
        name="nsa",
    )(qk, gnx, kcmp, vcmp, qk, v, qk, v, ya, gates)


def _oproj_kernel(y_ref, wo_ref, x_ref, g_ref, b_ref, wr_ref, x1_ref, x1p_ref, sc_ref, z_ref, *, alpha):
    j = pl.program_id(1)
    z_ref[j] = _dot(y_ref[...], wo_ref[...])

    @pl.when(j == pl.num_programs(1) - 1)
    def _():
        nj, tm, _ = z_ref.shape
        piece = min(128, tm)
        for r0 in range(0, tm, piece):
            rows = slice(r0, r0 + piece)
            z = jnp.concatenate([z_ref[jj, rows, :] for jj in range(nj)], axis=1)
            x1 = _layer_norm(alpha * x_ref[rows, :] + z, g_ref[...], b_ref[...])
            x1_ref[rows, :] = x1
            x1p_ref[rows, :] = _pack_halves(x1)
            sc_ref[rows, :] = _sigmoid(_dot(x1.astype(BF16), wr_ref[...]))


def _out_proj(y, w_o, x, ln_g, ln_b, w_r, alpha, tm=512, tn=512):
    n, d = x.shape
    tm = min(tm, n)
    once = dict(pipeline_mode=pl.Buffered(1))
    return pl.pallas_call(
        functools.partial(_oproj_kernel, alpha=alpha),
        grid=(n // tm, d // tn),
        in_specs=[pl.BlockSpec((tm, d), lambda i, j: (i, 0)),
                  pl.BlockSpec((d, tn), lambda i, j: (0, j)),
                  pl.BlockSpec((tm, d), lambda i, j: (i, 0), **once),
                  pl.BlockSpec((1, d), lambda i, j: (0, 0)),
                  pl.BlockSpec((1, d), lambda i, j: (0, 0)),
                  pl.BlockSpec((d, LANES), lambda i, j: (0, 0))],
        out_specs=[pl.BlockSpec((tm, d), lambda i, j: (i, 0), **once),
                   pl.BlockSpec((tm, d // 2), lambda i, j: (i, 0), **once),
                   pl.BlockSpec((tm, LANES), lambda i, j: (i, 0))],
        out_shape=[jax.ShapeDtypeStruct((n, d), F32), jax.ShapeDtypeStruct((n, d // 2), jnp.uint32),
                   jax.ShapeDtypeStruct((n, LANES), F32)],
        scratch_shapes=[pltpu.VMEM((d // tn, tm, tn), F32)],
        compiler_params=_cparams(("parallel", "arbitrary")),
        name="oproj_ln1_router",
    )(y, w_o, x, ln_g, ln_b, w_r)


def _route_kernel(sc_ref, bias_ref, ek_ref, rk_ref, gk_ref, cnt_ref, run_ref):
    i = pl.program_id(0)
    tm = sc_ref.shape[0]

    @pl.when(i == 0)
    def _():
        run_ref[...] = jnp.zeros_like(run_ref)

    sc = sc_ref[...]
    lane = lax.broadcasted_iota(jnp.int32, (1, LANES), 1)
    is_e = lane < N_EXPERTS
    biased = jnp.where(is_e, sc + bias_ref[...], -jnp.inf)
    rank = jnp.zeros((tm, LANES), jnp.int32)
    for e in range(N_EXPERTS):
        be = biased[:, e:e + 1]
        rank = rank + ((be > biased) | ((be == biased) & (e < lane))).astype(jnp.int32)
    sel = (rank < TOP_K) & is_e
    sv = jnp.where(sel, sc, 0.0)
    gates = sv / jnp.sum(sv, axis=-1, keepdims=True) * ROUTED_SCALE
    selb = sel.astype(BF16)
    ri = lax.broadcasted_iota(jnp.int32, (tm, tm), 0)
    ci = lax.broadcasted_iota(jnp.int32, (tm, tm), 1)
    cum = _dot((ci < ri).astype(BF16), selb) + run_ref[...]
    run_ref[...] += jnp.sum(sel.astype(F32), axis=0, keepdims=True)
    cnt_ref[...] = run_ref[...]
    ai = lax.broadcasted_iota(jnp.int32, (LANES, LANES), 0)
    bi = lax.broadcasted_iota(jnp.int32, (LANES, LANES), 1)
    before = _dot(selb, (ai < bi).astype(BF16))
    lane_f = lane.astype(F32)
    ek = jnp.zeros((tm, LANES), F32)
    rk = jnp.zeros((tm, LANES), F32)
    gk = jnp.zeros((tm, LANES), F32)
    for k in range(TOP_K):
        oh = sel & (before == k)
        ek = jnp.where(lane == k, jnp.sum(jnp.where(oh, lane_f, 0.0), axis=-1, keepdims=True), ek)
        rk = jnp.where(lane == k, jnp.sum(jnp.where(oh, cum, 0.0), axis=-1, keepdims=True), rk)
        gk = jnp.where(lane == k, jnp.sum(jnp.where(oh, gates, 0.0), axis=-1, keepdims=True), gk)
    ek_ref[...] = ek.astype(jnp.int32)
    rk_ref[...] = rk.astype(jnp.int32)
    gk_ref[...] = gk


def _route(scores, bias, tm=256):
    n = scores.shape[0]
    tm = min(tm, n)
    blk = pl.BlockSpec((tm, LANES), lambda i: (i, 0))
    one = pl.BlockSpec((1, LANES), lambda i: (0, 0))
    return pl.pallas_call(
        _route_kernel,
        grid=(n // tm,),
        in_specs=[blk, one],
        out_specs=[blk, blk, blk, one],
        out_shape=[jax.ShapeDtypeStruct((n, LANES), jnp.int32), jax.ShapeDtypeStruct((n, LANES), jnp.int32),
                   jax.ShapeDtypeStruct((n, LANES), F32), jax.ShapeDtypeStruct((1, LANES), F32)],
        scratch_shapes=[pltpu.VMEM((1, LANES), F32)],
        compiler_params=_cparams(("arbitrary",)),
        name="route",
    )(scores, bias)


def _row_copy(src, s_row, dst, d_row, sem):
    return pltpu.make_async_copy(src.at[pl.ds(s_row, 1)], dst.at[pl.ds(d_row, 1)], sem)


def _scatter_kernel(cnt_ref, pad_ref, pst_ref, nu_ref, dest_ref, x_ref, xs_ref, zrow_ref, zblk_ref, sem,
                    *, tm, n_tok_steps, n_blocks):
    i = pl.program_id(0)

    @pl.when(i < n_tok_steps)
    def _():
        def issue(t, c):
            for k in range(TOP_K):
                _row_copy(x_ref, t, xs_ref, dest_ref[t * TOP_K + k], sem).start()
            return c

        lax.fori_loop(0, tm, issue, 0)

        for k in range(TOP_K):
            pltpu.make_async_copy(x_ref, xs_ref.at[pl.ds(0, tm)], sem).wait()

    @pl.when((i >= n_tok_steps) & (i < n_tok_steps + N_EXPERTS))
    def _():
        e = i - n_tok_steps
        zrow_ref[...] = jnp.zeros_like(zrow_ref)
        lo, hi, base = cnt_ref[e], pad_ref[e], pst_ref[e]

        def issue(r, c):
            pltpu.make_async_copy(zrow_ref, xs_ref.at[pl.ds(base + r, 1)], sem).start()
            return c

        lax.fori_loop(lo, hi, issue, 0)

        def drain(r, c):
            pltpu.make_async_copy(zrow_ref, xs_ref.at[pl.ds(0, 1)], sem).wait()
            return c

        lax.fori_loop(lo, hi, drain, 0)

    @pl.when(i == n_tok_steps + N_EXPERTS)
    def _():
        zblk_ref[...] = jnp.zeros_like(zblk_ref)

        def issue(p, c):
            pltpu.make_async_copy(zblk_ref, xs_ref.at[pl.ds(p * MOE_BLOCK, MOE_BLOCK)], sem).start()
            return c

        lax.fori_loop(nu_ref[0], n_blocks, issue, 0)

        def drain(p, c):
            pltpu.make_async_copy(zblk_ref, xs_ref.at[pl.ds(0, MOE_BLOCK)], sem).wait()
            return c

        lax.fori_loop(nu_ref[0], n_blocks, drain, 0)


def _scatter_rows(x1, dest_flat, counts, padded, pstarts, n_used, n_blocks, tm=128):
    n, d = x1.shape
    tm = min(tm, n)
    n_tok_steps = n // tm
    return pl.pallas_call(
        functools.partial(_scatter_kernel, tm=tm, n_tok_steps=n_tok_steps, n_blocks=n_blocks),
        grid_spec=pltpu.PrefetchScalarGridSpec(
            num_scalar_prefetch=4,
            grid=(n_tok_steps + N_EXPERTS + 1,),
            in_specs=[pl.BlockSpec((tm * TOP_K,), lambda i, *_: (jnp.minimum(i, n_tok_steps - 1),),
                                   memory_space=pltpu.SMEM),
                      pl.BlockSpec((tm, d), lambda i, *_: (jnp.minimum(i, n_tok_steps - 1), 0))],
            out_specs=pl.BlockSpec(memory_space=pl.ANY),
            scratch_shapes=[pltpu.VMEM((1, d), x1.dtype), pltpu.VMEM((MOE_BLOCK, d), x1.dtype),
                            pltpu.SemaphoreType.DMA(())]),
        out_shape=jax.ShapeDtypeStruct((n_blocks * MOE_BLOCK, d), x1.dtype),
        compiler_params=_cparams(("arbitrary",)),
        name="moe_scatter",
    )(counts, padded, pstarts, n_used, dest_flat, x1)


def _expert_kernel(be_ref, first_ref, nxt_ref, nu_ref, xs_ref, wg_hbm, wu_hbm, wd_hbm, y_ref,
                   sg_ref, su_ref, sd_ref, wg_ref, wu_ref, wd_ref, sem):
    p = pl.program_id(0)
    used = p < nu_ref[0]

    def copies(e):
        return (pltpu.make_async_copy(wg_hbm.at[e], sg_ref, sem.at[0]),
                pltpu.make_async_copy(wu_hbm.at[e], su_ref, sem.at[1]),
                pltpu.make_async_copy(wd_hbm.at[e], sd_ref, sem.at[2]))

    @pl.when(p == 0)
    def _():
        for c in copies(be_ref[0]):
            c.start()

    @pl.when(used & (first_ref[p] == 1))
    def _():
        for c in copies(be_ref[p]):
            c.wait()
        wg_ref[...] = sg_ref[...].astype(BF16)
        wu_ref[...] = su_ref[...].astype(BF16)
        wd_ref[...] = sd_ref[...].astype(BF16)

        @pl.when(nxt_ref[p] >= 0)
        def _():
            for c in copies(nxt_ref[p]):
                c.start()

    @pl.when(used)
    def _():
        lo, hi = _unpack_halves(xs_ref[...])
        lo, hi = lo.astype(BF16), hi.astype(BF16)
        dh = lo.shape[1]
        hg = _dot(lo, wg_ref[:dh, :]) + _dot(hi, wg_ref[dh:, :])
        hu = _dot(lo, wu_ref[:dh, :]) + _dot(hi, wu_ref[dh:, :])
        h = (hg * _sigmoid(hg)) * hu
        y_ref[...] = _pack_halves(_dot(h.astype(BF16), wd_ref[...]))

    @pl.when(jnp.logical_not(used))
    def _():
        y_ref[...] = jnp.zeros_like(y_ref)


def _expert_mlp(xs, block_e, first, nxt, n_used, wg, wu, wd):
    n_rows, dp = xs.shape
    nblk = n_rows // MOE_BLOCK
    d, ff = wg.shape[-2:]
    assert d == 2 * dp

    def row_map(p, be, fi, nx, nu):
        return (jnp.minimum(p, nu[0] - 1), 0)

    hbm = pl.BlockSpec(memory_space=pl.ANY)
    return pl.pallas_call(
        _expert_kernel,
        grid_spec=pltpu.PrefetchScalarGridSpec(
            num_scalar_prefetch=4,
            grid=(nblk,),
            in_specs=[pl.BlockSpec((MOE_BLOCK, dp), row_map), hbm, hbm, hbm],
            out_specs=pl.BlockSpec((MOE_BLOCK, dp), lambda p, *_: (p, 0)),
            scratch_shapes=[pltpu.VMEM((d, ff), wg.dtype), pltpu.VMEM((d, ff), wu.dtype),
                            pltpu.VMEM((ff, d), wd.dtype),
                            pltpu.VMEM((d, ff), BF16), pltpu.VMEM((d, ff), BF16), pltpu.VMEM((ff, d), BF16),
                            pltpu.SemaphoreType.DMA((3,))]),
        out_shape=jax.ShapeDtypeStruct((n_rows, dp), jnp.uint32),
        compiler_params=_cparams(("arbitrary",), VMEM_LIMIT + 2 * 1024 * 1024),
        name="moe_experts",
    )(block_e, first, nxt, n_used, xs, wg, wu, wd)


COMBINE_PARTS = 4


def _combine_kernel(dest_ref, x1_ref, gk_ref, sg_ref, su_ref, sd_ref, g_ref, b_ref, ys_ref,
                    o_ref, ybuf_ref, sh_ref, sem, *, tm, alpha):
    tp = tm // COMBINE_PARTS

    def issue(part):
        def body(t, c):
            for k in range(TOP_K):
                _row_copy(ys_ref, dest_ref[t * TOP_K + k], ybuf_ref, k * tm + t, sem.at[part]).start()
            return c

        lax.fori_loop(part * tp, (part + 1) * tp, body, 0)

    issue(0)
    issue(1)
    xb = x1_ref[...].astype(BF16)
    hg = _dot(xb, sg_ref[...])
    hu = _dot(xb, su_ref[...])
    sh_ref[...] = _dot(((hg * _sigmoid(hg)) * hu).astype(BF16), sd_ref[...])

    for part in range(COMBINE_PARTS):
        rows = slice(part * tp, (part + 1) * tp)
        for k in range(TOP_K):
            pltpu.make_async_copy(ys_ref.at[pl.ds(0, tp)], ybuf_ref.at[pl.ds(0, tp)], sem.at[part]).wait()
        if part + 2 < COMBINE_PARTS:
            issue(part + 2)
        gk = gk_ref[rows, :]
        r_lo = jnp.zeros((tp, ybuf_ref.shape[1]), F32)
        r_hi = jnp.zeros((tp, ybuf_ref.shape[1]), F32)
        for k in range(TOP_K):
            lo, hi = _unpack_halves(ybuf_ref[k * tm + part * tp:k * tm + (part + 1) * tp, :])
            r_lo = r_lo + gk[:, k:k + 1] * lo
            r_hi = r_hi + gk[:, k:k + 1] * hi
        routed = jnp.concatenate([r_lo, r_hi], axis=1)
        o_ref[rows, :] = _layer_norm(alpha * x1_ref[rows, :] + (routed + sh_ref[rows, :]), g_ref[...], b_ref[...])


def _combine(x1, dest_flat, gk, ys, sg, su, sd, ln_g, ln_b, alpha, tm=128):
    n, d = x1.shape
    tm = min(tm, n)
    ff = sg.shape[-1]
    return pl.pallas_call(
        functools.partial(_combine_kernel, tm=tm, alpha=alpha),
        grid=(n // tm,),
        in_specs=[pl.BlockSpec((tm * TOP_K,), lambda i: (i,), memory_space=pltpu.SMEM),
                  pl.BlockSpec((tm, d), lambda i: (i, 0)),
                  pl.BlockSpec((tm, LANES), lambda i: (i, 0)),
                  pl.BlockSpec((d, ff), lambda i: (0, 0), pipeline_mode=pl.Buffered(1)),
                  pl.BlockSpec((d, ff), lambda i: (0, 0), pipeline_mode=pl.Buffered(1)),
                  pl.BlockSpec((ff, d), lambda i: (0, 0), pipeline_mode=pl.Buffered(1)),
                  pl.BlockSpec((1, d), lambda i: (0, 0)),
                  pl.BlockSpec((1, d), lambda i: (0, 0)),
                  pl.BlockSpec(memory_space=pl.ANY)],
        out_specs=pl.BlockSpec((tm, d), lambda i: (i, 0)),
        out_shape=jax.ShapeDtypeStruct((n, d), F32),
        scratch_shapes=[pltpu.VMEM((TOP_K * tm, d // 2), jnp.uint32), pltpu.VMEM((tm, d), F32),
                        pltpu.SemaphoreType.DMA((COMBINE_PARTS,))],
        compiler_params=_cparams(("arbitrary",)),
        name="moe_combine_ln2",
    )(dest_flat, x1, gk, sg, su, sd, ln_g, ln_b, ys)


def _rope_kernel(pos_ref, inv_ref, sign_ref, cos_ref, sin_ref):
    ang = pos_ref[...].astype(F32) * inv_ref[...]
    cos_ref[...] = jnp.cos(ang)
    sin_ref[...] = jnp.sin(ang) * sign_ref[...]


def _rope_tables(positions, dh, tm=512):
    half = dh // 2
    inv = ROPE_THETA ** (-jnp.arange(half, dtype=F32) * 2.0 / dh)
    reps = LANES // dh
    inv_l = jnp.tile(jnp.concatenate([inv, inv]), reps).reshape(1, LANES)
    sign_l = jnp.tile(jnp.concatenate([-jnp.ones((half,), F32), jnp.ones((half,), F32)]), reps).reshape(1, LANES)
    pos = positions.reshape(-1, 1)
    n = pos.shape[0]
    tm = min(tm, n)
    one = pl.BlockSpec((1, LANES), lambda i: (0, 0))
    blk = pl.BlockSpec((tm, LANES), lambda i: (i, 0))
    return pl.pallas_call(
        _rope_kernel,
        grid=(n // tm,),
        in_specs=[pl.BlockSpec((tm, 1), lambda i: (i, 0)), one, one],
        out_specs=[blk, blk],
        out_shape=[jax.ShapeDtypeStruct((n, LANES), F32)] * 2,
        compiler_params=_cparams(("parallel",)),
        name="rope_tables",
    )(pos, inv_l, sign_l)


def _dup_heads(w_rows, heads, dh):
    d = w_rows.shape[1]
    w3 = w_rows.reshape(heads, dh, d)
    return jnp.concatenate([w3, w3], axis=1).reshape(heads * 2 * dh, d)


def _hybrid_layer(x, positions, w_in, a_sinks, cmp_k_pos, cmp_k_w1, cmp_k_w2, cmp_v_pos, cmp_v_w1,
                  cmp_v_w2, w_o, ln1_g, ln1_b, w_router, router_bias, exp_w_gate, exp_w_up, exp_w_down,
                  sh_w_gate, sh_w_up, sh_w_down, ln2_g, ln2_b, alpha):
    bsz, seq, d = x.shape
    n = bsz * seq
    assert seq % NSA_TK == 0 and (seq // CMP_STRIDE) % LANES == 0 and A_WINDOW <= TQ
    assert 3 <= seq // SLC_LEN <= LANES
    xf = x.reshape(n, d)
    xb = xf.astype(BF16)

    qa_w, kva = A_HEADS * A_HEAD_DIM, A_KV_HEADS * A_HEAD_DIM
    qb_w, kvb = B_HEADS * B_HEAD_DIM, B_KV_HEADS * B_HEAD_DIM
    widths = [qa_w, kva, kva, qb_w, kvb, kvb, kvb, kvb, kvb, kvb, 3 * B_HEADS, d, d]
    offs = np.concatenate([[0], np.cumsum(widths)]).tolist()
    (o_qa, o_ka, o_va, o_qb, o_kc, o_vc, o_ks, o_vs, o_kw, o_vw, o_gn, o_ga, o_end) = (
        offs[i] for i in (0, 1, 2, 3, 4, 5, 6, 7, 8, 9, 10, 11, 13))
    w_t = w_in.T
    blk = lambda start, width: list(range(start // PROJ_TN, (start + width) // PROJ_TN))
    assert all(o % PROJ_TN == 0 for o in (o_qa, o_qb, o_kc, o_vc, o_ks, o_vs, o_kw, o_vw))
    rep = B_HEADS // B_KV_HEADS
    w_gn = w_t[o_gn:o_ga].reshape(3, B_KV_HEADS, rep, d).transpose(1, 0, 2, 3).reshape(B_KV_HEADS, 3 * rep, d)
    w_gn_x = jnp.pad(w_gn, ((0, 0), (0, LANES - 3 * rep), (0, 0))).reshape(B_KV_HEADS * LANES, d)
    w_gn_x = w_gn_x.astype(BF16)
    assert o_ga % PROJ_TN == 3 * B_HEADS and o_end == w_t.shape[0]
    w_ka2 = _dup_heads(w_t[o_ka:o_va], A_KV_HEADS, A_HEAD_DIM).astype(BF16)
    w_va2 = _dup_heads(w_t[o_va:o_qb], A_KV_HEADS, A_HEAD_DIM).astype(BF16)
    all_blocks = lambda w: list(range(w.shape[0] // PROJ_TN))

    cos64, sin64 = _rope_tables(positions, A_HEAD_DIM)
    cos128, sin128 = _rope_tables(positions, B_HEAD_DIM)

    q_a = _project(xb, w_t, blk(o_qa, qa_w), cos64, sin64, "rope64", BF16,
                   q_cols=qa_w, q_scale=A_HEAD_DIM ** -0.5 * LOG2E)
    k_a = _project(xb, w_ka2, all_blocks(w_ka2), cos64, sin64, "rope64", BF16)
    v_a = _project(xb, w_va2, all_blocks(w_va2), cos64, sin64, "none", BF16)
    qk_b = _project(xb, w_t, blk(o_qb, qb_w) + blk(o_ks, kvb) + blk(o_kw, kvb), cos128, sin128, "rope128", BF16,
                    q_cols=qb_w, q_scale=B_HEAD_DIM ** -0.5 * LOG2E)
    v_b = _project(xb, w_t, blk(o_vs, kvb) + blk(o_vw, kvb), cos128, sin128, "none", BF16)
    kc = _project(xb, w_t, blk(o_kc, kvb), cos128, sin128, "rope128", F32)
    vc = _project(xb, w_t, blk(o_vc, kvb), cos128, sin128, "none", F32)
    gates = _project(xb, w_t, blk(o_ga - 3 * B_HEADS, 2 * d), cos128, sin128, "sigmoid", F32,
                     tm=512, shift=3 * B_HEADS)
    gnx = _project(xb, w_gn_x, all_blocks(w_gn_x), cos128, sin128, "sigmoid", F32)

    ya = _swa_attention(q_a, k_a, v_a, a_sinks, gates, bsz, seq)

    kcmp = _compress(kc, 0, cmp_k_pos, cmp_k_w1.astype(BF16), cmp_k_w2.astype(BF16), bsz, seq)
    vcmp = _compress(vc, 0, cmp_v_pos, cmp_v_w1.astype(BF16), cmp_v_w2.astype(BF16), bsz, seq)
    nsa_cols = dict(q=0, ks=qb_w, kw=qb_w + kvb, vs=0, vw=kvb, g_b=d, gate=0)
    y = _nsa_attention(qk_b, v_b, gnx, gates, ya, kcmp, vcmp, bsz, seq, nsa_cols)

    w_r = jnp.pad(w_router, ((0, 0), (0, LANES - N_EXPERTS))).astype(BF16)
    x1, x1p, scores = _out_proj(y, w_o.astype(BF16), xf, ln1_g.reshape(1, d), ln1_b.reshape(1, d), w_r, alpha)

    bias = jnp.pad(router_bias.astype(F32), (0, LANES - N_EXPERTS)).reshape(1, LANES)
    ek, rk, gk, cnt = _route(scores, bias)

    counts = cnt[0, :N_EXPERTS].astype(jnp.int32)
    padded = (counts + MOE_BLOCK - 1) // MOE_BLOCK * MOE_BLOCK
    pends = jnp.cumsum(padded)
    pstarts = pends - padded
    nblk = -(-(n * TOP_K) // MOE_BLOCK) + N_EXPERTS
    blk_start = jnp.arange(nblk, dtype=jnp.int32) * MOE_BLOCK
    block_e = jnp.minimum(jnp.sum((pends[None, :] <= blk_start[:, None]).astype(jnp.int32), axis=1),
                          N_EXPERTS - 1)
    n_used = (pends[-1:] // MOE_BLOCK).astype(jnp.int32)
    hit = ek[:, :TOP_K, None] == jnp.arange(N_EXPERTS, dtype=jnp.int32)
    dest = (jnp.sum(jnp.where(hit, pstarts.astype(jnp.int32), 0), axis=-1) + rk[:, :TOP_K]).reshape(-1)

    xs = _scatter_rows(x1p, dest, counts, padded, pstarts.astype(jnp.int32), n_used, nblk)
    first = jnp.concatenate([jnp.ones((1,), jnp.int32), (block_e[1:] != block_e[:-1]).astype(jnp.int32)])
    end_blk = jnp.take(pends // MOE_BLOCK, block_e).astype(jnp.int32)
    nxt = jnp.where(end_blk < n_used[0], jnp.take(block_e, jnp.minimum(end_blk, nblk - 1)), -1).astype(jnp.int32)
    ys = _expert_mlp(xs, block_e, first, nxt, n_used, exp_w_gate, exp_w_up, exp_w_down)
    out = _combine(x1, dest, gk, ys, sh_w_gate.astype(BF16), sh_w_up.astype(BF16), sh_w_down.astype(BF16),
                   ln2_g.reshape(1, d), ln2_b.reshape(1, d), alpha)
    return out.reshape(bsz, seq, d)


def kernel(x, positions, w_in, a_sinks, cmp_k_pos, cmp_k_w1, cmp_k_w2, cmp_v_pos, cmp_v_w1, cmp_v_w2,
           w_o, ln1_g, ln1_b, w_router, router_bias, exp_w_gate, exp_w_up, exp_w_down,
           sh_w_gate, sh_w_up, sh_w_down, ln2_g, ln2_b):
    depth = w_in.shape[0]
    alpha = (2.0 * depth) ** 0.25
    for l in range(depth):
        x = _hybrid_layer(x, positions, w_in[l], a_sinks[l], cmp_k_pos[l], cmp_k_w1[l], cmp_k_w2[l],
                          cmp_v_pos[l], cmp_v_w1[l], cmp_v_w2[l], w_o[l], ln1_g[l], ln1_b[l],
                          w_router[l], router_bias[l], exp_w_gate[l], exp_w_up[l], exp_w_down[l],
                          sh_w_gate[l], sh_w_up[l], sh_w_down[l], ln2_g[l], ln2_b[l], alpha)
    return x
```

```python
import functools

import jax
import jax.numpy as jnp
import numpy as np
from jax import lax
from jax.experimental import pallas as pl
from jax.experimental.pallas import tpu as pltpu

F32 = jnp.float32
BF16 = jnp.bfloat16

ROPE_THETA = 10000.0
LN_EPS = 1e-5
NEG_INF = -1e30

A_HEADS, A_KV_HEADS, A_HEAD_DIM, A_WINDOW = 64, 8, 64, 128
B_HEADS, B_KV_HEADS, B_HEAD_DIM = 32, 4, 128
CMP_LEN, CMP_STRIDE, CMP_HIDDEN = 32, 16, 256
SLC_LEN, SLC_TOPK, WIN_LEN = 64, 16, 512
N_EXPERTS, TOP_K, EXPERT_FF, SHARED_FF = 64, 8, 512, 512
ROUTED_SCALE = 2.5
MOE_BLOCK = 256

LOG2E = 1.4426950408889634
LANES = 128
TQ = 128
VMEM_LIMIT = 56 * 1024 * 1024


def _cparams(sem, vmem_limit=VMEM_LIMIT):
    return pltpu.CompilerParams(dimension_semantics=sem, vmem_limit_bytes=vmem_limit)


def _dot(a, b):
    return jnp.dot(a, b, preferred_element_type=F32)


def _dot_nt(a, b):
    return lax.dot_general(a, b, (((1,), (1,)), ((), ())), preferred_element_type=F32)


def _sigmoid(v):
    return 1.0 / (1.0 + jnp.exp(-v))


def _pack_halves(v):
    h = v.shape[1] // 2
    lo = lax.bitcast_convert_type(v[:, :h].astype(BF16).astype(F32), jnp.uint32)
    hi = lax.bitcast_convert_type(v[:, h:].astype(BF16).astype(F32), jnp.uint32)
    return (hi & jnp.uint32(0xFFFF0000)) | (lo >> 16)


def _unpack_halves(w):
    lo = lax.bitcast_convert_type(w << 16, F32)
    hi = lax.bitcast_convert_type(w & jnp.uint32(0xFFFF0000), F32)
    return lo, hi


def _layer_norm(z, g, b):
    mu = jnp.mean(z, axis=-1, keepdims=True)
    zc = z - mu
    var = jnp.mean(zc * zc, axis=-1, keepdims=True)
    return zc * lax.rsqrt(var + LN_EPS) * g + b


def _proj_kernel(tbl_ref, x_ref, *rest, epi, q_blocks, q_scale, shift):
    del tbl_ref
    if shift:
        w_ref, w2_ref, cos_ref, sin_ref, o_ref, *scratch = rest
    else:
        w_ref, cos_ref, sin_ref, o_ref, *scratch = rest
    if scratch:
        (wb_ref,) = scratch

        @pl.when(pl.program_id(1) == 0)
        def _():
            if shift:
                keep = wb_ref.shape[0] - shift
                wb_ref[:keep, :] = w_ref[shift:, :].astype(BF16)
                wb_ref[keep:, :] = w2_ref[:shift, :].astype(BF16)
            else:
                wb_ref[...] = w_ref[...].astype(BF16)

        w = wb_ref[...]
    else:
        w = w_ref[...]
    acc = _dot_nt(x_ref[...], w)
    if epi == "none":
        o_ref[...] = acc.astype(o_ref.dtype)
    elif epi == "sigmoid":
        o_ref[...] = _sigmoid(acc).astype(o_ref.dtype)
    else:
        cs = jnp.where(pl.program_id(0) < q_blocks, q_scale, 1.0)
        cos = cos_ref[...] * cs
        sin = sin_ref[...] * cs
        lane = lax.broadcasted_iota(jnp.int32, (1, LANES), 1)
        for c in range(acc.shape[1] // LANES):
            y = acc[:, c * LANES:(c + 1) * LANES]
            if epi == "rope128":
                rot = pltpu.roll(y, 64, 1)
            else:
                rot = jnp.where((lane % 64) < 32, pltpu.roll(y, 96, 1), pltpu.roll(y, 32, 1))
            o_ref[:, c * LANES:(c + 1) * LANES] = (y * cos + rot * sin).astype(o_ref.dtype)


PROJ_TN = 512


def _project(xb, w_t, blocks, cos, sin, epi, out_dtype, q_cols=0, q_scale=1.0, tm=1024, shift=0):
    n, d = xb.shape
    tn = PROJ_TN
    ncol = len(blocks) * tn
    tm = min(tm, n)
    cast = w_t.dtype != BF16
    assert n % tm == 0 and q_cols % tn == 0 and (max(blocks) + 1) * tn + shift <= w_t.shape[0]
    assert shift % 16 == 0 and (cast or not shift)
    w_specs = [pl.BlockSpec((tn, d), lambda j, i, tbl: (tbl[j], 0))]
    if shift:
        w_specs.append(pl.BlockSpec((tn, d), lambda j, i, tbl: (tbl[j] + 1, 0)))
    return pl.pallas_call(
        functools.partial(_proj_kernel, epi=epi, q_blocks=q_cols // tn, q_scale=q_scale, shift=shift),
        grid_spec=pltpu.PrefetchScalarGridSpec(
            num_scalar_prefetch=1,
            grid=(ncol // tn, n // tm),
            in_specs=[pl.BlockSpec((tm, d), lambda j, i, tbl: (i, 0)), *w_specs,
                      pl.BlockSpec((tm, LANES), lambda j, i, tbl: (i, 0)),
                      pl.BlockSpec((tm, LANES), lambda j, i, tbl: (i, 0))],
            out_specs=pl.BlockSpec((tm, tn), lambda j, i, tbl: (i, j)),
            scratch_shapes=[pltpu.VMEM((tn, d), BF16)] if cast else []),
        out_shape=jax.ShapeDtypeStruct((n, ncol), out_dtype),
        compiler_params=_cparams(("arbitrary", "arbitrary")),
        name="proj_" + epi,
    )(jnp.asarray(blocks, jnp.int32), xb, *([w_t] * len(w_specs)), cos, sin)


SWA_GROUPS = 2


def _swa_kernel(sink_ref, q_ref, kp_ref, kc_ref, vp_ref, vc_ref, ga_ref, o_ref, s_ref, p_ref):
    n = pl.program_id(1)
    lane = lax.broadcasted_iota(jnp.int32, (1, LANES), 1)
    lo = lane < A_HEAD_DIM
    nk = 2 * TQ
    qpos = n * TQ + lax.broadcasted_iota(jnp.int32, (TQ, nk), 0)
    kpos = (n - 1) * TQ + lax.broadcasted_iota(jnp.int32, (TQ, nk), 1)
    delta = qpos - kpos
    mask = (kpos >= 0) & (delta >= 0) & (delta < A_WINDOW)
    rep = A_HEADS // A_KV_HEADS
    npair = rep // 2
    zq = jnp.zeros((TQ, LANES), q_ref.dtype)
    for gg in range(SWA_GROUPS):
        g = pl.program_id(2) * SWA_GROUPS + gg
        gl = slice(gg * LANES, (gg + 1) * LANES)
        kk = jnp.concatenate([kp_ref[:, gl], kc_ref[:, gl]], axis=0)
        vv = jnp.concatenate([vp_ref[:, gl], vc_ref[:, gl]], axis=0)
        zv = jnp.zeros_like(vv)
        v_halves = (jnp.where(lo, vv, zv), jnp.where(lo, zv, vv))
        qcols = [slice((gg * npair + c) * LANES, (gg * npair + c + 1) * LANES) for c in range(npair)]
        q_lo = [jnp.where(lo, q_ref[:, c], zq) for c in qcols]
        q_hi = [jnp.where(lo, zq, q_ref[:, c]) for c in qcols]
        r0 = gg * rep * TQ
        s_ref[r0:r0 + rep * TQ, :] = _dot_nt(jnp.concatenate(q_lo + q_hi, axis=0), kk)
        for half in range(2):
            for pair in range(npair):
                sl = slice(r0 + (half * npair + pair) * TQ, r0 + (half * npair + pair + 1) * TQ)
                s = jnp.where(mask, s_ref[sl, :], NEG_INF)
                sk = sink_ref[g * rep + pair * 2 + half] * LOG2E
                m = jnp.maximum(jnp.max(s, axis=-1, keepdims=True), sk)
                e = jnp.exp2(s - m)
                p = e / (jnp.sum(e, axis=-1, keepdims=True) + jnp.exp2(sk - m))
                p_ref[sl, :] = p.astype(BF16)
        half_rows = npair * TQ
        out = (_dot(p_ref[r0:r0 + half_rows, :], v_halves[0])
               + _dot(p_ref[r0 + half_rows:r0 + 2 * half_rows, :], v_halves[1]))
        for pair, c in enumerate(qcols):
            o_ref[:, c] = ga_ref[:, c] * out[pair * TQ:(pair + 1) * TQ, :]


def _swa_attention(q, k, v, sinks, gates, bsz, seq):
    n = bsz * seq
    nb = seq // TQ
    rows = SWA_GROUPS * (A_HEADS // A_KV_HEADS) * TQ
    qw = SWA_GROUPS * (A_HEADS // A_KV_HEADS) * A_HEAD_DIM
    kw = SWA_GROUPS * LANES
    return pl.pallas_call(
        _swa_kernel,
        grid=(bsz, nb, A_KV_HEADS // SWA_GROUPS),
        in_specs=[pl.BlockSpec(memory_space=pltpu.SMEM),
                  pl.BlockSpec((TQ, qw), lambda b, i, g: (b * nb + i, g)),
                  pl.BlockSpec((TQ, kw), lambda b, i, g: (b * nb + jnp.maximum(i - 1, 0), g)),
                  pl.BlockSpec((TQ, kw), lambda b, i, g: (b * nb + i, g)),
                  pl.BlockSpec((TQ, kw), lambda b, i, g: (b * nb + jnp.maximum(i - 1, 0), g)),
                  pl.BlockSpec((TQ, kw), lambda b, i, g: (b * nb + i, g)),
                  pl.BlockSpec((TQ, qw), lambda b, i, g: (b * nb + i, g))],
        out_specs=pl.BlockSpec((TQ, qw), lambda b, i, g: (b * nb + i, g)),
        out_shape=jax.ShapeDtypeStruct((n, A_HEADS * A_HEAD_DIM), F32),
        scratch_shapes=[pltpu.VMEM((rows, 2 * TQ), F32), pltpu.VMEM((rows, 2 * TQ), BF16)],
        compiler_params=_cparams(("parallel", "parallel", "parallel")),
        name="swa",
    )(sinks, q, k, k, v, v, gates)


def _compress_kernel(t_ref, pe_ref, w1_ref, w2_ref, o_ref):
    nch = o_ref.shape[0]
    u = jnp.zeros((nch, CMP_HIDDEN), F32)
    v = jnp.zeros((nch, CMP_HIDDEN), F32)
    for l in range(CMP_STRIDE):
        t_l = t_ref[pl.ds(l, nch, stride=CMP_STRIDE), :]
        a = (t_l + pe_ref[l:l + 1, :]).astype(BF16)
        b = (t_l + pe_ref[CMP_STRIDE + l:CMP_STRIDE + l + 1, :]).astype(BF16)
        u = u + _dot(a, w1_ref[l * LANES:(l + 1) * LANES, :])
        v = v + _dot(b, w1_ref[(CMP_STRIDE + l) * LANES:(CMP_STRIDE + l + 1) * LANES, :])
    pre = u + pltpu.roll(v, nch - 1, 0)
    h = jax.nn.gelu(pre)
    o_ref[...] = _dot(h.astype(BF16), w2_ref[...]).astype(o_ref.dtype)


def _compress(t, col0, pe, w1, w2, bsz, seq):
    nch = seq // CMP_STRIDE
    cb0 = col0 // LANES
    return pl.pallas_call(
        _compress_kernel,
        grid=(bsz, B_KV_HEADS),
        in_specs=[pl.BlockSpec((seq, LANES), lambda b, g: (b, cb0 + g)),
                  pl.BlockSpec((CMP_LEN, LANES), lambda b, g: (0, 0)),
                  pl.BlockSpec((CMP_LEN * LANES, CMP_HIDDEN), lambda b, g: (0, 0)),
                  pl.BlockSpec((CMP_HIDDEN, LANES), lambda b, g: (0, 0))],
        out_specs=pl.BlockSpec((None, None, nch, LANES), lambda b, g: (b, g, 0, 0)),
        out_shape=jax.ShapeDtypeStruct((bsz, B_KV_HEADS, nch, LANES), BF16),
        compiler_params=_cparams(("parallel", "parallel")),
        name="nsa_compress",
    )(t, pe, w1, w2)


NSA_QT = 2
NSA_SCRATCH = 7
NSA_TK = 512
NSA_NKW = (-(-(WIN_LEN - 1) // TQ) + 1) * TQ


def _nsa_kernel(q_ref, gate_ref, kcmp_ref, vcmp_ref, ks_ref, vs_ref, kw_ref, vw_ref, ya_ref, gb_ref, o_ref,
                *scratch, seq):
    qs, sc, pr, mx, ls, acc, ocmp = (scratch[k::NSA_SCRATCH] for k in range(NSA_SCRATCH))
    step = pl.program_id(2)
    rep = B_HEADS // B_KV_HEADS
    nch = seq // CMP_STRIDE
    ns = seq // SLC_LEN
    n_sel = min(SLC_TOPK, ns)
    heads = [slice(r * TQ, (r + 1) * TQ) for r in range(rep)]
    tiles = range(NSA_QT)
    trows = [slice(h * TQ, (h + 1) * TQ) for h in tiles]
    qis = [step * NSA_QT + h for h in tiles]
    tpos = [qis[h] * TQ + lax.broadcasted_iota(jnp.int32, (TQ, 1), 0) for h in tiles]
    for h in tiles:
        for r in range(rep):
            qs[h][heads[r], :] = q_ref[trows[h], r * LANES:(r + 1) * LANES]

    c_end = lax.broadcasted_iota(jnp.int32, (1, nch), 1) * CMP_STRIDE + (CMP_LEN - 1)
    jrow = lax.broadcasted_iota(jnp.int32, (ns, 1), 0)
    d = lax.broadcasted_iota(jnp.int32, (ns, nch), 1) - (SLC_LEN // CMP_STRIDE) * jrow
    msel = jnp.where((d == -1) | (d == 3), 1.0, jnp.where((d >= 0) & (d <= 2), 2.0, 0.0)).astype(BF16)
    sels = []
    for h in tiles:
        sc[h][:, :nch] = _dot_nt(qs[h][...], kcmp_ref[...])
        valid = c_end <= tpos[h]
        p_grp = jnp.zeros((TQ, nch), F32)
        for sl in heads:
            s = jnp.where(valid, sc[h][sl, :nch], NEG_INF)
            m = jnp.max(s, axis=-1, keepdims=True)
            e = jnp.where(valid, jnp.exp2(s - m), 0.0)
            den = jnp.sum(e, axis=-1, keepdims=True)
            p = e / jnp.where(den > 0, den, 1.0)
            p_grp = p_grp + p
            pr[h][sl, :nch] = p.astype(BF16)
        ocmp[h][...] = _dot(pr[h][:, :nch], vcmp_ref[...])

        p_hi = p_grp.astype(BF16)
        r1 = p_grp - p_hi.astype(F32)
        p_mid = r1.astype(BF16)
        p_lo = (r1 - p_mid.astype(F32)).astype(BF16)
        p_slc = _dot_nt(msel, p_hi) + _dot_nt(msel, p_mid) + _dot_nt(msel, p_lo)

        tpos_l = qis[h] * TQ + lax.broadcasted_iota(jnp.int32, (1, TQ), 1)
        cur = tpos_l // SLC_LEN
        forced = (jrow == 0) | (jrow == cur) | (jrow == cur - 1)
        score = jnp.where(forced, 1e9, jnp.where(jrow * SLC_LEN <= tpos_l, p_slc, NEG_INF))
        rank = jnp.zeros((ns, TQ), jnp.int32)
        for i in range(ns):
            si = score[i:i + 1, :]
            rank = rank + ((si > score) | ((si == score) & (jrow > i))).astype(jnp.int32)
        sel_t = (rank < n_sel).astype(F32)
        if ns < LANES:
            sel_t = jnp.concatenate([sel_t, jnp.zeros((LANES - ns, TQ), F32)], axis=0)
        sels.append(sel_t.T.astype(BF16))

    for h in tiles:
        mx[h][...] = jnp.full_like(mx[h], NEG_INF)
        ls[h][...] = jnp.zeros_like(ls[h])
        acc[h][...] = jnp.zeros_like(acc[h])
    jexp = lax.broadcasted_iota(jnp.int32, (LANES, 1), 0)

    def slc_body(kt, c):
        row0 = pl.multiple_of(kt * NSA_TK, NSA_TK)
        k_t = ks_ref[pl.ds(row0, NSA_TK), :]
        v_t = vs_ref[pl.ds(row0, NSA_TK), :]
        kpos = kt * NSA_TK + lax.broadcasted_iota(jnp.int32, (1, NSA_TK), 1)
        expand = (jexp == kpos // SLC_LEN).astype(BF16)
        for h in tiles:
            sc[h][:, :NSA_TK] = _dot_nt(qs[h][...], k_t)
        for h in tiles:
            keep = (_dot(sels[h], expand) > 0.5) & (kpos <= tpos[h])
            for sl in heads:
                s = jnp.where(keep, sc[h][sl, :NSA_TK], NEG_INF)
                m_old = mx[h][sl, :]
                m_new = jnp.maximum(m_old, jnp.max(s, axis=-1, keepdims=True))
                alpha = jnp.exp2(m_old - m_new)
                e = jnp.exp2(s - m_new)
                ls[h][sl, :] = alpha * ls[h][sl, :] + jnp.sum(e, axis=-1, keepdims=True)
                mx[h][sl, :] = m_new
                acc[h][sl, :] = alpha * acc[h][sl, :]
                pr[h][sl, :NSA_TK] = e.astype(BF16)
            acc[h][...] += _dot(pr[h][:, :NSA_TK], v_t)
        return c

    lax.fori_loop(0, (qis[-1] * TQ + TQ + NSA_TK - 1) // NSA_TK, slc_body, 0)

    nprev = NSA_NKW // TQ - 1
    gate_all = gate_ref[...]
    for h in tiles:
        k_parts, v_parts = [], []
        for i in range(nprev + 1):
            row0 = pl.multiple_of(jnp.maximum(qis[h] - nprev + i, 0) * TQ, TQ)
            k_parts.append(kw_ref[pl.ds(row0, TQ), :])
            v_parts.append(vw_ref[pl.ds(row0, TQ), :])
        sc[h][:, :NSA_NKW] = _dot_nt(qs[h][...], jnp.concatenate(k_parts, axis=0))
        kposw = (qis[h] - nprev) * TQ + lax.broadcasted_iota(jnp.int32, (1, NSA_NKW), 1)
        dlt = tpos[h] - kposw
        keepw = (kposw >= 0) & (dlt >= 0) & (dlt < WIN_LEN)
        for sl in heads:
            s = jnp.where(keepw, sc[h][sl, :NSA_NKW], NEG_INF)
            e = jnp.exp2(s - jnp.max(s, axis=-1, keepdims=True))
            pr[h][sl, :NSA_NKW] = (e / jnp.sum(e, axis=-1, keepdims=True)).astype(BF16)
        o_win = _dot(pr[h][:, :NSA_NKW], jnp.concatenate(v_parts, axis=0))

        gate = gate_all[trows[h], :]
        for r, sl in enumerate(heads):
            cols = slice(r * LANES, (r + 1) * LANES)
            o_b = (gate[:, r:r + 1] * ocmp[h][sl, :]
                   + gate[:, rep + r:rep + r + 1] * (acc[h][sl, :] / ls[h][sl, :])
                   + gate[:, 2 * rep + r:2 * rep + r + 1] * o_win[sl])
            o_ref[trows[h], cols] = (ya_ref[trows[h], cols] + gb_ref[trows[h], cols] * o_b).astype(o_ref.dtype)


def _nsa_attention(qk, v, gnx, gates, ya, kcmp, vcmp, bsz, seq, cols):
    n = bsz * seq
    tq = NSA_QT * TQ
    nq = seq // tq
    nch = seq // CMP_STRIDE
    rep = B_HEADS // B_KV_HEADS
    qb0 = cols["q"] // (rep * LANES)
    gbb = cols["g_b"] // (rep * LANES)
    ksb, kwb, vsb, vwb, gb0 = (cols[k] // LANES for k in ("ks", "kw", "vs", "vw", "gate"))
    rows, wide = rep * TQ, max(NSA_NKW, NSA_TK, nch)
    return pl.pallas_call(
        functools.partial(_nsa_kernel, seq=seq),
        grid=(bsz, B_KV_HEADS, nq),
        in_specs=[pl.BlockSpec((tq, rep * LANES), lambda b, g, i: (b * nq + i, qb0 + g)),
                  pl.BlockSpec((tq, LANES), lambda b, g, i: (b * nq + i, gb0 + g)),
                  pl.BlockSpec((None, None, nch, LANES), lambda b, g, i: (b, g, 0, 0)),
                  pl.BlockSpec((None, None, nch, LANES), lambda b, g, i: (b, g, 0, 0)),
                  pl.BlockSpec((seq, LANES), lambda b, g, i: (b, ksb + g)),
                  pl.BlockSpec((seq, LANES), lambda b, g, i: (b, vsb + g)),
                  pl.BlockSpec((seq, LANES), lambda b, g, i: (b, kwb + g)),
                  pl.BlockSpec((seq, LANES), lambda b, g, i: (b, vwb + g)),
                  pl.BlockSpec((tq, rep * LANES), lambda b, g, i: (b * nq + i, g)),
                  pl.BlockSpec((tq, rep * LANES), lambda b, g, i: (b * nq + i, gbb + g))],
        out_specs=pl.BlockSpec((tq, rep * LANES), lambda b, g, i: (b * nq + i, g)),
        out_shape=jax.ShapeDtypeStruct((n, B_HEADS * B_HEAD_DIM), BF16),
        scratch_shapes=[pltpu.VMEM((rows, LANES), BF16),
                        pltpu.VMEM((rows, wide), F32),
                        pltpu.VMEM((rows, wide), BF16),
                        pltpu.VMEM((rows, 1), F32), pltpu.VMEM((rows, 1), F32),
                        pltpu.VMEM((rows, LANES), F32),
                        pltpu.VMEM((rows, LANES), F32)] * NSA_QT,
        compiler_params=_cparams(("parallel", "parallel", "arbitrary")),
        name="nsa",
    )(qk, gnx, kcmp, vcmp, qk, v, qk, v, ya, gates)


def _oproj_kernel(y_ref, wo_ref, x_ref, g_ref, b_ref, wr_ref, x1_ref, x1p_ref, sc_ref, z_ref, *, alpha):
    j = pl.program_id(1)
    z_ref[j] = _dot(y_ref[...], wo_ref[...])

    @pl.when(j == pl.num_programs(1) - 1)
    def _():
        nj, tm, _ = z_ref.shape
        piece = min(128, tm)
        for r0 in range(0, tm, piece):
            rows = slice(r0, r0 + piece)
            z = jnp.concatenate([z_ref[jj, rows, :] for jj in range(nj)], axis=1)
            x1 = _layer_norm(alpha * x_ref[rows, :] + z, g_ref[...], b_ref[...])
            x1_ref[rows, :] = x1
            x1p_ref[rows, :] = _pack_halves(x1)
            sc_ref[rows, :] = _sigmoid(_dot(x1.astype(BF16), wr_ref[...]))


def _out_proj(y, w_o, x, ln_g, ln_b, w_r, alpha, tm=512, tn=512):
    n, d = x.shape
    tm = min(tm, n)
    once = dict(pipeline_mode=pl.Buffered(1))
    return pl.pallas_call(
        functools.partial(_oproj_kernel, alpha=alpha),
        grid=(n // tm, d // tn),
        in_specs=[pl.BlockSpec((tm, d), lambda i, j: (i, 0)),
                  pl.BlockSpec((d, tn), lambda i, j: (0, j)),
                  pl.BlockSpec((tm, d), lambda i, j: (i, 0), **once),
                  pl.BlockSpec((1, d), lambda i, j: (0, 0)),
                  pl.BlockSpec((1, d), lambda i, j: (0, 0)),
                  pl.BlockSpec((d, LANES), lambda i, j: (0, 0))],
        out_specs=[pl.BlockSpec((tm, d), lambda i, j: (i, 0), **once),
                   pl.BlockSpec((tm, d // 2), lambda i, j: (i, 0), **once),
                   pl.BlockSpec((tm, LANES), lambda i, j: (i, 0))],
        out_shape=[jax.ShapeDtypeStruct((n, d), F32), jax.ShapeDtypeStruct((n, d // 2), jnp.uint32),
                   jax.ShapeDtypeStruct((n, LANES), F32)],
        scratch_shapes=[pltpu.VMEM((d // tn, tm, tn), F32)],
        compiler_params=_cparams(("parallel", "arbitrary")),
        name="oproj_ln1_router",
    )(y, w_o, x, ln_g, ln_b, w_r)


def _route_kernel(sc_ref, bias_ref, ek_ref, rk_ref, gk_ref, cnt_ref, run_ref):
    i = pl.program_id(0)
    tm = sc_ref.shape[0]

    @pl.when(i == 0)
    def _():
        run_ref[...] = jnp.zeros_like(run_ref)

    sc = sc_ref[...]
    lane = lax.broadcasted_iota(jnp.int32, (1, LANES), 1)
    is_e = lane < N_EXPERTS
    biased = jnp.where(is_e, sc + bias_ref[...], -jnp.inf)
    rank = jnp.zeros((tm, LANES), jnp.int32)
    for e in range(N_EXPERTS):
        be = biased[:, e:e + 1]
        rank = rank + ((be > biased) | ((be == biased) & (e < lane))).astype(jnp.int32)
    sel = (rank < TOP_K) & is_e
    sv = jnp.where(sel, sc, 0.0)
    gates = sv / jnp.sum(sv, axis=-1, keepdims=True) * ROUTED_SCALE
    selb = sel.astype(BF16)
    ri = lax.broadcasted_iota(jnp.int32, (tm, tm), 0)
    ci = lax.broadcasted_iota(jnp.int32, (tm, tm), 1)
    cum = _dot((ci < ri).astype(BF16), selb) + run_ref[...]
    run_ref[...] += jnp.sum(sel.astype(F32), axis=0, keepdims=True)
    cnt_ref[...] = run_ref[...]
    ai = lax.broadcasted_iota(jnp.int32, (LANES, LANES), 0)
    bi = lax.broadcasted_iota(jnp.int32, (LANES, LANES), 1)
    before = _dot(selb, (ai < bi).astype(BF16))
    lane_f = lane.astype(F32)
    ek = jnp.zeros((tm, LANES), F32)
    rk = jnp.zeros((tm, LANES), F32)
    gk = jnp.zeros((tm, LANES), F32)
    for k in range(TOP_K):
        oh = sel & (before == k)
        ek = jnp.where(lane == k, jnp.sum(jnp.where(oh, lane_f, 0.0), axis=-1, keepdims=True), ek)
        rk = jnp.where(lane == k, jnp.sum(jnp.where(oh, cum, 0.0), axis=-1, keepdims=True), rk)
        gk = jnp.where(lane == k, jnp.sum(jnp.where(oh, gates, 0.0), axis=-1, keepdims=True), gk)
    ek_ref[...] = ek.astype(jnp.int32)
    rk_ref[...] = rk.astype(jnp.int32)
    gk_ref[...] = gk


def _route(scores, bias, tm=256):
    n = scores.shape[0]
    tm = min(tm, n)
    blk = pl.BlockSpec((tm, LANES), lambda i: (i, 0))
    one = pl.BlockSpec((1, LANES), lambda i: (0, 0))
    return pl.pallas_call(
        _route_kernel,
        grid=(n // tm,),
        in_specs=[blk, one],
        out_specs=[blk, blk, blk, one],
        out_shape=[jax.ShapeDtypeStruct((n, LANES), jnp.int32), jax.ShapeDtypeStruct((n, LANES), jnp.int32),
                   jax.ShapeDtypeStruct((n, LANES), F32), jax.ShapeDtypeStruct((1, LANES), F32)],
        scratch_shapes=[pltpu.VMEM((1, LANES), F32)],
        compiler_params=_cparams(("arbitrary",)),
        name="route",
    )(scores, bias)


def _row_copy(src, s_row, dst, d_row, sem):
    return pltpu.make_async_copy(src.at[pl.ds(s_row, 1)], dst.at[pl.ds(d_row, 1)], sem)


def _scatter_kernel(cnt_ref, pad_ref, pst_ref, nu_ref, dest_ref, x_ref, xs_ref, zrow_ref, zblk_ref, sem,
                    *, tm, n_tok_steps, n_blocks):
    i = pl.program_id(0)

    @pl.when(i < n_tok_steps)
    def _():
        def issue(t, c):
            for k in range(TOP_K):
                _row_copy(x_ref, t, xs_ref, dest_ref[t * TOP_K + k], sem).start()
            return c

        lax.fori_loop(0, tm, issue, 0)

        for k in range(TOP_K):
            pltpu.make_async_copy(x_ref, xs_ref.at[pl.ds(0, tm)], sem).wait()

    @pl.when((i >= n_tok_steps) & (i < n_tok_steps + N_EXPERTS))
    def _():
        e = i - n_tok_steps
        zrow_ref[...] = jnp.zeros_like(zrow_ref)
        lo, hi, base = cnt_ref[e], pad_ref[e], pst_ref[e]

        def issue(r, c):
            pltpu.make_async_copy(zrow_ref, xs_ref.at[pl.ds(base + r, 1)], sem).start()
            return c

        lax.fori_loop(lo, hi, issue, 0)

        def drain(r, c):
            pltpu.make_async_copy(zrow_ref, xs_ref.at[pl.ds(0, 1)], sem).wait()
            return c

        lax.fori_loop(lo, hi, drain, 0)

    @pl.when(i == n_tok_steps + N_EXPERTS)
    def _():
        zblk_ref[...] = jnp.zeros_like(zblk_ref)

        def issue(p, c):
            pltpu.make_async_copy(zblk_ref, xs_ref.at[pl.ds(p * MOE_BLOCK, MOE_BLOCK)], sem).start()
            return c

        lax.fori_loop(nu_ref[0], n_blocks, issue, 0)

        def drain(p, c):
            pltpu.make_async_copy(zblk_ref, xs_ref.at[pl.ds(0, MOE_BLOCK)], sem).wait()
            return c

        lax.fori_loop(nu_ref[0], n_blocks, drain, 0)


def _scatter_rows(x1, dest_flat, counts, padded, pstarts, n_used, n_blocks, tm=128):
    n, d = x1.shape
    tm = min(tm, n)
    n_tok_steps = n // tm
    return pl.pallas_call(
        functools.partial(_scatter_kernel, tm=tm, n_tok_steps=n_tok_steps, n_blocks=n_blocks),
        grid_spec=pltpu.PrefetchScalarGridSpec(
            num_scalar_prefetch=4,
            grid=(n_tok_steps + N_EXPERTS + 1,),
            in_specs=[pl.BlockSpec((tm * TOP_K,), lambda i, *_: (jnp.minimum(i, n_tok_steps - 1),),
                                   memory_space=pltpu.SMEM),
                      pl.BlockSpec((tm, d), lambda i, *_: (jnp.minimum(i, n_tok_steps - 1), 0))],
            out_specs=pl.BlockSpec(memory_space=pl.ANY),
            scratch_shapes=[pltpu.VMEM((1, d), x1.dtype), pltpu.VMEM((MOE_BLOCK, d), x1.dtype),
                            pltpu.SemaphoreType.DMA(())]),
        out_shape=jax.ShapeDtypeStruct((n_blocks * MOE_BLOCK, d), x1.dtype),
        compiler_params=_cparams(("arbitrary",)),
        name="moe_scatter",
    )(counts, padded, pstarts, n_used, dest_flat, x1)


def _expert_kernel(be_ref, first_ref, nxt_ref, nu_ref, xs_ref, wg_hbm, wu_hbm, wd_hbm, y_ref,
                   sg_ref, su_ref, sd_ref, wg_ref, wu_ref, wd_ref, sem):
    p = pl.program_id(0)
    used = p < nu_ref[0]

    def copies(e):
        return (pltpu.make_async_copy(wg_hbm.at[e], sg_ref, sem.at[0]),
                pltpu.make_async_copy(wu_hbm.at[e], su_ref, sem.at[1]),
                pltpu.make_async_copy(wd_hbm.at[e], sd_ref, sem.at[2]))

    @pl.when(p == 0)
    def _():
        for c in copies(be_ref[0]):
            c.start()

    @pl.when(used & (first_ref[p] == 1))
    def _():
        for c in copies(be_ref[p]):
            c.wait()
        wg_ref[...] = sg_ref[...].astype(BF16)
        wu_ref[...] = su_ref[...].astype(BF16)
        wd_ref[...] = sd_ref[...].astype(BF16)

        @pl.when(nxt_ref[p] >= 0)
        def _():
            for c in copies(nxt_ref[p]):
                c.start()

    @pl.when(used)
    def _():
        lo, hi = _unpack_halves(xs_ref[...])
        lo, hi = lo.astype(BF16), hi.astype(BF16)
        dh = lo.shape[1]
        hg = _dot(lo, wg_ref[:dh, :]) + _dot(hi, wg_ref[dh:, :])
        hu = _dot(lo, wu_ref[:dh, :]) + _dot(hi, wu_ref[dh:, :])
        h = (hg * _sigmoid(hg)) * hu
        y_ref[...] = _pack_halves(_dot(h.astype(BF16), wd_ref[...]))

    @pl.when(jnp.logical_not(used))
    def _():
        y_ref[...] = jnp.zeros_like(y_ref)


def _expert_mlp(xs, block_e, first, nxt, n_used, wg, wu, wd):
    n_rows, dp = xs.shape
    nblk = n_rows // MOE_BLOCK
    d, ff = wg.shape[-2:]
    assert d == 2 * dp

    def row_map(p, be, fi, nx, nu):
        return (jnp.minimum(p, nu[0] - 1), 0)

    hbm = pl.BlockSpec(memory_space=pl.ANY)
    return pl.pallas_call(
        _expert_kernel,
        grid_spec=pltpu.PrefetchScalarGridSpec(
            num_scalar_prefetch=4,
            grid=(nblk,),
            in_specs=[pl.BlockSpec((MOE_BLOCK, dp), row_map), hbm, hbm, hbm],
            out_specs=pl.BlockSpec((MOE_BLOCK, dp), lambda p, *_: (p, 0)),
            scratch_shapes=[pltpu.VMEM((d, ff), wg.dtype), pltpu.VMEM((d, ff), wu.dtype),
                            pltpu.VMEM((ff, d), wd.dtype),
                            pltpu.VMEM((d, ff), BF16), pltpu.VMEM((d, ff), BF16), pltpu.VMEM((ff, d), BF16),
                            pltpu.SemaphoreType.DMA((3,))]),
        out_shape=jax.ShapeDtypeStruct((n_rows, dp), jnp.uint32),
        compiler_params=_cparams(("arbitrary",), VMEM_LIMIT + 2 * 1024 * 1024),
        name="moe_experts",
    )(block_e, first, nxt, n_used, xs, wg, wu, wd)


COMBINE_PARTS = 4


def _combine_kernel(dest_ref, x1_ref, gk_ref, sg_ref, su_ref, sd_ref, g_ref, b_ref, ys_ref,
                    o_ref, ybuf_ref, sh_ref, sem, *, tm, alpha):
    tp = tm // COMBINE_PARTS

    def issue(part):
        def body(t, c):
            for k in range(TOP_K):
                _row_copy(ys_ref, dest_ref[t * TOP_K + k], ybuf_ref, k * tm + t, sem.at[part]).start()
            return c

        lax.fori_loop(part * tp, (part + 1) * tp, body, 0)

    issue(0)
    issue(1)
    xb = x1_ref[...].astype(BF16)
    hg = _dot(xb, sg_ref[...])
    hu = _dot(xb, su_ref[...])
    sh_ref[...] = _dot(((hg * _sigmoid(hg)) * hu).astype(BF16), sd_ref[...])

    for part in range(COMBINE_PARTS):
        rows = slice(part * tp, (part + 1) * tp)
        for k in range(TOP_K):
            pltpu.make_async_copy(ys_ref.at[pl.ds(0, tp)], ybuf_ref.at[pl.ds(0, tp)], sem.at[part]).wait()
        if part + 2 < COMBINE_PARTS:
            issue(part + 2)
        gk = gk_ref[rows, :]
        r_lo = jnp.zeros((tp, ybuf_ref.shape[1]), F32)
        r_hi = jnp.zeros((tp, ybuf_ref.shape[1]), F32)
        for k in range(TOP_K):
            lo, hi = _unpack_halves(ybuf_ref[k * tm + part * tp:k * tm + (part + 1) * tp, :])
            r_lo = r_lo + gk[:, k:k + 1] * lo
            r_hi = r_hi + gk[:, k:k + 1] * hi
        routed = jnp.concatenate([r_lo, r_hi], axis=1)
        o_ref[rows, :] = _layer_norm(alpha * x1_ref[rows, :] + (routed + sh_ref[rows, :]), g_ref[...], b_ref[...])


def _combine(x1, dest_flat, gk, ys, sg, su, sd, ln_g, ln_b, alpha, tm=128):
    n, d = x1.shape
    tm = min(tm, n)
    ff = sg.shape[-1]
    return pl.pallas_call(
        functools.partial(_combine_kernel, tm=tm, alpha=alpha),
        grid=(n // tm,),
        in_specs=[pl.BlockSpec((tm * TOP_K,), lambda i: (i,), memory_space=pltpu.SMEM),
                  pl.BlockSpec((tm, d), lambda i: (i, 0)),
                  pl.BlockSpec((tm, LANES), lambda i: (i, 0)),
                  pl.BlockSpec((d, ff), lambda i: (0, 0), pipeline_mode=pl.Buffered(1)),
                  pl.BlockSpec((d, ff), lambda i: (0, 0), pipeline_mode=pl.Buffered(1)),
                  pl.BlockSpec((ff, d), lambda i: (0, 0), pipeline_mode=pl.Buffered(1)),
                  pl.BlockSpec((1, d), lambda i: (0, 0)),
                  pl.BlockSpec((1, d), lambda i: (0, 0)),
                  pl.BlockSpec(memory_space=pl.ANY)],
        out_specs=pl.BlockSpec((tm, d), lambda i: (i, 0)),
        out_shape=jax.ShapeDtypeStruct((n, d), F32),
        scratch_shapes=[pltpu.VMEM((TOP_K * tm, d // 2), jnp.uint32), pltpu.VMEM((tm, d), F32),
                        pltpu.SemaphoreType.DMA((COMBINE_PARTS,))],
        compiler_params=_cparams(("arbitrary",)),
        name="moe_combine_ln2",
    )(dest_flat, x1, gk, sg, su, sd, ln_g, ln_b, ys)


def _rope_kernel(pos_ref, inv_ref, sign_ref, cos_ref, sin_ref):
    ang = pos_ref[...].astype(F32) * inv_ref[...]
    cos_ref[...] = jnp.cos(ang)
    sin_ref[...] = jnp.sin(ang) * sign_ref[...]


def _rope_tables(positions, dh, tm=512):
    half = dh // 2
    inv = ROPE_THETA ** (-jnp.arange(half, dtype=F32) * 2.0 / dh)
    reps = LANES // dh
    inv_l = jnp.tile(jnp.concatenate([inv, inv]), reps).reshape(1, LANES)
    sign_l = jnp.tile(jnp.concatenate([-jnp.ones((half,), F32), jnp.ones((half,), F32)]), reps).reshape(1, LANES)
    pos = positions.reshape(-1, 1)
    n = pos.shape[0]
    tm = min(tm, n)
    one = pl.BlockSpec((1, LANES), lambda i: (0, 0))
    blk = pl.BlockSpec((tm, LANES), lambda i: (i, 0))
    return pl.pallas_call(
        _rope_kernel,
        grid=(n // tm,),
        in_specs=[pl.BlockSpec((tm, 1), lambda i: (i, 0)), one, one],
        out_specs=[blk, blk],
        out_shape=[jax.ShapeDtypeStruct((n, LANES), F32)] * 2,
        compiler_params=_cparams(("parallel",)),
        name="rope_tables",
    )(pos, inv_l, sign_l)


def _dup_heads(w_rows, heads, dh):
    d = w_rows.shape[1]
    w3 = w_rows.reshape(heads, dh, d)
    return jnp.concatenate([w3, w3], axis=1).reshape(heads * 2 * dh, d)


def _hybrid_layer(x, positions, w_in, a_sinks, cmp_k_pos, cmp_k_w1, cmp_k_w2, cmp_v_pos, cmp_v_w1,
                  cmp_v_w2, w_o, ln1_g, ln1_b, w_router, router_bias, exp_w_gate, exp_w_up, exp_w_down,
                  sh_w_gate, sh_w_up, sh_w_down, ln2_g, ln2_b, alpha):
    bsz, seq, d = x.shape
    n = bsz * seq
    assert seq % NSA_TK == 0 and (seq // CMP_STRIDE) % LANES == 0 and A_WINDOW <= TQ and seq % (NSA_QT * TQ) == 0
    assert 3 <= seq // SLC_LEN <= LANES
    xf = x.reshape(n, d)
    xb = xf.astype(BF16)

    qa_w, kva = A_HEADS * A_HEAD_DIM, A_KV_HEADS * A_HEAD_DIM
    qb_w, kvb = B_HEADS * B_HEAD_DIM, B_KV_HEADS * B_HEAD_DIM
    widths = [qa_w, kva, kva, qb_w, kvb, kvb, kvb, kvb, kvb, kvb, 3 * B_HEADS, d, d]
    offs = np.concatenate([[0], np.cumsum(widths)]).tolist()
    (o_qa, o_ka, o_va, o_qb, o_kc, o_vc, o_ks, o_vs, o_kw, o_vw, o_gn, o_ga, o_end) = (
        offs[i] for i in (0, 1, 2, 3, 4, 5, 6, 7, 8, 9, 10, 11, 13))
    w_t = w_in.T
    blk = lambda start, width: list(range(start // PROJ_TN, (start + width) // PROJ_TN))
    assert all(o % PROJ_TN == 0 for o in (o_qa, o_qb, o_kc, o_vc, o_ks, o_vs, o_kw, o_vw))
    rep = B_HEADS // B_KV_HEADS
    w_gn = w_t[o_gn:o_ga].reshape(3, B_KV_HEADS, rep, d).transpose(1, 0, 2, 3).reshape(B_KV_HEADS, 3 * rep, d)
    w_gn_x = jnp.pad(w_gn, ((0, 0), (0, LANES - 3 * rep), (0, 0))).reshape(B_KV_HEADS * LANES, d)
    w_gn_x = w_gn_x.astype(BF16)
    assert o_ga % PROJ_TN == 3 * B_HEADS and o_end == w_t.shape[0]
    w_ka2 = _dup_heads(w_t[o_ka:o_va], A_KV_HEADS, A_HEAD_DIM).astype(BF16)
    w_va2 = _dup_heads(w_t[o_va:o_qb], A_KV_HEADS, A_HEAD_DIM).astype(BF16)
    all_blocks = lambda w: list(range(w.shape[0] // PROJ_TN))

    cos64, sin64 = _rope_tables(positions, A_HEAD_DIM)
    cos128, sin128 = _rope_tables(positions, B_HEAD_DIM)

    q_a = _project(xb, w_t, blk(o_qa, qa_w), cos64, sin64, "rope64", BF16,
                   q_cols=qa_w, q_scale=A_HEAD_DIM ** -0.5 * LOG2E)
    k_a = _project(xb, w_ka2, all_blocks(w_ka2), cos64, sin64, "rope64", BF16)
    v_a = _project(xb, w_va2, all_blocks(w_va2), cos64, sin64, "none", BF16)
    qk_b = _project(xb, w_t, blk(o_qb, qb_w) + blk(o_ks, kvb) + blk(o_kw, kvb), cos128, sin128, "rope128", BF16,
                    q_cols=qb_w, q_scale=B_HEAD_DIM ** -0.5 * LOG2E)
    v_b = _project(xb, w_t, blk(o_vs, kvb) + blk(o_vw, kvb), cos128, sin128, "none", BF16)
    kc = _project(xb, w_t, blk(o_kc, kvb), cos128, sin128, "rope128", F32)
    vc = _project(xb, w_t, blk(o_vc, kvb), cos128, sin128, "none", F32)
    gates = _project(xb, w_t, blk(o_ga - 3 * B_HEADS, 2 * d), cos128, sin128, "sigmoid", F32,
                     tm=512, shift=3 * B_HEADS)
    gnx = _project(xb, w_gn_x, all_blocks(w_gn_x), cos128, sin128, "sigmoid", F32)

    ya = _swa_attention(q_a, k_a, v_a, a_sinks, gates, bsz, seq)

    kcmp = _compress(kc, 0, cmp_k_pos, cmp_k_w1.astype(BF16), cmp_k_w2.astype(BF16), bsz, seq)
    vcmp = _compress(vc, 0, cmp_v_pos, cmp_v_w1.astype(BF16), cmp_v_w2.astype(BF16), bsz, seq)
    nsa_cols = dict(q=0, ks=qb_w, kw=qb_w + kvb, vs=0, vw=kvb, g_b=d, gate=0)
    y = _nsa_attention(qk_b, v_b, gnx, gates, ya, kcmp, vcmp, bsz, seq, nsa_cols)

    w_r = jnp.pad(w_router, ((0, 0), (0, LANES - N_EXPERTS))).astype(BF16)
    x1, x1p, scores = _out_proj(y, w_o.astype(BF16), xf, ln1_g.reshape(1, d), ln1_b.reshape(1, d), w_r, alpha)

    bias = jnp.pad(router_bias.astype(F32), (0, LANES - N_EXPERTS)).reshape(1, LANES)
    ek, rk, gk, cnt = _route(scores, bias)

    counts = cnt[0, :N_EXPERTS].astype(jnp.int32)
    padded = (counts + MOE_BLOCK - 1) // MOE_BLOCK * MOE_BLOCK
    pends = jnp.cumsum(padded)
    pstarts = pends - padded
    nblk = -(-(n * TOP_K) // MOE_BLOCK) + N_EXPERTS
    blk_start = jnp.arange(nblk, dtype=jnp.int32) * MOE_BLOCK
    block_e = jnp.minimum(jnp.sum((pends[None, :] <= blk_start[:, None]).astype(jnp.int32), axis=1),
                          N_EXPERTS - 1)
    n_used = (pends[-1:] // MOE_BLOCK).astype(jnp.int32)
    hit = ek[:, :TOP_K, None] == jnp.arange(N_EXPERTS, dtype=jnp.int32)
    dest = (jnp.sum(jnp.where(hit, pstarts.astype(jnp.int32), 0), axis=-1) + rk[:, :TOP_K]).reshape(-1)

    xs = _scatter_rows(x1p, dest, counts, padded, pstarts.astype(jnp.int32), n_used, nblk)
    first = jnp.concatenate([jnp.ones((1,), jnp.int32), (block_e[1:] != block_e[:-1]).astype(jnp.int32)])
    end_blk = jnp.take(pends // MOE_BLOCK, block_e).astype(jnp.int32)
    nxt = jnp.where(end_blk < n_used[0], jnp.take(block_e, jnp.minimum(end_blk, nblk - 1)), -1).astype(jnp.int32)
    ys = _expert_mlp(xs, block_e, first, nxt, n_used, exp_w_gate, exp_w_up, exp_w_down)
    out = _combine(x1, dest, gk, ys, sh_w_gate.astype(BF16), sh_w_up.astype(BF16), sh_w_down.astype(BF16),
                   ln2_g.reshape(1, d), ln2_b.reshape(1, d), alpha)
    return out.reshape(bsz, seq, d)


def kernel(x, positions, w_in, a_sinks, cmp_k_pos, cmp_k_w1, cmp_k_w2, cmp_v_pos, cmp_v_w1, cmp_v_w2,
           w_o, ln1_g, ln1_b, w_router, router_bias, exp_w_gate, exp_w_up, exp_w_down,
           sh_w_gate, sh_w_up, sh_w_down, ln2_g, ln2_b):
    depth = w_in.shape[0]
    alpha = (2.0 * depth) ** 0.25
    for l in range(depth):
        x = _hybrid_layer(x, positions, w_in[l], a_sinks[l], cmp_k_pos[l], cmp_k_w1[l], cmp_k_w2[l],
                          cmp_v_pos[l], cmp_v_w1[l], cmp_v_w2[l], w_o[l], ln1_g[l], ln1_b[l],
                          w_router[l], router_bias[l], exp_w_gate[l], exp_w_up[l], exp_w_down[l],
                          sh_w_gate[l], sh_w_up[l], sh_w_down[l], ln2_g[l], ln2_b[l], alpha)
    return x
```

```python
import functools

import jax
import jax.numpy as jnp
import numpy as np
from jax import lax
from jax.experimental import pallas as pl
from jax.experimental.pallas import tpu as pltpu

F32 = jnp.float32
BF16 = jnp.bfloat16

ROPE_THETA = 10000.0
LN_EPS = 1e-5
NEG_INF = -1e30

A_HEADS, A_KV_HEADS, A_HEAD_DIM, A_WINDOW = 64, 8, 64, 128
B_HEADS, B_KV_HEADS, B_HEAD_DIM = 32, 4, 128
CMP_LEN, CMP_STRIDE, CMP_HIDDEN = 32, 16, 256
SLC_LEN, SLC_TOPK, WIN_LEN = 64, 16, 512
N_EXPERTS, TOP_K, EXPERT_FF, SHARED_FF = 64, 8, 512, 512
ROUTED_SCALE = 2.5
MOE_BLOCK = 256

LOG2E = 1.4426950408889634
LANES = 128
TQ = 128
VMEM_LIMIT = 56 * 1024 * 1024


def _cparams(sem, vmem_limit=VMEM_LIMIT):
    return pltpu.CompilerParams(dimension_semantics=sem, vmem_limit_bytes=vmem_limit)


def _dot(a, b):
    return jnp.dot(a, b, preferred_element_type=F32)


def _dot_nt(a, b):
    return lax.dot_general(a, b, (((1,), (1,)), ((), ())), preferred_element_type=F32)


def _sigmoid(v):
    return 1.0 / (1.0 + jnp.exp(-v))


def _pack_halves(v):
    h = v.shape[1] // 2
    lo = lax.bitcast_convert_type(v[:, :h].astype(BF16).astype(F32), jnp.uint32)
    hi = lax.bitcast_convert_type(v[:, h:].astype(BF16).astype(F32), jnp.uint32)
    return (hi & jnp.uint32(0xFFFF0000)) | (lo >> 16)


def _unpack_halves(w):
    lo = lax.bitcast_convert_type(w << 16, F32)
    hi = lax.bitcast_convert_type(w & jnp.uint32(0xFFFF0000), F32)
    return lo, hi


def _layer_norm(z, g, b):
    mu = jnp.mean(z, axis=-1, keepdims=True)
    zc = z - mu
    var = jnp.mean(zc * zc, axis=-1, keepdims=True)
    return zc * lax.rsqrt(var + LN_EPS) * g + b


def _proj_kernel(tbl_ref, x_ref, *rest, epi, q_blocks, q_scale, shift):
    del tbl_ref
    if shift:
        w_ref, w2_ref, cos_ref, sin_ref, o_ref, *scratch = rest
    else:
        w_ref, cos_ref, sin_ref, o_ref, *scratch = rest
    if scratch:
        (wb_ref,) = scratch

        @pl.when(pl.program_id(1) == 0)
        def _():
            if shift:
                keep = wb_ref.shape[0] - shift
                wb_ref[:keep, :] = w_ref[shift:, :].astype(BF16)
                wb_ref[keep:, :] = w2_ref[:shift, :].astype(BF16)
            else:
                wb_ref[...] = w_ref[...].astype(BF16)

        w = wb_ref[...]
    else:
        w = w_ref[...]
    acc = _dot_nt(x_ref[...], w)
    if epi == "none":
        o_ref[...] = acc.astype(o_ref.dtype)
    elif epi == "sigmoid":
        o_ref[...] = _sigmoid(acc).astype(o_ref.dtype)
    else:
        cs = jnp.where(pl.program_id(0) < q_blocks, q_scale, 1.0)
        cos = cos_ref[...] * cs
        sin = sin_ref[...] * cs
        lane = lax.broadcasted_iota(jnp.int32, (1, LANES), 1)
        for c in range(acc.shape[1] // LANES):
            y = acc[:, c * LANES:(c + 1) * LANES]
            if epi == "rope128":
                rot = pltpu.roll(y, 64, 1)
            else:
                rot = jnp.where((lane % 64) < 32, pltpu.roll(y, 96, 1), pltpu.roll(y, 32, 1))
            o_ref[:, c * LANES:(c + 1) * LANES] = (y * cos + rot * sin).astype(o_ref.dtype)


PROJ_TN = 512


def _project(xb, w_t, blocks, cos, sin, epi, out_dtype, q_cols=0, q_scale=1.0, tm=1024, shift=0):
    n, d = xb.shape
    tn = PROJ_TN
    ncol = len(blocks) * tn
    tm = min(tm, n)
    cast = w_t.dtype != BF16
    assert n % tm == 0 and q_cols % tn == 0 and (max(blocks) + 1) * tn + shift <= w_t.shape[0]
    assert shift % 16 == 0 and (cast or not shift)
    w_specs = [pl.BlockSpec((tn, d), lambda j, i, tbl: (tbl[j], 0))]
    if shift:
        w_specs.append(pl.BlockSpec((tn, d), lambda j, i, tbl: (tbl[j] + 1, 0)))
    return pl.pallas_call(
        functools.partial(_proj_kernel, epi=epi, q_blocks=q_cols // tn, q_scale=q_scale, shift=shift),
        grid_spec=pltpu.PrefetchScalarGridSpec(
            num_scalar_prefetch=1,
            grid=(ncol // tn, n // tm),
            in_specs=[pl.BlockSpec((tm, d), lambda j, i, tbl: (i, 0)), *w_specs,
                      pl.BlockSpec((tm, LANES), lambda j, i, tbl: (i, 0)),
                      pl.BlockSpec((tm, LANES), lambda j, i, tbl: (i, 0))],
            out_specs=pl.BlockSpec((tm, tn), lambda j, i, tbl: (i, j)),
            scratch_shapes=[pltpu.VMEM((tn, d), BF16)] if cast else []),
        out_shape=jax.ShapeDtypeStruct((n, ncol), out_dtype),
        compiler_params=_cparams(("arbitrary", "arbitrary")),
        name="proj_" + epi,
    )(jnp.asarray(blocks, jnp.int32), xb, *([w_t] * len(w_specs)), cos, sin)


SWA_GROUPS = 4


def _swa_kernel(sink_ref, q_ref, kp_ref, kc_ref, vp_ref, vc_ref, ga_ref, o_ref, s_ref, p_ref):
    n = pl.program_id(1)
    lane = lax.broadcasted_iota(jnp.int32, (1, LANES), 1)
    lo = lane < A_HEAD_DIM
    nk = 2 * TQ
    qpos = n * TQ + lax.broadcasted_iota(jnp.int32, (TQ, nk), 0)
    kpos = (n - 1) * TQ + lax.broadcasted_iota(jnp.int32, (TQ, nk), 1)
    delta = qpos - kpos
    mask = (kpos >= 0) & (delta >= 0) & (delta < A_WINDOW)
    rep = A_HEADS // A_KV_HEADS
    npair = rep // 2
    zq = jnp.zeros((TQ, LANES), q_ref.dtype)
    for gg in range(SWA_GROUPS):
        g = pl.program_id(2) * SWA_GROUPS + gg
        gl = slice(gg * LANES, (gg + 1) * LANES)
        kk = jnp.concatenate([kp_ref[:, gl], kc_ref[:, gl]], axis=0)
        vv = jnp.concatenate([vp_ref[:, gl], vc_ref[:, gl]], axis=0)
        zv = jnp.zeros_like(vv)
        v_halves = (jnp.where(lo, vv, zv), jnp.where(lo, zv, vv))
        qcols = [slice((gg * npair + c) * LANES, (gg * npair + c + 1) * LANES) for c in range(npair)]
        q_lo = [jnp.where(lo, q_ref[:, c], zq) for c in qcols]
        q_hi = [jnp.where(lo, zq, q_ref[:, c]) for c in qcols]
        r0 = gg * rep * TQ
        s_ref[r0:r0 + rep * TQ, :] = _dot_nt(jnp.concatenate(q_lo + q_hi, axis=0), kk)
        for half in range(2):
            for pair in range(npair):
                sl = slice(r0 + (half * npair + pair) * TQ, r0 + (half * npair + pair + 1) * TQ)
                s = jnp.where(mask, s_ref[sl, :], NEG_INF)
                sk = sink_ref[g * rep + pair * 2 + half] * LOG2E
                m = jnp.maximum(jnp.max(s, axis=-1, keepdims=True), sk)
                e = jnp.exp2(s - m)
                p = e / (jnp.sum(e, axis=-1, keepdims=True) + jnp.exp2(sk - m))
                p_ref[sl, :] = p.astype(BF16)
        half_rows = npair * TQ
        out = (_dot(p_ref[r0:r0 + half_rows, :], v_halves[0])
               + _dot(p_ref[r0 + half_rows:r0 + 2 * half_rows, :], v_halves[1]))
        for pair, c in enumerate(qcols):
            o_ref[:, c] = ga_ref[:, c] * out[pair * TQ:(pair + 1) * TQ, :]


def _swa_attention(q, k, v, sinks, gates, bsz, seq):
    n = bsz * seq
    nb = seq // TQ
    rows = SWA_GROUPS * (A_HEADS // A_KV_HEADS) * TQ
    qw = SWA_GROUPS * (A_HEADS // A_KV_HEADS) * A_HEAD_DIM
    kw = SWA_GROUPS * LANES
    return pl.pallas_call(
        _swa_kernel,
        grid=(bsz, nb, A_KV_HEADS // SWA_GROUPS),
        in_specs=[pl.BlockSpec(memory_space=pltpu.SMEM),
                  pl.BlockSpec((TQ, qw), lambda b, i, g: (b * nb + i, g)),
                  pl.BlockSpec((TQ, kw), lambda b, i, g: (b * nb + jnp.maximum(i - 1, 0), g)),
                  pl.BlockSpec((TQ, kw), lambda b, i, g: (b * nb + i, g)),
                  pl.BlockSpec((TQ, kw), lambda b, i, g: (b * nb + jnp.maximum(i - 1, 0), g)),
                  pl.BlockSpec((TQ, kw), lambda b, i, g: (b * nb + i, g)),
                  pl.BlockSpec((TQ, qw), lambda b, i, g: (b * nb + i, g))],
        out_specs=pl.BlockSpec((TQ, qw), lambda b, i, g: (b * nb + i, g)),
        out_shape=jax.ShapeDtypeStruct((n, A_HEADS * A_HEAD_DIM), F32),
        scratch_shapes=[pltpu.VMEM((rows, 2 * TQ), F32), pltpu.VMEM((rows, 2 * TQ), BF16)],
        compiler_params=_cparams(("parallel", "parallel", "parallel")),
        name="swa",
    )(sinks, q, k, k, v, v, gates)


def _compress_kernel(t_ref, pe_ref, w1_ref, w2_ref, o_ref):
    nch = o_ref.shape[0]
    u = jnp.zeros((nch, CMP_HIDDEN), F32)
    v = jnp.zeros((nch, CMP_HIDDEN), F32)
    for l in range(CMP_STRIDE):
        t_l = t_ref[pl.ds(l, nch, stride=CMP_STRIDE), :]
        a = (t_l + pe_ref[l:l + 1, :]).astype(BF16)
        b = (t_l + pe_ref[CMP_STRIDE + l:CMP_STRIDE + l + 1, :]).astype(BF16)
        u = u + _dot(a, w1_ref[l * LANES:(l + 1) * LANES, :])
        v = v + _dot(b, w1_ref[(CMP_STRIDE + l) * LANES:(CMP_STRIDE + l + 1) * LANES, :])
    pre = u + pltpu.roll(v, nch - 1, 0)
    h = jax.nn.gelu(pre)
    o_ref[...] = _dot(h.astype(BF16), w2_ref[...]).astype(o_ref.dtype)


def _compress(t, col0, pe, w1, w2, bsz, seq):
    nch = seq // CMP_STRIDE
    cb0 = col0 // LANES
    return pl.pallas_call(
        _compress_kernel,
        grid=(bsz, B_KV_HEADS),
        in_specs=[pl.BlockSpec((seq, LANES), lambda b, g: (b, cb0 + g)),
                  pl.BlockSpec((CMP_LEN, LANES), lambda b, g: (0, 0)),
                  pl.BlockSpec((CMP_LEN * LANES, CMP_HIDDEN), lambda b, g: (0, 0)),
                  pl.BlockSpec((CMP_HIDDEN, LANES), lambda b, g: (0, 0))],
        out_specs=pl.BlockSpec((None, None, nch, LANES), lambda b, g: (b, g, 0, 0)),
        out_shape=jax.ShapeDtypeStruct((bsz, B_KV_HEADS, nch, LANES), BF16),
        compiler_params=_cparams(("parallel", "parallel")),
        name="nsa_compress",
    )(t, pe, w1, w2)


NSA_QT = 2
NSA_SCRATCH = 7
NSA_TK = 512
NSA_NKW = (-(-(WIN_LEN - 1) // TQ) + 1) * TQ


def _nsa_kernel(q_ref, gate_ref, kcmp_ref, vcmp_ref, ks_ref, vs_ref, kw_ref, vw_ref, ya_ref, gb_ref, o_ref,
                *scratch, seq):
    qs, sc, pr, mx, ls, acc, ocmp = (scratch[k::NSA_SCRATCH] for k in range(NSA_SCRATCH))
    step = pl.program_id(2)
    rep = B_HEADS // B_KV_HEADS
    nch = seq // CMP_STRIDE
    ns = seq // SLC_LEN
    n_sel = min(SLC_TOPK, ns)
    heads = [slice(r * TQ, (r + 1) * TQ) for r in range(rep)]
    tiles = range(NSA_QT)
    trows = [slice(h * TQ, (h + 1) * TQ) for h in tiles]
    qis = [step * NSA_QT + h for h in tiles]
    tpos = [qis[h] * TQ + lax.broadcasted_iota(jnp.int32, (TQ, 1), 0) for h in tiles]
    for h in tiles:
        for r in range(rep):
            qs[h][heads[r], :] = q_ref[trows[h], r * LANES:(r + 1) * LANES]

    c_end = lax.broadcasted_iota(jnp.int32, (1, nch), 1) * CMP_STRIDE + (CMP_LEN - 1)
    jrow = lax.broadcasted_iota(jnp.int32, (ns, 1), 0)
    d = lax.broadcasted_iota(jnp.int32, (ns, nch), 1) - (SLC_LEN // CMP_STRIDE) * jrow
    msel = jnp.where((d == -1) | (d == 3), 1.0, jnp.where((d >= 0) & (d <= 2), 2.0, 0.0)).astype(BF16)
    sels = []
    for h in tiles:
        sc[h][:, :nch] = _dot_nt(qs[h][...], kcmp_ref[...])
        valid = c_end <= tpos[h]
        p_grp = jnp.zeros((TQ, nch), F32)
        for sl in heads:
            s = jnp.where(valid, sc[h][sl, :nch], NEG_INF)
            m = jnp.max(s, axis=-1, keepdims=True)
            e = jnp.where(valid, jnp.exp2(s - m), 0.0)
            den = jnp.sum(e, axis=-1, keepdims=True)
            p = e / jnp.where(den > 0, den, 1.0)
            p_grp = p_grp + p
            pr[h][sl, :nch] = p.astype(BF16)
        ocmp[h][...] = _dot(pr[h][:, :nch], vcmp_ref[...])

        p_hi = p_grp.astype(BF16)
        r1 = p_grp - p_hi.astype(F32)
        p_mid = r1.astype(BF16)
        p_lo = (r1 - p_mid.astype(F32)).astype(BF16)
        p_slc = _dot_nt(msel, p_hi) + _dot_nt(msel, p_mid) + _dot_nt(msel, p_lo)

        tpos_l = qis[h] * TQ + lax.broadcasted_iota(jnp.int32, (1, TQ), 1)
        cur = tpos_l // SLC_LEN
        forced = (jrow == 0) | (jrow == cur) | (jrow == cur - 1)
        score = jnp.where(forced, 1e9, jnp.where(jrow * SLC_LEN <= tpos_l, p_slc, NEG_INF))
        rank = jnp.zeros((ns, TQ), jnp.int32)
        for i in range(ns):
            si = score[i:i + 1, :]
            rank = rank + ((si > score) | ((si == score) & (jrow > i))).astype(jnp.int32)
        sel_t = (rank < n_sel).astype(F32)
        if ns < LANES:
            sel_t = jnp.concatenate([sel_t, jnp.zeros((LANES - ns, TQ), F32)], axis=0)
        sels.append(sel_t.T.astype(BF16))

    for h in tiles:
        mx[h][...] = jnp.full_like(mx[h], NEG_INF)
        ls[h][...] = jnp.zeros_like(ls[h])
        acc[h][...] = jnp.zeros_like(acc[h])
    jexp = lax.broadcasted_iota(jnp.int32, (LANES, 1), 0)

    def slc_body(kt, c):
        row0 = pl.multiple_of(kt * NSA_TK, NSA_TK)
        k_t = ks_ref[pl.ds(row0, NSA_TK), :]
        v_t = vs_ref[pl.ds(row0, NSA_TK), :]
        kpos = kt * NSA_TK + lax.broadcasted_iota(jnp.int32, (1, NSA_TK), 1)
        expand = (jexp == kpos // SLC_LEN).astype(BF16)
        for h in tiles:
            sc[h][:, :NSA_TK] = _dot_nt(qs[h][...], k_t)
        for h in tiles:
            keep = (_dot(sels[h], expand) > 0.5) & (kpos <= tpos[h])
            for sl in heads:
                s = jnp.where(keep, sc[h][sl, :NSA_TK], NEG_INF)
                m_old = mx[h][sl, :]
                m_new = jnp.maximum(m_old, jnp.max(s, axis=-1, keepdims=True))
                alpha = jnp.exp2(m_old - m_new)
                e = jnp.exp2(s - m_new)
                ls[h][sl, :] = alpha * ls[h][sl, :] + jnp.sum(e, axis=-1, keepdims=True)
                mx[h][sl, :] = m_new
                acc[h][sl, :] = alpha * acc[h][sl, :]
                pr[h][sl, :NSA_TK] = e.astype(BF16)
            acc[h][...] += _dot(pr[h][:, :NSA_TK], v_t)
        return c

    lax.fori_loop(0, (qis[-1] * TQ + TQ + NSA_TK - 1) // NSA_TK, slc_body, 0)

    nprev = NSA_NKW // TQ - 1
    gate_all = gate_ref[...]
    for h in tiles:
        k_parts, v_parts = [], []
        for i in range(nprev + 1):
            row0 = pl.multiple_of(jnp.maximum(qis[h] - nprev + i, 0) * TQ, TQ)
            k_parts.append(kw_ref[pl.ds(row0, TQ), :])
            v_parts.append(vw_ref[pl.ds(row0, TQ), :])
        sc[h][:, :NSA_NKW] = _dot_nt(qs[h][...], jnp.concatenate(k_parts, axis=0))
        kposw = (qis[h] - nprev) * TQ + lax.broadcasted_iota(jnp.int32, (1, NSA_NKW), 1)
        dlt = tpos[h] - kposw
        keepw = (kposw >= 0) & (dlt >= 0) & (dlt < WIN_LEN)
        for sl in heads:
            s = jnp.where(keepw, sc[h][sl, :NSA_NKW], NEG_INF)
            e = jnp.exp2(s - jnp.max(s, axis=-1, keepdims=True))
            pr[h][sl, :NSA_NKW] = (e / jnp.sum(e, axis=-1, keepdims=True)).astype(BF16)
        o_win = _dot(pr[h][:, :NSA_NKW], jnp.concatenate(v_parts, axis=0))

        gate = gate_all[trows[h], :]
        for r, sl in enumerate(heads):
            cols = slice(r * LANES, (r + 1) * LANES)
            o_b = (gate[:, r:r + 1] * ocmp[h][sl, :]
                   + gate[:, rep + r:rep + r + 1] * (acc[h][sl, :] / ls[h][sl, :])
                   + gate[:, 2 * rep + r:2 * rep + r + 1] * o_win[sl])
            o_ref[trows[h], cols] = (ya_ref[trows[h], cols] + gb_ref[trows[h], cols] * o_b).astype(o_ref.dtype)


def _nsa_attention(qk, v, gnx, gates, ya, kcmp, vcmp, bsz, seq, cols):
    n = bsz * seq
    tq = NSA_QT * TQ
    nq = seq // tq
    nch = seq // CMP_STRIDE
    rep = B_HEADS // B_KV_HEADS
    qb0 = cols["q"] // (rep * LANES)
    gbb = cols["g_b"] // (rep * LANES)
    ksb, kwb, vsb, vwb, gb0 = (cols[k] // LANES for k in ("ks", "kw", "vs", "vw", "gate"))
    rows, wide = rep * TQ, max(NSA_NKW, NSA_TK, nch)
    return pl.pallas_call(
        functools.partial(_nsa_kernel, seq=seq),
        grid=(bsz, B_KV_HEADS, nq),
        in_specs=[pl.BlockSpec((tq, rep * LANES), lambda b, g, i: (b * nq + i, qb0 + g)),
                  pl.BlockSpec((tq, LANES), lambda b, g, i: (b * nq + i, gb0 + g)),
                  pl.BlockSpec((None, None, nch, LANES), lambda b, g, i: (b, g, 0, 0)),
                  pl.BlockSpec((None, None, nch, LANES), lambda b, g, i: (b, g, 0, 0)),
                  pl.BlockSpec((seq, LANES), lambda b, g, i: (b, ksb + g)),
                  pl.BlockSpec((seq, LANES), lambda b, g, i: (b, vsb + g)),
                  pl.BlockSpec((seq, LANES), lambda b, g, i: (b, kwb + g)),
                  pl.BlockSpec((seq, LANES), lambda b, g, i: (b, vwb + g)),
                  pl.BlockSpec((tq, rep * LANES), lambda b, g, i: (b * nq + i, g)),
                  pl.BlockSpec((tq, rep * LANES), lambda b, g, i: (b * nq + i, gbb + g))],
        out_specs=pl.BlockSpec((tq, rep * LANES), lambda b, g, i: (b * nq + i, g)),
        out_shape=jax.ShapeDtypeStruct((n, B_HEADS * B_HEAD_DIM), BF16),
        scratch_shapes=[pltpu.VMEM((rows, LANES), BF16),
                        pltpu.VMEM((rows, wide), F32),
                        pltpu.VMEM((rows, wide), BF16),
                        pltpu.VMEM((rows, 1), F32), pltpu.VMEM((rows, 1), F32),
                        pltpu.VMEM((rows, LANES), F32),
                        pltpu.VMEM((rows, LANES), F32)] * NSA_QT,
        compiler_params=_cparams(("parallel", "parallel", "arbitrary")),
        name="nsa",
    )(qk, gnx, kcmp, vcmp, qk, v, qk, v, ya, gates)


def _oproj_kernel(y_ref, wo_ref, x_ref, g_ref, b_ref, wr_ref, x1_ref, x1p_ref, sc_ref, z_ref, *, alpha):
    j = pl.program_id(1)
    z_ref[j] = _dot(y_ref[...], wo_ref[...])

    @pl.when(j == pl.num_programs(1) - 1)
    def _():
        nj, tm, _ = z_ref.shape
        piece = min(128, tm)
        for r0 in range(0, tm, piece):
            rows = slice(r0, r0 + piece)
            z = jnp.concatenate([z_ref[jj, rows, :] for jj in range(nj)], axis=1)
            x1 = _layer_norm(alpha * x_ref[rows, :] + z, g_ref[...], b_ref[...])
            x1_ref[rows, :] = x1
            x1p_ref[rows, :] = _pack_halves(x1)
            sc_ref[rows, :] = _sigmoid(_dot(x1.astype(BF16), wr_ref[...]))


def _out_proj(y, w_o, x, ln_g, ln_b, w_r, alpha, tm=512, tn=512):
    n, d = x.shape
    tm = min(tm, n)
    once = dict(pipeline_mode=pl.Buffered(1))
    return pl.pallas_call(
        functools.partial(_oproj_kernel, alpha=alpha),
        grid=(n // tm, d // tn),
        in_specs=[pl.BlockSpec((tm, d), lambda i, j: (i, 0)),
                  pl.BlockSpec((d, tn), lambda i, j: (0, j)),
                  pl.BlockSpec((tm, d), lambda i, j: (i, 0), **once),
                  pl.BlockSpec((1, d), lambda i, j: (0, 0)),
                  pl.BlockSpec((1, d), lambda i, j: (0, 0)),
                  pl.BlockSpec((d, LANES), lambda i, j: (0, 0))],
        out_specs=[pl.BlockSpec((tm, d), lambda i, j: (i, 0), **once),
                   pl.BlockSpec((tm, d // 2), lambda i, j: (i, 0), **once),
                   pl.BlockSpec((tm, LANES), lambda i, j: (i, 0))],
        out_shape=[jax.ShapeDtypeStruct((n, d), F32), jax.ShapeDtypeStruct((n, d // 2), jnp.uint32),
                   jax.ShapeDtypeStruct((n, LANES), F32)],
        scratch_shapes=[pltpu.VMEM((d // tn, tm, tn), F32)],
        compiler_params=_cparams(("parallel", "arbitrary")),
        name="oproj_ln1_router",
    )(y, w_o, x, ln_g, ln_b, w_r)


def _route_kernel(sc_ref, bias_ref, ek_ref, rk_ref, gk_ref, cnt_ref, run_ref):
    i = pl.program_id(0)
    tm = sc_ref.shape[0]

    @pl.when(i == 0)
    def _():
        run_ref[...] = jnp.zeros_like(run_ref)

    sc = sc_ref[...]
    lane = lax.broadcasted_iota(jnp.int32, (1, LANES), 1)
    is_e = lane < N_EXPERTS
    biased = jnp.where(is_e, sc + bias_ref[...], -jnp.inf)
    rank = jnp.zeros((tm, LANES), jnp.int32)
    for e in range(N_EXPERTS):
        be = biased[:, e:e + 1]
        rank = rank + ((be > biased) | ((be == biased) & (e < lane))).astype(jnp.int32)
    sel = (rank < TOP_K) & is_e
    sv = jnp.where(sel, sc, 0.0)
    gates = sv / jnp.sum(sv, axis=-1, keepdims=True) * ROUTED_SCALE
    selb = sel.astype(BF16)
    ri = lax.broadcasted_iota(jnp.int32, (tm, tm), 0)
    ci = lax.broadcasted_iota(jnp.int32, (tm, tm), 1)
    cum = _dot((ci < ri).astype(BF16), selb) + run_ref[...]
    run_ref[...] += jnp.sum(sel.astype(F32), axis=0, keepdims=True)
    cnt_ref[...] = run_ref[...]
    ai = lax.broadcasted_iota(jnp.int32, (LANES, LANES), 0)
    bi = lax.broadcasted_iota(jnp.int32, (LANES, LANES), 1)
    before = _dot(selb, (ai < bi).astype(BF16))
    lane_f = lane.astype(F32)
    ek = jnp.zeros((tm, LANES), F32)
    rk = jnp.zeros((tm, LANES), F32)
    gk = jnp.zeros((tm, LANES), F32)
    for k in range(TOP_K):
        oh = sel & (before == k)
        ek = jnp.where(lane == k, jnp.sum(jnp.where(oh, lane_f, 0.0), axis=-1, keepdims=True), ek)
        rk = jnp.where(lane == k, jnp.sum(jnp.where(oh, cum, 0.0), axis=-1, keepdims=True), rk)
        gk = jnp.where(lane == k, jnp.sum(jnp.where(oh, gates, 0.0), axis=-1, keepdims=True), gk)
    ek_ref[...] = ek.astype(jnp.int32)
    rk_ref[...] = rk.astype(jnp.int32)
    gk_ref[...] = gk


def _route(scores, bias, tm=256):
    n = scores.shape[0]
    tm = min(tm, n)
    blk = pl.BlockSpec((tm, LANES), lambda i: (i, 0))
    one = pl.BlockSpec((1, LANES), lambda i: (0, 0))
    return pl.pallas_call(
        _route_kernel,
        grid=(n // tm,),
        in_specs=[blk, one],
        out_specs=[blk, blk, blk, one],
        out_shape=[jax.ShapeDtypeStruct((n, LANES), jnp.int32), jax.ShapeDtypeStruct((n, LANES), jnp.int32),
                   jax.ShapeDtypeStruct((n, LANES), F32), jax.ShapeDtypeStruct((1, LANES), F32)],
        scratch_shapes=[pltpu.VMEM((1, LANES), F32)],
        compiler_params=_cparams(("arbitrary",)),
        name="route",
    )(scores, bias)


def _row_copy(src, s_row, dst, d_row, sem):
    return pltpu.make_async_copy(src.at[pl.ds(s_row, 1)], dst.at[pl.ds(d_row, 1)], sem)


def _scatter_kernel(cnt_ref, pad_ref, pst_ref, nu_ref, dest_ref, x_ref, xs_ref, zrow_ref, zblk_ref, sem,
                    *, tm, n_tok_steps, n_blocks):
    i = pl.program_id(0)

    @pl.when(i < n_tok_steps)
    def _():
        def issue(t, c):
            for k in range(TOP_K):
                _row_copy(x_ref, t, xs_ref, dest_ref[t * TOP_K + k], sem).start(priority=k % 2)
            return c

        lax.fori_loop(0, tm, issue, 0)

        for k in range(TOP_K):
            pltpu.make_async_copy(x_ref, xs_ref.at[pl.ds(0, tm)], sem).wait()

    @pl.when((i >= n_tok_steps) & (i < n_tok_steps + N_EXPERTS))
    def _():
        e = i - n_tok_steps
        zrow_ref[...] = jnp.zeros_like(zrow_ref)
        lo, hi, base = cnt_ref[e], pad_ref[e], pst_ref[e]

        def issue(r, c):
            pltpu.make_async_copy(zrow_ref, xs_ref.at[pl.ds(base + r, 1)], sem).start()
            return c

        lax.fori_loop(lo, hi, issue, 0)

        def drain(r, c):
            pltpu.make_async_copy(zrow_ref, xs_ref.at[pl.ds(0, 1)], sem).wait()
            return c

        lax.fori_loop(lo, hi, drain, 0)

    @pl.when(i == n_tok_steps + N_EXPERTS)
    def _():
        zblk_ref[...] = jnp.zeros_like(zblk_ref)

        def issue(p, c):
            pltpu.make_async_copy(zblk_ref, xs_ref.at[pl.ds(p * MOE_BLOCK, MOE_BLOCK)], sem).start()
            return c

        lax.fori_loop(nu_ref[0], n_blocks, issue, 0)

        def drain(p, c):
            pltpu.make_async_copy(zblk_ref, xs_ref.at[pl.ds(0, MOE_BLOCK)], sem).wait()
            return c

        lax.fori_loop(nu_ref[0], n_blocks, drain, 0)


def _scatter_rows(x1, dest_flat, counts, padded, pstarts, n_used, n_blocks, tm=128):
    n, d = x1.shape
    tm = min(tm, n)
    n_tok_steps = n // tm
    return pl.pallas_call(
        functools.partial(_scatter_kernel, tm=tm, n_tok_steps=n_tok_steps, n_blocks=n_blocks),
        grid_spec=pltpu.PrefetchScalarGridSpec(
            num_scalar_prefetch=4,
            grid=(n_tok_steps + N_EXPERTS + 1,),
            in_specs=[pl.BlockSpec((tm * TOP_K,), lambda i, *_: (jnp.minimum(i, n_tok_steps - 1),),
                                   memory_space=pltpu.SMEM),
                      pl.BlockSpec((tm, d), lambda i, *_: (jnp.minimum(i, n_tok_steps - 1), 0))],
            out_specs=pl.BlockSpec(memory_space=pl.ANY),
            scratch_shapes=[pltpu.VMEM((1, d), x1.dtype), pltpu.VMEM((MOE_BLOCK, d), x1.dtype),
                            pltpu.SemaphoreType.DMA(())]),
        out_shape=jax.ShapeDtypeStruct((n_blocks * MOE_BLOCK, d), x1.dtype),
        compiler_params=_cparams(("arbitrary",)),
        name="moe_scatter",
    )(counts, padded, pstarts, n_used, dest_flat, x1)


def _expert_kernel(be_ref, first_ref, nxt_ref, nu_ref, xs_ref, wg_hbm, wu_hbm, wd_hbm, y_ref,
                   sg_ref, su_ref, sd_ref, wg_ref, wu_ref, wd_ref, sem):
    p = pl.program_id(0)
    used = p < nu_ref[0]

    def copies(e):
        return (pltpu.make_async_copy(wg_hbm.at[e], sg_ref, sem.at[0]),
                pltpu.make_async_copy(wu_hbm.at[e], su_ref, sem.at[1]),
                pltpu.make_async_copy(wd_hbm.at[e], sd_ref, sem.at[2]))

    @pl.when(p == 0)
    def _():
        for c in copies(be_ref[0]):
            c.start()

    @pl.when(used & (first_ref[p] == 1))
    def _():
        for c in copies(be_ref[p]):
            c.wait()
        wg_ref[...] = sg_ref[...].astype(BF16)
        wu_ref[...] = su_ref[...].astype(BF16)
        wd_ref[...] = sd_ref[...].astype(BF16)

        @pl.when(nxt_ref[p] >= 0)
        def _():
            for c in copies(nxt_ref[p]):
                c.start()

    @pl.when(used)
    def _():
        lo, hi = _unpack_halves(xs_ref[...])
        lo, hi = lo.astype(BF16), hi.astype(BF16)
        dh = lo.shape[1]
        hg = _dot(lo, wg_ref[:dh, :]) + _dot(hi, wg_ref[dh:, :])
        hu = _dot(lo, wu_ref[:dh, :]) + _dot(hi, wu_ref[dh:, :])
        h = (hg * _sigmoid(hg)) * hu
        y_ref[...] = _pack_halves(_dot(h.astype(BF16), wd_ref[...]))

    @pl.when(jnp.logical_not(used))
    def _():
        y_ref[...] = jnp.zeros_like(y_ref)


def _expert_mlp(xs, block_e, first, nxt, n_used, wg, wu, wd):
    n_rows, dp = xs.shape
    nblk = n_rows // MOE_BLOCK
    d, ff = wg.shape[-2:]
    assert d == 2 * dp

    def row_map(p, be, fi, nx, nu):
        return (jnp.minimum(p, nu[0] - 1), 0)

    hbm = pl.BlockSpec(memory_space=pl.ANY)
    return pl.pallas_call(
        _expert_kernel,
        grid_spec=pltpu.PrefetchScalarGridSpec(
            num_scalar_prefetch=4,
            grid=(nblk,),
            in_specs=[pl.BlockSpec((MOE_BLOCK, dp), row_map), hbm, hbm, hbm],
            out_specs=pl.BlockSpec((MOE_BLOCK, dp), lambda p, *_: (p, 0)),
            scratch_shapes=[pltpu.VMEM((d, ff), wg.dtype), pltpu.VMEM((d, ff), wu.dtype),
                            pltpu.VMEM((ff, d), wd.dtype),
                            pltpu.VMEM((d, ff), BF16), pltpu.VMEM((d, ff), BF16), pltpu.VMEM((ff, d), BF16),
                            pltpu.SemaphoreType.DMA((3,))]),
        out_shape=jax.ShapeDtypeStruct((n_rows, dp), jnp.uint32),
        compiler_params=_cparams(("arbitrary",), VMEM_LIMIT + 2 * 1024 * 1024),
        name="moe_experts",
    )(block_e, first, nxt, n_used, xs, wg, wu, wd)


COMBINE_PARTS = 4


def _combine_kernel(dest_ref, x1_ref, gk_ref, sg_ref, su_ref, sd_ref, g_ref, b_ref, ys_ref,
                    o_ref, ybuf_ref, sh_ref, sem, *, tm, alpha):
    tp = tm // COMBINE_PARTS

    def issue(part):
        def body(t, c):
            for k in range(TOP_K):
                _row_copy(ys_ref, dest_ref[t * TOP_K + k], ybuf_ref, k * tm + t,
                          sem.at[part]).start(priority=k % 2)
            return c

        lax.fori_loop(part * tp, (part + 1) * tp, body, 0)

    issue(0)
    issue(1)
    xb = x1_ref[...].astype(BF16)
    hg = _dot(xb, sg_ref[...])
    hu = _dot(xb, su_ref[...])
    sh_ref[...] = _dot(((hg * _sigmoid(hg)) * hu).astype(BF16), sd_ref[...])

    for part in range(COMBINE_PARTS):
        rows = slice(part * tp, (part + 1) * tp)
        for k in range(TOP_K):
            pltpu.make_async_copy(ys_ref.at[pl.ds(0, tp)], ybuf_ref.at[pl.ds(0, tp)], sem.at[part]).wait()
        if part + 2 < COMBINE_PARTS:
            issue(part + 2)
        gk = gk_ref[rows, :]
        r_lo = jnp.zeros((tp, ybuf_ref.shape[1]), F32)
        r_hi = jnp.zeros((tp, ybuf_ref.shape[1]), F32)
        for k in range(TOP_K):
            lo, hi = _unpack_halves(ybuf_ref[k * tm + part * tp:k * tm + (part + 1) * tp, :])
            r_lo = r_lo + gk[:, k:k + 1] * lo
            r_hi = r_hi + gk[:, k:k + 1] * hi
        routed = jnp.concatenate([r_lo, r_hi], axis=1)
        o_ref[rows, :] = _layer_norm(alpha * x1_ref[rows, :] + (routed + sh_ref[rows, :]), g_ref[...], b_ref[...])


def _combine(x1, dest_flat, gk, ys, sg, su, sd, ln_g, ln_b, alpha, tm=128):
    n, d = x1.shape
    tm = min(tm, n)
    ff = sg.shape[-1]
    return pl.pallas_call(
        functools.partial(_combine_kernel, tm=tm, alpha=alpha),
        grid=(n // tm,),
        in_specs=[pl.BlockSpec((tm * TOP_K,), lambda i: (i,), memory_space=pltpu.SMEM),
                  pl.BlockSpec((tm, d), lambda i: (i, 0)),
                  pl.BlockSpec((tm, LANES), lambda i: (i, 0)),
                  pl.BlockSpec((d, ff), lambda i: (0, 0), pipeline_mode=pl.Buffered(1)),
                  pl.BlockSpec((d, ff), lambda i: (0, 0), pipeline_mode=pl.Buffered(1)),
                  pl.BlockSpec((ff, d), lambda i: (0, 0), pipeline_mode=pl.Buffered(1)),
                  pl.BlockSpec((1, d), lambda i: (0, 0)),
                  pl.BlockSpec((1, d), lambda i: (0, 0)),
                  pl.BlockSpec(memory_space=pl.ANY)],
        out_specs=pl.BlockSpec((tm, d), lambda i: (i, 0)),
        out_shape=jax.ShapeDtypeStruct((n, d), F32),
        scratch_shapes=[pltpu.VMEM((TOP_K * tm, d // 2), jnp.uint32), pltpu.VMEM((tm, d), F32),
                        pltpu.SemaphoreType.DMA((COMBINE_PARTS,))],
        compiler_params=_cparams(("arbitrary",)),
        name="moe_combine_ln2",
    )(dest_flat, x1, gk, sg, su, sd, ln_g, ln_b, ys)


def _rope_kernel(pos_ref, inv_ref, sign_ref, cos_ref, sin_ref):
    ang = pos_ref[...].astype(F32) * inv_ref[...]
    cos_ref[...] = jnp.cos(ang)
    sin_ref[...] = jnp.sin(ang) * sign_ref[...]


def _rope_tables(positions, dh, tm=512):
    half = dh // 2
    inv = ROPE_THETA ** (-jnp.arange(half, dtype=F32) * 2.0 / dh)
    reps = LANES // dh
    inv_l = jnp.tile(jnp.concatenate([inv, inv]), reps).reshape(1, LANES)
    sign_l = jnp.tile(jnp.concatenate([-jnp.ones((half,), F32), jnp.ones((half,), F32)]), reps).reshape(1, LANES)
    pos = positions.reshape(-1, 1)
    n = pos.shape[0]
    tm = min(tm, n)
    one = pl.BlockSpec((1, LANES), lambda i: (0, 0))
    blk = pl.BlockSpec((tm, LANES), lambda i: (i, 0))
    return pl.pallas_call(
        _rope_kernel,
        grid=(n // tm,),
        in_specs=[pl.BlockSpec((tm, 1), lambda i: (i, 0)), one, one],
        out_specs=[blk, blk],
        out_shape=[jax.ShapeDtypeStruct((n, LANES), F32)] * 2,
        compiler_params=_cparams(("parallel",)),
        name="rope_tables",
    )(pos, inv_l, sign_l)


def _dup_heads(w_rows, heads, dh):
    d = w_rows.shape[1]
    w3 = w_rows.reshape(heads, dh, d)
    return jnp.concatenate([w3, w3], axis=1).reshape(heads * 2 * dh, d)


def _hybrid_layer(x, positions, w_in, a_sinks, cmp_k_pos, cmp_k_w1, cmp_k_w2, cmp_v_pos, cmp_v_w1,
                  cmp_v_w2, w_o, ln1_g, ln1_b, w_router, router_bias, exp_w_gate, exp_w_up, exp_w_down,
                  sh_w_gate, sh_w_up, sh_w_down, ln2_g, ln2_b, alpha):
    bsz, seq, d = x.shape
    n = bsz * seq
    assert seq % NSA_TK == 0 and (seq // CMP_STRIDE) % LANES == 0 and A_WINDOW <= TQ and seq % (NSA_QT * TQ) == 0
    assert 3 <= seq // SLC_LEN <= LANES
    xf = x.reshape(n, d)
    xb = xf.astype(BF16)

    qa_w, kva = A_HEADS * A_HEAD_DIM, A_KV_HEADS * A_HEAD_DIM
    qb_w, kvb = B_HEADS * B_HEAD_DIM, B_KV_HEADS * B_HEAD_DIM
    widths = [qa_w, kva, kva, qb_w, kvb, kvb, kvb, kvb, kvb, kvb, 3 * B_HEADS, d, d]
    offs = np.concatenate([[0], np.cumsum(widths)]).tolist()
    (o_qa, o_ka, o_va, o_qb, o_kc, o_vc, o_ks, o_vs, o_kw, o_vw, o_gn, o_ga, o_end) = (
        offs[i] for i in (0, 1, 2, 3, 4, 5, 6, 7, 8, 9, 10, 11, 13))
    w_t = w_in.T
    blk = lambda start, width: list(range(start // PROJ_TN, (start + width) // PROJ_TN))
    assert all(o % PROJ_TN == 0 for o in (o_qa, o_qb, o_kc, o_vc, o_ks, o_vs, o_kw, o_vw))
    rep = B_HEADS // B_KV_HEADS
    w_gn = w_t[o_gn:o_ga].reshape(3, B_KV_HEADS, rep, d).transpose(1, 0, 2, 3).reshape(B_KV_HEADS, 3 * rep, d)
    w_gn_x = jnp.pad(w_gn, ((0, 0), (0, LANES - 3 * rep), (0, 0))).reshape(B_KV_HEADS * LANES, d)
    w_gn_x = w_gn_x.astype(BF16)
    assert o_ga % PROJ_TN == 3 * B_HEADS and o_end == w_t.shape[0]
    w_ka2 = _dup_heads(w_t[o_ka:o_va], A_KV_HEADS, A_HEAD_DIM).astype(BF16)
    w_va2 = _dup_heads(w_t[o_va:o_qb], A_KV_HEADS, A_HEAD_DIM).astype(BF16)
    all_blocks = lambda w: list(range(w.shape[0] // PROJ_TN))

    cos64, sin64 = _rope_tables(positions, A_HEAD_DIM)
    cos128, sin128 = _rope_tables(positions, B_HEAD_DIM)

    q_a = _project(xb, w_t, blk(o_qa, qa_w), cos64, sin64, "rope64", BF16,
                   q_cols=qa_w, q_scale=A_HEAD_DIM ** -0.5 * LOG2E)
    k_a = _project(xb, w_ka2, all_blocks(w_ka2), cos64, sin64, "rope64", BF16)
    v_a = _project(xb, w_va2, all_blocks(w_va2), cos64, sin64, "none", BF16)
    qk_b = _project(xb, w_t, blk(o_qb, qb_w) + blk(o_ks, kvb) + blk(o_kw, kvb), cos128, sin128, "rope128", BF16,
                    q_cols=qb_w, q_scale=B_HEAD_DIM ** -0.5 * LOG2E)
    v_b = _project(xb, w_t, blk(o_vs, kvb) + blk(o_vw, kvb), cos128, sin128, "none", BF16)
    kc = _project(xb, w_t, blk(o_kc, kvb), cos128, sin128, "rope128", F32)
    vc = _project(xb, w_t, blk(o_vc, kvb), cos128, sin128, "none", F32)
    gates = _project(xb, w_t, blk(o_ga - 3 * B_HEADS, 2 * d), cos128, sin128, "sigmoid", F32,
                     tm=512, shift=3 * B_HEADS)
    gnx = _project(xb, w_gn_x, all_blocks(w_gn_x), cos128, sin128, "sigmoid", F32)

    ya = _swa_attention(q_a, k_a, v_a, a_sinks, gates, bsz, seq)

    kcmp = _compress(kc, 0, cmp_k_pos, cmp_k_w1.astype(BF16), cmp_k_w2.astype(BF16), bsz, seq)
    vcmp = _compress(vc, 0, cmp_v_pos, cmp_v_w1.astype(BF16), cmp_v_w2.astype(BF16), bsz, seq)
    nsa_cols = dict(q=0, ks=qb_w, kw=qb_w + kvb, vs=0, vw=kvb, g_b=d, gate=0)
    y = _nsa_attention(qk_b, v_b, gnx, gates, ya, kcmp, vcmp, bsz, seq, nsa_cols)

    w_r = jnp.pad(w_router, ((0, 0), (0, LANES - N_EXPERTS))).astype(BF16)
    x1, x1p, scores = _out_proj(y, w_o.astype(BF16), xf, ln1_g.reshape(1, d), ln1_b.reshape(1, d), w_r, alpha)

    bias = jnp.pad(router_bias.astype(F32), (0, LANES - N_EXPERTS)).reshape(1, LANES)
    ek, rk, gk, cnt = _route(scores, bias)

    counts = cnt[0, :N_EXPERTS].astype(jnp.int32)
    padded = (counts + MOE_BLOCK - 1) // MOE_BLOCK * MOE_BLOCK
    pends = jnp.cumsum(padded)
    pstarts = pends - padded
    nblk = -(-(n * TOP_K) // MOE_BLOCK) + N_EXPERTS
    blk_start = jnp.arange(nblk, dtype=jnp.int32) * MOE_BLOCK
    block_e = jnp.minimum(jnp.sum((pends[None, :] <= blk_start[:, None]).astype(jnp.int32), axis=1),
                          N_EXPERTS - 1)
    n_used = (pends[-1:] // MOE_BLOCK).astype(jnp.int32)
    hit = ek[:, :TOP_K, None] == jnp.arange(N_EXPERTS, dtype=jnp.int32)
    dest = (jnp.sum(jnp.where(hit, pstarts.astype(jnp.int32), 0), axis=-1) + rk[:, :TOP_K]).reshape(-1)

    xs = _scatter_rows(x1p, dest, counts, padded, pstarts.astype(jnp.int32), n_used, nblk)
    first = jnp.concatenate([jnp.ones((1,), jnp.int32), (block_e[1:] != block_e[:-1]).astype(jnp.int32)])
    end_blk = jnp.take(pends // MOE_BLOCK, block_e).astype(jnp.int32)
    nxt = jnp.where(end_blk < n_used[0], jnp.take(block_e, jnp.minimum(end_blk, nblk - 1)), -1).astype(jnp.int32)
    ys = _expert_mlp(xs, block_e, first, nxt, n_used, exp_w_gate, exp_w_up, exp_w_down)
    out = _combine(x1, dest, gk, ys, sh_w_gate.astype(BF16), sh_w_up.astype(BF16), sh_w_down.astype(BF16),
                   ln2_g.reshape(1, d), ln2_b.reshape(1, d), alpha)
    return out.reshape(bsz, seq, d)


def kernel(x, positions, w_in, a_sinks, cmp_k_pos, cmp_k_w1, cmp_k_w2, cmp_v_pos, cmp_v_w1, cmp_v_w2,
           w_o, ln1_g, ln1_b, w_router, router_bias, exp_w_gate, exp_w_up, exp_w_down,
           sh_w_gate, sh_w_up, sh_w_down, ln2_g, ln2_b):
    depth = w_in.shape[0]
    alpha = (2.0 * depth) ** 0.25
    for l in range(depth):
        x = _hybrid_layer(x, positions, w_in[l], a_sinks[l], cmp_k_pos[l], cmp_k_w1[l], cmp_k_w2[l],
                          cmp_v_pos[l], cmp_v_w1[l], cmp_v_w2[l], w_o[l], ln1_g[l], ln1_b[l],
                          w_router[l], router_bias[l], exp_w_gate[l], exp_w_up[l], exp_w_down[l],
                          sh_w_gate[l], sh_w_up[l], sh_w_down[l], ln2_g[l], ln2_b[l], alpha)
    return x
```

```python
import functools

import jax
import jax.numpy as jnp
import numpy as np
from jax import lax
from jax.experimental import pallas as pl
from jax.experimental.pallas import tpu as pltpu

F32 = jnp.float32
BF16 = jnp.bfloat16

ROPE_THETA = 10000.0
LN_EPS = 1e-5
NEG_INF = -1e30

A_HEADS, A_KV_HEADS, A_HEAD_DIM, A_WINDOW = 64, 8, 64, 128
B_HEADS, B_KV_HEADS, B_HEAD_DIM = 32, 4, 128
CMP_LEN, CMP_STRIDE, CMP_HIDDEN = 32, 16, 256
SLC_LEN, SLC_TOPK, WIN_LEN = 64, 16, 512
N_EXPERTS, TOP_K, EXPERT_FF, SHARED_FF = 64, 8, 512, 512
ROUTED_SCALE = 2.5
MOE_BLOCK = 256

LOG2E = 1.4426950408889634
LANES = 128
TQ = 128
VMEM_LIMIT = 56 * 1024 * 1024


def _cparams(sem, vmem_limit=VMEM_LIMIT):
    return pltpu.CompilerParams(dimension_semantics=sem, vmem_limit_bytes=vmem_limit)


def _dot(a, b):
    return jnp.dot(a, b, preferred_element_type=F32)


def _dot_nt(a, b):
    return lax.dot_general(a, b, (((1,), (1,)), ((), ())), preferred_element_type=F32)


def _sigmoid(v):
    return 1.0 / (1.0 + jnp.exp(-v))


def _pack_halves(v):
    h = v.shape[1] // 2
    lo = lax.bitcast_convert_type(v[:, :h].astype(BF16).astype(F32), jnp.uint32)
    hi = lax.bitcast_convert_type(v[:, h:].astype(BF16).astype(F32), jnp.uint32)
    return (hi & jnp.uint32(0xFFFF0000)) | (lo >> 16)


def _unpack_halves(w):
    lo = lax.bitcast_convert_type(w << 16, F32)
    hi = lax.bitcast_convert_type(w & jnp.uint32(0xFFFF0000), F32)
    return lo, hi


def _layer_norm(z, g, b):
    mu = jnp.mean(z, axis=-1, keepdims=True)
    zc = z - mu
    var = jnp.mean(zc * zc, axis=-1, keepdims=True)
    return zc * lax.rsqrt(var + LN_EPS) * g + b


def _proj_kernel(tbl_ref, x_ref, *rest, epi, q_blocks, q_scale, shift):
    del tbl_ref
    if shift:
        w_ref, w2_ref, cos_ref, sin_ref, o_ref, *scratch = rest
    else:
        w_ref, cos_ref, sin_ref, o_ref, *scratch = rest
    if scratch:
        (wb_ref,) = scratch

        @pl.when(pl.program_id(1) == 0)
        def _():
            if shift:
                keep = wb_ref.shape[0] - shift
                wb_ref[:keep, :] = w_ref[shift:, :].astype(BF16)
                wb_ref[keep:, :] = w2_ref[:shift, :].astype(BF16)
            else:
                wb_ref[...] = w_ref[...].astype(BF16)

        w = wb_ref[...]
    else:
        w = w_ref[...]
    acc = _dot_nt(x_ref[...], w)
    if epi == "none":
        o_ref[...] = acc.astype(o_ref.dtype)
    elif epi == "sigmoid":
        o_ref[...] = _sigmoid(acc).astype(o_ref.dtype)
    else:
        cs = jnp.where(pl.program_id(0) < q_blocks, q_scale, 1.0)
        cos = cos_ref[...] * cs
        sin = sin_ref[...] * cs
        lane = lax.broadcasted_iota(jnp.int32, (1, LANES), 1)
        for c in range(acc.shape[1] // LANES):
            y = acc[:, c * LANES:(c + 1) * LANES]
            if epi == "rope128":
                rot = pltpu.roll(y, 64, 1)
            else:
                rot = jnp.where((lane % 64) < 32, pltpu.roll(y, 96, 1), pltpu.roll(y, 32, 1))
            o_ref[:, c * LANES:(c + 1) * LANES] = (y * cos + rot * sin).astype(o_ref.dtype)


PROJ_TN = 512


def _project(xb, w_t, blocks, cos, sin, epi, out_dtype, q_cols=0, q_scale=1.0, tm=1024, shift=0):
    n, d = xb.shape
    tn = PROJ_TN
    ncol = len(blocks) * tn
    tm = min(tm, n)
    cast = w_t.dtype != BF16
    assert n % tm == 0 and q_cols % tn == 0 and (max(blocks) + 1) * tn + shift <= w_t.shape[0]
    assert shift % 16 == 0 and (cast or not shift)
    w_specs = [pl.BlockSpec((tn, d), lambda j, i, tbl: (tbl[j], 0))]
    if shift:
        w_specs.append(pl.BlockSpec((tn, d), lambda j, i, tbl: (tbl[j] + 1, 0)))
    return pl.pallas_call(
        functools.partial(_proj_kernel, epi=epi, q_blocks=q_cols // tn, q_scale=q_scale, shift=shift),
        grid_spec=pltpu.PrefetchScalarGridSpec(
            num_scalar_prefetch=1,
            grid=(ncol // tn, n // tm),
            in_specs=[pl.BlockSpec((tm, d), lambda j, i, tbl: (i, 0)), *w_specs,
                      pl.BlockSpec((tm, LANES), lambda j, i, tbl: (i, 0)),
                      pl.BlockSpec((tm, LANES), lambda j, i, tbl: (i, 0))],
            out_specs=pl.BlockSpec((tm, tn), lambda j, i, tbl: (i, j)),
            scratch_shapes=[pltpu.VMEM((tn, d), BF16)] if cast else []),
        out_shape=jax.ShapeDtypeStruct((n, ncol), out_dtype),
        compiler_params=_cparams(("arbitrary", "arbitrary")),
        name="proj_" + epi,
    )(jnp.asarray(blocks, jnp.int32), xb, *([w_t] * len(w_specs)), cos, sin)


SWA_GROUPS = 4


def _swa_kernel(sink_ref, q_ref, kp_ref, kc_ref, vp_ref, vc_ref, ga_ref, o_ref, s_ref, p_ref):
    n = pl.program_id(1)
    lane = lax.broadcasted_iota(jnp.int32, (1, LANES), 1)
    lo = lane < A_HEAD_DIM
    nk = 2 * TQ
    qpos = n * TQ + lax.broadcasted_iota(jnp.int32, (TQ, nk), 0)
    kpos = (n - 1) * TQ + lax.broadcasted_iota(jnp.int32, (TQ, nk), 1)
    delta = qpos - kpos
    mask = (kpos >= 0) & (delta >= 0) & (delta < A_WINDOW)
    rep = A_HEADS // A_KV_HEADS
    npair = rep // 2
    zq = jnp.zeros((TQ, LANES), q_ref.dtype)
    for gg in range(SWA_GROUPS):
        g = pl.program_id(2) * SWA_GROUPS + gg
        gl = slice(gg * LANES, (gg + 1) * LANES)
        kk = jnp.concatenate([kp_ref[:, gl], kc_ref[:, gl]], axis=0)
        vv = jnp.concatenate([vp_ref[:, gl], vc_ref[:, gl]], axis=0)
        zv = jnp.zeros_like(vv)
        v_halves = (jnp.where(lo, vv, zv), jnp.where(lo, zv, vv))
        qcols = [slice((gg * npair + c) * LANES, (gg * npair + c + 1) * LANES) for c in range(npair)]
        q_lo = [jnp.where(lo, q_ref[:, c], zq) for c in qcols]
        q_hi = [jnp.where(lo, zq, q_ref[:, c]) for c in qcols]
        r0 = gg * rep * TQ
        s_ref[r0:r0 + rep * TQ, :] = _dot_nt(jnp.concatenate(q_lo + q_hi, axis=0), kk)
        for half in range(2):
            for pair in range(npair):
                sl = slice(r0 + (half * npair + pair) * TQ, r0 + (half * npair + pair + 1) * TQ)
                s = jnp.where(mask, s_ref[sl, :], NEG_INF)
                sk = sink_ref[g * rep + pair * 2 + half] * LOG2E
                m = jnp.maximum(jnp.max(s, axis=-1, keepdims=True), sk)
                e = jnp.exp2(s - m)
                p = e / (jnp.sum(e, axis=-1, keepdims=True) + jnp.exp2(sk - m))
                p_ref[sl, :] = p.astype(BF16)
        half_rows = npair * TQ
        out = (_dot(p_ref[r0:r0 + half_rows, :], v_halves[0])
               + _dot(p_ref[r0 + half_rows:r0 + 2 * half_rows, :], v_halves[1]))
        for pair, c in enumerate(qcols):
            o_ref[:, c] = ga_ref[:, c] * out[pair * TQ:(pair + 1) * TQ, :]


def _swa_attention(q, k, v, sinks, gates, bsz, seq):
    n = bsz * seq
    nb = seq // TQ
    rows = SWA_GROUPS * (A_HEADS // A_KV_HEADS) * TQ
    qw = SWA_GROUPS * (A_HEADS // A_KV_HEADS) * A_HEAD_DIM
    kw = SWA_GROUPS * LANES
    return pl.pallas_call(
        _swa_kernel,
        grid=(bsz, nb, A_KV_HEADS // SWA_GROUPS),
        in_specs=[pl.BlockSpec(memory_space=pltpu.SMEM),
                  pl.BlockSpec((TQ, qw), lambda b, i, g: (b * nb + i, g)),
                  pl.BlockSpec((TQ, kw), lambda b, i, g: (b * nb + jnp.maximum(i - 1, 0), g)),
                  pl.BlockSpec((TQ, kw), lambda b, i, g: (b * nb + i, g)),
                  pl.BlockSpec((TQ, kw), lambda b, i, g: (b * nb + jnp.maximum(i - 1, 0), g)),
                  pl.BlockSpec((TQ, kw), lambda b, i, g: (b * nb + i, g)),
                  pl.BlockSpec((TQ, qw), lambda b, i, g: (b * nb + i, g))],
        out_specs=pl.BlockSpec((TQ, qw), lambda b, i, g: (b * nb + i, g)),
        out_shape=jax.ShapeDtypeStruct((n, A_HEADS * A_HEAD_DIM), F32),
        scratch_shapes=[pltpu.VMEM((rows, 2 * TQ), F32), pltpu.VMEM((rows, 2 * TQ), BF16)],
        compiler_params=_cparams(("parallel", "parallel", "parallel")),
        name="swa",
    )(sinks, q, k, k, v, v, gates)


def _compress_kernel(t_ref, pe_ref, w1_ref, w2_ref, o_ref):
    nch = o_ref.shape[0]
    u = jnp.zeros((nch, CMP_HIDDEN), F32)
    v = jnp.zeros((nch, CMP_HIDDEN), F32)
    for l in range(CMP_STRIDE):
        t_l = t_ref[pl.ds(l, nch, stride=CMP_STRIDE), :]
        a = (t_l + pe_ref[l:l + 1, :]).astype(BF16)
        b = (t_l + pe_ref[CMP_STRIDE + l:CMP_STRIDE + l + 1, :]).astype(BF16)
        u = u + _dot(a, w1_ref[l * LANES:(l + 1) * LANES, :])
        v = v + _dot(b, w1_ref[(CMP_STRIDE + l) * LANES:(CMP_STRIDE + l + 1) * LANES, :])
    pre = u + pltpu.roll(v, nch - 1, 0)
    h = jax.nn.gelu(pre)
    o_ref[...] = _dot(h.astype(BF16), w2_ref[...]).astype(o_ref.dtype)


def _compress(t, col0, pe, w1, w2, bsz, seq):
    nch = seq // CMP_STRIDE
    cb0 = col0 // LANES
    return pl.pallas_call(
        _compress_kernel,
        grid=(bsz, B_KV_HEADS),
        in_specs=[pl.BlockSpec((seq, LANES), lambda b, g: (b, cb0 + g)),
                  pl.BlockSpec((CMP_LEN, LANES), lambda b, g: (0, 0)),
                  pl.BlockSpec((CMP_LEN * LANES, CMP_HIDDEN), lambda b, g: (0, 0)),
                  pl.BlockSpec((CMP_HIDDEN, LANES), lambda b, g: (0, 0))],
        out_specs=pl.BlockSpec((None, None, nch, LANES), lambda b, g: (b, g, 0, 0)),
        out_shape=jax.ShapeDtypeStruct((bsz, B_KV_HEADS, nch, LANES), BF16),
        compiler_params=_cparams(("parallel", "parallel")),
        name="nsa_compress",
    )(t, pe, w1, w2)


NSA_QT = 2
NSA_SCRATCH = 7
NSA_TK = 1024
NSA_NKW = (-(-(WIN_LEN - 1) // TQ) + 1) * TQ


def _nsa_kernel(q_ref, gate_ref, kcmp_ref, vcmp_ref, ks_ref, vs_ref, kw_ref, vw_ref, ya_ref, gb_ref, o_ref,
                *scratch, seq):
    qs, sc, pr, mx, ls, acc, ocmp = (scratch[k::NSA_SCRATCH] for k in range(NSA_SCRATCH))
    step = pl.program_id(2)
    rep = B_HEADS // B_KV_HEADS
    nch = seq // CMP_STRIDE
    ns = seq // SLC_LEN
    n_sel = min(SLC_TOPK, ns)
    heads = [slice(r * TQ, (r + 1) * TQ) for r in range(rep)]
    tiles = range(NSA_QT)
    trows = [slice(h * TQ, (h + 1) * TQ) for h in tiles]
    qis = [step * NSA_QT + h for h in tiles]
    tpos = [qis[h] * TQ + lax.broadcasted_iota(jnp.int32, (TQ, 1), 0) for h in tiles]
    for h in tiles:
        for r in range(rep):
            qs[h][heads[r], :] = q_ref[trows[h], r * LANES:(r + 1) * LANES]

    c_end = lax.broadcasted_iota(jnp.int32, (1, nch), 1) * CMP_STRIDE + (CMP_LEN - 1)
    jrow = lax.broadcasted_iota(jnp.int32, (ns, 1), 0)
    d = lax.broadcasted_iota(jnp.int32, (ns, nch), 1) - (SLC_LEN // CMP_STRIDE) * jrow
    msel = jnp.where((d == -1) | (d == 3), 1.0, jnp.where((d >= 0) & (d <= 2), 2.0, 0.0)).astype(BF16)
    sels = []
    for h in tiles:
        sc[h][:, :nch] = _dot_nt(qs[h][...], kcmp_ref[...])
        valid = c_end <= tpos[h]
        p_grp = jnp.zeros((TQ, nch), F32)
        for sl in heads:
            s = jnp.where(valid, sc[h][sl, :nch], NEG_INF)
            m = jnp.max(s, axis=-1, keepdims=True)
            e = jnp.where(valid, jnp.exp2(s - m), 0.0)
            den = jnp.sum(e, axis=-1, keepdims=True)
            p = e / jnp.where(den > 0, den, 1.0)
            p_grp = p_grp + p
            pr[h][sl, :nch] = p.astype(BF16)
        ocmp[h][...] = _dot(pr[h][:, :nch], vcmp_ref[...])

        p_hi = p_grp.astype(BF16)
        r1 = p_grp - p_hi.astype(F32)
        p_mid = r1.astype(BF16)
        p_lo = (r1 - p_mid.astype(F32)).astype(BF16)
        p_slc = _dot_nt(msel, p_hi) + _dot_nt(msel, p_mid) + _dot_nt(msel, p_lo)

        tpos_l = qis[h] * TQ + lax.broadcasted_iota(jnp.int32, (1, TQ), 1)
        cur = tpos_l // SLC_LEN
        forced = (jrow == 0) | (jrow == cur) | (jrow == cur - 1)
        score = jnp.where(forced, 1e9, jnp.where(jrow * SLC_LEN <= tpos_l, p_slc, NEG_INF))
        rank = jnp.zeros((ns, TQ), jnp.int32)
        for i in range(ns):
            si = score[i:i + 1, :]
            rank = rank + ((si > score) | ((si == score) & (jrow > i))).astype(jnp.int32)
        sel_t = (rank < n_sel).astype(F32)
        if ns < LANES:
            sel_t = jnp.concatenate([sel_t, jnp.zeros((LANES - ns, TQ), F32)], axis=0)
        sels.append(sel_t.T.astype(BF16))

    for h in tiles:
        mx[h][...] = jnp.full_like(mx[h], NEG_INF)
        ls[h][...] = jnp.zeros_like(ls[h])
        acc[h][...] = jnp.zeros_like(acc[h])
    jexp = lax.broadcasted_iota(jnp.int32, (LANES, 1), 0)

    def slc_body(kt, c):
        row0 = pl.multiple_of(kt * NSA_TK, NSA_TK)
        k_t = ks_ref[pl.ds(row0, NSA_TK), :]
        v_t = vs_ref[pl.ds(row0, NSA_TK), :]
        kpos = kt * NSA_TK + lax.broadcasted_iota(jnp.int32, (1, NSA_TK), 1)
        expand = (jexp == kpos // SLC_LEN).astype(BF16)
        for h in tiles:
            sc[h][:, :NSA_TK] = _dot_nt(qs[h][...], k_t)
        for h in tiles:
            keep = (_dot(sels[h], expand) > 0.5) & (kpos <= tpos[h])
            for sl in heads:
                s = jnp.where(keep, sc[h][sl, :NSA_TK], NEG_INF)
                m_old = mx[h][sl, :]
                m_new = jnp.maximum(m_old, jnp.max(s, axis=-1, keepdims=True))
                alpha = jnp.exp2(m_old - m_new)
                e = jnp.exp2(s - m_new)
                ls[h][sl, :] = alpha * ls[h][sl, :] + jnp.sum(e, axis=-1, keepdims=True)
                mx[h][sl, :] = m_new
                acc[h][sl, :] = alpha * acc[h][sl, :]
                pr[h][sl, :NSA_TK] = e.astype(BF16)
            acc[h][...] += _dot(pr[h][:, :NSA_TK], v_t)
        return c

    lax.fori_loop(0, (qis[-1] * TQ + TQ + NSA_TK - 1) // NSA_TK, slc_body, 0)

    nprev = NSA_NKW // TQ - 1
    gate_all = gate_ref[...]
    for h in tiles:
        k_parts, v_parts = [], []
        for i in range(nprev + 1):
            row0 = pl.multiple_of(jnp.maximum(qis[h] - nprev + i, 0) * TQ, TQ)
            k_parts.append(kw_ref[pl.ds(row0, TQ), :])
            v_parts.append(vw_ref[pl.ds(row0, TQ), :])
        sc[h][:, :NSA_NKW] = _dot_nt(qs[h][...], jnp.concatenate(k_parts, axis=0))
        kposw = (qis[h] - nprev) * TQ + lax.broadcasted_iota(jnp.int32, (1, NSA_NKW), 1)
        dlt = tpos[h] - kposw
        keepw = (kposw >= 0) & (dlt >= 0) & (dlt < WIN_LEN)
        for sl in heads:
            s = jnp.where(keepw, sc[h][sl, :NSA_NKW], NEG_INF)
            e = jnp.exp2(s - jnp.max(s, axis=-1, keepdims=True))
            pr[h][sl, :NSA_NKW] = (e / jnp.sum(e, axis=-1, keepdims=True)).astype(BF16)
        o_win = _dot(pr[h][:, :NSA_NKW], jnp.concatenate(v_parts, axis=0))

        gate = gate_all[trows[h], :]
        for r, sl in enumerate(heads):
            cols = slice(r * LANES, (r + 1) * LANES)
            o_b = (gate[:, r:r + 1] * ocmp[h][sl, :]
                   + gate[:, rep + r:rep + r + 1] * (acc[h][sl, :] / ls[h][sl, :])
                   + gate[:, 2 * rep + r:2 * rep + r + 1] * o_win[sl])
            o_ref[trows[h], cols] = (ya_ref[trows[h], cols] + gb_ref[trows[h], cols] * o_b).astype(o_ref.dtype)


def _nsa_attention(qk, v, gnx, gates, ya, kcmp, vcmp, bsz, seq, cols):
    n = bsz * seq
    tq = NSA_QT * TQ
    nq = seq // tq
    nch = seq // CMP_STRIDE
    rep = B_HEADS // B_KV_HEADS
    qb0 = cols["q"] // (rep * LANES)
    gbb = cols["g_b"] // (rep * LANES)
    ksb, kwb, vsb, vwb, gb0 = (cols[k] // LANES for k in ("ks", "kw", "vs", "vw", "gate"))
    rows, wide = rep * TQ, max(NSA_NKW, NSA_TK, nch)
    return pl.pallas_call(
        functools.partial(_nsa_kernel, seq=seq),
        grid=(bsz, B_KV_HEADS, nq),
        in_specs=[pl.BlockSpec((tq, rep * LANES), lambda b, g, i: (b * nq + i, qb0 + g)),
                  pl.BlockSpec((tq, LANES), lambda b, g, i: (b * nq + i, gb0 + g)),
                  pl.BlockSpec((None, None, nch, LANES), lambda b, g, i: (b, g, 0, 0)),
                  pl.BlockSpec((None, None, nch, LANES), lambda b, g, i: (b, g, 0, 0)),
                  pl.BlockSpec((seq, LANES), lambda b, g, i: (b, ksb + g)),
                  pl.BlockSpec((seq, LANES), lambda b, g, i: (b, vsb + g)),
                  pl.BlockSpec((seq, LANES), lambda b, g, i: (b, kwb + g)),
                  pl.BlockSpec((seq, LANES), lambda b, g, i: (b, vwb + g)),
                  pl.BlockSpec((tq, rep * LANES), lambda b, g, i: (b * nq + i, g)),
                  pl.BlockSpec((tq, rep * LANES), lambda b, g, i: (b * nq + i, gbb + g))],
        out_specs=pl.BlockSpec((tq, rep * LANES), lambda b, g, i: (b * nq + i, g)),
        out_shape=jax.ShapeDtypeStruct((n, B_HEADS * B_HEAD_DIM), BF16),
        scratch_shapes=[pltpu.VMEM((rows, LANES), BF16),
                        pltpu.VMEM((rows, wide), F32),
                        pltpu.VMEM((rows, wide), BF16),
                        pltpu.VMEM((rows, 1), F32), pltpu.VMEM((rows, 1), F32),
                        pltpu.VMEM((rows, LANES), F32),
                        pltpu.VMEM((rows, LANES), F32)] * NSA_QT,
        compiler_params=_cparams(("parallel", "parallel", "arbitrary")),
        name="nsa",
    )(qk, gnx, kcmp, vcmp, qk, v, qk, v, ya, gates)


def _oproj_kernel(y_ref, wo_ref, x_ref, g_ref, b_ref, wr_ref, x1_ref, x1p_ref, sc_ref, z_ref, *, alpha):
    j = pl.program_id(1)
    z_ref[j] = _dot(y_ref[...], wo_ref[...])

    @pl.when(j == pl.num_programs(1) - 1)
    def _():
        nj, tm, _ = z_ref.shape
        piece = min(128, tm)
        for r0 in range(0, tm, piece):
            rows = slice(r0, r0 + piece)
            z = jnp.concatenate([z_ref[jj, rows, :] for jj in range(nj)], axis=1)
            x1 = _layer_norm(alpha * x_ref[rows, :] + z, g_ref[...], b_ref[...])
            x1_ref[rows, :] = x1
            x1p_ref[rows, :] = _pack_halves(x1)
            sc_ref[rows, :] = _sigmoid(_dot(x1.astype(BF16), wr_ref[...]))


def _out_proj(y, w_o, x, ln_g, ln_b, w_r, alpha, tm=512, tn=512):
    n, d = x.shape
    tm = min(tm, n)
    once = dict(pipeline_mode=pl.Buffered(1))
    return pl.pallas_call(
        functools.partial(_oproj_kernel, alpha=alpha),
        grid=(n // tm, d // tn),
        in_specs=[pl.BlockSpec((tm, d), lambda i, j: (i, 0)),
                  pl.BlockSpec((d, tn), lambda i, j: (0, j)),
                  pl.BlockSpec((tm, d), lambda i, j: (i, 0), **once),
                  pl.BlockSpec((1, d), lambda i, j: (0, 0)),
                  pl.BlockSpec((1, d), lambda i, j: (0, 0)),
                  pl.BlockSpec((d, LANES), lambda i, j: (0, 0))],
        out_specs=[pl.BlockSpec((tm, d), lambda i, j: (i, 0), **once),
                   pl.BlockSpec((tm, d // 2), lambda i, j: (i, 0), **once),
                   pl.BlockSpec((tm, LANES), lambda i, j: (i, 0))],
        out_shape=[jax.ShapeDtypeStruct((n, d), F32), jax.ShapeDtypeStruct((n, d // 2), jnp.uint32),
                   jax.ShapeDtypeStruct((n, LANES), F32)],
        scratch_shapes=[pltpu.VMEM((d // tn, tm, tn), F32)],
        compiler_params=_cparams(("parallel", "arbitrary")),
        name="oproj_ln1_router",
    )(y, w_o, x, ln_g, ln_b, w_r)


def _route_kernel(sc_ref, bias_ref, ek_ref, rk_ref, gk_ref, cnt_ref, run_ref):
    i = pl.program_id(0)
    tm = sc_ref.shape[0]

    @pl.when(i == 0)
    def _():
        run_ref[...] = jnp.zeros_like(run_ref)

    sc = sc_ref[...]
    lane = lax.broadcasted_iota(jnp.int32, (1, LANES), 1)
    is_e = lane < N_EXPERTS
    biased = jnp.where(is_e, sc + bias_ref[...], -jnp.inf)
    rank = jnp.zeros((tm, LANES), jnp.int32)
    for e in range(N_EXPERTS):
        be = biased[:, e:e + 1]
        rank = rank + ((be > biased) | ((be == biased) & (e < lane))).astype(jnp.int32)
    sel = (rank < TOP_K) & is_e
    sv = jnp.where(sel, sc, 0.0)
    gates = sv / jnp.sum(sv, axis=-1, keepdims=True) * ROUTED_SCALE
    selb = sel.astype(BF16)
    ri = lax.broadcasted_iota(jnp.int32, (tm, tm), 0)
    ci = lax.broadcasted_iota(jnp.int32, (tm, tm), 1)
    cum = _dot((ci < ri).astype(BF16), selb) + run_ref[...]
    run_ref[...] += jnp.sum(sel.astype(F32), axis=0, keepdims=True)
    cnt_ref[...] = run_ref[...]
    ai = lax.broadcasted_iota(jnp.int32, (LANES, LANES), 0)
    bi = lax.broadcasted_iota(jnp.int32, (LANES, LANES), 1)
    before = _dot(selb, (ai < bi).astype(BF16))
    lane_f = lane.astype(F32)
    ek = jnp.zeros((tm, LANES), F32)
    rk = jnp.zeros((tm, LANES), F32)
    gk = jnp.zeros((tm, LANES), F32)
    for k in range(TOP_K):
        oh = sel & (before == k)
        ek = jnp.where(lane == k, jnp.sum(jnp.where(oh, lane_f, 0.0), axis=-1, keepdims=True), ek)
        rk = jnp.where(lane == k, jnp.sum(jnp.where(oh, cum, 0.0), axis=-1, keepdims=True), rk)
        gk = jnp.where(lane == k, jnp.sum(jnp.where(oh, gates, 0.0), axis=-1, keepdims=True), gk)
    ek_ref[...] = ek.astype(jnp.int32)
    rk_ref[...] = rk.astype(jnp.int32)
    gk_ref[...] = gk


def _route(scores, bias, tm=256):
    n = scores.shape[0]
    tm = min(tm, n)
    blk = pl.BlockSpec((tm, LANES), lambda i: (i, 0))
    one = pl.BlockSpec((1, LANES), lambda i: (0, 0))
    return pl.pallas_call(
        _route_kernel,
        grid=(n // tm,),
        in_specs=[blk, one],
        out_specs=[blk, blk, blk, one],
        out_shape=[jax.ShapeDtypeStruct((n, LANES), jnp.int32), jax.ShapeDtypeStruct((n, LANES), jnp.int32),
                   jax.ShapeDtypeStruct((n, LANES), F32), jax.ShapeDtypeStruct((1, LANES), F32)],
        scratch_shapes=[pltpu.VMEM((1, LANES), F32)],
        compiler_params=_cparams(("arbitrary",)),
        name="route",
    )(scores, bias)


def _row_copy(src, s_row, dst, d_row, sem):
    return pltpu.make_async_copy(src.at[pl.ds(s_row, 1)], dst.at[pl.ds(d_row, 1)], sem)


def _scatter_kernel(cnt_ref, pad_ref, pst_ref, nu_ref, dest_ref, x_ref, xs_ref, zrow_ref, zblk_ref, sem,
                    *, tm, n_tok_steps, n_blocks):
    i = pl.program_id(0)

    @pl.when(i < n_tok_steps)
    def _():
        def issue(t, c):
            for k in range(TOP_K):
                _row_copy(x_ref, t, xs_ref, dest_ref[t * TOP_K + k], sem).start(priority=k % 2)
            return c

        lax.fori_loop(0, tm, issue, 0)

        for k in range(TOP_K):
            pltpu.make_async_copy(x_ref, xs_ref.at[pl.ds(0, tm)], sem).wait()

    @pl.when((i >= n_tok_steps) & (i < n_tok_steps + N_EXPERTS))
    def _():
        e = i - n_tok_steps
        zrow_ref[...] = jnp.zeros_like(zrow_ref)
        lo, hi, base = cnt_ref[e], pad_ref[e], pst_ref[e]

        def issue(r, c):
            pltpu.make_async_copy(zrow_ref, xs_ref.at[pl.ds(base + r, 1)], sem).start()
            return c

        lax.fori_loop(lo, hi, issue, 0)

        def drain(r, c):
            pltpu.make_async_copy(zrow_ref, xs_ref.at[pl.ds(0, 1)], sem).wait()
            return c

        lax.fori_loop(lo, hi, drain, 0)

    @pl.when(i == n_tok_steps + N_EXPERTS)
    def _():
        zblk_ref[...] = jnp.zeros_like(zblk_ref)

        def issue(p, c):
            pltpu.make_async_copy(zblk_ref, xs_ref.at[pl.ds(p * MOE_BLOCK, MOE_BLOCK)], sem).start()
            return c

        lax.fori_loop(nu_ref[0], n_blocks, issue, 0)

        def drain(p, c):
            pltpu.make_async_copy(zblk_ref, xs_ref.at[pl.ds(0, MOE_BLOCK)], sem).wait()
            return c

        lax.fori_loop(nu_ref[0], n_blocks, drain, 0)


def _scatter_rows(x1, dest_flat, counts, padded, pstarts, n_used, n_blocks, tm=128):
    n, d = x1.shape
    tm = min(tm, n)
    n_tok_steps = n // tm
    return pl.pallas_call(
        functools.partial(_scatter_kernel, tm=tm, n_tok_steps=n_tok_steps, n_blocks=n_blocks),
        grid_spec=pltpu.PrefetchScalarGridSpec(
            num_scalar_prefetch=4,
            grid=(n_tok_steps + N_EXPERTS + 1,),
            in_specs=[pl.BlockSpec((tm * TOP_K,), lambda i, *_: (jnp.minimum(i, n_tok_steps - 1),),
                                   memory_space=pltpu.SMEM),
                      pl.BlockSpec((tm, d), lambda i, *_: (jnp.minimum(i, n_tok_steps - 1), 0))],
            out_specs=pl.BlockSpec(memory_space=pl.ANY),
            scratch_shapes=[pltpu.VMEM((1, d), x1.dtype), pltpu.VMEM((MOE_BLOCK, d), x1.dtype),
                            pltpu.SemaphoreType.DMA(())]),
        out_shape=jax.ShapeDtypeStruct((n_blocks * MOE_BLOCK, d), x1.dtype),
        compiler_params=_cparams(("arbitrary",)),
        name="moe_scatter",
    )(counts, padded, pstarts, n_used, dest_flat, x1)


def _expert_kernel(be_ref, first_ref, nxt_ref, nu_ref, xs_ref, wg_hbm, wu_hbm, wd_hbm, y_ref,
                   sg_ref, su_ref, sd_ref, wg_ref, wu_ref, wd_ref, sem):
    p = pl.program_id(0)
    used = p < nu_ref[0]

    def copies(e):
        return (pltpu.make_async_copy(wg_hbm.at[e], sg_ref, sem.at[0]),
                pltpu.make_async_copy(wu_hbm.at[e], su_ref, sem.at[1]),
                pltpu.make_async_copy(wd_hbm.at[e], sd_ref, sem.at[2]))

    @pl.when(p == 0)
    def _():
        for c in copies(be_ref[0]):
            c.start()

    @pl.when(used & (first_ref[p] == 1))
    def _():
        for c in copies(be_ref[p]):
            c.wait()
        wg_ref[...] = sg_ref[...].astype(BF16)
        wu_ref[...] = su_ref[...].astype(BF16)
        wd_ref[...] = sd_ref[...].astype(BF16)

        @pl.when(nxt_ref[p] >= 0)
        def _():
            for c in copies(nxt_ref[p]):
                c.start()

    @pl.when(used)
    def _():
        lo, hi = _unpack_halves(xs_ref[...])
        lo, hi = lo.astype(BF16), hi.astype(BF16)
        dh = lo.shape[1]
        hg = _dot(lo, wg_ref[:dh, :]) + _dot(hi, wg_ref[dh:, :])
        hu = _dot(lo, wu_ref[:dh, :]) + _dot(hi, wu_ref[dh:, :])
        h = (hg * _sigmoid(hg)) * hu
        y_ref[...] = _pack_halves(_dot(h.astype(BF16), wd_ref[...]))

    @pl.when(jnp.logical_not(used))
    def _():
        y_ref[...] = jnp.zeros_like(y_ref)


def _expert_mlp(xs, block_e, first, nxt, n_used, wg, wu, wd):
    n_rows, dp = xs.shape
    nblk = n_rows // MOE_BLOCK
    d, ff = wg.shape[-2:]
    assert d == 2 * dp

    def row_map(p, be, fi, nx, nu):
        return (jnp.minimum(p, nu[0] - 1), 0)

    hbm = pl.BlockSpec(memory_space=pl.ANY)
    return pl.pallas_call(
        _expert_kernel,
        grid_spec=pltpu.PrefetchScalarGridSpec(
            num_scalar_prefetch=4,
            grid=(nblk,),
            in_specs=[pl.BlockSpec((MOE_BLOCK, dp), row_map), hbm, hbm, hbm],
            out_specs=pl.BlockSpec((MOE_BLOCK, dp), lambda p, *_: (p, 0)),
            scratch_shapes=[pltpu.VMEM((d, ff), wg.dtype), pltpu.VMEM((d, ff), wu.dtype),
                            pltpu.VMEM((ff, d), wd.dtype),
                            pltpu.VMEM((d, ff), BF16), pltpu.VMEM((d, ff), BF16), pltpu.VMEM((ff, d), BF16),
                            pltpu.SemaphoreType.DMA((3,))]),
        out_shape=jax.ShapeDtypeStruct((n_rows, dp), jnp.uint32),
        compiler_params=_cparams(("arbitrary",), VMEM_LIMIT + 2 * 1024 * 1024),
        name="moe_experts",
    )(block_e, first, nxt, n_used, xs, wg, wu, wd)


COMBINE_PARTS = 4


def _combine_kernel(dest_ref, x1_ref, gk_ref, sg_ref, su_ref, sd_ref, g_ref, b_ref, ys_ref,
                    o_ref, ybuf_ref, sh_ref, sem, *, tm, alpha):
    tp = tm // COMBINE_PARTS

    def issue(part):
        def body(t, c):
            for k in range(TOP_K):
                _row_copy(ys_ref, dest_ref[t * TOP_K + k], ybuf_ref, k * tm + t,
                          sem.at[part]).start(priority=k % 2)
            return c

        lax.fori_loop(part * tp, (part + 1) * tp, body, 0)

    issue(0)
    issue(1)
    xb = x1_ref[...].astype(BF16)
    hg = _dot(xb, sg_ref[...])
    hu = _dot(xb, su_ref[...])
    sh_ref[...] = _dot(((hg * _sigmoid(hg)) * hu).astype(BF16), sd_ref[...])

    for part in range(COMBINE_PARTS):
        rows = slice(part * tp, (part + 1) * tp)
        for k in range(TOP_K):
            pltpu.make_async_copy(ys_ref.at[pl.ds(0, tp)], ybuf_ref.at[pl.ds(0, tp)], sem.at[part]).wait()
        if part + 2 < COMBINE_PARTS:
            issue(part + 2)
        gk = gk_ref[rows, :]
        r_lo = jnp.zeros((tp, ybuf_ref.shape[1]), F32)
        r_hi = jnp.zeros((tp, ybuf_ref.shape[1]), F32)
        for k in range(TOP_K):
            lo, hi = _unpack_halves(ybuf_ref[k * tm + part * tp:k * tm + (part + 1) * tp, :])
            r_lo = r_lo + gk[:, k:k + 1] * lo
            r_hi = r_hi + gk[:, k:k + 1] * hi
        routed = jnp.concatenate([r_lo, r_hi], axis=1)
        o_ref[rows, :] = _layer_norm(alpha * x1_ref[rows, :] + (routed + sh_ref[rows, :]), g_ref[...], b_ref[...])


def _combine(x1, dest_flat, gk, ys, sg, su, sd, ln_g, ln_b, alpha, tm=128):
    n, d = x1.shape
    tm = min(tm, n)
    ff = sg.shape[-1]
    return pl.pallas_call(
        functools.partial(_combine_kernel, tm=tm, alpha=alpha),
        grid=(n // tm,),
        in_specs=[pl.BlockSpec((tm * TOP_K,), lambda i: (i,), memory_space=pltpu.SMEM),
                  pl.BlockSpec((tm, d), lambda i: (i, 0)),
                  pl.BlockSpec((tm, LANES), lambda i: (i, 0)),
                  pl.BlockSpec((d, ff), lambda i: (0, 0), pipeline_mode=pl.Buffered(1)),
                  pl.BlockSpec((d, ff), lambda i: (0, 0), pipeline_mode=pl.Buffered(1)),
                  pl.BlockSpec((ff, d), lambda i: (0, 0), pipeline_mode=pl.Buffered(1)),
                  pl.BlockSpec((1, d), lambda i: (0, 0)),
                  pl.BlockSpec((1, d), lambda i: (0, 0)),
                  pl.BlockSpec(memory_space=pl.ANY)],
        out_specs=pl.BlockSpec((tm, d), lambda i: (i, 0)),
        out_shape=jax.ShapeDtypeStruct((n, d), F32),
        scratch_shapes=[pltpu.VMEM((TOP_K * tm, d // 2), jnp.uint32), pltpu.VMEM((tm, d), F32),
                        pltpu.SemaphoreType.DMA((COMBINE_PARTS,))],
        compiler_params=_cparams(("arbitrary",)),
        name="moe_combine_ln2",
    )(dest_flat, x1, gk, sg, su, sd, ln_g, ln_b, ys)


def _rope_kernel(pos_ref, inv_ref, sign_ref, cos_ref, sin_ref):
    ang = pos_ref[...].astype(F32) * inv_ref[...]
    cos_ref[...] = jnp.cos(ang)
    sin_ref[...] = jnp.sin(ang) * sign_ref[...]


def _rope_tables(positions, dh, tm=512):
    half = dh // 2
    inv = ROPE_THETA ** (-jnp.arange(half, dtype=F32) * 2.0 / dh)
    reps = LANES // dh
    inv_l = jnp.tile(jnp.concatenate([inv, inv]), reps).reshape(1, LANES)
    sign_l = jnp.tile(jnp.concatenate([-jnp.ones((half,), F32), jnp.ones((half,), F32)]), reps).reshape(1, LANES)
    pos = positions.reshape(-1, 1)
    n = pos.shape[0]
    tm = min(tm, n)
    one = pl.BlockSpec((1, LANES), lambda i: (0, 0))
    blk = pl.BlockSpec((tm, LANES), lambda i: (i, 0))
    return pl.pallas_call(
        _rope_kernel,
        grid=(n // tm,),
        in_specs=[pl.BlockSpec((tm, 1), lambda i: (i, 0)), one, one],
        out_specs=[blk, blk],
        out_shape=[jax.ShapeDtypeStruct((n, LANES), F32)] * 2,
        compiler_params=_cparams(("parallel",)),
        name="rope_tables",
    )(pos, inv_l, sign_l)


def _dup_heads(w_rows, heads, dh):
    d = w_rows.shape[1]
    w3 = w_rows.reshape(heads, dh, d)
    return jnp.concatenate([w3, w3], axis=1).reshape(heads * 2 * dh, d)


def _hybrid_layer(x, positions, w_in, a_sinks, cmp_k_pos, cmp_k_w1, cmp_k_w2, cmp_v_pos, cmp_v_w1,
                  cmp_v_w2, w_o, ln1_g, ln1_b, w_router, router_bias, exp_w_gate, exp_w_up, exp_w_down,
                  sh_w_gate, sh_w_up, sh_w_down, ln2_g, ln2_b, alpha):
    bsz, seq, d = x.shape
    n = bsz * seq
    assert seq % NSA_TK == 0 and (seq // CMP_STRIDE) % LANES == 0 and A_WINDOW <= TQ and seq % (NSA_QT * TQ) == 0
    assert 3 <= seq // SLC_LEN <= LANES
    xf = x.reshape(n, d)
    xb = xf.astype(BF16)

    qa_w, kva = A_HEADS * A_HEAD_DIM, A_KV_HEADS * A_HEAD_DIM
    qb_w, kvb = B_HEADS * B_HEAD_DIM, B_KV_HEADS * B_HEAD_DIM
    widths = [qa_w, kva, kva, qb_w, kvb, kvb, kvb, kvb, kvb, kvb, 3 * B_HEADS, d, d]
    offs = np.concatenate([[0], np.cumsum(widths)]).tolist()
    (o_qa, o_ka, o_va, o_qb, o_kc, o_vc, o_ks, o_vs, o_kw, o_vw, o_gn, o_ga, o_end) = (
        offs[i] for i in (0, 1, 2, 3, 4, 5, 6, 7, 8, 9, 10, 11, 13))
    w_t = w_in.T
    blk = lambda start, width: list(range(start // PROJ_TN, (start + width) // PROJ_TN))
    assert all(o % PROJ_TN == 0 for o in (o_qa, o_qb, o_kc, o_vc, o_ks, o_vs, o_kw, o_vw))
    rep = B_HEADS // B_KV_HEADS
    w_gn = w_t[o_gn:o_ga].reshape(3, B_KV_HEADS, rep, d).transpose(1, 0, 2, 3).reshape(B_KV_HEADS, 3 * rep, d)
    w_gn_x = jnp.pad(w_gn, ((0, 0), (0, LANES - 3 * rep), (0, 0))).reshape(B_KV_HEADS * LANES, d)
    w_gn_x = w_gn_x.astype(BF16)
    assert o_ga % PROJ_TN == 3 * B_HEADS and o_end == w_t.shape[0]
    w_ka2 = _dup_heads(w_t[o_ka:o_va], A_KV_HEADS, A_HEAD_DIM).astype(BF16)
    w_va2 = _dup_heads(w_t[o_va:o_qb], A_KV_HEADS, A_HEAD_DIM).astype(BF16)
    all_blocks = lambda w: list(range(w.shape[0] // PROJ_TN))

    cos64, sin64 = _rope_tables(positions, A_HEAD_DIM)
    cos128, sin128 = _rope_tables(positions, B_HEAD_DIM)

    q_a = _project(xb, w_t, blk(o_qa, qa_w), cos64, sin64, "rope64", BF16,
                   q_cols=qa_w, q_scale=A_HEAD_DIM ** -0.5 * LOG2E)
    k_a = _project(xb, w_ka2, all_blocks(w_ka2), cos64, sin64, "rope64", BF16)
    v_a = _project(xb, w_va2, all_blocks(w_va2), cos64, sin64, "none", BF16)
    qk_b = _project(xb, w_t, blk(o_qb, qb_w) + blk(o_ks, kvb) + blk(o_kw, kvb), cos128, sin128, "rope128", BF16,
                    q_cols=qb_w, q_scale=B_HEAD_DIM ** -0.5 * LOG2E)
    v_b = _project(xb, w_t, blk(o_vs, kvb) + blk(o_vw, kvb), cos128, sin128, "none", BF16)
    kc = _project(xb, w_t, blk(o_kc, kvb), cos128, sin128, "rope128", F32)
    vc = _project(xb, w_t, blk(o_vc, kvb), cos128, sin128, "none", F32)
    gates = _project(xb, w_t, blk(o_ga - 3 * B_HEADS, 2 * d), cos128, sin128, "sigmoid", F32,
                     tm=512, shift=3 * B_HEADS)
    gnx = _project(xb, w_gn_x, all_blocks(w_gn_x), cos128, sin128, "sigmoid", F32)

    ya = _swa_attention(q_a, k_a, v_a, a_sinks, gates, bsz, seq)

    kcmp = _compress(kc, 0, cmp_k_pos, cmp_k_w1.astype(BF16), cmp_k_w2.astype(BF16), bsz, seq)
    vcmp = _compress(vc, 0, cmp_v_pos, cmp_v_w1.astype(BF16), cmp_v_w2.astype(BF16), bsz, seq)
    nsa_cols = dict(q=0, ks=qb_w, kw=qb_w + kvb, vs=0, vw=kvb, g_b=d, gate=0)
    y = _nsa_attention(qk_b, v_b, gnx, gates, ya, kcmp, vcmp, bsz, seq, nsa_cols)

    w_r = jnp.pad(w_router, ((0, 0), (0, LANES - N_EXPERTS))).astype(BF16)
    x1, x1p, scores = _out_proj(y, w_o.astype(BF16), xf, ln1_g.reshape(1, d), ln1_b.reshape(1, d), w_r, alpha)

    bias = jnp.pad(router_bias.astype(F32), (0, LANES - N_EXPERTS)).reshape(1, LANES)
    ek, rk, gk, cnt = _route(scores, bias)

    counts = cnt[0, :N_EXPERTS].astype(jnp.int32)
    padded = (counts + MOE_BLOCK - 1) // MOE_BLOCK * MOE_BLOCK
    pends = jnp.cumsum(padded)
    pstarts = pends - padded
    nblk = -(-(n * TOP_K) // MOE_BLOCK) + N_EXPERTS
    blk_start = jnp.arange(nblk, dtype=jnp.int32) * MOE_BLOCK
    block_e = jnp.minimum(jnp.sum((pends[None, :] <= blk_start[:, None]).astype(jnp.int32), axis=1),
                          N_EXPERTS - 1)
    n_used = (pends[-1:] // MOE_BLOCK).astype(jnp.int32)
    hit = ek[:, :TOP_K, None] == jnp.arange(N_EXPERTS, dtype=jnp.int32)
    dest = (jnp.sum(jnp.where(hit, pstarts.astype(jnp.int32), 0), axis=-1) + rk[:, :TOP_K]).reshape(-1)

    xs = _scatter_rows(x1p, dest, counts, padded, pstarts.astype(jnp.int32), n_used, nblk)
    first = jnp.concatenate([jnp.ones((1,), jnp.int32), (block_e[1:] != block_e[:-1]).astype(jnp.int32)])
    end_blk = jnp.take(pends // MOE_BLOCK, block_e).astype(jnp.int32)
    nxt = jnp.where(end_blk < n_used[0], jnp.take(block_e, jnp.minimum(end_blk, nblk - 1)), -1).astype(jnp.int32)
    ys = _expert_mlp(xs, block_e, first, nxt, n_used, exp_w_gate, exp_w_up, exp_w_down)
    out = _combine(x1, dest, gk, ys, sh_w_gate.astype(BF16), sh_w_up.astype(BF16), sh_w_down.astype(BF16),
                   ln2_g.reshape(1, d), ln2_b.reshape(1, d), alpha)
    return out.reshape(bsz, seq, d)


def kernel(x, positions, w_in, a_sinks, cmp_k_pos, cmp_k_w1, cmp_k_w2, cmp_v_pos, cmp_v_w1, cmp_v_w2,
           w_o, ln1_g, ln1_b, w_router, router_bias, exp_w_gate, exp_w_up, exp_w_down,
           sh_w_gate, sh_w_up, sh_w_down, ln2_g, ln2_b):
    depth = w_in.shape[0]
    alpha = (2.0 * depth) ** 0.25
    for l in range(depth):
        x = _hybrid_layer(x, positions, w_in[l], a_sinks[l], cmp_k_pos[l], cmp_k_w1[l], cmp_k_w2[l],
                          cmp_v_pos[l], cmp_v_w1[l], cmp_v_w2[l], w_o[l], ln1_g[l], ln1_b[l],
                          w_router[l], router_bias[l], exp_w_gate[l], exp_w_up[l], exp_w_down[l],
                          sh_w_gate[l], sh_w_up[l], sh_w_down[l], ln2_g[l], ln2_b[l], alpha)
    return x
```

```python
import functools

import jax
import jax.numpy as jnp
import numpy as np
from jax import lax
from jax.experimental import pallas as pl
from jax.experimental.pallas import tpu as pltpu

F32 = jnp.float32
BF16 = jnp.bfloat16

ROPE_THETA = 10000.0
LN_EPS = 1e-5
NEG_INF = -1e30

A_HEADS, A_KV_HEADS, A_HEAD_DIM, A_WINDOW = 64, 8, 64, 128
B_HEADS, B_KV_HEADS, B_HEAD_DIM = 32, 4, 128
CMP_LEN, CMP_STRIDE, CMP_HIDDEN = 32, 16, 256
SLC_LEN, SLC_TOPK, WIN_LEN = 64, 16, 512
N_EXPERTS, TOP_K, EXPERT_FF, SHARED_FF = 64, 8, 512, 512
ROUTED_SCALE = 2.5
MOE_BLOCK = 256

LOG2E = 1.4426950408889634
LANES = 128
TQ = 128
VMEM_LIMIT = 56 * 1024 * 1024


def _cparams(sem, vmem_limit=VMEM_LIMIT):
    return pltpu.CompilerParams(dimension_semantics=sem, vmem_limit_bytes=vmem_limit)


def _dot(a, b):
    return jnp.dot(a, b, preferred_element_type=F32)


def _dot_nt(a, b):
    return lax.dot_general(a, b, (((1,), (1,)), ((), ())), preferred_element_type=F32)


def _sigmoid(v):
    return 1.0 / (1.0 + jnp.exp(-v))


def _pack_halves(v):
    h = v.shape[1] // 2
    lo = lax.bitcast_convert_type(v[:, :h].astype(BF16).astype(F32), jnp.uint32)
    hi = lax.bitcast_convert_type(v[:, h:].astype(BF16).astype(F32), jnp.uint32)
    return (hi & jnp.uint32(0xFFFF0000)) | (lo >> 16)


def _unpack_halves(w):
    lo = lax.bitcast_convert_type(w << 16, F32)
    hi = lax.bitcast_convert_type(w & jnp.uint32(0xFFFF0000), F32)
    return lo, hi


def _layer_norm(z, g, b):
    mu = jnp.mean(z, axis=-1, keepdims=True)
    zc = z - mu
    var = jnp.mean(zc * zc, axis=-1, keepdims=True)
    return zc * lax.rsqrt(var + LN_EPS) * g + b


def _proj_kernel(tbl_ref, x_ref, *rest, epi, q_blocks, q_scale, shift):
    del tbl_ref
    if shift:
        w_ref, w2_ref, cos_ref, sin_ref, o_ref, *scratch = rest
    else:
        w_ref, cos_ref, sin_ref, o_ref, *scratch = rest
    if scratch:
        (wb_ref,) = scratch

        @pl.when(pl.program_id(1) == 0)
        def _():
            if shift:
                keep = wb_ref.shape[0] - shift
                wb_ref[:keep, :] = w_ref[shift:, :].astype(BF16)
                wb_ref[keep:, :] = w2_ref[:shift, :].astype(BF16)
            else:
                wb_ref[...] = w_ref[...].astype(BF16)

        w = wb_ref[...]
    else:
        w = w_ref[...]
    acc = _dot_nt(x_ref[...], w)
    if epi == "none":
        o_ref[...] = acc.astype(o_ref.dtype)
    elif epi == "sigmoid":
        o_ref[...] = _sigmoid(acc).astype(o_ref.dtype)
    else:
        cs = jnp.where(pl.program_id(0) < q_blocks, q_scale, 1.0)
        cos = cos_ref[...] * cs
        sin = sin_ref[...] * cs
        lane = lax.broadcasted_iota(jnp.int32, (1, LANES), 1)
        for c in range(acc.shape[1] // LANES):
            y = acc[:, c * LANES:(c + 1) * LANES]
            if epi == "rope128":
                rot = pltpu.roll(y, 64, 1)
            else:
                rot = jnp.where((lane % 64) < 32, pltpu.roll(y, 96, 1), pltpu.roll(y, 32, 1))
            o_ref[:, c * LANES:(c + 1) * LANES] = (y * cos + rot * sin).astype(o_ref.dtype)


PROJ_TN = 512


def _project(xb, w_t, blocks, cos, sin, epi, out_dtype, q_cols=0, q_scale=1.0, tm=1024, shift=0):
    n, d = xb.shape
    tn = PROJ_TN
    ncol = len(blocks) * tn
    tm = min(tm, n)
    cast = w_t.dtype != BF16
    assert n % tm == 0 and q_cols % tn == 0 and (max(blocks) + 1) * tn + shift <= w_t.shape[0]
    assert shift % 16 == 0 and (cast or not shift)
    w_specs = [pl.BlockSpec((tn, d), lambda j, i, tbl: (tbl[j], 0))]
    if shift:
        w_specs.append(pl.BlockSpec((tn, d), lambda j, i, tbl: (tbl[j] + 1, 0)))
    return pl.pallas_call(
        functools.partial(_proj_kernel, epi=epi, q_blocks=q_cols // tn, q_scale=q_scale, shift=shift),
        grid_spec=pltpu.PrefetchScalarGridSpec(
            num_scalar_prefetch=1,
            grid=(ncol // tn, n // tm),
            in_specs=[pl.BlockSpec((tm, d), lambda j, i, tbl: (i, 0)), *w_specs,
                      pl.BlockSpec((tm, LANES), lambda j, i, tbl: (i, 0)),
                      pl.BlockSpec((tm, LANES), lambda j, i, tbl: (i, 0))],
            out_specs=pl.BlockSpec((tm, tn), lambda j, i, tbl: (i, j)),
            scratch_shapes=[pltpu.VMEM((tn, d), BF16)] if cast else []),
        out_shape=jax.ShapeDtypeStruct((n, ncol), out_dtype),
        compiler_params=_cparams(("arbitrary", "arbitrary")),
        name="proj_" + epi,
    )(jnp.asarray(blocks, jnp.int32), xb, *([w_t] * len(w_specs)), cos, sin)


SWA_GROUPS = 8


def _swa_kernel(sink_ref, q_ref, kp_ref, kc_ref, vp_ref, vc_ref, ga_ref, o_ref, s_ref, p_ref):
    n = pl.program_id(1)
    lane = lax.broadcasted_iota(jnp.int32, (1, LANES), 1)
    lo = lane < A_HEAD_DIM
    nk = 2 * TQ
    qpos = n * TQ + lax.broadcasted_iota(jnp.int32, (TQ, nk), 0)
    kpos = (n - 1) * TQ + lax.broadcasted_iota(jnp.int32, (TQ, nk), 1)
    delta = qpos - kpos
    mask = (kpos >= 0) & (delta >= 0) & (delta < A_WINDOW)
    rep = A_HEADS // A_KV_HEADS
    npair = rep // 2
    zq = jnp.zeros((TQ, LANES), q_ref.dtype)
    for gg in range(SWA_GROUPS):
        g = pl.program_id(2) * SWA_GROUPS + gg
        gl = slice(gg * LANES, (gg + 1) * LANES)
        kk = jnp.concatenate([kp_ref[:, gl], kc_ref[:, gl]], axis=0)
        vv = jnp.concatenate([vp_ref[:, gl], vc_ref[:, gl]], axis=0)
        zv = jnp.zeros_like(vv)
        v_halves = (jnp.where(lo, vv, zv), jnp.where(lo, zv, vv))
        qcols = [slice((gg * npair + c) * LANES, (gg * npair + c + 1) * LANES) for c in range(npair)]
        q_lo = [jnp.where(lo, q_ref[:, c], zq) for c in qcols]
        q_hi = [jnp.where(lo, zq, q_ref[:, c]) for c in qcols]
        r0 = gg * rep * TQ
        s_ref[r0:r0 + rep * TQ, :] = _dot_nt(jnp.concatenate(q_lo + q_hi, axis=0), kk)
        for half in range(2):
            for pair in range(npair):
                sl = slice(r0 + (half * npair + pair) * TQ, r0 + (half * npair + pair + 1) * TQ)
                s = jnp.where(mask, s_ref[sl, :], NEG_INF)
                sk = sink_ref[g * rep + pair * 2 + half] * LOG2E
                m = jnp.maximum(jnp.max(s, axis=-1, keepdims=True), sk)
                e = jnp.exp2(s - m)
                p = e / (jnp.sum(e, axis=-1, keepdims=True) + jnp.exp2(sk - m))
                p_ref[sl, :] = p.astype(BF16)
        half_rows = npair * TQ
        out = (_dot(p_ref[r0:r0 + half_rows, :], v_halves[0])
               + _dot(p_ref[r0 + half_rows:r0 + 2 * half_rows, :], v_halves[1]))
        for pair, c in enumerate(qcols):
            o_ref[:, c] = ga_ref[:, c] * out[pair * TQ:(pair + 1) * TQ, :]


def _swa_attention(q, k, v, sinks, gates, bsz, seq):
    n = bsz * seq
    nb = seq // TQ
    rows = SWA_GROUPS * (A_HEADS // A_KV_HEADS) * TQ
    qw = SWA_GROUPS * (A_HEADS // A_KV_HEADS) * A_HEAD_DIM
    kw = SWA_GROUPS * LANES
    return pl.pallas_call(
        _swa_kernel,
        grid=(bsz, nb, A_KV_HEADS // SWA_GROUPS),
        in_specs=[pl.BlockSpec(memory_space=pltpu.SMEM),
                  pl.BlockSpec((TQ, qw), lambda b, i, g: (b * nb + i, g)),
                  pl.BlockSpec((TQ, kw), lambda b, i, g: (b * nb + jnp.maximum(i - 1, 0), g)),
                  pl.BlockSpec((TQ, kw), lambda b, i, g: (b * nb + i, g)),
                  pl.BlockSpec((TQ, kw), lambda b, i, g: (b * nb + jnp.maximum(i - 1, 0), g)),
                  pl.BlockSpec((TQ, kw), lambda b, i, g: (b * nb + i, g)),
                  pl.BlockSpec((TQ, qw), lambda b, i, g: (b * nb + i, g))],
        out_specs=pl.BlockSpec((TQ, qw), lambda b, i, g: (b * nb + i, g)),
        out_shape=jax.ShapeDtypeStruct((n, A_HEADS * A_HEAD_DIM), F32),
        scratch_shapes=[pltpu.VMEM((rows, 2 * TQ), F32), pltpu.VMEM((rows, 2 * TQ), BF16)],
        compiler_params=_cparams(("parallel", "parallel", "parallel")),
        name="swa",
    )(sinks, q, k, k, v, v, gates)


def _compress_kernel(t_ref, pe_ref, w1_ref, w2_ref, o_ref):
    nch = o_ref.shape[0]
    u = jnp.zeros((nch, CMP_HIDDEN), F32)
    v = jnp.zeros((nch, CMP_HIDDEN), F32)
    for l in range(CMP_STRIDE):
        t_l = t_ref[pl.ds(l, nch, stride=CMP_STRIDE), :]
        a = (t_l + pe_ref[l:l + 1, :]).astype(BF16)
        b = (t_l + pe_ref[CMP_STRIDE + l:CMP_STRIDE + l + 1, :]).astype(BF16)
        u = u + _dot(a, w1_ref[l * LANES:(l + 1) * LANES, :])
        v = v + _dot(b, w1_ref[(CMP_STRIDE + l) * LANES:(CMP_STRIDE + l + 1) * LANES, :])
    pre = u + pltpu.roll(v, nch - 1, 0)
    h = jax.nn.gelu(pre)
    o_ref[...] = _dot(h.astype(BF16), w2_ref[...]).astype(o_ref.dtype)


def _compress(t, col0, pe, w1, w2, bsz, seq):
    nch = seq // CMP_STRIDE
    cb0 = col0 // LANES
    return pl.pallas_call(
        _compress_kernel,
        grid=(bsz, B_KV_HEADS),
        in_specs=[pl.BlockSpec((seq, LANES), lambda b, g: (b, cb0 + g)),
                  pl.BlockSpec((CMP_LEN, LANES), lambda b, g: (0, 0)),
                  pl.BlockSpec((CMP_LEN * LANES, CMP_HIDDEN), lambda b, g: (0, 0)),
                  pl.BlockSpec((CMP_HIDDEN, LANES), lambda b, g: (0, 0))],
        out_specs=pl.BlockSpec((None, None, nch, LANES), lambda b, g: (b, g, 0, 0)),
        out_shape=jax.ShapeDtypeStruct((bsz, B_KV_HEADS, nch, LANES), BF16),
        compiler_params=_cparams(("parallel", "parallel")),
        name="nsa_compress",
    )(t, pe, w1, w2)


NSA_QT = 2
NSA_SCRATCH = 7
NSA_TK = 1024
NSA_NKW = (-(-(WIN_LEN - 1) // TQ) + 1) * TQ


def _nsa_kernel(q_ref, gate_ref, kcmp_ref, vcmp_ref, ks_ref, vs_ref, kw_ref, vw_ref, ya_ref, gb_ref, o_ref,
                *scratch, seq):
    qs, sc, pr, mx, ls, acc, ocmp = (scratch[k::NSA_SCRATCH] for k in range(NSA_SCRATCH))
    step = pl.program_id(2)
    rep = B_HEADS // B_KV_HEADS
    nch = seq // CMP_STRIDE
    ns = seq // SLC_LEN
    n_sel = min(SLC_TOPK, ns)
    heads = [slice(r * TQ, (r + 1) * TQ) for r in range(rep)]
    tiles = range(NSA_QT)
    trows = [slice(h * TQ, (h + 1) * TQ) for h in tiles]
    qis = [step * NSA_QT + h for h in tiles]
    tpos = [qis[h] * TQ + lax.broadcasted_iota(jnp.int32, (TQ, 1), 0) for h in tiles]
    for h in tiles:
        for r in range(rep):
            qs[h][heads[r], :] = q_ref[trows[h], r * LANES:(r + 1) * LANES]

    c_end = lax.broadcasted_iota(jnp.int32, (1, nch), 1) * CMP_STRIDE + (CMP_LEN - 1)
    jrow = lax.broadcasted_iota(jnp.int32, (ns, 1), 0)
    d = lax.broadcasted_iota(jnp.int32, (ns, nch), 1) - (SLC_LEN // CMP_STRIDE) * jrow
    msel = jnp.where((d == -1) | (d == 3), 1.0, jnp.where((d >= 0) & (d <= 2), 2.0, 0.0)).astype(BF16)
    sels = []
    for h in tiles:
        sc[h][:, :nch] = _dot_nt(qs[h][...], kcmp_ref[...])
        valid = c_end <= tpos[h]
        p_grp = jnp.zeros((TQ, nch), F32)
        for sl in heads:
            s = jnp.where(valid, sc[h][sl, :nch], NEG_INF)
            m = jnp.max(s, axis=-1, keepdims=True)
            e = jnp.where(valid, jnp.exp2(s - m), 0.0)
            den = jnp.sum(e, axis=-1, keepdims=True)
            p = e / jnp.where(den > 0, den, 1.0)
            p_grp = p_grp + p
            pr[h][sl, :nch] = p.astype(BF16)
        ocmp[h][...] = _dot(pr[h][:, :nch], vcmp_ref[...])

        p_hi = p_grp.astype(BF16)
        r1 = p_grp - p_hi.astype(F32)
        p_mid = r1.astype(BF16)
        p_lo = (r1 - p_mid.astype(F32)).astype(BF16)
        p_slc = _dot_nt(msel, p_hi) + _dot_nt(msel, p_mid) + _dot_nt(msel, p_lo)

        tpos_l = qis[h] * TQ + lax.broadcasted_iota(jnp.int32, (1, TQ), 1)
        cur = tpos_l // SLC_LEN
        forced = (jrow == 0) | (jrow == cur) | (jrow == cur - 1)
        score = jnp.where(forced, 1e9, jnp.where(jrow * SLC_LEN <= tpos_l, p_slc, NEG_INF))
        rank = jnp.zeros((ns, TQ), jnp.int32)
        for i in range(ns):
            si = score[i:i + 1, :]
            rank = rank + ((si > score) | ((si == score) & (jrow > i))).astype(jnp.int32)
        sel_t = (rank < n_sel).astype(F32)
        if ns < LANES:
            sel_t = jnp.concatenate([sel_t, jnp.zeros((LANES - ns, TQ), F32)], axis=0)
        sels.append(sel_t.T.astype(BF16))

    for h in tiles:
        mx[h][...] = jnp.full_like(mx[h], NEG_INF)
        ls[h][...] = jnp.zeros_like(ls[h])
        acc[h][...] = jnp.zeros_like(acc[h])
    jexp = lax.broadcasted_iota(jnp.int32, (LANES, 1), 0)

    def slc_body(kt, c):
        row0 = pl.multiple_of(kt * NSA_TK, NSA_TK)
        k_t = ks_ref[pl.ds(row0, NSA_TK), :]
        v_t = vs_ref[pl.ds(row0, NSA_TK), :]
        kpos = kt * NSA_TK + lax.broadcasted_iota(jnp.int32, (1, NSA_TK), 1)
        expand = (jexp == kpos // SLC_LEN).astype(BF16)
        for h in tiles:
            sc[h][:, :NSA_TK] = _dot_nt(qs[h][...], k_t)
        for h in tiles:
            keep = (_dot(sels[h], expand) > 0.5) & (kpos <= tpos[h])
            for sl in heads:
                s = jnp.where(keep, sc[h][sl, :NSA_TK], NEG_INF)
                m_old = mx[h][sl, :]
                m_new = jnp.maximum(m_old, jnp.max(s, axis=-1, keepdims=True))
                alpha = jnp.exp2(m_old - m_new)
                e = jnp.exp2(s - m_new)
                ls[h][sl, :] = alpha * ls[h][sl, :] + jnp.sum(e, axis=-1, keepdims=True)
                mx[h][sl, :] = m_new
                acc[h][sl, :] = alpha * acc[h][sl, :]
                pr[h][sl, :NSA_TK] = e.astype(BF16)
            acc[h][...] += _dot(pr[h][:, :NSA_TK], v_t)
        return c

    lax.fori_loop(0, (qis[-1] * TQ + TQ + NSA_TK - 1) // NSA_TK, slc_body, 0)

    nprev = NSA_NKW // TQ - 1
    gate_all = gate_ref[...]
    for h in tiles:
        k_parts, v_parts = [], []
        for i in range(nprev + 1):
            row0 = pl.multiple_of(jnp.maximum(qis[h] - nprev + i, 0) * TQ, TQ)
            k_parts.append(kw_ref[pl.ds(row0, TQ), :])
            v_parts.append(vw_ref[pl.ds(row0, TQ), :])
        sc[h][:, :NSA_NKW] = _dot_nt(qs[h][...], jnp.concatenate(k_parts, axis=0))
        kposw = (qis[h] - nprev) * TQ + lax.broadcasted_iota(jnp.int32, (1, NSA_NKW), 1)
        dlt = tpos[h] - kposw
        keepw = (kposw >= 0) & (dlt >= 0) & (dlt < WIN_LEN)
        for sl in heads:
            s = jnp.where(keepw, sc[h][sl, :NSA_NKW], NEG_INF)
            e = jnp.exp2(s - jnp.max(s, axis=-1, keepdims=True))
            pr[h][sl, :NSA_NKW] = (e / jnp.sum(e, axis=-1, keepdims=True)).astype(BF16)
        o_win = _dot(pr[h][:, :NSA_NKW], jnp.concatenate(v_parts, axis=0))

        gate = gate_all[trows[h], :]
        for r, sl in enumerate(heads):
            cols = slice(r * LANES, (r + 1) * LANES)
            o_b = (gate[:, r:r + 1] * ocmp[h][sl, :]
                   + gate[:, rep + r:rep + r + 1] * (acc[h][sl, :] / ls[h][sl, :])
                   + gate[:, 2 * rep + r:2 * rep + r + 1] * o_win[sl])
            o_ref[trows[h], cols] = (ya_ref[trows[h], cols] + gb_ref[trows[h], cols] * o_b).astype(o_ref.dtype)


def _nsa_attention(qk, v, gnx, gates, ya, kcmp, vcmp, bsz, seq, cols):
    n = bsz * seq
    tq = NSA_QT * TQ
    nq = seq // tq
    nch = seq // CMP_STRIDE
    rep = B_HEADS // B_KV_HEADS
    qb0 = cols["q"] // (rep * LANES)
    gbb = cols["g_b"] // (rep * LANES)
    ksb, kwb, vsb, vwb, gb0 = (cols[k] // LANES for k in ("ks", "kw", "vs", "vw", "gate"))
    rows, wide = rep * TQ, max(NSA_NKW, NSA_TK, nch)
    return pl.pallas_call(
        functools.partial(_nsa_kernel, seq=seq),
        grid=(bsz, B_KV_HEADS, nq),
        in_specs=[pl.BlockSpec((tq, rep * LANES), lambda b, g, i: (b * nq + i, qb0 + g)),
                  pl.BlockSpec((tq, LANES), lambda b, g, i: (b * nq + i, gb0 + g)),
                  pl.BlockSpec((None, None, nch, LANES), lambda b, g, i: (b, g, 0, 0)),
                  pl.BlockSpec((None, None, nch, LANES), lambda b, g, i: (b, g, 0, 0)),
                  pl.BlockSpec((seq, LANES), lambda b, g, i: (b, ksb + g)),
                  pl.BlockSpec((seq, LANES), lambda b, g, i: (b, vsb + g)),
                  pl.BlockSpec((seq, LANES), lambda b, g, i: (b, kwb + g)),
                  pl.BlockSpec((seq, LANES), lambda b, g, i: (b, vwb + g)),
                  pl.BlockSpec((tq, rep * LANES), lambda b, g, i: (b * nq + i, g)),
                  pl.BlockSpec((tq, rep * LANES), lambda b, g, i: (b * nq + i, gbb + g))],
        out_specs=pl.BlockSpec((tq, rep * LANES), lambda b, g, i: (b * nq + i, g)),
        out_shape=jax.ShapeDtypeStruct((n, B_HEADS * B_HEAD_DIM), BF16),
        scratch_shapes=[pltpu.VMEM((rows, LANES), BF16),
                        pltpu.VMEM((rows, wide), F32),
                        pltpu.VMEM((rows, wide), BF16),
                        pltpu.VMEM((rows, 1), F32), pltpu.VMEM((rows, 1), F32),
                        pltpu.VMEM((rows, LANES), F32),
                        pltpu.VMEM((rows, LANES), F32)] * NSA_QT,
        compiler_params=_cparams(("parallel", "parallel", "arbitrary")),
        name="nsa",
    )(qk, gnx, kcmp, vcmp, qk, v, qk, v, ya, gates)


def _oproj_kernel(y_ref, wo_ref, x_ref, g_ref, b_ref, wr_ref, x1_ref, x1p_ref, sc_ref, z_ref, *, alpha):
    j = pl.program_id(1)
    z_ref[j] = _dot(y_ref[...], wo_ref[...])

    @pl.when(j == pl.num_programs(1) - 1)
    def _():
        nj, tm, _ = z_ref.shape
        piece = min(128, tm)
        for r0 in range(0, tm, piece):
            rows = slice(r0, r0 + piece)
            z = jnp.concatenate([z_ref[jj, rows, :] for jj in range(nj)], axis=1)
            x1 = _layer_norm(alpha * x_ref[rows, :] + z, g_ref[...], b_ref[...])
            x1_ref[rows, :] = x1
            x1p_ref[rows, :] = _pack_halves(x1)
            sc_ref[rows, :] = _sigmoid(_dot(x1.astype(BF16), wr_ref[...]))


def _out_proj(y, w_o, x, ln_g, ln_b, w_r, alpha, tm=512, tn=512):
    n, d = x.shape
    tm = min(tm, n)
    once = dict(pipeline_mode=pl.Buffered(1))
    return pl.pallas_call(
        functools.partial(_oproj_kernel, alpha=alpha),
        grid=(n // tm, d // tn),
        in_specs=[pl.BlockSpec((tm, d), lambda i, j: (i, 0)),
                  pl.BlockSpec((d, tn), lambda i, j: (0, j)),
                  pl.BlockSpec((tm, d), lambda i, j: (i, 0), **once),
                  pl.BlockSpec((1, d), lambda i, j: (0, 0)),
                  pl.BlockSpec((1, d), lambda i, j: (0, 0)),
                  pl.BlockSpec((d, LANES), lambda i, j: (0, 0))],
        out_specs=[pl.BlockSpec((tm, d), lambda i, j: (i, 0), **once),
                   pl.BlockSpec((tm, d // 2), lambda i, j: (i, 0), **once),
                   pl.BlockSpec((tm, LANES), lambda i, j: (i, 0))],
        out_shape=[jax.ShapeDtypeStruct((n, d), F32), jax.ShapeDtypeStruct((n, d // 2), jnp.uint32),
                   jax.ShapeDtypeStruct((n, LANES), F32)],
        scratch_shapes=[pltpu.VMEM((d // tn, tm, tn), F32)],
        compiler_params=_cparams(("parallel", "arbitrary")),
        name="oproj_ln1_router",
    )(y, w_o, x, ln_g, ln_b, w_r)


def _route_kernel(sc_ref, bias_ref, ek_ref, rk_ref, gk_ref, cnt_ref, run_ref):
    i = pl.program_id(0)
    tm = sc_ref.shape[0]

    @pl.when(i == 0)
    def _():
        run_ref[...] = jnp.zeros_like(run_ref)

    sc = sc_ref[...]
    lane = lax.broadcasted_iota(jnp.int32, (1, LANES), 1)
    is_e = lane < N_EXPERTS
    biased = jnp.where(is_e, sc + bias_ref[...], -jnp.inf)
    rank = jnp.zeros((tm, LANES), jnp.int32)
    for e in range(N_EXPERTS):
        be = biased[:, e:e + 1]
        rank = rank + ((be > biased) | ((be == biased) & (e < lane))).astype(jnp.int32)
    sel = (rank < TOP_K) & is_e
    sv = jnp.where(sel, sc, 0.0)
    gates = sv / jnp.sum(sv, axis=-1, keepdims=True) * ROUTED_SCALE
    selb = sel.astype(BF16)
    ri = lax.broadcasted_iota(jnp.int32, (tm, tm), 0)
    ci = lax.broadcasted_iota(jnp.int32, (tm, tm), 1)
    cum = _dot((ci < ri).astype(BF16), selb) + run_ref[...]
    run_ref[...] += jnp.sum(sel.astype(F32), axis=0, keepdims=True)
    cnt_ref[...] = run_ref[...]
    ai = lax.broadcasted_iota(jnp.int32, (LANES, LANES), 0)
    bi = lax.broadcasted_iota(jnp.int32, (LANES, LANES), 1)
    before = _dot(selb, (ai < bi).astype(BF16))
    lane_f = lane.astype(F32)
    ek = jnp.zeros((tm, LANES), F32)
    rk = jnp.zeros((tm, LANES), F32)
    gk = jnp.zeros((tm, LANES), F32)
    for k in range(TOP_K):
        oh = sel & (before == k)
        ek = jnp.where(lane == k, jnp.sum(jnp.where(oh, lane_f, 0.0), axis=-1, keepdims=True), ek)
        rk = jnp.where(lane == k, jnp.sum(jnp.where(oh, cum, 0.0), axis=-1, keepdims=True), rk)
        gk = jnp.where(lane == k, jnp.sum(jnp.where(oh, gates, 0.0), axis=-1, keepdims=True), gk)
    ek_ref[...] = ek.astype(jnp.int32)
    rk_ref[...] = rk.astype(jnp.int32)
    gk_ref[...] = gk


def _route(scores, bias, tm=256):
    n = scores.shape[0]
    tm = min(tm, n)
    blk = pl.BlockSpec((tm, LANES), lambda i: (i, 0))
    one = pl.BlockSpec((1, LANES), lambda i: (0, 0))
    return pl.pallas_call(
        _route_kernel,
        grid=(n // tm,),
        in_specs=[blk, one],
        out_specs=[blk, blk, blk, one],
        out_shape=[jax.ShapeDtypeStruct((n, LANES), jnp.int32), jax.ShapeDtypeStruct((n, LANES), jnp.int32),
                   jax.ShapeDtypeStruct((n, LANES), F32), jax.ShapeDtypeStruct((1, LANES), F32)],
        scratch_shapes=[pltpu.VMEM((1, LANES), F32)],
        compiler_params=_cparams(("arbitrary",)),
        name="route",
    )(scores, bias)


def _row_copy(src, s_row, dst, d_row, sem):
    return pltpu.make_async_copy(src.at[pl.ds(s_row, 1)], dst.at[pl.ds(d_row, 1)], sem)


def _scatter_kernel(cnt_ref, pad_ref, pst_ref, nu_ref, dest_ref, x_ref, xs_ref, zrow_ref, zblk_ref, sem,
                    *, tm, n_tok_steps, n_blocks):
    i = pl.program_id(0)

    @pl.when(i < n_tok_steps)
    def _():
        def issue(t, c):
            for k in range(TOP_K):
                _row_copy(x_ref, t, xs_ref, dest_ref[t * TOP_K + k], sem).start(priority=k % 2)
            return c

        lax.fori_loop(0, tm, issue, 0)

        for k in range(TOP_K):
            pltpu.make_async_copy(x_ref, xs_ref.at[pl.ds(0, tm)], sem).wait()

    @pl.when((i >= n_tok_steps) & (i < n_tok_steps + N_EXPERTS))
    def _():
        e = i - n_tok_steps
        zrow_ref[...] = jnp.zeros_like(zrow_ref)
        lo, hi, base = cnt_ref[e], pad_ref[e], pst_ref[e]

        def issue(r, c):
            pltpu.make_async_copy(zrow_ref, xs_ref.at[pl.ds(base + r, 1)], sem).start()
            return c

        lax.fori_loop(lo, hi, issue, 0)

        def drain(r, c):
            pltpu.make_async_copy(zrow_ref, xs_ref.at[pl.ds(0, 1)], sem).wait()
            return c

        lax.fori_loop(lo, hi, drain, 0)

    @pl.when(i == n_tok_steps + N_EXPERTS)
    def _():
        zblk_ref[...] = jnp.zeros_like(zblk_ref)

        def issue(p, c):
            pltpu.make_async_copy(zblk_ref, xs_ref.at[pl.ds(p * MOE_BLOCK, MOE_BLOCK)], sem).start()
            return c

        lax.fori_loop(nu_ref[0], n_blocks, issue, 0)

        def drain(p, c):
            pltpu.make_async_copy(zblk_ref, xs_ref.at[pl.ds(0, MOE_BLOCK)], sem).wait()
            return c

        lax.fori_loop(nu_ref[0], n_blocks, drain, 0)


def _scatter_rows(x1, dest_flat, counts, padded, pstarts, n_used, n_blocks, tm=128):
    n, d = x1.shape
    tm = min(tm, n)
    n_tok_steps = n // tm
    return pl.pallas_call(
        functools.partial(_scatter_kernel, tm=tm, n_tok_steps=n_tok_steps, n_blocks=n_blocks),
        grid_spec=pltpu.PrefetchScalarGridSpec(
            num_scalar_prefetch=4,
            grid=(n_tok_steps + N_EXPERTS + 1,),
            in_specs=[pl.BlockSpec((tm * TOP_K,), lambda i, *_: (jnp.minimum(i, n_tok_steps - 1),),
                                   memory_space=pltpu.SMEM),
                      pl.BlockSpec((tm, d), lambda i, *_: (jnp.minimum(i, n_tok_steps - 1), 0))],
            out_specs=pl.BlockSpec(memory_space=pl.ANY),
            scratch_shapes=[pltpu.VMEM((1, d), x1.dtype), pltpu.VMEM((MOE_BLOCK, d), x1.dtype),
                            pltpu.SemaphoreType.DMA(())]),
        out_shape=jax.ShapeDtypeStruct((n_blocks * MOE_BLOCK, d), x1.dtype),
        compiler_params=_cparams(("arbitrary",)),
        name="moe_scatter",
    )(counts, padded, pstarts, n_used, dest_flat, x1)


def _expert_kernel(be_ref, first_ref, nxt_ref, nu_ref, xs_ref, wg_hbm, wu_hbm, wd_hbm, y_ref,
                   sg_ref, su_ref, sd_ref, wg_ref, wu_ref, wd_ref, sem):
    p = pl.program_id(0)
    used = p < nu_ref[0]

    def copies(e):
        return (pltpu.make_async_copy(wg_hbm.at[e], sg_ref, sem.at[0]),
                pltpu.make_async_copy(wu_hbm.at[e], su_ref, sem.at[1]),
                pltpu.make_async_copy(wd_hbm.at[e], sd_ref, sem.at[2]))

    @pl.when(p == 0)
    def _():
        for c in copies(be_ref[0]):
            c.start()

    @pl.when(used & (first_ref[p] == 1))
    def _():
        for c in copies(be_ref[p]):
            c.wait()
        wg_ref[...] = sg_ref[...].astype(BF16)
        wu_ref[...] = su_ref[...].astype(BF16)
        wd_ref[...] = sd_ref[...].astype(BF16)

        @pl.when(nxt_ref[p] >= 0)
        def _():
            for c in copies(nxt_ref[p]):
                c.start()

    @pl.when(used)
    def _():
        lo, hi = _unpack_halves(xs_ref[...])
        lo, hi = lo.astype(BF16), hi.astype(BF16)
        dh = lo.shape[1]
        hg = _dot(lo, wg_ref[:dh, :]) + _dot(hi, wg_ref[dh:, :])
        hu = _dot(lo, wu_ref[:dh, :]) + _dot(hi, wu_ref[dh:, :])
        h = (hg * _sigmoid(hg)) * hu
        y_ref[...] = _pack_halves(_dot(h.astype(BF16), wd_ref[...]))

    @pl.when(jnp.logical_not(used))
    def _():
        y_ref[...] = jnp.zeros_like(y_ref)


def _expert_mlp(xs, block_e, first, nxt, n_used, wg, wu, wd):
    n_rows, dp = xs.shape
    nblk = n_rows // MOE_BLOCK
    d, ff = wg.shape[-2:]
    assert d == 2 * dp

    def row_map(p, be, fi, nx, nu):
        return (jnp.minimum(p, nu[0] - 1), 0)

    hbm = pl.BlockSpec(memory_space=pl.ANY)
    return pl.pallas_call(
        _expert_kernel,
        grid_spec=pltpu.PrefetchScalarGridSpec(
            num_scalar_prefetch=4,
            grid=(nblk,),
            in_specs=[pl.BlockSpec((MOE_BLOCK, dp), row_map), hbm, hbm, hbm],
            out_specs=pl.BlockSpec((MOE_BLOCK, dp), lambda p, *_: (p, 0)),
            scratch_shapes=[pltpu.VMEM((d, ff), wg.dtype), pltpu.VMEM((d, ff), wu.dtype),
                            pltpu.VMEM((ff, d), wd.dtype),
                            pltpu.VMEM((d, ff), BF16), pltpu.VMEM((d, ff), BF16), pltpu.VMEM((ff, d), BF16),
                            pltpu.SemaphoreType.DMA((3,))]),
        out_shape=jax.ShapeDtypeStruct((n_rows, dp), jnp.uint32),
        compiler_params=_cparams(("arbitrary",), VMEM_LIMIT + 2 * 1024 * 1024),
        name="moe_experts",
    )(block_e, first, nxt, n_used, xs, wg, wu, wd)


COMBINE_PARTS = 4


def _combine_kernel(dest_ref, x1_ref, gk_ref, sg_ref, su_ref, sd_ref, g_ref, b_ref, ys_ref,
                    o_ref, ybuf_ref, sh_ref, sem, *, tm, alpha):
    tp = tm // COMBINE_PARTS

    def issue(part):
        def body(t, c):
            for k in range(TOP_K):
                _row_copy(ys_ref, dest_ref[t * TOP_K + k], ybuf_ref, k * tm + t,
                          sem.at[part]).start(priority=k % 2)
            return c

        lax.fori_loop(part * tp, (part + 1) * tp, body, 0)

    issue(0)
    issue(1)
    xb = x1_ref[...].astype(BF16)
    hg = _dot(xb, sg_ref[...])
    hu = _dot(xb, su_ref[...])
    sh_ref[...] = _dot(((hg * _sigmoid(hg)) * hu).astype(BF16), sd_ref[...])

    for part in range(COMBINE_PARTS):
        rows = slice(part * tp, (part + 1) * tp)
        for k in range(TOP_K):
            pltpu.make_async_copy(ys_ref.at[pl.ds(0, tp)], ybuf_ref.at[pl.ds(0, tp)], sem.at[part]).wait()
        if part + 2 < COMBINE_PARTS:
            issue(part + 2)
        gk = gk_ref[rows, :]
        r_lo = jnp.zeros((tp, ybuf_ref.shape[1]), F32)
        r_hi = jnp.zeros((tp, ybuf_ref.shape[1]), F32)
        for k in range(TOP_K):
            lo, hi = _unpack_halves(ybuf_ref[k * tm + part * tp:k * tm + (part + 1) * tp, :])
            r_lo = r_lo + gk[:, k:k + 1] * lo
            r_hi = r_hi + gk[:, k:k + 1] * hi
        routed = jnp.concatenate([r_lo, r_hi], axis=1)
        o_ref[rows, :] = _layer_norm(alpha * x1_ref[rows, :] + (routed + sh_ref[rows, :]), g_ref[...], b_ref[...])


def _combine(x1, dest_flat, gk, ys, sg, su, sd, ln_g, ln_b, alpha, tm=128):
    n, d = x1.shape
    tm = min(tm, n)
    ff = sg.shape[-1]
    return pl.pallas_call(
        functools.partial(_combine_kernel, tm=tm, alpha=alpha),
        grid=(n // tm,),
        in_specs=[pl.BlockSpec((tm * TOP_K,), lambda i: (i,), memory_space=pltpu.SMEM),
                  pl.BlockSpec((tm, d), lambda i: (i, 0)),
                  pl.BlockSpec((tm, LANES), lambda i: (i, 0)),
                  pl.BlockSpec((d, ff), lambda i: (0, 0), pipeline_mode=pl.Buffered(1)),
                  pl.BlockSpec((d, ff), lambda i: (0, 0), pipeline_mode=pl.Buffered(1)),
                  pl.BlockSpec((ff, d), lambda i: (0, 0), pipeline_mode=pl.Buffered(1)),
                  pl.BlockSpec((1, d), lambda i: (0, 0)),
                  pl.BlockSpec((1, d), lambda i: (0, 0)),
                  pl.BlockSpec(memory_space=pl.ANY)],
        out_specs=pl.BlockSpec((tm, d), lambda i: (i, 0)),
        out_shape=jax.ShapeDtypeStruct((n, d), F32),
        scratch_shapes=[pltpu.VMEM((TOP_K * tm, d // 2), jnp.uint32), pltpu.VMEM((tm, d), F32),
                        pltpu.SemaphoreType.DMA((COMBINE_PARTS,))],
        compiler_params=_cparams(("arbitrary",)),
        name="moe_combine_ln2",
    )(dest_flat, x1, gk, sg, su, sd, ln_g, ln_b, ys)


def _rope_kernel(pos_ref, inv_ref, sign_ref, cos_ref, sin_ref):
    ang = pos_ref[...].astype(F32) * inv_ref[...]
    cos_ref[...] = jnp.cos(ang)
    sin_ref[...] = jnp.sin(ang) * sign_ref[...]


def _rope_tables(positions, dh, tm=512):
    half = dh // 2
    inv = ROPE_THETA ** (-jnp.arange(half, dtype=F32) * 2.0 / dh)
    reps = LANES // dh
    inv_l = jnp.tile(jnp.concatenate([inv, inv]), reps).reshape(1, LANES)
    sign_l = jnp.tile(jnp.concatenate([-jnp.ones((half,), F32), jnp.ones((half,), F32)]), reps).reshape(1, LANES)
    pos = positions.reshape(-1, 1)
    n = pos.shape[0]
    tm = min(tm, n)
    one = pl.BlockSpec((1, LANES), lambda i: (0, 0))
    blk = pl.BlockSpec((tm, LANES), lambda i: (i, 0))
    return pl.pallas_call(
        _rope_kernel,
        grid=(n // tm,),
        in_specs=[pl.BlockSpec((tm, 1), lambda i: (i, 0)), one, one],
        out_specs=[blk, blk],
        out_shape=[jax.ShapeDtypeStruct((n, LANES), F32)] * 2,
        compiler_params=_cparams(("parallel",)),
        name="rope_tables",
    )(pos, inv_l, sign_l)


def _dup_heads(w_rows, heads, dh):
    d = w_rows.shape[1]
    w3 = w_rows.reshape(heads, dh, d)
    return jnp.concatenate([w3, w3], axis=1).reshape(heads * 2 * dh, d)


def _hybrid_layer(x, positions, w_in, a_sinks, cmp_k_pos, cmp_k_w1, cmp_k_w2, cmp_v_pos, cmp_v_w1,
                  cmp_v_w2, w_o, ln1_g, ln1_b, w_router, router_bias, exp_w_gate, exp_w_up, exp_w_down,
                  sh_w_gate, sh_w_up, sh_w_down, ln2_g, ln2_b, alpha):
    bsz, seq, d = x.shape
    n = bsz * seq
    assert seq % NSA_TK == 0 and (seq // CMP_STRIDE) % LANES == 0 and A_WINDOW <= TQ and seq % (NSA_QT * TQ) == 0
    assert 3 <= seq // SLC_LEN <= LANES
    xf = x.reshape(n, d)
    xb = xf.astype(BF16)

    qa_w, kva = A_HEADS * A_HEAD_DIM, A_KV_HEADS * A_HEAD_DIM
    qb_w, kvb = B_HEADS * B_HEAD_DIM, B_KV_HEADS * B_HEAD_DIM
    widths = [qa_w, kva, kva, qb_w, kvb, kvb, kvb, kvb, kvb, kvb, 3 * B_HEADS, d, d]
    offs = np.concatenate([[0], np.cumsum(widths)]).tolist()
    (o_qa, o_ka, o_va, o_qb, o_kc, o_vc, o_ks, o_vs, o_kw, o_vw, o_gn, o_ga, o_end) = (
        offs[i] for i in (0, 1, 2, 3, 4, 5, 6, 7, 8, 9, 10, 11, 13))
    w_t = w_in.T
    blk = lambda start, width: list(range(start // PROJ_TN, (start + width) // PROJ_TN))
    assert all(o % PROJ_TN == 0 for o in (o_qa, o_qb, o_kc, o_vc, o_ks, o_vs, o_kw, o_vw))
    rep = B_HEADS // B_KV_HEADS
    w_gn = w_t[o_gn:o_ga].reshape(3, B_KV_HEADS, rep, d).transpose(1, 0, 2, 3).reshape(B_KV_HEADS, 3 * rep, d)
    w_gn_x = jnp.pad(w_gn, ((0, 0), (0, LANES - 3 * rep), (0, 0))).reshape(B_KV_HEADS * LANES, d)
    w_gn_x = w_gn_x.astype(BF16)
    assert o_ga % PROJ_TN == 3 * B_HEADS and o_end == w_t.shape[0]
    w_ka2 = _dup_heads(w_t[o_ka:o_va], A_KV_HEADS, A_HEAD_DIM).astype(BF16)
    w_va2 = _dup_heads(w_t[o_va:o_qb], A_KV_HEADS, A_HEAD_DIM).astype(BF16)
    all_blocks = lambda w: list(range(w.shape[0] // PROJ_TN))

    cos64, sin64 = _rope_tables(positions, A_HEAD_DIM)
    cos128, sin128 = _rope_tables(positions, B_HEAD_DIM)

    q_a = _project(xb, w_t, blk(o_qa, qa_w), cos64, sin64, "rope64", BF16,
                   q_cols=qa_w, q_scale=A_HEAD_DIM ** -0.5 * LOG2E)
    k_a = _project(xb, w_ka2, all_blocks(w_ka2), cos64, sin64, "rope64", BF16)
    v_a = _project(xb, w_va2, all_blocks(w_va2), cos64, sin64, "none", BF16)
    qk_b = _project(xb, w_t, blk(o_qb, qb_w) + blk(o_ks, kvb) + blk(o_kw, kvb), cos128, sin128, "rope128", BF16,
                    q_cols=qb_w, q_scale=B_HEAD_DIM ** -0.5 * LOG2E)
    v_b = _project(xb, w_t, blk(o_vs, kvb) + blk(o_vw, kvb), cos128, sin128, "none", BF16)
    kc = _project(xb, w_t, blk(o_kc, kvb), cos128, sin128, "rope128", F32)
    vc = _project(xb, w_t, blk(o_vc, kvb), cos128, sin128, "none", F32)
    gates = _project(xb, w_t, blk(o_ga - 3 * B_HEADS, 2 * d), cos128, sin128, "sigmoid", F32,
                     tm=512, shift=3 * B_HEADS)
    gnx = _project(xb, w_gn_x, all_blocks(w_gn_x), cos128, sin128, "sigmoid", F32)

    ya = _swa_attention(q_a, k_a, v_a, a_sinks, gates, bsz, seq)

    kcmp = _compress(kc, 0, cmp_k_pos, cmp_k_w1.astype(BF16), cmp_k_w2.astype(BF16), bsz, seq)
    vcmp = _compress(vc, 0, cmp_v_pos, cmp_v_w1.astype(BF16), cmp_v_w2.astype(BF16), bsz, seq)
    nsa_cols = dict(q=0, ks=qb_w, kw=qb_w + kvb, vs=0, vw=kvb, g_b=d, gate=0)
    y = _nsa_attention(qk_b, v_b, gnx, gates, ya, kcmp, vcmp, bsz, seq, nsa_cols)

    w_r = jnp.pad(w_router, ((0, 0), (0, LANES - N_EXPERTS))).astype(BF16)
    x1, x1p, scores = _out_proj(y, w_o.astype(BF16), xf, ln1_g.reshape(1, d), ln1_b.reshape(1, d), w_r, alpha)

    bias = jnp.pad(router_bias.astype(F32), (0, LANES - N_EXPERTS)).reshape(1, LANES)
    ek, rk, gk, cnt = _route(scores, bias)

    counts = cnt[0, :N_EXPERTS].astype(jnp.int32)
    padded = (counts + MOE_BLOCK - 1) // MOE_BLOCK * MOE_BLOCK
    pends = jnp.cumsum(padded)
    pstarts = pends - padded
    nblk = -(-(n * TOP_K) // MOE_BLOCK) + N_EXPERTS
    blk_start = jnp.arange(nblk, dtype=jnp.int32) * MOE_BLOCK
    block_e = jnp.minimum(jnp.sum((pends[None, :] <= blk_start[:, None]).astype(jnp.int32), axis=1),
                          N_EXPERTS - 1)
    n_used = (pends[-1:] // MOE_BLOCK).astype(jnp.int32)
    hit = ek[:, :TOP_K, None] == jnp.arange(N_EXPERTS, dtype=jnp.int32)
    dest = (jnp.sum(jnp.where(hit, pstarts.astype(jnp.int32), 0), axis=-1) + rk[:, :TOP_K]).reshape(-1)

    xs = _scatter_rows(x1p, dest, counts, padded, pstarts.astype(jnp.int32), n_used, nblk)
    first = jnp.concatenate([jnp.ones((1,), jnp.int32), (block_e[1:] != block_e[:-1]).astype(jnp.int32)])
    end_blk = jnp.take(pends // MOE_BLOCK, block_e).astype(jnp.int32)
    nxt = jnp.where(end_blk < n_used[0], jnp.take(block_e, jnp.minimum(end_blk, nblk - 1)), -1).astype(jnp.int32)
    ys = _expert_mlp(xs, block_e, first, nxt, n_used, exp_w_gate, exp_w_up, exp_w_down)
    out = _combine(x1, dest, gk, ys, sh_w_gate.astype(BF16), sh_w_up.astype(BF16), sh_w_down.astype(BF16),
                   ln2_g.reshape(1, d), ln2_b.reshape(1, d), alpha)
    return out.reshape(bsz, seq, d)


def kernel(x, positions, w_in, a_sinks, cmp_k_pos, cmp_k_w1, cmp_k_w2, cmp_v_pos, cmp_v_w1, cmp_v_w2,
           w_o, ln1_g, ln1_b, w_router, router_bias, exp_w_gate, exp_w_up, exp_w_down,
           sh_w_gate, sh_w_up, sh_w_down, ln2_g, ln2_b):
    depth = w_in.shape[0]
    alpha = (2.0 * depth) ** 0.25
    for l in range(depth):
        x = _hybrid_layer(x, positions, w_in[l], a_sinks[l], cmp_k_pos[l], cmp_k_w1[l], cmp_k_w2[l],
                          cmp_v_pos[l], cmp_v_w1[l], cmp_v_w2[l], w_o[l], ln1_g[l], ln1_b[l],
                          w_router[l], router_bias[l], exp_w_gate[l], exp_w_up[l], exp_w_down[l],
                          sh_w_gate[l], sh_w_up[l], sh_w_down[l], ln2_g[l], ln2_b[l], alpha)
    return x
```

```python
import functools

import jax
import jax.numpy as jnp
import numpy as np
from jax import lax
from jax.experimental import pallas as pl
from jax.experimental.pallas import tpu as pltpu

F32 = jnp.float32
BF16 = jnp.bfloat16

ROPE_THETA = 10000.0
LN_EPS = 1e-5
NEG_INF = -1e30

A_HEADS, A_KV_HEADS, A_HEAD_DIM, A_WINDOW = 64, 8, 64, 128
B_HEADS, B_KV_HEADS, B_HEAD_DIM = 32, 4, 128
CMP_LEN, CMP_STRIDE, CMP_HIDDEN = 32, 16, 256
SLC_LEN, SLC_TOPK, WIN_LEN = 64, 16, 512
N_EXPERTS, TOP_K, EXPERT_FF, SHARED_FF = 64, 8, 512, 512
ROUTED_SCALE = 2.5
MOE_BLOCK = 256

LOG2E = 1.4426950408889634
LANES = 128
TQ = 128
VMEM_LIMIT = 56 * 1024 * 1024


def _cparams(sem, vmem_limit=VMEM_LIMIT):
    return pltpu.CompilerParams(dimension_semantics=sem, vmem_limit_bytes=vmem_limit)


def _dot(a, b):
    return jnp.dot(a, b, preferred_element_type=F32)


def _dot_nt(a, b):
    return lax.dot_general(a, b, (((1,), (1,)), ((), ())), preferred_element_type=F32)


def _sigmoid(v):
    return 1.0 / (1.0 + jnp.exp(-v))


def _pack_halves(v):
    h = v.shape[1] // 2
    lo = lax.bitcast_convert_type(v[:, :h].astype(BF16).astype(F32), jnp.uint32)
    hi = lax.bitcast_convert_type(v[:, h:].astype(BF16).astype(F32), jnp.uint32)
    return (hi & jnp.uint32(0xFFFF0000)) | (lo >> 16)


def _unpack_halves(w):
    lo = lax.bitcast_convert_type(w << 16, F32)
    hi = lax.bitcast_convert_type(w & jnp.uint32(0xFFFF0000), F32)
    return lo, hi


def _layer_norm(z, g, b):
    mu = jnp.mean(z, axis=-1, keepdims=True)
    zc = z - mu
    var = jnp.mean(zc * zc, axis=-1, keepdims=True)
    return zc * lax.rsqrt(var + LN_EPS) * g + b


def _proj_kernel(tbl_ref, x_ref, *rest, epi, q_blocks, q_scale, shift, rope_blocks):
    del tbl_ref
    if shift:
        w_ref, w2_ref, cos_ref, sin_ref, o_ref, *scratch = rest
    else:
        w_ref, cos_ref, sin_ref, o_ref, *scratch = rest
    if scratch:
        (wb_ref,) = scratch

        @pl.when(pl.program_id(1) == 0)
        def _():
            if shift:
                keep = wb_ref.shape[0] - shift
                wb_ref[:keep, :] = w_ref[shift:, :].astype(BF16)
                wb_ref[keep:, :] = w2_ref[:shift, :].astype(BF16)
            else:
                wb_ref[...] = w_ref[...].astype(BF16)

        w = wb_ref[...]
    else:
        w = w_ref[...]
    acc = _dot_nt(x_ref[...], w)
    if epi == "none":
        o_ref[...] = acc.astype(o_ref.dtype)
    elif epi == "sigmoid":
        o_ref[...] = _sigmoid(acc).astype(o_ref.dtype)
    else:
        cs = jnp.where(pl.program_id(0) < q_blocks, q_scale, 1.0)
        rotated = pl.program_id(0) < rope_blocks
        cos = jnp.where(rotated, cos_ref[...] * cs, 1.0)
        sin = jnp.where(rotated, sin_ref[...] * cs, 0.0)
        lane = lax.broadcasted_iota(jnp.int32, (1, LANES), 1)
        for c in range(acc.shape[1] // LANES):
            y = acc[:, c * LANES:(c + 1) * LANES]
            if epi == "rope128":
                rot = pltpu.roll(y, 64, 1)
            else:
                rot = jnp.where((lane % 64) < 32, pltpu.roll(y, 96, 1), pltpu.roll(y, 32, 1))
            o_ref[:, c * LANES:(c + 1) * LANES] = (y * cos + rot * sin).astype(o_ref.dtype)


PROJ_TN = 512


def _project(xb, w_t, blocks, cos, sin, epi, out_dtype, q_cols=0, q_scale=1.0, tm=1024, shift=0, plain_tail=0):
    n, d = xb.shape
    tn = PROJ_TN
    ncol = len(blocks) * tn
    tm = min(tm, n)
    cast = w_t.dtype != BF16
    assert n % tm == 0 and q_cols % tn == 0 and (max(blocks) + 1) * tn + shift <= w_t.shape[0]
    assert shift % 16 == 0 and (cast or not shift)
    w_specs = [pl.BlockSpec((tn, d), lambda j, i, tbl: (tbl[j], 0))]
    if shift:
        w_specs.append(pl.BlockSpec((tn, d), lambda j, i, tbl: (tbl[j] + 1, 0)))
    return pl.pallas_call(
        functools.partial(_proj_kernel, epi=epi, q_blocks=q_cols // tn, q_scale=q_scale, shift=shift,
                          rope_blocks=len(blocks) - plain_tail),
        grid_spec=pltpu.PrefetchScalarGridSpec(
            num_scalar_prefetch=1,
            grid=(ncol // tn, n // tm),
            in_specs=[pl.BlockSpec((tm, d), lambda j, i, tbl: (i, 0)), *w_specs,
                      pl.BlockSpec((tm, LANES), lambda j, i, tbl: (i, 0)),
                      pl.BlockSpec((tm, LANES), lambda j, i, tbl: (i, 0))],
            out_specs=pl.BlockSpec((tm, tn), lambda j, i, tbl: (i, j)),
            scratch_shapes=[pltpu.VMEM((tn, d), BF16)] if cast else []),
        out_shape=jax.ShapeDtypeStruct((n, ncol), out_dtype),
        compiler_params=_cparams(("arbitrary", "arbitrary")),
        name="proj_" + epi,
    )(jnp.asarray(blocks, jnp.int32), xb, *([w_t] * len(w_specs)), cos, sin)


SWA_GROUPS = 8


def _swa_kernel(sink_ref, q_ref, kp_ref, kc_ref, vp_ref, vc_ref, ga_ref, o_ref, s_ref, p_ref):
    n = pl.program_id(1)
    lane = lax.broadcasted_iota(jnp.int32, (1, LANES), 1)
    lo = lane < A_HEAD_DIM
    nk = 2 * TQ
    qpos = n * TQ + lax.broadcasted_iota(jnp.int32, (TQ, nk), 0)
    kpos = (n - 1) * TQ + lax.broadcasted_iota(jnp.int32, (TQ, nk), 1)
    delta = qpos - kpos
    mask = (kpos >= 0) & (delta >= 0) & (delta < A_WINDOW)
    rep = A_HEADS // A_KV_HEADS
    npair = rep // 2
    zq = jnp.zeros((TQ, LANES), q_ref.dtype)
    for gg in range(SWA_GROUPS):
        g = pl.program_id(2) * SWA_GROUPS + gg
        gl = slice(gg * LANES, (gg + 1) * LANES)
        kk = jnp.concatenate([kp_ref[:, gl], kc_ref[:, gl]], axis=0)
        vv = jnp.concatenate([vp_ref[:, gl], vc_ref[:, gl]], axis=0)
        zv = jnp.zeros_like(vv)
        v_halves = (jnp.where(lo, vv, zv), jnp.where(lo, zv, vv))
        qcols = [slice((gg * npair + c) * LANES, (gg * npair + c + 1) * LANES) for c in range(npair)]
        q_lo = [jnp.where(lo, q_ref[:, c], zq) for c in qcols]
        q_hi = [jnp.where(lo, zq, q_ref[:, c]) for c in qcols]
        r0 = gg * rep * TQ
        s_ref[r0:r0 + rep * TQ, :] = _dot_nt(jnp.concatenate(q_lo + q_hi, axis=0), kk)
        for half in range(2):
            for pair in range(npair):
                sl = slice(r0 + (half * npair + pair) * TQ, r0 + (half * npair + pair + 1) * TQ)
                s = jnp.where(mask, s_ref[sl, :], NEG_INF)
                sk = sink_ref[g * rep + pair * 2 + half] * LOG2E
                m = jnp.maximum(jnp.max(s, axis=-1, keepdims=True), sk)
                e = jnp.exp2(s - m)
                p = e / (jnp.sum(e, axis=-1, keepdims=True) + jnp.exp2(sk - m))
                p_ref[sl, :] = p.astype(BF16)
        half_rows = npair * TQ
        out = (_dot(p_ref[r0:r0 + half_rows, :], v_halves[0])
               + _dot(p_ref[r0 + half_rows:r0 + 2 * half_rows, :], v_halves[1]))
        for pair, c in enumerate(qcols):
            o_ref[:, c] = ga_ref[:, c] * out[pair * TQ:(pair + 1) * TQ, :]


def _swa_attention(q, k, v, sinks, gates, bsz, seq):
    n = bsz * seq
    nb = seq // TQ
    rows = SWA_GROUPS * (A_HEADS // A_KV_HEADS) * TQ
    qw = SWA_GROUPS * (A_HEADS // A_KV_HEADS) * A_HEAD_DIM
    kw = SWA_GROUPS * LANES
    return pl.pallas_call(
        _swa_kernel,
        grid=(bsz, nb, A_KV_HEADS // SWA_GROUPS),
        in_specs=[pl.BlockSpec(memory_space=pltpu.SMEM),
                  pl.BlockSpec((TQ, qw), lambda b, i, g: (b * nb + i, g)),
                  pl.BlockSpec((TQ, kw), lambda b, i, g: (b * nb + jnp.maximum(i - 1, 0), g)),
                  pl.BlockSpec((TQ, kw), lambda b, i, g: (b * nb + i, g)),
                  pl.BlockSpec((TQ, kw), lambda b, i, g: (b * nb + jnp.maximum(i - 1, 0), g)),
                  pl.BlockSpec((TQ, kw), lambda b, i, g: (b * nb + i, g)),
                  pl.BlockSpec((TQ, qw), lambda b, i, g: (b * nb + i, g))],
        out_specs=pl.BlockSpec((TQ, qw), lambda b, i, g: (b * nb + i, g)),
        out_shape=jax.ShapeDtypeStruct((n, A_HEADS * A_HEAD_DIM), F32),
        scratch_shapes=[pltpu.VMEM((rows, 2 * TQ), F32), pltpu.VMEM((rows, 2 * TQ), BF16)],
        compiler_params=_cparams(("parallel", "parallel", "parallel")),
        name="swa",
    )(sinks, q, k, k, v, v, gates)


def _compress_kernel(t_ref, pe_ref, w1_ref, w2_ref, o_ref):
    nch = o_ref.shape[0]
    u = jnp.zeros((nch, CMP_HIDDEN), F32)
    v = jnp.zeros((nch, CMP_HIDDEN), F32)
    for l in range(CMP_STRIDE):
        t_l = t_ref[pl.ds(l, nch, stride=CMP_STRIDE), :]
        a = (t_l + pe_ref[l:l + 1, :]).astype(BF16)
        b = (t_l + pe_ref[CMP_STRIDE + l:CMP_STRIDE + l + 1, :]).astype(BF16)
        u = u + _dot(a, w1_ref[l * LANES:(l + 1) * LANES, :])
        v = v + _dot(b, w1_ref[(CMP_STRIDE + l) * LANES:(CMP_STRIDE + l + 1) * LANES, :])
    pre = u + pltpu.roll(v, nch - 1, 0)
    h = jax.nn.gelu(pre)
    o_ref[...] = _dot(h.astype(BF16), w2_ref[...]).astype(o_ref.dtype)


def _compress(t, col0, pe, w1, w2, bsz, seq):
    nch = seq // CMP_STRIDE
    cb0 = col0 // LANES
    return pl.pallas_call(
        _compress_kernel,
        grid=(bsz, B_KV_HEADS),
        in_specs=[pl.BlockSpec((seq, LANES), lambda b, g: (b, cb0 + g)),
                  pl.BlockSpec((CMP_LEN, LANES), lambda b, g: (0, 0)),
                  pl.BlockSpec((CMP_LEN * LANES, CMP_HIDDEN), lambda b, g: (0, 0)),
                  pl.BlockSpec((CMP_HIDDEN, LANES), lambda b, g: (0, 0))],
        out_specs=pl.BlockSpec((None, None, nch, LANES), lambda b, g: (b, g, 0, 0)),
        out_shape=jax.ShapeDtypeStruct((bsz, B_KV_HEADS, nch, LANES), BF16),
        compiler_params=_cparams(("parallel", "parallel")),
        name="nsa_compress",
    )(t, pe, w1, w2)


NSA_QT = 2
NSA_SCRATCH = 7
NSA_TK = 1024
NSA_NKW = (-(-(WIN_LEN - 1) // TQ) + 1) * TQ


def _nsa_kernel(q_ref, gate_ref, kcmp_ref, vcmp_ref, ks_ref, vs_ref, kw_ref, vw_ref, ya_ref, gb_ref, o_ref,
                *scratch, seq):
    qs, sc, pr, mx, ls, acc, ocmp = (scratch[k::NSA_SCRATCH] for k in range(NSA_SCRATCH))
    step = pl.program_id(2)
    rep = B_HEADS // B_KV_HEADS
    nch = seq // CMP_STRIDE
    ns = seq // SLC_LEN
    n_sel = min(SLC_TOPK, ns)
    heads = [slice(r * TQ, (r + 1) * TQ) for r in range(rep)]
    tiles = range(NSA_QT)
    trows = [slice(h * TQ, (h + 1) * TQ) for h in tiles]
    qis = [step * NSA_QT + h for h in tiles]
    tpos = [qis[h] * TQ + lax.broadcasted_iota(jnp.int32, (TQ, 1), 0) for h in tiles]
    for h in tiles:
        for r in range(rep):
            qs[h][heads[r], :] = q_ref[trows[h], r * LANES:(r + 1) * LANES]

    c_end = lax.broadcasted_iota(jnp.int32, (1, nch), 1) * CMP_STRIDE + (CMP_LEN - 1)
    jrow = lax.broadcasted_iota(jnp.int32, (ns, 1), 0)
    d = lax.broadcasted_iota(jnp.int32, (ns, nch), 1) - (SLC_LEN // CMP_STRIDE) * jrow
    msel = jnp.where((d == -1) | (d == 3), 1.0, jnp.where((d >= 0) & (d <= 2), 2.0, 0.0)).astype(BF16)
    sels = []
    for h in tiles:
        sc[h][:, :nch] = _dot_nt(qs[h][...], kcmp_ref[...])
        valid = c_end <= tpos[h]
        p_grp = jnp.zeros((TQ, nch), F32)
        for sl in heads:
            s = jnp.where(valid, sc[h][sl, :nch], NEG_INF)
            m = jnp.max(s, axis=-1, keepdims=True)
            e = jnp.where(valid, jnp.exp2(s - m), 0.0)
            den = jnp.sum(e, axis=-1, keepdims=True)
            p = e / jnp.where(den > 0, den, 1.0)
            p_grp = p_grp + p
            pr[h][sl, :nch] = p.astype(BF16)
        ocmp[h][...] = _dot(pr[h][:, :nch], vcmp_ref[...])

        p_hi = p_grp.astype(BF16)
        r1 = p_grp - p_hi.astype(F32)
        p_mid = r1.astype(BF16)
        p_lo = (r1 - p_mid.astype(F32)).astype(BF16)
        p_slc = _dot_nt(msel, p_hi) + _dot_nt(msel, p_mid) + _dot_nt(msel, p_lo)

        tpos_l = qis[h] * TQ + lax.broadcasted_iota(jnp.int32, (1, TQ), 1)
        cur = tpos_l // SLC_LEN
        forced = (jrow == 0) | (jrow == cur) | (jrow == cur - 1)
        score = jnp.where(forced, 1e9, jnp.where(jrow * SLC_LEN <= tpos_l, p_slc, NEG_INF))
        rank = jnp.zeros((ns, TQ), jnp.int32)
        for i in range(ns):
            si = score[i:i + 1, :]
            rank = rank + ((si > score) | ((si == score) & (jrow > i))).astype(jnp.int32)
        sel_t = (rank < n_sel).astype(F32)
        if ns < LANES:
            sel_t = jnp.concatenate([sel_t, jnp.zeros((LANES - ns, TQ), F32)], axis=0)
        sels.append(sel_t.T.astype(BF16))

    for h in tiles:
        mx[h][...] = jnp.full_like(mx[h], NEG_INF)
        ls[h][...] = jnp.zeros_like(ls[h])
        acc[h][...] = jnp.zeros_like(acc[h])
    jexp = lax.broadcasted_iota(jnp.int32, (LANES, 1), 0)

    def slc_body(kt, c):
        row0 = pl.multiple_of(kt * NSA_TK, NSA_TK)
        k_t = ks_ref[pl.ds(row0, NSA_TK), :]
        v_t = vs_ref[pl.ds(row0, NSA_TK), :]
        kpos = kt * NSA_TK + lax.broadcasted_iota(jnp.int32, (1, NSA_TK), 1)
        expand = (jexp == kpos // SLC_LEN).astype(BF16)
        for h in tiles:
            sc[h][:, :NSA_TK] = _dot_nt(qs[h][...], k_t)
        for h in tiles:
            keep = (_dot(sels[h], expand) > 0.5) & (kpos <= tpos[h])
            for sl in heads:
                s = jnp.where(keep, sc[h][sl, :NSA_TK], NEG_INF)
                m_old = mx[h][sl, :]
                m_new = jnp.maximum(m_old, jnp.max(s, axis=-1, keepdims=True))
                alpha = jnp.exp2(m_old - m_new)
                e = jnp.exp2(s - m_new)
                ls[h][sl, :] = alpha * ls[h][sl, :] + jnp.sum(e, axis=-1, keepdims=True)
                mx[h][sl, :] = m_new
                acc[h][sl, :] = alpha * acc[h][sl, :]
                pr[h][sl, :NSA_TK] = e.astype(BF16)
            acc[h][...] += _dot(pr[h][:, :NSA_TK], v_t)
        return c

    lax.fori_loop(0, (qis[-1] * TQ + TQ + NSA_TK - 1) // NSA_TK, slc_body, 0)

    nprev = NSA_NKW // TQ - 1
    gate_all = gate_ref[...]
    for h in tiles:
        k_parts, v_parts = [], []
        for i in range(nprev + 1):
            row0 = pl.multiple_of(jnp.maximum(qis[h] - nprev + i, 0) * TQ, TQ)
            k_parts.append(kw_ref[pl.ds(row0, TQ), :])
            v_parts.append(vw_ref[pl.ds(row0, TQ), :])
        sc[h][:, :NSA_NKW] = _dot_nt(qs[h][...], jnp.concatenate(k_parts, axis=0))
        kposw = (qis[h] - nprev) * TQ + lax.broadcasted_iota(jnp.int32, (1, NSA_NKW), 1)
        dlt = tpos[h] - kposw
        keepw = (kposw >= 0) & (dlt >= 0) & (dlt < WIN_LEN)
        for sl in heads:
            s = jnp.where(keepw, sc[h][sl, :NSA_NKW], NEG_INF)
            e = jnp.exp2(s - jnp.max(s, axis=-1, keepdims=True))
            pr[h][sl, :NSA_NKW] = (e / jnp.sum(e, axis=-1, keepdims=True)).astype(BF16)
        o_win = _dot(pr[h][:, :NSA_NKW], jnp.concatenate(v_parts, axis=0))

        gate = gate_all[trows[h], :]
        for r, sl in enumerate(heads):
            cols = slice(r * LANES, (r + 1) * LANES)
            o_b = (gate[:, r:r + 1] * ocmp[h][sl, :]
                   + gate[:, rep + r:rep + r + 1] * (acc[h][sl, :] / ls[h][sl, :])
                   + gate[:, 2 * rep + r:2 * rep + r + 1] * o_win[sl])
            o_ref[trows[h], cols] = (ya_ref[trows[h], cols] + gb_ref[trows[h], cols] * o_b).astype(o_ref.dtype)


def _nsa_attention(qk, v, gnx, gates, ya, kcmp, vcmp, bsz, seq, cols):
    n = bsz * seq
    tq = NSA_QT * TQ
    nq = seq // tq
    nch = seq // CMP_STRIDE
    rep = B_HEADS // B_KV_HEADS
    qb0 = cols["q"] // (rep * LANES)
    gbb = cols["g_b"] // (rep * LANES)
    ksb, kwb, vsb, vwb, gb0 = (cols[k] // LANES for k in ("ks", "kw", "vs", "vw", "gate"))
    rows, wide = rep * TQ, max(NSA_NKW, NSA_TK, nch)
    return pl.pallas_call(
        functools.partial(_nsa_kernel, seq=seq),
        grid=(bsz, B_KV_HEADS, nq),
        in_specs=[pl.BlockSpec((tq, rep * LANES), lambda b, g, i: (b * nq + i, qb0 + g)),
                  pl.BlockSpec((tq, LANES), lambda b, g, i: (b * nq + i, gb0 + g)),
                  pl.BlockSpec((None, None, nch, LANES), lambda b, g, i: (b, g, 0, 0)),
                  pl.BlockSpec((None, None, nch, LANES), lambda b, g, i: (b, g, 0, 0)),
                  pl.BlockSpec((seq, LANES), lambda b, g, i: (b, ksb + g)),
                  pl.BlockSpec((seq, LANES), lambda b, g, i: (b, vsb + g)),
                  pl.BlockSpec((seq, LANES), lambda b, g, i: (b, kwb + g)),
                  pl.BlockSpec((seq, LANES), lambda b, g, i: (b, vwb + g)),
                  pl.BlockSpec((tq, rep * LANES), lambda b, g, i: (b * nq + i, g)),
                  pl.BlockSpec((tq, rep * LANES), lambda b, g, i: (b * nq + i, gbb + g))],
        out_specs=pl.BlockSpec((tq, rep * LANES), lambda b, g, i: (b * nq + i, g)),
        out_shape=jax.ShapeDtypeStruct((n, B_HEADS * B_HEAD_DIM), BF16),
        scratch_shapes=[pltpu.VMEM((rows, LANES), BF16),
                        pltpu.VMEM((rows, wide), F32),
                        pltpu.VMEM((rows, wide), BF16),
                        pltpu.VMEM((rows, 1), F32), pltpu.VMEM((rows, 1), F32),
                        pltpu.VMEM((rows, LANES), F32),
                        pltpu.VMEM((rows, LANES), F32)] * NSA_QT,
        compiler_params=_cparams(("parallel", "parallel", "arbitrary")),
        name="nsa",
    )(qk, gnx, kcmp, vcmp, qk, v, qk, v, ya, gates)


def _oproj_kernel(y_ref, wo_ref, x_ref, g_ref, b_ref, wr_ref, x1_ref, x1p_ref, sc_ref, z_ref, *, alpha):
    j = pl.program_id(1)
    z_ref[j] = _dot(y_ref[...], wo_ref[...])

    @pl.when(j == pl.num_programs(1) - 1)
    def _():
        nj, tm, _ = z_ref.shape
        piece = min(128, tm)
        for r0 in range(0, tm, piece):
            rows = slice(r0, r0 + piece)
            z = jnp.concatenate([z_ref[jj, rows, :] for jj in range(nj)], axis=1)
            x1 = _layer_norm(alpha * x_ref[rows, :] + z, g_ref[...], b_ref[...])
            x1_ref[rows, :] = x1
            x1p_ref[rows, :] = _pack_halves(x1)
            sc_ref[rows, :] = _sigmoid(_dot(x1.astype(BF16), wr_ref[...]))


def _out_proj(y, w_o, x, ln_g, ln_b, w_r, alpha, tm=512, tn=512):
    n, d = x.shape
    tm = min(tm, n)
    once = dict(pipeline_mode=pl.Buffered(1))
    return pl.pallas_call(
        functools.partial(_oproj_kernel, alpha=alpha),
        grid=(n // tm, d // tn),
        in_specs=[pl.BlockSpec((tm, d), lambda i, j: (i, 0)),
                  pl.BlockSpec((d, tn), lambda i, j: (0, j)),
                  pl.BlockSpec((tm, d), lambda i, j: (i, 0), **once),
                  pl.BlockSpec((1, d), lambda i, j: (0, 0)),
                  pl.BlockSpec((1, d), lambda i, j: (0, 0)),
                  pl.BlockSpec((d, LANES), lambda i, j: (0, 0))],
        out_specs=[pl.BlockSpec((tm, d), lambda i, j: (i, 0), **once),
                   pl.BlockSpec((tm, d // 2), lambda i, j: (i, 0), **once),
                   pl.BlockSpec((tm, LANES), lambda i, j: (i, 0))],
        out_shape=[jax.ShapeDtypeStruct((n, d), F32), jax.ShapeDtypeStruct((n, d // 2), jnp.uint32),
                   jax.ShapeDtypeStruct((n, LANES), F32)],
        scratch_shapes=[pltpu.VMEM((d // tn, tm, tn), F32)],
        compiler_params=_cparams(("parallel", "arbitrary")),
        name="oproj_ln1_router",
    )(y, w_o, x, ln_g, ln_b, w_r)


def _route_kernel(sc_ref, bias_ref, ek_ref, rk_ref, gk_ref, cnt_ref, run_ref):
    i = pl.program_id(0)
    tm = sc_ref.shape[0]

    @pl.when(i == 0)
    def _():
        run_ref[...] = jnp.zeros_like(run_ref)

    sc = sc_ref[...]
    lane = lax.broadcasted_iota(jnp.int32, (1, LANES), 1)
    is_e = lane < N_EXPERTS
    biased = jnp.where(is_e, sc + bias_ref[...], -jnp.inf)
    rank = jnp.zeros((tm, LANES), jnp.int32)
    for e in range(N_EXPERTS):
        be = biased[:, e:e + 1]
        rank = rank + ((be > biased) | ((be == biased) & (e < lane))).astype(jnp.int32)
    sel = (rank < TOP_K) & is_e
    sv = jnp.where(sel, sc, 0.0)
    gates = sv / jnp.sum(sv, axis=-1, keepdims=True) * ROUTED_SCALE
    selb = sel.astype(BF16)
    ri = lax.broadcasted_iota(jnp.int32, (tm, tm), 0)
    ci = lax.broadcasted_iota(jnp.int32, (tm, tm), 1)
    cum = _dot((ci < ri).astype(BF16), selb) + run_ref[...]
    run_ref[...] += jnp.sum(sel.astype(F32), axis=0, keepdims=True)
    cnt_ref[...] = run_ref[...]
    ai = lax.broadcasted_iota(jnp.int32, (LANES, LANES), 0)
    bi = lax.broadcasted_iota(jnp.int32, (LANES, LANES), 1)
    before = _dot(selb, (ai < bi).astype(BF16))
    lane_f = lane.astype(F32)
    ek = jnp.zeros((tm, LANES), F32)
    rk = jnp.zeros((tm, LANES), F32)
    gk = jnp.zeros((tm, LANES), F32)
    for k in range(TOP_K):
        oh = sel & (before == k)
        ek = jnp.where(lane == k, jnp.sum(jnp.where(oh, lane_f, 0.0), axis=-1, keepdims=True), ek)
        rk = jnp.where(lane == k, jnp.sum(jnp.where(oh, cum, 0.0), axis=-1, keepdims=True), rk)
        gk = jnp.where(lane == k, jnp.sum(jnp.where(oh, gates, 0.0), axis=-1, keepdims=True), gk)
    ek_ref[...] = ek.astype(jnp.int32)
    rk_ref[...] = rk.astype(jnp.int32)
    gk_ref[...] = gk


def _route(scores, bias, tm=256):
    n = scores.shape[0]
    tm = min(tm, n)
    blk = pl.BlockSpec((tm, LANES), lambda i: (i, 0))
    one = pl.BlockSpec((1, LANES), lambda i: (0, 0))
    return pl.pallas_call(
        _route_kernel,
        grid=(n // tm,),
        in_specs=[blk, one],
        out_specs=[blk, blk, blk, one],
        out_shape=[jax.ShapeDtypeStruct((n, LANES), jnp.int32), jax.ShapeDtypeStruct((n, LANES), jnp.int32),
                   jax.ShapeDtypeStruct((n, LANES), F32), jax.ShapeDtypeStruct((1, LANES), F32)],
        scratch_shapes=[pltpu.VMEM((1, LANES), F32)],
        compiler_params=_cparams(("arbitrary",)),
        name="route",
    )(scores, bias)


def _row_copy(src, s_row, dst, d_row, sem):
    return pltpu.make_async_copy(src.at[pl.ds(s_row, 1)], dst.at[pl.ds(d_row, 1)], sem)


def _scatter_kernel(cnt_ref, pad_ref, pst_ref, nu_ref, dest_ref, x_ref, xs_ref, zrow_ref, zblk_ref, sem,
                    *, tm, n_tok_steps, n_blocks):
    i = pl.program_id(0)

    @pl.when(i < n_tok_steps)
    def _():
        def issue(t, c):
            for k in range(TOP_K):
                _row_copy(x_ref, t, xs_ref, dest_ref[t * TOP_K + k], sem).start(priority=k % 2)
            return c

        lax.fori_loop(0, tm, issue, 0)

        for k in range(TOP_K):
            pltpu.make_async_copy(x_ref, xs_ref.at[pl.ds(0, tm)], sem).wait()

    @pl.when((i >= n_tok_steps) & (i < n_tok_steps + N_EXPERTS))
    def _():
        e = i - n_tok_steps
        zrow_ref[...] = jnp.zeros_like(zrow_ref)
        lo, hi, base = cnt_ref[e], pad_ref[e], pst_ref[e]

        def issue(r, c):
            pltpu.make_async_copy(zrow_ref, xs_ref.at[pl.ds(base + r, 1)], sem).start()
            return c

        lax.fori_loop(lo, hi, issue, 0)

        def drain(r, c):
            pltpu.make_async_copy(zrow_ref, xs_ref.at[pl.ds(0, 1)], sem).wait()
            return c

        lax.fori_loop(lo, hi, drain, 0)

    @pl.when(i == n_tok_steps + N_EXPERTS)
    def _():
        zblk_ref[...] = jnp.zeros_like(zblk_ref)

        def issue(p, c):
            pltpu.make_async_copy(zblk_ref, xs_ref.at[pl.ds(p * MOE_BLOCK, MOE_BLOCK)], sem).start()
            return c

        lax.fori_loop(nu_ref[0], n_blocks, issue, 0)

        def drain(p, c):
            pltpu.make_async_copy(zblk_ref, xs_ref.at[pl.ds(0, MOE_BLOCK)], sem).wait()
            return c

        lax.fori_loop(nu_ref[0], n_blocks, drain, 0)


def _scatter_rows(x1, dest_flat, counts, padded, pstarts, n_used, n_blocks, tm=128):
    n, d = x1.shape
    tm = min(tm, n)
    n_tok_steps = n // tm
    return pl.pallas_call(
        functools.partial(_scatter_kernel, tm=tm, n_tok_steps=n_tok_steps, n_blocks=n_blocks),
        grid_spec=pltpu.PrefetchScalarGridSpec(
            num_scalar_prefetch=4,
            grid=(n_tok_steps + N_EXPERTS + 1,),
            in_specs=[pl.BlockSpec((tm * TOP_K,), lambda i, *_: (jnp.minimum(i, n_tok_steps - 1),),
                                   memory_space=pltpu.SMEM),
                      pl.BlockSpec((tm, d), lambda i, *_: (jnp.minimum(i, n_tok_steps - 1), 0))],
            out_specs=pl.BlockSpec(memory_space=pl.ANY),
            scratch_shapes=[pltpu.VMEM((1, d), x1.dtype), pltpu.VMEM((MOE_BLOCK, d), x1.dtype),
                            pltpu.SemaphoreType.DMA(())]),
        out_shape=jax.ShapeDtypeStruct((n_blocks * MOE_BLOCK, d), x1.dtype),
        compiler_params=_cparams(("arbitrary",)),
        name="moe_scatter",
    )(counts, padded, pstarts, n_used, dest_flat, x1)


def _expert_kernel(be_ref, first_ref, nxt_ref, nu_ref, xs_ref, wg_hbm, wu_hbm, wd_hbm, y_ref,
                   sg_ref, su_ref, sd_ref, wg_ref, wu_ref, wd_ref, sem):
    p = pl.program_id(0)
    used = p < nu_ref[0]

    def copies(e):
        return (pltpu.make_async_copy(wg_hbm.at[e], sg_ref, sem.at[0]),
                pltpu.make_async_copy(wu_hbm.at[e], su_ref, sem.at[1]),
                pltpu.make_async_copy(wd_hbm.at[e], sd_ref, sem.at[2]))

    @pl.when(p == 0)
    def _():
        for c in copies(be_ref[0]):
            c.start()

    @pl.when(used & (first_ref[p] == 1))
    def _():
        for c in copies(be_ref[p]):
            c.wait()
        wg_ref[...] = sg_ref[...].astype(BF16)
        wu_ref[...] = su_ref[...].astype(BF16)
        wd_ref[...] = sd_ref[...].astype(BF16)

        @pl.when(nxt_ref[p] >= 0)
        def _():
            for c in copies(nxt_ref[p]):
                c.start()

    @pl.when(used)
    def _():
        lo, hi = _unpack_halves(xs_ref[...])
        lo, hi = lo.astype(BF16), hi.astype(BF16)
        dh = lo.shape[1]
        hg = _dot(lo, wg_ref[:dh, :]) + _dot(hi, wg_ref[dh:, :])
        hu = _dot(lo, wu_ref[:dh, :]) + _dot(hi, wu_ref[dh:, :])
        h = (hg * _sigmoid(hg)) * hu
        y_ref[...] = _pack_halves(_dot(h.astype(BF16), wd_ref[...]))

    @pl.when(jnp.logical_not(used))
    def _():
        y_ref[...] = jnp.zeros_like(y_ref)


def _expert_mlp(xs, block_e, first, nxt, n_used, wg, wu, wd):
    n_rows, dp = xs.shape
    nblk = n_rows // MOE_BLOCK
    d, ff = wg.shape[-2:]
    assert d == 2 * dp

    def row_map(p, be, fi, nx, nu):
        return (jnp.minimum(p, nu[0] - 1), 0)

    hbm = pl.BlockSpec(memory_space=pl.ANY)
    return pl.pallas_call(
        _expert_kernel,
        grid_spec=pltpu.PrefetchScalarGridSpec(
            num_scalar_prefetch=4,
            grid=(nblk,),
            in_specs=[pl.BlockSpec((MOE_BLOCK, dp), row_map), hbm, hbm, hbm],
            out_specs=pl.BlockSpec((MOE_BLOCK, dp), lambda p, *_: (p, 0)),
            scratch_shapes=[pltpu.VMEM((d, ff), wg.dtype), pltpu.VMEM((d, ff), wu.dtype),
                            pltpu.VMEM((ff, d), wd.dtype),
                            pltpu.VMEM((d, ff), BF16), pltpu.VMEM((d, ff), BF16), pltpu.VMEM((ff, d), BF16),
                            pltpu.SemaphoreType.DMA((3,))]),
        out_shape=jax.ShapeDtypeStruct((n_rows, dp), jnp.uint32),
        compiler_params=_cparams(("arbitrary",), VMEM_LIMIT + 2 * 1024 * 1024),
        name="moe_experts",
    )(block_e, first, nxt, n_used, xs, wg, wu, wd)


COMBINE_PARTS = 4


def _combine_kernel(dest_ref, x1_ref, gk_ref, sg_ref, su_ref, sd_ref, g_ref, b_ref, ys_ref,
                    o_ref, ybuf_ref, sh_ref, sem, *, tm, alpha):
    tp = tm // COMBINE_PARTS

    def issue(part):
        def body(t, c):
            for k in range(TOP_K):
                _row_copy(ys_ref, dest_ref[t * TOP_K + k], ybuf_ref, k * tm + t,
                          sem.at[part]).start(priority=k % 2)
            return c

        lax.fori_loop(part * tp, (part + 1) * tp, body, 0)

    issue(0)
    issue(1)
    xb = x1_ref[...].astype(BF16)
    hg = _dot(xb, sg_ref[...])
    hu = _dot(xb, su_ref[...])
    sh_ref[...] = _dot(((hg * _sigmoid(hg)) * hu).astype(BF16), sd_ref[...])

    for part in range(COMBINE_PARTS):
        rows = slice(part * tp, (part + 1) * tp)
        for k in range(TOP_K):
            pltpu.make_async_copy(ys_ref.at[pl.ds(0, tp)], ybuf_ref.at[pl.ds(0, tp)], sem.at[part]).wait()
        if part + 2 < COMBINE_PARTS:
            issue(part + 2)
        gk = gk_ref[rows, :]
        r_lo = jnp.zeros((tp, ybuf_ref.shape[1]), F32)
        r_hi = jnp.zeros((tp, ybuf_ref.shape[1]), F32)
        for k in range(TOP_K):
            lo, hi = _unpack_halves(ybuf_ref[k * tm + part * tp:k * tm + (part + 1) * tp, :])
            r_lo = r_lo + gk[:, k:k + 1] * lo
            r_hi = r_hi + gk[:, k:k + 1] * hi
        routed = jnp.concatenate([r_lo, r_hi], axis=1)
        o_ref[rows, :] = _layer_norm(alpha * x1_ref[rows, :] + (routed + sh_ref[rows, :]), g_ref[...], b_ref[...])


def _combine(x1, dest_flat, gk, ys, sg, su, sd, ln_g, ln_b, alpha, tm=128):
    n, d = x1.shape
    tm = min(tm, n)
    ff = sg.shape[-1]
    return pl.pallas_call(
        functools.partial(_combine_kernel, tm=tm, alpha=alpha),
        grid=(n // tm,),
        in_specs=[pl.BlockSpec((tm * TOP_K,), lambda i: (i,), memory_space=pltpu.SMEM),
                  pl.BlockSpec((tm, d), lambda i: (i, 0)),
                  pl.BlockSpec((tm, LANES), lambda i: (i, 0)),
                  pl.BlockSpec((d, ff), lambda i: (0, 0), pipeline_mode=pl.Buffered(1)),
                  pl.BlockSpec((d, ff), lambda i: (0, 0), pipeline_mode=pl.Buffered(1)),
                  pl.BlockSpec((ff, d), lambda i: (0, 0), pipeline_mode=pl.Buffered(1)),
                  pl.BlockSpec((1, d), lambda i: (0, 0)),
                  pl.BlockSpec((1, d), lambda i: (0, 0)),
                  pl.BlockSpec(memory_space=pl.ANY)],
        out_specs=pl.BlockSpec((tm, d), lambda i: (i, 0)),
        out_shape=jax.ShapeDtypeStruct((n, d), F32),
        scratch_shapes=[pltpu.VMEM((TOP_K * tm, d // 2), jnp.uint32), pltpu.VMEM((tm, d), F32),
                        pltpu.SemaphoreType.DMA((COMBINE_PARTS,))],
        compiler_params=_cparams(("arbitrary",)),
        name="moe_combine_ln2",
    )(dest_flat, x1, gk, sg, su, sd, ln_g, ln_b, ys)


def _rope_kernel(pos_ref, inv_ref, sign_ref, cos_ref, sin_ref):
    ang = pos_ref[...].astype(F32) * inv_ref[...]
    cos_ref[...] = jnp.cos(ang)
    sin_ref[...] = jnp.sin(ang) * sign_ref[...]


def _rope_tables(positions, dh, tm=512):
    half = dh // 2
    inv = ROPE_THETA ** (-jnp.arange(half, dtype=F32) * 2.0 / dh)
    reps = LANES // dh
    inv_l = jnp.tile(jnp.concatenate([inv, inv]), reps).reshape(1, LANES)
    sign_l = jnp.tile(jnp.concatenate([-jnp.ones((half,), F32), jnp.ones((half,), F32)]), reps).reshape(1, LANES)
    pos = positions.reshape(-1, 1)
    n = pos.shape[0]
    tm = min(tm, n)
    one = pl.BlockSpec((1, LANES), lambda i: (0, 0))
    blk = pl.BlockSpec((tm, LANES), lambda i: (i, 0))
    return pl.pallas_call(
        _rope_kernel,
        grid=(n // tm,),
        in_specs=[pl.BlockSpec((tm, 1), lambda i: (i, 0)), one, one],
        out_specs=[blk, blk],
        out_shape=[jax.ShapeDtypeStruct((n, LANES), F32)] * 2,
        compiler_params=_cparams(("parallel",)),
        name="rope_tables",
    )(pos, inv_l, sign_l)


def _dup_heads(w_rows, heads, dh):
    d = w_rows.shape[1]
    w3 = w_rows.reshape(heads, dh, d)
    return jnp.concatenate([w3, w3], axis=1).reshape(heads * 2 * dh, d)


def _hybrid_layer(x, positions, w_in, a_sinks, cmp_k_pos, cmp_k_w1, cmp_k_w2, cmp_v_pos, cmp_v_w1,
                  cmp_v_w2, w_o, ln1_g, ln1_b, w_router, router_bias, exp_w_gate, exp_w_up, exp_w_down,
                  sh_w_gate, sh_w_up, sh_w_down, ln2_g, ln2_b, alpha):
    bsz, seq, d = x.shape
    n = bsz * seq
    assert seq % NSA_TK == 0 and (seq // CMP_STRIDE) % LANES == 0 and A_WINDOW <= TQ and seq % (NSA_QT * TQ) == 0
    assert 3 <= seq // SLC_LEN <= LANES
    xf = x.reshape(n, d)
    xb = xf.astype(BF16)

    qa_w, kva = A_HEADS * A_HEAD_DIM, A_KV_HEADS * A_HEAD_DIM
    qb_w, kvb = B_HEADS * B_HEAD_DIM, B_KV_HEADS * B_HEAD_DIM
    widths = [qa_w, kva, kva, qb_w, kvb, kvb, kvb, kvb, kvb, kvb, 3 * B_HEADS, d, d]
    offs = np.concatenate([[0], np.cumsum(widths)]).tolist()
    (o_qa, o_ka, o_va, o_qb, o_kc, o_vc, o_ks, o_vs, o_kw, o_vw, o_gn, o_ga, o_end) = (
        offs[i] for i in (0, 1, 2, 3, 4, 5, 6, 7, 8, 9, 10, 11, 13))
    w_t = w_in.T
    blk = lambda start, width: list(range(start // PROJ_TN, (start + width) // PROJ_TN))
    assert all(o % PROJ_TN == 0 for o in (o_qa, o_qb, o_kc, o_vc, o_ks, o_vs, o_kw, o_vw))
    rep = B_HEADS // B_KV_HEADS
    w_gn = w_t[o_gn:o_ga].reshape(3, B_KV_HEADS, rep, d).transpose(1, 0, 2, 3).reshape(B_KV_HEADS, 3 * rep, d)
    w_gn_x = jnp.pad(w_gn, ((0, 0), (0, LANES - 3 * rep), (0, 0))).reshape(B_KV_HEADS * LANES, d)
    w_gn_x = w_gn_x.astype(BF16)
    assert o_ga % PROJ_TN == 3 * B_HEADS and o_end == w_t.shape[0]
    w_ka2 = _dup_heads(w_t[o_ka:o_va], A_KV_HEADS, A_HEAD_DIM).astype(BF16)
    w_va2 = _dup_heads(w_t[o_va:o_qb], A_KV_HEADS, A_HEAD_DIM).astype(BF16)
    all_blocks = lambda w: list(range(w.shape[0] // PROJ_TN))

    cos64, sin64 = _rope_tables(positions, A_HEAD_DIM)
    cos128, sin128 = _rope_tables(positions, B_HEAD_DIM)

    q_a = _project(xb, w_t, blk(o_qa, qa_w), cos64, sin64, "rope64", BF16,
                   q_cols=qa_w, q_scale=A_HEAD_DIM ** -0.5 * LOG2E)
    k_a = _project(xb, w_ka2, all_blocks(w_ka2), cos64, sin64, "rope64", BF16)
    v_a = _project(xb, w_va2, all_blocks(w_va2), cos64, sin64, "none", BF16)
    qk_b = _project(xb, w_t, blk(o_qb, qb_w) + blk(o_ks, kvb) + blk(o_kw, kvb) + blk(o_vs, kvb) + blk(o_vw, kvb),
                    cos128, sin128, "rope128", BF16, plain_tail=2 * kvb // PROJ_TN,
                    q_cols=qb_w, q_scale=B_HEAD_DIM ** -0.5 * LOG2E)
    kvc = _project(xb, w_t, blk(o_kc, kvb) + blk(o_vc, kvb), cos128, sin128, "rope128", F32,
                   plain_tail=kvb // PROJ_TN)
    gates = _project(xb, w_t, blk(o_ga - 3 * B_HEADS, 2 * d), cos128, sin128, "sigmoid", F32,
                     tm=512, shift=3 * B_HEADS)
    gnx = _project(xb, w_gn_x, all_blocks(w_gn_x), cos128, sin128, "sigmoid", F32)

    ya = _swa_attention(q_a, k_a, v_a, a_sinks, gates, bsz, seq)

    kcmp = _compress(kvc, 0, cmp_k_pos, cmp_k_w1.astype(BF16), cmp_k_w2.astype(BF16), bsz, seq)
    vcmp = _compress(kvc, kvb, cmp_v_pos, cmp_v_w1.astype(BF16), cmp_v_w2.astype(BF16), bsz, seq)
    nsa_cols = dict(q=0, ks=qb_w, kw=qb_w + kvb, vs=qb_w + 2 * kvb, vw=qb_w + 3 * kvb, g_b=d, gate=0)
    y = _nsa_attention(qk_b, qk_b, gnx, gates, ya, kcmp, vcmp, bsz, seq, nsa_cols)

    w_r = jnp.pad(w_router, ((0, 0), (0, LANES - N_EXPERTS))).astype(BF16)
    x1, x1p, scores = _out_proj(y, w_o.astype(BF16), xf, ln1_g.reshape(1, d), ln1_b.reshape(1, d), w_r, alpha)

    bias = jnp.pad(router_bias.astype(F32), (0, LANES - N_EXPERTS)).reshape(1, LANES)
    ek, rk, gk, cnt = _route(scores, bias)

    counts = cnt[0, :N_EXPERTS].astype(jnp.int32)
    padded = (counts + MOE_BLOCK - 1) // MOE_BLOCK * MOE_BLOCK
    pends = jnp.cumsum(padded)
    pstarts = pends - padded
    nblk = -(-(n * TOP_K) // MOE_BLOCK) + N_EXPERTS
    blk_start = jnp.arange(nblk, dtype=jnp.int32) * MOE_BLOCK
    block_e = jnp.minimum(jnp.sum((pends[None, :] <= blk_start[:, None]).astype(jnp.int32), axis=1),
                          N_EXPERTS - 1)
    n_used = (pends[-1:] // MOE_BLOCK).astype(jnp.int32)
    hit = ek[:, :TOP_K, None] == jnp.arange(N_EXPERTS, dtype=jnp.int32)
    dest = (jnp.sum(jnp.where(hit, pstarts.astype(jnp.int32), 0), axis=-1) + rk[:, :TOP_K]).reshape(-1)

    xs = _scatter_rows(x1p, dest, counts, padded, pstarts.astype(jnp.int32), n_used, nblk)
    first = jnp.concatenate([jnp.ones((1,), jnp.int32), (block_e[1:] != block_e[:-1]).astype(jnp.int32)])
    end_blk = jnp.take(pends // MOE_BLOCK, block_e).astype(jnp.int32)
    nxt = jnp.where(end_blk < n_used[0], jnp.take(block_e, jnp.minimum(end_blk, nblk - 1)), -1).astype(jnp.int32)
    ys = _expert_mlp(xs, block_e, first, nxt, n_used, exp_w_gate, exp_w_up, exp_w_down)
    out = _combine(x1, dest, gk, ys, sh_w_gate.astype(BF16), sh_w_up.astype(BF16), sh_w_down.astype(BF16),
                   ln2_g.reshape(1, d), ln2_b.reshape(1, d), alpha)
    return out.reshape(bsz, seq, d)


def kernel(x, positions, w_in, a_sinks, cmp_k_pos, cmp_k_w1, cmp_k_w2, cmp_v_pos, cmp_v_w1, cmp_v_w2,
           w_o, ln1_g, ln1_b, w_router, router_bias, exp_w_gate, exp_w_up, exp_w_down,
           sh_w_gate, sh_w_up, sh_w_down, ln2_g, ln2_b):
    depth = w_in.shape[0]
    alpha = (2.0 * depth) ** 0.25
    for l in range(depth):
        x = _hybrid_layer(x, positions, w_in[l], a_sinks[l], cmp_k_pos[l], cmp_k_w1[l], cmp_k_w2[l],
                          cmp_v_pos[l], cmp_v_w1[l], cmp_v_w2[l], w_o[l], ln1_g[l], ln1_b[l],
                          w_router[l], router_bias[l], exp_w_gate[l], exp_w_up[l], exp_w_down[l],
                          sh_w_gate[l], sh_w_up[l], sh_w_down[l], ln2_g[l], ln2_b[l], alpha)
    return x
```
